```python
import jax
import jax.numpy as jnp
from jax import lax
import numpy as np

D_MODEL = 1024
BATCH = 2
SEQ = 16384
DEPTH = 2

GRID_W = 64
CTX_LEN = 256
HEAD_DIM = 64
ROPE_THETA = 10000.0
RMS_EPS = 1e-6
A_Q_HEADS = 8
A_KV_HEADS = 2
B_Q_HEADS = 8
B_KV_HEADS = 2
WINDOW = 128
Q_BLOCK = 128
ATTN_IN_DIM = (A_Q_HEADS + 2 * A_KV_HEADS + B_Q_HEADS + 2 * B_KV_HEADS) * HEAD_DIM
ATTN_OUT_DIM = (A_Q_HEADS + B_Q_HEADS) * HEAD_DIM
RWKV_HEADS = D_MODEL // HEAD_DIM
DECAY_LORA = 64
ICLR_LORA = 64
GATE_LORA = 160
GN_EPS = 64e-5
N_EXPERTS = 16
N_GROUPS = 4
EXPERTS_PER_GROUP = N_EXPERTS // N_GROUPS
TOP_K = 2
D_EXPERT = 1024
MOE_BLOCK = 256
N_ATTN_LAYERS = (DEPTH + 1) // 2
N_RWKV_LAYERS = DEPTH // 2

kernel_name = 'hybrid_dit_swa_axial_rwkv7_groupmoe'


def rmsnorm(x, g):
    xf = x.astype(jnp.float32)
    y = xf * lax.rsqrt(jnp.mean(xf * xf, axis=-1, keepdims=True) + RMS_EPS)
    return (y * g.astype(jnp.float32)).astype(x.dtype)


def modulate(h, shift, scale):
    return h * (1 + scale) + shift


def axial_rope_tables(n_tokens):
    rows = n_tokens // GRID_W
    row = jnp.repeat(jnp.arange(rows, dtype=jnp.float32), GRID_W)
    col = (jnp.arange(rows * GRID_W) % GRID_W).astype(jnp.float32)
    n_freq = HEAD_DIM // 4
    inv = ROPE_THETA ** (-jnp.arange(n_freq, dtype=jnp.float32) / n_freq)
    ang = jnp.stack([row, col], axis=-1)[..., None] * inv
    ang = ang[:, None, :, None, :]
    return jnp.cos(ang), jnp.sin(ang)


def apply_rope(x, cos, sin):
    xs = x.reshape(*x.shape[:-1], 2, 2, HEAD_DIM // 4).astype(jnp.float32)
    rot = jnp.stack([-xs[..., 1, :], xs[..., 0, :]], axis=-2)
    return (xs * cos + rot * sin).reshape(x.shape).astype(x.dtype)


def attend(q, k, v, sink=None, mask=None):
    s = jnp.einsum('bqhgd,bshd->bhgqs', q, k, preferred_element_type=jnp.float32)
    if mask is not None:
        s = jnp.where(mask, s, -jnp.inf)
    if sink is not None:
        sk = jnp.broadcast_to(sink.astype(jnp.float32)[None, :, :, None, None], s.shape[:-1] + (1,))
        p = jax.nn.softmax(jnp.concatenate([sk, s], axis=-1), axis=-1)[..., 1:]
    else:
        p = jax.nn.softmax(s, axis=-1)
    return jnp.einsum('bhgqs,bshd->bqhgd', p.astype(v.dtype), v)


def window_attention(q, k, v, kc, vc, sink):
    B, S = q.shape[:2]
    L = kc.shape[1]
    nb = S // Q_BLOCK
    span = Q_BLOCK + 2 * WINDOW
    pad = ((0, 0), (WINDOW, WINDOW), (0, 0), (0, 0))
    kp, vp = jnp.pad(k, pad), jnp.pad(v, pad)
    qb = jnp.swapaxes(q.reshape(B, nb, Q_BLOCK, *q.shape[2:]), 0, 1)
    ctx_ok = jnp.ones((Q_BLOCK, L), dtype=bool)

    def one_block(args):
        q_blk, i = args
        start = i * Q_BLOCK
        qpos = start + jnp.arange(Q_BLOCK)
        kpos = start - WINDOW + jnp.arange(span)
        band = (jnp.abs(qpos[:, None] - kpos[None, :]) <= WINDOW) & (kpos >= 0)[None, :] & (kpos < S)[None, :]
        k_blk = jnp.concatenate([kc, lax.dynamic_slice_in_dim(kp, start, span, axis=1)], axis=1)
        v_blk = jnp.concatenate([vc, lax.dynamic_slice_in_dim(vp, start, span, axis=1)], axis=1)
        return attend(q_blk, k_blk, v_blk, sink=sink, mask=jnp.concatenate([ctx_ok, band], axis=1))

    o = lax.map(one_block, (qb, jnp.arange(nb)))
    return jnp.swapaxes(o, 0, 1).reshape(q.shape)


def block_attention(q, k, v):
    B, S = q.shape[:2]
    nb = S // Q_BLOCK
    qb = jnp.swapaxes(q.reshape(B, nb, Q_BLOCK, *q.shape[2:]), 0, 1)
    o = lax.map(lambda q_blk: attend(q_blk, k, v), qb)
    return jnp.swapaxes(o, 0, 1).reshape(q.shape)


def attn_mixer(hc, hl, w_in, w_out, sink, q_norm_g, k_norm_g, cos, sin, with_ctx):
    B, S, _ = hl.shape
    qscale = HEAD_DIM ** -0.5
    sizes = [A_Q_HEADS, A_KV_HEADS, A_KV_HEADS, B_Q_HEADS, B_KV_HEADS, B_KV_HEADS]
    cuts = [int(n) * HEAD_DIM for n in np.cumsum(sizes)[:-1]]

    def project(h):
        T = h.shape[1]
        parts = jnp.split(h @ w_in, cuts, axis=-1)
        qa, ka, va, qb, kb, vb = [p.reshape(B, T, -1, HEAD_DIM) for p in parts]
        return qa, ka, va, rmsnorm(qb, q_norm_g), rmsnorm(kb, k_norm_g), vb

    def group(q, n_kv):
        return (q * qscale).reshape(q.shape[0], q.shape[1], n_kv, -1, HEAD_DIM)

    sink_g = sink.reshape(A_KV_HEADS, -1)
    cqa, cka, cva, cqb, ckb, cvb = project(hc)
    qa, ka, va, qb, kb, vb = project(hl)
    qa, ka, qb, kb = (apply_rope(t, cos, sin) for t in (qa, ka, qb, kb))
    oa = window_attention(group(qa, A_KV_HEADS), ka, va, cka, cva, sink_g)
    ob = block_attention(group(qb, B_KV_HEADS), jnp.concatenate([ckb, kb], axis=1), jnp.concatenate([cvb, vb], axis=1))
    yl = jnp.concatenate([oa.reshape(B, S, -1), ob.reshape(B, S, -1)], axis=-1) @ w_out
    yc = None
    if with_ctx:
        L = hc.shape[1]
        oca = attend(group(cqa, A_KV_HEADS), cka, cva, sink=sink_g)
        ocb = attend(group(cqb, B_KV_HEADS), ckb, cvb)
        yc = jnp.concatenate([oca.reshape(B, L, -1), ocb.reshape(B, L, -1)], axis=-1) @ w_out
    return yc, yl


def centred_shift(h):
    hp = jnp.pad(h, ((0, 0), (1, 1), (0, 0)))
    return 0.5 * (hp[:, :-2] + hp[:, 2:])


def wkv_scan(state, r, decay, k, v, a, b, reverse):
    def step(s, inp):
        r_t, w_t, k_t, v_t, a_t, b_t = inp
        sa = jnp.einsum('bhvk,bhk->bhv', s, a_t)
        s = s * w_t[:, :, None, :] + sa[..., None] * b_t[:, :, None, :] + v_t[..., None] * k_t[:, :, None, :]
        return s, jnp.einsum('bhvk,bhk->bhv', s, r_t)
    xs = tuple(jnp.swapaxes(t, 0, 1) for t in (r, decay, k, v, a, b))
    state, y = lax.scan(step, state, xs, reverse=reverse)
    return state, jnp.swapaxes(y, 0, 1)


def rwkv_mixer(hc, hl, x_mix, w_r, w_k, w_v, w_o, w0, w1, w2, a0, a1, a2, g1, g2, k_k, k_a, r_k, ln_g, ln_b, with_ctx):
    B, L, D = hc.shape
    f32 = jnp.float32
    h = jnp.concatenate([hc, hl], axis=1)
    T = h.shape[1]
    xx = jnp.concatenate([centred_shift(hc), centred_shift(hl)], axis=1) - h
    mix = lambda j: h + xx * x_mix[j]
    heads = lambda t: t.reshape(*t.shape[:-1], RWKV_HEADS, HEAD_DIM).astype(f32)
    r = heads(mix(0) @ w_r)
    k = heads(mix(2) @ w_k)
    v = heads(mix(3) @ w_v)
    g = jax.nn.sigmoid(mix(5) @ g1) @ g2
    kk = k * heads(k_k)
    kk = kk * lax.rsqrt(jnp.maximum(jnp.sum(kk * kk, axis=-1, keepdims=True), 1e-24))
    y = jnp.zeros_like(v)
    bonus = jnp.zeros(v.shape[:-1] + (1,), f32)
    for d in range(2):
        reverse = d == 1
        w_log = -jax.nn.softplus(-heads(w0[d] + jnp.tanh(mix(1) @ w1[d]) @ w2[d])) - 0.5
        decay = jnp.exp(-jnp.exp(w_log))
        iclr = jax.nn.sigmoid(heads(a0[d] + (mix(4) @ a1[d]) @ a2[d]))
        k_d = k * (1.0 + (iclr - 1.0) * heads(k_a))
        ins = (r, decay, k_d, v, -kk, kk * iclr)
        zero = jnp.zeros((B, RWKV_HEADS, HEAD_DIM, HEAD_DIM), f32)
        s_ctx, y_c = wkv_scan(zero, *(t[:, :L] for t in ins), reverse=reverse)
        _, y_l = wkv_scan(s_ctx, *(t[:, L:] for t in ins), reverse=reverse)
        y = y + jnp.concatenate([y_c, y_l], axis=1)
        bonus = bonus + jnp.sum(r * k_d * r_k.astype(f32), axis=-1, keepdims=True)
    mu = jnp.mean(y, axis=-1, keepdims=True)
    yn = (y - mu) * lax.rsqrt(jnp.mean(jnp.square(y - mu), axis=-1, keepdims=True) + GN_EPS)
    o = yn.reshape(B, T, D) * ln_g.astype(f32) + ln_b.astype(f32) + (bonus * v).reshape(B, T, D)
    o = o.astype(h.dtype) * g
    yl = o[:, L:] @ w_o
    yc = o[:, :L] @ w_o if with_ctx else None
    return yc, yl


def route(h, router_w, router_bias):
    n = h.shape[0]
    probs = jax.nn.softmax((h @ router_w).astype(jnp.float32), axis=-1)
    sel = (probs + router_bias.astype(jnp.float32)).reshape(n, N_GROUPS, EXPERTS_PER_GROUP)
    group_score = jnp.sum(lax.top_k(sel, TOP_K)[0], axis=-1)
    g_idx = jnp.argmax(group_score, axis=-1)
    in_group = jnp.take_along_axis(sel, g_idx[:, None, None], axis=1)[:, 0]
    local = lax.top_k(in_group, TOP_K)[1]
    expert_idx = g_idx[:, None] * EXPERTS_PER_GROUP + local
    w = jnp.take_along_axis(probs, expert_idx, axis=1)
    return expert_idx, w / jnp.sum(w, axis=-1, keepdims=True)


def moe_ffn(h, expert_idx, gate_w, w1, w3, w2):
    N, D = h.shape
    NK = N * TOP_K
    flat_e = expert_idx.reshape(NK)
    flat_tok = jnp.repeat(jnp.arange(N, dtype=jnp.int32), TOP_K)
    flat_w = gate_w.reshape(NK)
    order = jnp.argsort(flat_e)
    e_sorted = flat_e[order]
    counts = jnp.bincount(flat_e, length=N_EXPERTS)
    padded = (counts + MOE_BLOCK - 1) // MOE_BLOCK * MOE_BLOCK
    pad_end = jnp.cumsum(padded)
    pad_start = pad_end - padded
    start = jnp.cumsum(counts) - counts
    dest = pad_start[e_sorted] + jnp.arange(NK) - start[e_sorted]
    n_blocks = -(-NK // MOE_BLOCK) + N_EXPERTS
    n_slots = n_blocks * MOE_BLOCK
    slot_tok = jnp.full((n_slots,), N, dtype=jnp.int32).at[dest].set(flat_tok[order])
    slot_w = jnp.zeros((n_slots,), gate_w.dtype).at[dest].set(flat_w[order])
    block_expert = jnp.minimum(jnp.searchsorted(pad_end, jnp.arange(n_blocks) * MOE_BLOCK, side='right'), N_EXPERTS - 1)
    h_pad = jnp.concatenate([h, jnp.zeros((1, D), h.dtype)], axis=0)
    xb = h_pad[slot_tok].reshape(n_blocks, MOE_BLOCK, D)

    def expert_block(args):
        x_blk, e = args
        return (jax.nn.silu(x_blk @ w1[e]) * (x_blk @ w3[e])) @ w2[e]

    yb = lax.map(expert_block, (xb, block_expert)).reshape(n_slots, D)
    out = jnp.zeros((N + 1, D), h.dtype).at[slot_tok].add(yb * slot_w[:, None].astype(h.dtype))
    return out[:N]


def setup_inputs(seed: int = 0) -> dict:
    key = jax.random.key(seed)
    ks = iter(jax.random.split(key, 64))
    nrm = lambda shape, scale: jax.random.normal(next(ks), shape, jnp.float32) * scale
    D = D_MODEL
    NA, NR = N_ATTN_LAYERS, N_RWKV_LAYERS
    H, N = RWKV_HEADS, HEAD_DIM
    E, F = N_EXPERTS, D_EXPERT
    return {
        'x': nrm((BATCH, SEQ, D), 1.0),
        'c': nrm((BATCH, D), 1.0),
        'ctx': nrm((BATCH, CTX_LEN, D), 1.0),
        'c_ctx': nrm((D,), 1.0),
        'ada_w': nrm((DEPTH, D, 6 * D), 0.5 * D ** -0.5),
        'ada_b': nrm((DEPTH, 6 * D), 0.02),
        'norm_mix_g': 1.0 + nrm((DEPTH, D), 0.02),
        'norm_ffn_g': 1.0 + nrm((DEPTH, D), 0.02),
        'attn_w_in': nrm((NA, D, ATTN_IN_DIM), D ** -0.5),
        'attn_w_out': nrm((NA, ATTN_OUT_DIM, D), ATTN_OUT_DIM ** -0.5),
        'attn_sink': nrm((NA, A_Q_HEADS), 1.0),
        'attn_q_norm_g': 1.0 + nrm((NA, HEAD_DIM), 0.02),
        'attn_k_norm_g': 1.0 + nrm((NA, HEAD_DIM), 0.02),
        'rwkv_x_mix': jax.random.uniform(next(ks), (NR, 6, D), jnp.float32),
        'rwkv_w_r': nrm((NR, D, D), D ** -0.5),
        'rwkv_w_k': nrm((NR, D, D), D ** -0.5),
        'rwkv_w_v': nrm((NR, D, D), D ** -0.5),
        'rwkv_w_o': nrm((NR, D, D), D ** -0.5),
        'rwkv_decay_w0': jax.random.uniform(next(ks), (NR, 2, D), jnp.float32, minval=-6.0, maxval=-1.0),
        'rwkv_decay_w1': nrm((NR, 2, D, DECAY_LORA), 0.5 * D ** -0.5),
        'rwkv_decay_w2': nrm((NR, 2, DECAY_LORA, D), 0.5 * DECAY_LORA ** -0.5),
        'rwkv_iclr_a0': nrm((NR, 2, D), 0.1),
        'rwkv_iclr_a1': nrm((NR, 2, D, ICLR_LORA), 0.5 * D ** -0.5),
        'rwkv_iclr_a2': nrm((NR, 2, ICLR_LORA, D), 0.5 * ICLR_LORA ** -0.5),
        'rwkv_gate_g1': nrm((NR, D, GATE_LORA), D ** -0.5),
        'rwkv_gate_g2': nrm((NR, GATE_LORA, D), GATE_LORA ** -0.5),
        'rwkv_k_k': 0.85 + nrm((NR, D), 0.02),
        'rwkv_k_a': 1.0 + nrm((NR, D), 0.02),
        'rwkv_r_k': nrm((NR, H, N), 0.1),
        'rwkv_ln_g': 1.0 + nrm((NR, D), 0.02),
        'rwkv_ln_b': nrm((NR, D), 0.02),
        'router_w': nrm((D, E), D ** -0.5),
        'router_bias': nrm((E,), 0.01),
        'moe_w1': nrm((DEPTH, E, D, F), D ** -0.5),
        'moe_w3': nrm((DEPTH, E, D, F), D ** -0.5),
        'moe_w2': nrm((DEPTH, E, F, D), F ** -0.5),
        'final_norm_g': 1.0 + nrm((D,), 0.02),
    }


def reference(x, c, ctx, c_ctx, ada_w, ada_b, norm_mix_g, norm_ffn_g, attn_w_in, attn_w_out, attn_sink,
              attn_q_norm_g, attn_k_norm_g, rwkv_x_mix, rwkv_w_r, rwkv_w_k, rwkv_w_v, rwkv_w_o,
              rwkv_decay_w0, rwkv_decay_w1, rwkv_decay_w2, rwkv_iclr_a0, rwkv_iclr_a1, rwkv_iclr_a2,
              rwkv_gate_g1, rwkv_gate_g2, rwkv_k_k, rwkv_k_a, rwkv_r_k, rwkv_ln_g, rwkv_ln_b,
              router_w, router_bias, moe_w1, moe_w3, moe_w2, final_norm_g):
    B, S, D = x.shape
    L = ctx.shape[1]
    cos, sin = axial_rope_tables(S)
    xl, xc = x, ctx
    for i in range(DEPTH):
        with_ctx = i < DEPTH - 1
        j = i // 2
        ml = jnp.moveaxis((jax.nn.silu(c) @ ada_w[i] + ada_b[i]).reshape(B, 6, D), 1, 0)[:, :, None, :]
        mc = (jax.nn.silu(c_ctx) @ ada_w[i] + ada_b[i]).reshape(6, D)
        hl = modulate(rmsnorm(xl, norm_mix_g[i]), ml[0], ml[1])
        hc = modulate(rmsnorm(xc, norm_mix_g[i]), mc[0], mc[1])
        if i % 2 == 0:
            yc, yl = attn_mixer(hc, hl, attn_w_in[j], attn_w_out[j], attn_sink[j], attn_q_norm_g[j],
                                attn_k_norm_g[j], cos, sin, with_ctx)
        else:
            yc, yl = rwkv_mixer(hc, hl, rwkv_x_mix[j], rwkv_w_r[j], rwkv_w_k[j], rwkv_w_v[j], rwkv_w_o[j],
                                rwkv_decay_w0[j], rwkv_decay_w1[j], rwkv_decay_w2[j],
                                rwkv_iclr_a0[j], rwkv_iclr_a1[j], rwkv_iclr_a2[j],
                                rwkv_gate_g1[j], rwkv_gate_g2[j], rwkv_k_k[j], rwkv_k_a[j], rwkv_r_k[j],
                                rwkv_ln_g[j], rwkv_ln_b[j], with_ctx)
        xl = xl + ml[2] * yl
        hl = modulate(rmsnorm(xl, norm_ffn_g[i]), ml[3], ml[4]).reshape(B * S, D)
        if with_ctx:
            xc = xc + mc[2] * yc
            hc = modulate(rmsnorm(xc, norm_ffn_g[i]), mc[3], mc[4]).reshape(B * L, D)
            tokens = jnp.concatenate([hc, hl], axis=0)
        else:
            tokens = hl
        expert_idx, gate_w = route(tokens, router_w, router_bias)
        f = moe_ffn(tokens, expert_idx, gate_w, moe_w1[i], moe_w3[i], moe_w2[i])
        if with_ctx:
            xc = xc + mc[5] * f[:B * L].reshape(B, L, D)
            f = f[B * L:]
        xl = xl + ml[5] * f.reshape(B, S, D)
    return rmsnorm(xl, final_norm_g)
```

```python
import functools

import numpy as np
import jax
import jax.numpy as jnp
from jax import lax
from jax.experimental import pallas as pl
from jax.experimental.pallas import tpu as pltpu

F32 = jnp.float32
BF16 = jnp.bfloat16

D_MODEL = 1024
HEAD_DIM = 64
GRID_W = 64
ROPE_THETA = 10000.0
RMS_EPS = 1e-6
GN_EPS = 64e-5
A_Q_HEADS = 8
A_KV_HEADS = 2
B_Q_HEADS = 8
B_KV_HEADS = 2
GROUP = 4
WINDOW = 128
N_EXPERTS = 16
N_GROUPS = 4
EXPERTS_PER_GROUP = 4
TOP_K = 2
LANES = 128
TM = 256
MOE_ROWS = 512
SCAN_CHUNK = 64
VMEM_LIMIT = 56 * 1024 * 1024
NEG = -1e30


def _cparams(sem):
    return pltpu.CompilerParams(dimension_semantics=sem, vmem_limit_bytes=VMEM_LIMIT)


def _dot(a, b):
    return jnp.dot(a, b, preferred_element_type=F32)


def _dot_nt(a, b):
    return lax.dot_general(a, b, (((1,), (1,)), ((), ())), preferred_element_type=F32)


def _split(x):
    hi = x.astype(BF16)
    lo = (x - hi.astype(F32)).astype(BF16)
    return hi, lo


def _dot3(x, w):
    xh, xl = _split(x)
    wh, wl = _split(w)
    return _dot(xh, wh) + _dot(xh, wl) + _dot(xl, wh)


def _segsum(v, ones):
    hi, lo = _split(v)
    return _dot(hi, ones) + _dot(lo, ones)


def _segsum_wide(v, ones):
    n = v.shape[1] // LANES
    return jnp.concatenate([_segsum(v[:, j * LANES:(j + 1) * LANES], ones) for j in range(n)], axis=1)


def _norm_mod(x, g, shift, scale):
    ms = jnp.mean(x * x, axis=-1, keepdims=True)
    return (x * lax.rsqrt(ms + RMS_EPS) * g) * (1.0 + scale) + shift


def _sigmoid(x):
    return 1.0 / (1.0 + jnp.exp(-x))


def _seg_ones():
    i = np.arange(LANES)
    return jnp.asarray((i[:, None] // HEAD_DIM) == (i[None, :] // HEAD_DIM), dtype=BF16)


def _ada_kernel(c_ref, w_ref, b_ref, o_ref):
    c = c_ref[...]
    s = c * _sigmoid(c)
    o_ref[0] = _dot3(s, w_ref[0]) + b_ref[0]


def _ada(cs, ada_w, ada_b):
    depth, d, n = ada_w.shape
    tn = 1536
    return pl.pallas_call(
        _ada_kernel,
        out_shape=jax.ShapeDtypeStruct((depth, 8, n), F32),
        grid=(depth, n // tn),
        in_specs=[
            pl.BlockSpec((8, d), lambda l, j: (0, 0)),
            pl.BlockSpec((1, d, tn), lambda l, j: (l, 0, j)),
            pl.BlockSpec((1, 1, tn), lambda l, j: (l, 0, j)),
        ],
        out_specs=pl.BlockSpec((1, 8, tn), lambda l, j: (l, 0, j)),
        compiler_params=_cparams(("arbitrary", "arbitrary")),
        name="ada",
    )(cs, ada_w, ada_b.reshape(depth, 1, n))


def _inproj_kernel(x_ref, mod_ref, g_ref, w_ref, wrot_ref, cos_ref, sin_ref, gq_ref, gk_ref, ones_ref,
                   qa_ref, ka_ref, va_ref, qb_ref, kb_ref, vb_ref):
    mod = mod_ref[0, 0]
    h = _norm_mod(x_ref[0], g_ref[...], mod[0:1], mod[1:2]).astype(BF16)
    y = _dot(h, w_ref[...])
    yr = _dot(h, wrot_ref[...])
    cos = cos_ref[...]
    sin = sin_ref[...]
    ones = ones_ref[...]
    qscale = HEAD_DIM ** -0.5

    def put(ref, tile, val):
        ref[0, 2 * tile] = val[:, :HEAD_DIM].astype(ref.dtype)
        ref[0, 2 * tile + 1] = val[:, HEAD_DIM:].astype(ref.dtype)

    def chunk(a, c):
        return a[:, c * LANES:(c + 1) * LANES]

    for c in range(4):
        put(qa_ref, c, (chunk(y, c) * cos + chunk(yr, c) * sin) * qscale)
    put(ka_ref, 0, chunk(y, 4) * cos + chunk(yr, 4) * sin)
    put(va_ref, 0, chunk(y, 5))

    def normed(c, cr, gain_ref):
        v = chunk(y, c)
        rs = lax.rsqrt(_segsum(v * v, ones) * (1.0 / HEAD_DIM) + RMS_EPS)
        return (v * rs * gain_ref[0:1]) * cos + (chunk(yr, cr) * rs * gain_ref[1:2]) * sin

    for c in range(4):
        put(qb_ref, c, normed(6 + c, 5 + c, gq_ref) * qscale)
    put(kb_ref, 0, normed(10, 9, gk_ref))
    put(vb_ref, 0, chunk(y, 11))


def _inproj(x, mods, g, w_in, w_rot, cos, sin, gq2, gk2, ones):
    B, T, D = x.shape
    nt = T // TM
    heads = lambda n: jax.ShapeDtypeStruct((B, n, T, HEAD_DIM), BF16)
    hspec = lambda n: pl.BlockSpec((1, n, TM, HEAD_DIM), lambda b, i: (b, 0, i, 0))
    full = lambda a: pl.BlockSpec(a.shape, lambda b, i: (0,) * a.ndim)
    return pl.pallas_call(
        _inproj_kernel,
        out_shape=(heads(8), heads(2), heads(2), heads(8), heads(2), heads(2)),
        grid=(B, nt),
        in_specs=[
            pl.BlockSpec((1, TM, D), lambda b, i: (b, i, 0)),
            pl.BlockSpec((1, 1, 6, D), lambda b, i: (b, jnp.minimum(i, 1), 0, 0)),
            full(g), full(w_in), full(w_rot),
            pl.BlockSpec((TM, LANES), lambda b, i: (i, 0)),
            pl.BlockSpec((TM, LANES), lambda b, i: (i, 0)),
            full(gq2), full(gk2), full(ones),
        ],
        out_specs=(hspec(8), hspec(2), hspec(2), hspec(8), hspec(2), hspec(2)),
        compiler_params=_cparams(("parallel", "parallel")),
        name="attn_inproj",
    )(x, mods, g, w_in, w_rot, cos, sin, gq2, gk2, ones)


def _flash_kernel(sink_ref, q_ref, k_ref, v_ref, o_ref, m_scr, l_scr, acc_scr, *, tq, nk):
    h = pl.program_id(1)
    kj = pl.program_id(3)

    @pl.when(kj == 0)
    def _():
        m_scr[...] = jnp.full(m_scr.shape, NEG, F32)
        l_scr[...] = jnp.zeros(l_scr.shape, F32)
        acc_scr[...] = jnp.zeros(acc_scr.shape, F32)

    q = q_ref[0, 0].reshape(GROUP * tq, HEAD_DIM)
    s = _dot_nt(q, k_ref[0, 0])
    m_prev = m_scr[...]
    m_new = jnp.maximum(m_prev, jnp.max(s, axis=-1, keepdims=True))
    alpha = jnp.exp(m_prev - m_new)
    p = jnp.exp(s - m_new)
    l_scr[...] = alpha * l_scr[...] + jnp.sum(p, axis=-1, keepdims=True)
    acc_scr[...] = alpha * acc_scr[...] + _dot(p.astype(BF16), v_ref[0, 0])
    m_scr[...] = m_new

    @pl.when(kj == nk - 1)
    def _():
        for g in range(GROUP):
            rows = slice(g * tq, (g + 1) * tq)
            l = l_scr[rows] + jnp.exp(sink_ref[h * GROUP + g] - m_scr[rows])
            o_ref[0, :, g * HEAD_DIM:(g + 1) * HEAD_DIM] = (acc_scr[rows] / l).astype(o_ref.dtype)


def _flash(sink, q, k, v, *, q_rows, q_off, k_rows, tq, tk):
    B, Hkv = k.shape[:2]
    nq, nk = q_rows // tq, k_rows // tk
    qo = q_off // tq
    return pl.pallas_call(
        functools.partial(_flash_kernel, tq=tq, nk=nk),
        out_shape=jax.ShapeDtypeStruct((B, q_rows, Hkv * GROUP * HEAD_DIM), BF16),
        grid=(B, Hkv, nq, nk),
        in_specs=[
            pl.BlockSpec(memory_space=pltpu.SMEM),
            pl.BlockSpec((1, 1, GROUP, tq, HEAD_DIM), lambda b, h, i, j: (b, h, 0, i + qo, 0)),
            pl.BlockSpec((1, 1, tk, HEAD_DIM), lambda b, h, i, j: (b, h, j, 0)),
            pl.BlockSpec((1, 1, tk, HEAD_DIM), lambda b, h, i, j: (b, h, j, 0)),
        ],
        out_specs=pl.BlockSpec((1, tq, GROUP * HEAD_DIM), lambda b, h, i, j: (b, i, h)),
        scratch_shapes=[
            pltpu.VMEM((GROUP * tq, 1), F32),
            pltpu.VMEM((GROUP * tq, 1), F32),
            pltpu.VMEM((GROUP * tq, HEAD_DIM), F32),
        ],
        compiler_params=_cparams(("parallel", "parallel", "parallel", "arbitrary")),
        name="flash_attn",
    )(sink, q, k, v)


def _window_kernel(sink_ref, q_ref, kc_ref, vc_ref, k0_ref, k1_ref, k2_ref, v0_ref, v1_ref, v2_ref, o_ref, *, nb):
    h = pl.program_id(1)
    i = pl.program_id(2)
    rows = GROUP * WINDOW
    q = q_ref[0, 0].reshape(rows, HEAD_DIM)
    r = lax.broadcasted_iota(jnp.int32, (rows, WINDOW), 0) & (WINDOW - 1)
    c = lax.broadcasted_iota(jnp.int32, (rows, WINDOW), 1)
    sc = _dot_nt(q, kc_ref[0, 0])
    s0 = jnp.where((c >= r) & (i > 0), _dot_nt(q, k0_ref[0, 0]), NEG)
    s1 = _dot_nt(q, k1_ref[0, 0])
    s2 = jnp.where((c <= r) & (i < nb - 1), _dot_nt(q, k2_ref[0, 0]), NEG)
    sink = jnp.concatenate(
        [jnp.full((WINDOW, 1), sink_ref[h * GROUP + g], F32) for g in range(GROUP)], axis=0)
    rowmax = lambda s: jnp.max(s, axis=-1, keepdims=True)
    m = jnp.maximum(jnp.maximum(rowmax(sc), rowmax(s0)), jnp.maximum(rowmax(s1), rowmax(s2)))
    m = jnp.maximum(m, sink)
    pc, p0, p1, p2 = (jnp.exp(s - m) for s in (sc, s0, s1, s2))
    rowsum = lambda p: jnp.sum(p, axis=-1, keepdims=True)
    l = rowsum(pc) + rowsum(p0) + rowsum(p1) + rowsum(p2) + jnp.exp(sink - m)
    acc = (_dot(pc.astype(BF16), vc_ref[0, 0]) + _dot(p0.astype(BF16), v0_ref[0, 0])
           + _dot(p1.astype(BF16), v1_ref[0, 0]) + _dot(p2.astype(BF16), v2_ref[0, 0]))
    out = acc / l
    for g in range(GROUP):
        o_ref[0, :, g * HEAD_DIM:(g + 1) * HEAD_DIM] = out[g * WINDOW:(g + 1) * WINDOW].astype(o_ref.dtype)


def _window_attn(sink, q, k, v, L, S):
    B, Hkv = k.shape[:2]
    nb = S // WINDOW
    pad = ((0, 0), (0, 0), (WINDOW, WINDOW), (0, 0))
    kp = jnp.pad(k[:, :, L:], pad)
    vp = jnp.pad(v[:, :, L:], pad)
    qo = L // WINDOW
    band = lambda j: pl.BlockSpec((1, 1, WINDOW, HEAD_DIM), lambda b, h, i: (b, h, i + j, 0))
    ctx = pl.BlockSpec((1, 1, L, HEAD_DIM), lambda b, h, i: (b, h, 0, 0))
    return pl.pallas_call(
        functools.partial(_window_kernel, nb=nb),
        out_shape=jax.ShapeDtypeStruct((B, S, Hkv * GROUP * HEAD_DIM), BF16),
        grid=(B, Hkv, nb),
        in_specs=[
            pl.BlockSpec(memory_space=pltpu.SMEM),
            pl.BlockSpec((1, 1, GROUP, WINDOW, HEAD_DIM), lambda b, h, i: (b, h, 0, i + qo, 0)),
            ctx, ctx, band(0), band(1), band(2), band(0), band(1), band(2),
        ],
        out_specs=pl.BlockSpec((1, WINDOW, GROUP * HEAD_DIM), lambda b, h, i: (b, i, h)),
        compiler_params=_cparams(("parallel", "parallel", "parallel")),
        name="window_attn",
    )(sink, q, k, v, kp, kp, kp, vp, vp, vp)


def _ffn_prep(x, y, mod, gffn, rwh_ref, rwl_ref, xo_ref, h_ref, lg_ref):
    xn = x + mod[2:3] * y
    h = _norm_mod(xn, gffn, mod[3:4], mod[4:5])
    xo_ref[0] = xn
    hh, hl = _split(h)
    h_ref[0] = hh
    rwh = rwh_ref[...]
    lg_ref[0] = _dot(hh, rwh) + _dot(hl, rwh) + _dot(hh, rwl_ref[...])


def _attn_out_kernel(oa_ref, ob_ref, wa_ref, wb_ref, x_ref, mod_ref, g_ref, rwh_ref, rwl_ref,
                     xo_ref, h_ref, lg_ref):
    y = _dot(oa_ref[0], wa_ref[...]) + _dot(ob_ref[0], wb_ref[...])
    _ffn_prep(x_ref[0], y, mod_ref[0, 0], g_ref[...], rwh_ref, rwl_ref, xo_ref, h_ref, lg_ref)


def _row_specs(D):
    row = lambda w: pl.BlockSpec((1, TM, w), lambda b, i: (b, i, 0))
    mod = pl.BlockSpec((1, 1, 6, D), lambda b, i: (b, jnp.minimum(i, 1), 0, 0))
    full = lambda a: pl.BlockSpec(a.shape, lambda b, i: (0,) * a.ndim)
    return row, mod, full


def _ffn_prep_outs(B, T, D):
    row, _, _ = _row_specs(D)
    shapes = (jax.ShapeDtypeStruct((B, T, D), F32), jax.ShapeDtypeStruct((B, T, D), BF16),
              jax.ShapeDtypeStruct((B, T, LANES), F32))
    return shapes, (row(D), row(D), row(LANES))


def _attn_out(oa, ob, wa, wb, x, mods, g, rwh, rwl):
    B, T, D = x.shape
    row, mod, full = _row_specs(D)
    shapes, specs = _ffn_prep_outs(B, T, D)
    return pl.pallas_call(
        _attn_out_kernel,
        out_shape=shapes,
        grid=(B, T // TM),
        in_specs=[row(oa.shape[-1]), row(ob.shape[-1]), full(wa), full(wb), row(D), mod, full(g),
                  full(rwh), full(rwl)],
        out_specs=specs,
        compiler_params=_cparams(("parallel", "parallel")),
        name="attn_out",
    )(oa, ob, wa, wb, x, mods, g, rwh, rwl)


def _gmm_kernel(be_ref, nu_ref, x_ref, w1_ref, w3_ref, w2_ref, o_ref):
    i = pl.program_id(0)

    @pl.when(i < nu_ref[0])
    def _():
        x = x_ref[...]
        a = _dot(x, w1_ref[0])
        b = _dot(x, w3_ref[0])
        mid = (a * _sigmoid(a)) * b
        o_ref[...] = _dot(mid.astype(BF16), w2_ref[0])

    @pl.when(i >= nu_ref[0])
    def _():
        o_ref[...] = jnp.zeros(o_ref.shape, o_ref.dtype)


def _gmm(block_expert, n_used, xs, w1, w3, w2):
    n_slots, D = xs.shape
    F = w1.shape[-1]
    nblk = n_slots // MOE_ROWS
    return pl.pallas_call(
        _gmm_kernel,
        out_shape=jax.ShapeDtypeStruct((n_slots, D), F32),
        grid_spec=pltpu.PrefetchScalarGridSpec(
            num_scalar_prefetch=2,
            grid=(nblk,),
            in_specs=[
                pl.BlockSpec((MOE_ROWS, D), lambda i, be, nu: (i, 0)),
                pl.BlockSpec((1, D, F), lambda i, be, nu: (be[i], 0, 0)),
                pl.BlockSpec((1, D, F), lambda i, be, nu: (be[i], 0, 0)),
                pl.BlockSpec((1, F, D), lambda i, be, nu: (be[i], 0, 0)),
            ],
            out_specs=pl.BlockSpec((MOE_ROWS, D), lambda i, be, nu: (i, 0)),
        ),
        compiler_params=_cparams(("arbitrary",)),
        name="moe_gmm",
    )(block_expert, n_used, xs, w1, w3, w2)


def _route(logits, router_bias):
    n = logits.shape[0]
    probs = jax.nn.softmax(logits, axis=-1)
    sel = (probs + router_bias.astype(F32)).reshape(n, N_GROUPS, EXPERTS_PER_GROUP)
    group_score = jnp.sum(lax.top_k(sel, TOP_K)[0], axis=-1)
    g_idx = jnp.argmax(group_score, axis=-1)
    in_group = jnp.take_along_axis(sel, g_idx[:, None, None], axis=1)[:, 0]
    local = lax.top_k(in_group, TOP_K)[1]
    expert_idx = g_idx[:, None] * EXPERTS_PER_GROUP + local
    w = jnp.take_along_axis(probs, expert_idx, axis=1)
    return expert_idx.astype(jnp.int32), w / jnp.sum(w, axis=-1, keepdims=True)


def _moe(h, logits, router_bias, w1, w3, w2):
    N, D = h.shape
    expert_idx, gate_w = _route(logits, router_bias)
    NK = N * TOP_K
    flat_e = expert_idx.reshape(NK)
    onehot = (flat_e[:, None] == jnp.arange(N_EXPERTS, dtype=jnp.int32)[None, :]).astype(jnp.int32)
    csum = jnp.cumsum(onehot, axis=0)
    counts = csum[-1]
    rank = jnp.take_along_axis(csum, flat_e[:, None], axis=1)[:, 0] - 1
    padded = (counts + MOE_ROWS - 1) // MOE_ROWS * MOE_ROWS
    pad_end = jnp.cumsum(padded)
    pad_start = pad_end - padded
    start = jnp.cumsum(counts) - counts
    dest = pad_start[flat_e] + rank
    nblk = -(-NK // MOE_ROWS) + N_EXPERTS
    n_slots = nblk * MOE_ROWS
    n_used = (pad_end[-1] // MOE_ROWS).astype(jnp.int32)
    blk = jnp.arange(nblk, dtype=jnp.int32)
    be = jnp.minimum(jnp.searchsorted(pad_end, blk * MOE_ROWS, side="right"), N_EXPERTS - 1).astype(jnp.int32)
    be = jnp.where(blk < n_used, be, be[jnp.maximum(n_used - 1, 0)])
    order = jnp.argsort(flat_e, stable=True).astype(jnp.int32)
    slot = jnp.arange(n_slots, dtype=jnp.int32)
    se = be[slot // MOE_ROWS]
    j = start[se] + (slot - pad_start[se])
    slot_tok = order[jnp.clip(j, 0, NK - 1)] // TOP_K
    xs = h[slot_tok]
    ys = _gmm(be, n_used.reshape(1), xs, w1, w3, w2)
    pos = dest.reshape(N, TOP_K)
    return ys[pos[:, 0]] * gate_w[:, 0:1] + ys[pos[:, 1]] * gate_w[:, 1:2]


def _residual_kernel(x_ref, f_ref, mod_ref, o_ref):
    o_ref[0] = x_ref[0] + mod_ref[0, 0][5:6] * f_ref[0]


def _final_kernel(x_ref, f_ref, mod_ref, g_ref, o_ref):
    x = x_ref[0] + mod_ref[0, 0][5:6] * f_ref[0]
    ms = jnp.mean(x * x, axis=-1, keepdims=True)
    o_ref[0] = x * lax.rsqrt(ms + RMS_EPS) * g_ref[...]


def _residual(x, f, mods):
    B, T, D = x.shape
    row, mod, _ = _row_specs(D)
    return pl.pallas_call(
        _residual_kernel,
        out_shape=jax.ShapeDtypeStruct((B, T, D), F32),
        grid=(B, T // TM),
        in_specs=[row(D), row(D), mod],
        out_specs=row(D),
        compiler_params=_cparams(("parallel", "parallel")),
        name="residual",
    )(x, f, mods)


def _final(x, f, mods, g, L):
    B, S, D = f.shape
    off = L // TM
    mod = pl.BlockSpec((1, 1, 6, D), lambda b, i: (b, 1, 0, 0))
    return pl.pallas_call(
        _final_kernel,
        out_shape=jax.ShapeDtypeStruct((B, S, D), F32),
        grid=(B, S // TM),
        in_specs=[pl.BlockSpec((1, TM, D), lambda b, i: (b, i + off, 0)),
                  pl.BlockSpec((1, TM, D), lambda b, i: (b, i, 0)), mod,
                  pl.BlockSpec(g.shape, lambda b, i: (0, 0))],
        out_specs=pl.BlockSpec((1, TM, D), lambda b, i: (b, i, 0)),
        compiler_params=_cparams(("parallel", "parallel")),
        name="final_norm",
    )(x, f, mods, g)


def _rwkv_proj_kernel(x_ref, xp_ref, xn_ref, mod_ref, g_ref, xmix_ref, wr_ref, wk_ref, wv_ref,
                      dw1_ref, dw2_ref, da1_ref, da2_ref, g1_ref, g2_ref, vec_ref, ones_ref,
                      r_ref, v_ref, kk_ref, bv_ref, gate_ref, w0_ref, w1_ref, kd0_ref, kd1_ref, bd0_ref, bd1_ref,
                      *, nt):
    i = pl.program_id(1)
    mod = mod_ref[0, 0]
    g = g_ref[...]
    nm = lambda x: _norm_mod(x, g, mod[0:1], mod[1:2])
    h = nm(x_ref[0])
    hp = nm(xp_ref[0])[7:8] * jnp.where(i >= 2, 1.0, 0.0)
    hn = nm(xn_ref[0])[0:1] * jnp.where((i >= 1) & (i < nt - 1), 1.0, 0.0)
    ridx = lax.broadcasted_iota(jnp.int32, h.shape, 0)
    h_dn = jnp.where(ridx == 0, hp, pltpu.roll(h, 1, axis=0))
    h_up = jnp.where(ridx == TM - 1, hn, pltpu.roll(h, TM - 1, axis=0))
    xx = 0.5 * (h_dn + h_up) - h
    xmix = xmix_ref[...]
    mix = lambda j: (h + xx * xmix[j:j + 1]).astype(BF16)
    vec = vec_ref[...]
    ones = ones_ref[...]

    r = _dot(mix(0), wr_ref[...])
    k = _dot(mix(2), wk_ref[...])
    v = _dot(mix(3), wv_ref[...])
    gate_ref[0] = _dot(_sigmoid(_dot(mix(5), g1_ref[...])).astype(BF16), g2_ref[...])
    kk = k * vec[0:1]
    kk = kk * lax.rsqrt(jnp.maximum(_segsum_wide(kk * kk, ones), 1e-24))
    lw = jnp.tanh(_dot(mix(1), dw1_ref[...])).astype(BF16)
    la = _dot(mix(4), da1_ref[...]).astype(BF16)
    r_ref[0] = r
    v_ref[0] = v
    kk_ref[0] = kk
    bonus = jnp.zeros_like(r)
    lora = DECAY_LORA
    for d, (w_ref, kd_ref, bd_ref) in enumerate(((w0_ref, kd0_ref, bd0_ref), (w1_ref, kd1_ref, bd1_ref))):
        z = -(vec[3 + d:4 + d] + _dot(lw[:, d * lora:(d + 1) * lora], dw2_ref[d]))
        softplus = jnp.maximum(z, 0.0) + jnp.log(1.0 + jnp.exp(-jnp.abs(z)))
        w_ref[0] = jnp.exp(-jnp.exp(-softplus - 0.5))
        iclr = _sigmoid(vec[5 + d:6 + d] + _dot(la[:, d * lora:(d + 1) * lora], da2_ref[d]))
        kd = k * (1.0 + (iclr - 1.0) * vec[1:2])
        kd_ref[0] = kd
        bd_ref[0] = kk * iclr
        bonus = bonus + _segsum_wide(r * kd * vec[2:3], ones)
    bv_ref[0] = bonus * v


DECAY_LORA = 64


def _rwkv_proj(x, mods, g, xmix, wr, wk, wv, dw1, dw2, da1, da2, g1, g2, vec, ones):
    B, T, D = x.shape
    nt = T // TM
    row, mod, full = _row_specs(D)
    r8 = TM // 8
    prev = pl.BlockSpec((1, 8, D), lambda b, i: (b, jnp.maximum(i * r8 - 1, 0), 0))
    nxt = pl.BlockSpec((1, 8, D), lambda b, i: (b, jnp.minimum((i + 1) * r8, T // 8 - 1), 0))
    out = jax.ShapeDtypeStruct((B, T, D), F32)
    return pl.pallas_call(
        functools.partial(_rwkv_proj_kernel, nt=nt),
        out_shape=(out,) * 11,
        grid=(B, nt),
        in_specs=[row(D), prev, nxt, mod, full(g), full(xmix), full(wr), full(wk), full(wv),
                  full(dw1), full(dw2), full(da1), full(da2), full(g1), full(g2), full(vec), full(ones)],
        out_specs=(row(D),) * 11,
        compiler_params=_cparams(("parallel", "parallel")),
        name="rwkv_proj",
    )(x, x, x, mods, g, xmix, wr, wk, wv, dw1, dw2, da1, da2, g1, g2, vec, ones)


def _lane_tiles_to_rows(x):
    n = x.shape[1] // LANES
    return jnp.concatenate([x[:, j * LANES:(j + 1) * LANES] for j in range(n)], axis=0)


def _rows_to_lane_tiles(x, n):
    m = x.shape[0] // n
    return jnp.concatenate([x[j * m:(j + 1) * m] for j in range(n)], axis=1)


def _scan_kernel(rf, rb, vf, vb, kkf, kkb, wf, wb, kdf, kdb, bdf, bdb, eye_ref, ones_ref, yf, yb, st, *, tc, nb):
    n = pl.program_id(0)

    @pl.when(n == 0)
    def _():
        st[...] = jnp.zeros(st.shape, F32)

    ones = ones_ref[...]
    N = HEAD_DIM
    nl = st.shape[-1] // LANES
    dirs = ((rf, vf, kkf, wf, kdf, bdf, yf), (rb, vb, kkb, wb, kdb, bdb, yb))

    def seg(x):
        res = _dot(_lane_tiles_to_rows(x.astype(BF16)), ones)
        return _rows_to_lane_tiles(res, nl)

    def extract(ybc):
        return jnp.sum(ybc * eye_ref[...], axis=0, keepdims=True)

    def step(i, carry):
        ip = jnp.maximum(i - 1, 0)
        for d, (R, V, KK, W, KD, BD, Y) in enumerate(dirs):
            row = i if d == 0 else tc - 1 - i
            rowp = ip if d == 0 else tc - 1 - ip
            for b in range(nb):
                gi = d * nb + b
                S = st[gi]
                cur = pl.ds(row, 1)
                a = -KK[b, cur, :]
                x = jnp.concatenate([S * a, S * R[b, pl.ds(rowp, 1), :], eye_ref[...] * V[b, cur, :]], axis=0)
                res = seg(x)
                sa, ybc, vcol = res[0:N], res[N:2 * N], res[2 * N:3 * N]
                st[gi] = S * W[b, cur, :] + sa * BD[b, cur, :] + vcol * KD[b, cur, :]
                Y[b, pl.ds(rowp, 1), :] = extract(ybc)
        return carry

    lax.fori_loop(0, tc, step, 0)
    for d, (R, V, KK, W, KD, BD, Y) in enumerate(dirs):
        last = tc - 1 if d == 0 else 0
        for b in range(nb):
            Y[b, last:last + 1, :] = extract(seg(st[d * nb + b] * R[b, last:last + 1, :]))


def _rwkv_scan(r, v, kk, w0, w1, kd0, kd1, bd0, bd1, eye, ones, L):
    B, T, D = r.shape
    tc = SCAN_CHUNK
    nc, nchunks = L // tc, T // tc
    fwd = pl.BlockSpec((B, tc, D), lambda n: (0, n, 0))
    rev_idx = lambda n: jnp.where(n < nc, nc - 1 - n, nchunks - 1 - (n - nc))
    rev = pl.BlockSpec((B, tc, D), lambda n: (0, rev_idx(n), 0))
    full = lambda a: pl.BlockSpec(a.shape, lambda n: (0,) * a.ndim)
    out = jax.ShapeDtypeStruct((B, T, D), F32)
    return pl.pallas_call(
        functools.partial(_scan_kernel, tc=tc, nb=B),
        out_shape=(out, out),
        grid=(nchunks,),
        in_specs=[fwd, rev, fwd, rev, fwd, rev, fwd, rev, fwd, rev, fwd, rev, full(eye), full(ones)],
        out_specs=(fwd, rev),
        scratch_shapes=[pltpu.VMEM((2 * B, HEAD_DIM, D), F32)],
        compiler_params=_cparams(("arbitrary",)),
        name="rwkv_scan",
    )(r, r, v, v, kk, kk, w0, w1, kd0, kd1, bd0, bd1, eye, ones)


def _rwkv_out_kernel(yf_ref, yb_ref, bv_ref, gate_ref, ln_ref, wo_ref, ones_ref, x_ref, mod_ref, g_ref,
                     rwh_ref, rwl_ref, xo_ref, h_ref, lg_ref):
    ones = ones_ref[...]
    y = yf_ref[0] + yb_ref[0]
    inv = 1.0 / HEAD_DIM
    dlt = y - _segsum_wide(y, ones) * inv
    yn = dlt * lax.rsqrt(_segsum_wide(dlt * dlt, ones) * inv + GN_EPS)
    ln = ln_ref[...]
    o = (yn * ln[0:1] + ln[1:2] + bv_ref[0]) * gate_ref[0]
    yl = _dot(o.astype(BF16), wo_ref[...])
    _ffn_prep(x_ref[0], yl, mod_ref[0, 0], g_ref[...], rwh_ref, rwl_ref, xo_ref, h_ref, lg_ref)


def _rwkv_out(yf, yb, bv, gate, ln, wo, ones, x, mods, g, rwh, rwl):
    B, T, D = x.shape
    row, mod, full = _row_specs(D)
    shapes, specs = _ffn_prep_outs(B, T, D)
    return pl.pallas_call(
        _rwkv_out_kernel,
        out_shape=shapes,
        grid=(B, T // TM),
        in_specs=[row(D), row(D), row(D), row(D), full(ln), full(wo), full(ones), row(D), mod, full(g),
                  full(rwh), full(rwl)],
        out_specs=specs,
        compiler_params=_cparams(("parallel", "parallel")),
        name="rwkv_out",
    )(yf, yb, bv, gate, ln, wo, ones, x, mods, g, rwh, rwl)


def _rope_tables(S, L):
    rows = S // GRID_W
    row = jnp.repeat(jnp.arange(rows, dtype=F32), GRID_W)
    col = (jnp.arange(rows * GRID_W) % GRID_W).astype(F32)
    n_freq = HEAD_DIM // 4
    inv = ROPE_THETA ** (-jnp.arange(n_freq, dtype=F32) / n_freq)
    lane = np.arange(LANES) % HEAD_DIM
    axis, half, freq = lane // 32, (lane % 32) // 16, lane % 16
    pos = jnp.where(jnp.asarray(axis == 0)[None, :], row[:, None], col[:, None])
    ang = pos * inv[freq][None, :]
    sgn = jnp.asarray(np.where(half == 0, -1.0, 1.0), dtype=F32)
    cos = jnp.concatenate([jnp.ones((L, LANES), F32), jnp.cos(ang)], axis=0)
    sin = jnp.concatenate([jnp.zeros((L, LANES), F32), jnp.sin(ang) * sgn[None, :]], axis=0)
    return cos, sin


def kernel(x, c, ctx, c_ctx, ada_w, ada_b, norm_mix_g, norm_ffn_g, attn_w_in, attn_w_out, attn_sink,
           attn_q_norm_g, attn_k_norm_g, rwkv_x_mix, rwkv_w_r, rwkv_w_k, rwkv_w_v, rwkv_w_o,
           rwkv_decay_w0, rwkv_decay_w1, rwkv_decay_w2, rwkv_iclr_a0, rwkv_iclr_a1, rwkv_iclr_a2,
           rwkv_gate_g1, rwkv_gate_g2, rwkv_k_k, rwkv_k_a, rwkv_r_k, rwkv_ln_g, rwkv_ln_b,
           router_w, router_bias, moe_w1, moe_w3, moe_w2, final_norm_g):
    B, S, D = x.shape
    L = ctx.shape[1]
    T = L + S
    depth = ada_w.shape[0]
    assert D == D_MODEL and L == TM and S % TM == 0 and B == 2 and depth == 2
    ones = _seg_ones()
    bf = lambda a: a.astype(BF16)

    cs = jnp.zeros((8, D), F32).at[:B].set(c).at[B].set(c_ctx)
    ada = _ada(cs, ada_w, ada_b).reshape(depth, 8, 6, D)
    mods = [jnp.stack([jnp.broadcast_to(ada[i, B], (B, 6, D)), ada[i, :B]], axis=1) for i in range(depth)]

    xa = jnp.concatenate([ctx, x], axis=1)
    rw = jnp.zeros((D, LANES), F32).at[:, :N_EXPERTS].set(router_w)
    rwh, rwl = _split(rw)

    w_in = attn_w_in[0]
    roped = np.concatenate([np.arange(0, 640), np.arange(768, 1408)])
    w_rot = w_in[:, roped ^ 16]
    cos, sin = _rope_tables(S, L)
    lane = np.arange(LANES) % HEAD_DIM
    gains = lambda g: jnp.stack([g[lane], g[lane ^ 16]], axis=0)
    qa, ka, va, qb, kb, vb = _inproj(xa, mods[0], norm_mix_g[0].reshape(1, D), bf(w_in), bf(w_rot), cos, sin,
                                     gains(attn_q_norm_g[0]), gains(attn_k_norm_g[0]), ones)
    grouped = lambda q: q.reshape(B, A_KV_HEADS, GROUP, T, HEAD_DIM)
    qa, qb = grouped(qa), grouped(qb)
    sink = attn_sink[0].astype(F32)
    nosink = jnp.full((B_Q_HEADS,), NEG, F32)
    oa_l = _window_attn(sink, qa, ka, va, L, S)
    oa_c = _flash(sink, qa, ka, va, q_rows=L, q_off=0, k_rows=L, tq=L, tk=L)
    ob_l = _flash(nosink, qb, kb, vb, q_rows=S, q_off=L, k_rows=T, tq=256, tk=_key_tile(T))
    ob_c = _flash(nosink, qb, kb, vb, q_rows=L, q_off=0, k_rows=L, tq=L, tk=L)
    oa = jnp.concatenate([oa_c, oa_l], axis=1)
    ob = jnp.concatenate([ob_c, ob_l], axis=1)
    w_out = bf(attn_w_out[0])
    na = A_Q_HEADS * HEAD_DIM
    xa, h, lg = _attn_out(oa, ob, w_out[:na], w_out[na:], xa, mods[0], norm_ffn_g[0].reshape(1, D), rwh, rwl)
    f = _moe(h.reshape(B * T, D), lg.reshape(B * T, LANES)[:, :N_EXPERTS], router_bias,
             bf(moe_w1[0]), bf(moe_w3[0]), bf(moe_w2[0]))
    xa = _residual(xa, f.reshape(B, T, D), mods[0])

    cat2 = lambda a: jnp.concatenate([a[0], a[1]], axis=1)
    vec = jnp.stack([rwkv_k_k[0], rwkv_k_a[0], rwkv_r_k[0].reshape(D), rwkv_decay_w0[0, 0], rwkv_decay_w0[0, 1],
                     rwkv_iclr_a0[0, 0], rwkv_iclr_a0[0, 1], jnp.zeros((D,), F32)], axis=0)
    outs = _rwkv_proj(xa, mods[1], norm_mix_g[1].reshape(1, D), jnp.pad(rwkv_x_mix[0], ((0, 2), (0, 0))),
                      bf(rwkv_w_r[0]), bf(rwkv_w_k[0]), bf(rwkv_w_v[0]),
                      bf(cat2(rwkv_decay_w1[0])), bf(rwkv_decay_w2[0]),
                      bf(cat2(rwkv_iclr_a1[0])), bf(rwkv_iclr_a2[0]),
                      bf(rwkv_gate_g1[0]), bf(rwkv_gate_g2[0]), vec, ones)
    r, v, kk, bv, gate, w0, w1, kd0, kd1, bd0, bd1 = outs
    vi = np.arange(HEAD_DIM)
    eye = jnp.asarray(vi[:, None] == (np.arange(D) % HEAD_DIM)[None, :], dtype=F32)
    yf, yb = _rwkv_scan(r, v, kk, w0, w1, kd0, kd1, bd0, bd1, eye, ones, L)
    ln = jnp.stack([rwkv_ln_g[0], rwkv_ln_b[0]] + [jnp.zeros((D,), F32)] * 6, axis=0)
    xa, h, lg = _rwkv_out(yf, yb, bv, gate, ln, bf(rwkv_w_o[0]), ones, xa, mods[1],
                          norm_ffn_g[1].reshape(1, D), rwh, rwl)
    f = _moe(h[:, L:].reshape(B * S, D), lg[:, L:].reshape(B * S, LANES)[:, :N_EXPERTS], router_bias,
             bf(moe_w1[1]), bf(moe_w3[1]), bf(moe_w2[1]))
    return _final(xa, f.reshape(B, S, D), mods[1], final_norm_g.reshape(1, D), L)


def _key_tile(T):
    for tk in (1280, 1024, 768, 512, 256):
        if T % tk == 0:
            return tk
    raise ValueError(T)
```

```python
import functools

import numpy as np
import jax
import jax.numpy as jnp
from jax import lax
from jax.experimental import pallas as pl
from jax.experimental.pallas import tpu as pltpu

F32 = jnp.float32
BF16 = jnp.bfloat16

D_MODEL = 1024
HEAD_DIM = 64
GRID_W = 64
ROPE_THETA = 10000.0
RMS_EPS = 1e-6
GN_EPS = 64e-5
A_Q_HEADS = 8
A_KV_HEADS = 2
B_Q_HEADS = 8
B_KV_HEADS = 2
GROUP = 4
WINDOW = 128
N_EXPERTS = 16
N_GROUPS = 4
EXPERTS_PER_GROUP = 4
TOP_K = 2
LANES = 128
TM = 256
MOE_ROWS = 512
SCAN_CHUNK = 64
VMEM_LIMIT = 56 * 1024 * 1024
NEG = -1e30


def _cparams(sem):
    return pltpu.CompilerParams(dimension_semantics=sem, vmem_limit_bytes=VMEM_LIMIT)


def _dot(a, b):
    return jnp.dot(a, b, preferred_element_type=F32)


def _dot_nt(a, b):
    return lax.dot_general(a, b, (((1,), (1,)), ((), ())), preferred_element_type=F32)


def _split(x):
    hi = x.astype(BF16)
    lo = (x - hi.astype(F32)).astype(BF16)
    return hi, lo


def _dot3(x, w):
    xh, xl = _split(x)
    wh, wl = _split(w)
    return _dot(xh, wh) + _dot(xh, wl) + _dot(xl, wh)


def _segsum(v, ones):
    hi, lo = _split(v)
    return _dot(hi, ones) + _dot(lo, ones)


def _segsum_wide(v, ones):
    n = v.shape[1] // LANES
    return jnp.concatenate([_segsum(v[:, j * LANES:(j + 1) * LANES], ones) for j in range(n)], axis=1)


def _norm_mod(x, g, shift, scale):
    ms = jnp.mean(x * x, axis=-1, keepdims=True)
    return (x * lax.rsqrt(ms + RMS_EPS) * g) * (1.0 + scale) + shift


def _sigmoid(x):
    return 1.0 / (1.0 + jnp.exp(-x))


def _seg_ones():
    i = np.arange(LANES)
    return jnp.asarray((i[:, None] // HEAD_DIM) == (i[None, :] // HEAD_DIM), dtype=BF16)


def _ada_kernel(c_ref, w_ref, b_ref, o_ref):
    c = c_ref[...]
    s = c * _sigmoid(c)
    o_ref[0] = _dot3(s, w_ref[0]) + b_ref[0]


def _ada(cs, ada_w, ada_b):
    depth, d, n = ada_w.shape
    tn = 1536
    return pl.pallas_call(
        _ada_kernel,
        out_shape=jax.ShapeDtypeStruct((depth, 8, n), F32),
        grid=(depth, n // tn),
        in_specs=[
            pl.BlockSpec((8, d), lambda l, j: (0, 0)),
            pl.BlockSpec((1, d, tn), lambda l, j: (l, 0, j)),
            pl.BlockSpec((1, 1, tn), lambda l, j: (l, 0, j)),
        ],
        out_specs=pl.BlockSpec((1, 8, tn), lambda l, j: (l, 0, j)),
        compiler_params=_cparams(("arbitrary", "arbitrary")),
        name="ada",
    )(cs, ada_w, ada_b.reshape(depth, 1, n))


def _inproj_kernel(x_ref, mod_ref, g_ref, w_ref, wrot_ref, cos_ref, sin_ref, gq_ref, gk_ref, ones_ref,
                   qa_ref, ka_ref, va_ref, qb_ref, kb_ref, vb_ref):
    mod = mod_ref[0, 0]
    h = _norm_mod(x_ref[0], g_ref[...], mod[0:1], mod[1:2]).astype(BF16)
    y = _dot(h, w_ref[...])
    yr = _dot(h, wrot_ref[...])
    cos = cos_ref[...]
    sin = sin_ref[...]
    ones = ones_ref[...]
    qscale = HEAD_DIM ** -0.5

    def put(ref, tile, val):
        ref[0, 2 * tile] = val[:, :HEAD_DIM].astype(ref.dtype)
        ref[0, 2 * tile + 1] = val[:, HEAD_DIM:].astype(ref.dtype)

    def chunk(a, c):
        return a[:, c * LANES:(c + 1) * LANES]

    for c in range(4):
        put(qa_ref, c, (chunk(y, c) * cos + chunk(yr, c) * sin) * qscale)
    put(ka_ref, 0, chunk(y, 4) * cos + chunk(yr, 4) * sin)
    put(va_ref, 0, chunk(y, 5))

    def normed(c, cr, gain_ref):
        v = chunk(y, c)
        rs = lax.rsqrt(_segsum(v * v, ones) * (1.0 / HEAD_DIM) + RMS_EPS)
        return (v * rs * gain_ref[0:1]) * cos + (chunk(yr, cr) * rs * gain_ref[1:2]) * sin

    for c in range(4):
        put(qb_ref, c, normed(6 + c, 5 + c, gq_ref) * qscale)
    put(kb_ref, 0, normed(10, 9, gk_ref))
    put(vb_ref, 0, chunk(y, 11))


def _inproj(x, mods, g, w_in, w_rot, cos, sin, gq2, gk2, ones):
    B, T, D = x.shape
    nt = T // TM
    heads = lambda n: jax.ShapeDtypeStruct((B, n, T, HEAD_DIM), BF16)
    hspec = lambda n: pl.BlockSpec((1, n, TM, HEAD_DIM), lambda b, i: (b, 0, i, 0))
    full = lambda a: pl.BlockSpec(a.shape, lambda b, i: (0,) * a.ndim)
    return pl.pallas_call(
        _inproj_kernel,
        out_shape=(heads(8), heads(2), heads(2), heads(8), heads(2), heads(2)),
        grid=(B, nt),
        in_specs=[
            pl.BlockSpec((1, TM, D), lambda b, i: (b, i, 0)),
            pl.BlockSpec((1, 1, 6, D), lambda b, i: (b, jnp.minimum(i, 1), 0, 0)),
            full(g), full(w_in), full(w_rot),
            pl.BlockSpec((TM, LANES), lambda b, i: (i, 0)),
            pl.BlockSpec((TM, LANES), lambda b, i: (i, 0)),
            full(gq2), full(gk2), full(ones),
        ],
        out_specs=(hspec(8), hspec(2), hspec(2), hspec(8), hspec(2), hspec(2)),
        compiler_params=_cparams(("parallel", "parallel")),
        name="attn_inproj",
    )(x, mods, g, w_in, w_rot, cos, sin, gq2, gk2, ones)


def _flash_kernel(sink_ref, q_ref, k_ref, v_ref, o_ref, m_scr, l_scr, acc_scr, *, tq, nk):
    h = pl.program_id(1)
    kj = pl.program_id(3)

    @pl.when(kj == 0)
    def _():
        m_scr[...] = jnp.full(m_scr.shape, NEG, F32)
        l_scr[...] = jnp.zeros(l_scr.shape, F32)
        acc_scr[...] = jnp.zeros(acc_scr.shape, F32)

    q = q_ref[0, 0].reshape(GROUP * tq, HEAD_DIM)
    s = _dot_nt(q, k_ref[0, 0])
    m_prev = m_scr[...]
    m_new = jnp.maximum(m_prev, jnp.max(s, axis=-1, keepdims=True))
    alpha = jnp.exp(m_prev - m_new)
    p = jnp.exp(s - m_new)
    l_scr[...] = alpha * l_scr[...] + jnp.sum(p, axis=-1, keepdims=True)
    acc_scr[...] = alpha * acc_scr[...] + _dot(p.astype(BF16), v_ref[0, 0])
    m_scr[...] = m_new

    @pl.when(kj == nk - 1)
    def _():
        for g in range(GROUP):
            rows = slice(g * tq, (g + 1) * tq)
            l = l_scr[rows] + jnp.exp(sink_ref[h * GROUP + g] - m_scr[rows])
            o_ref[0, :, g * HEAD_DIM:(g + 1) * HEAD_DIM] = (acc_scr[rows] / l).astype(o_ref.dtype)


def _flash(sink, q, k, v, *, q_rows, q_off, k_rows, tq, tk):
    B, Hkv = k.shape[:2]
    nq, nk = q_rows // tq, k_rows // tk
    qo = q_off // tq
    return pl.pallas_call(
        functools.partial(_flash_kernel, tq=tq, nk=nk),
        out_shape=jax.ShapeDtypeStruct((B, q_rows, Hkv * GROUP * HEAD_DIM), BF16),
        grid=(B, Hkv, nq, nk),
        in_specs=[
            pl.BlockSpec(memory_space=pltpu.SMEM),
            pl.BlockSpec((1, 1, GROUP, tq, HEAD_DIM), lambda b, h, i, j: (b, h, 0, i + qo, 0)),
            pl.BlockSpec((1, 1, tk, HEAD_DIM), lambda b, h, i, j: (b, h, j, 0)),
            pl.BlockSpec((1, 1, tk, HEAD_DIM), lambda b, h, i, j: (b, h, j, 0)),
        ],
        out_specs=pl.BlockSpec((1, tq, GROUP * HEAD_DIM), lambda b, h, i, j: (b, i, h)),
        scratch_shapes=[
            pltpu.VMEM((GROUP * tq, 1), F32),
            pltpu.VMEM((GROUP * tq, 1), F32),
            pltpu.VMEM((GROUP * tq, HEAD_DIM), F32),
        ],
        compiler_params=_cparams(("parallel", "parallel", "parallel", "arbitrary")),
        name="flash_attn",
    )(sink, q, k, v)


def _window_kernel(sink_ref, q_ref, kc_ref, vc_ref, k0_ref, k1_ref, k2_ref, v0_ref, v1_ref, v2_ref, o_ref, *, nb):
    h = pl.program_id(1)
    i = pl.program_id(2)
    rows = GROUP * WINDOW
    q = q_ref[0, 0].reshape(rows, HEAD_DIM)
    r = lax.broadcasted_iota(jnp.int32, (rows, WINDOW), 0) & (WINDOW - 1)
    c = lax.broadcasted_iota(jnp.int32, (rows, WINDOW), 1)
    sc = _dot_nt(q, kc_ref[0, 0])
    s0 = jnp.where((c >= r) & (i > 0), _dot_nt(q, k0_ref[0, 0]), NEG)
    s1 = _dot_nt(q, k1_ref[0, 0])
    s2 = jnp.where((c <= r) & (i < nb - 1), _dot_nt(q, k2_ref[0, 0]), NEG)
    sink = jnp.concatenate(
        [jnp.full((WINDOW, 1), sink_ref[h * GROUP + g], F32) for g in range(GROUP)], axis=0)
    rowmax = lambda s: jnp.max(s, axis=-1, keepdims=True)
    m = jnp.maximum(jnp.maximum(rowmax(sc), rowmax(s0)), jnp.maximum(rowmax(s1), rowmax(s2)))
    m = jnp.maximum(m, sink)
    pc, p0, p1, p2 = (jnp.exp(s - m) for s in (sc, s0, s1, s2))
    rowsum = lambda p: jnp.sum(p, axis=-1, keepdims=True)
    l = rowsum(pc) + rowsum(p0) + rowsum(p1) + rowsum(p2) + jnp.exp(sink - m)
    acc = (_dot(pc.astype(BF16), vc_ref[0, 0]) + _dot(p0.astype(BF16), v0_ref[0, 0])
           + _dot(p1.astype(BF16), v1_ref[0, 0]) + _dot(p2.astype(BF16), v2_ref[0, 0]))
    out = acc / l
    for g in range(GROUP):
        o_ref[0, :, g * HEAD_DIM:(g + 1) * HEAD_DIM] = out[g * WINDOW:(g + 1) * WINDOW].astype(o_ref.dtype)


def _window_attn(sink, q, k, v, L, S):
    B, Hkv = k.shape[:2]
    nb = S // WINDOW
    pad = ((0, 0), (0, 0), (WINDOW, WINDOW), (0, 0))
    kp = jnp.pad(k[:, :, L:], pad)
    vp = jnp.pad(v[:, :, L:], pad)
    qo = L // WINDOW
    band = lambda j: pl.BlockSpec((1, 1, WINDOW, HEAD_DIM), lambda b, h, i: (b, h, i + j, 0))
    ctx = pl.BlockSpec((1, 1, L, HEAD_DIM), lambda b, h, i: (b, h, 0, 0))
    return pl.pallas_call(
        functools.partial(_window_kernel, nb=nb),
        out_shape=jax.ShapeDtypeStruct((B, S, Hkv * GROUP * HEAD_DIM), BF16),
        grid=(B, Hkv, nb),
        in_specs=[
            pl.BlockSpec(memory_space=pltpu.SMEM),
            pl.BlockSpec((1, 1, GROUP, WINDOW, HEAD_DIM), lambda b, h, i: (b, h, 0, i + qo, 0)),
            ctx, ctx, band(0), band(1), band(2), band(0), band(1), band(2),
        ],
        out_specs=pl.BlockSpec((1, WINDOW, GROUP * HEAD_DIM), lambda b, h, i: (b, i, h)),
        compiler_params=_cparams(("parallel", "parallel", "parallel")),
        name="window_attn",
    )(sink, q, k, v, kp, kp, kp, vp, vp, vp)


def _ffn_prep(x, y, mod, gffn, rwh_ref, rwl_ref, xo_ref, h_ref, lg_ref):
    xn = x + mod[2:3] * y
    h = _norm_mod(xn, gffn, mod[3:4], mod[4:5])
    xo_ref[0] = xn
    hh, hl = _split(h)
    h_ref[0] = hh
    rwh = rwh_ref[...]
    lg_ref[0] = _dot(hh, rwh) + _dot(hl, rwh) + _dot(hh, rwl_ref[...])


def _attn_out_kernel(oa_ref, ob_ref, wa_ref, wb_ref, x_ref, mod_ref, g_ref, rwh_ref, rwl_ref,
                     xo_ref, h_ref, lg_ref):
    y = _dot(oa_ref[0], wa_ref[...]) + _dot(ob_ref[0], wb_ref[...])
    _ffn_prep(x_ref[0], y, mod_ref[0, 0], g_ref[...], rwh_ref, rwl_ref, xo_ref, h_ref, lg_ref)


def _row_specs(D):
    row = lambda w: pl.BlockSpec((1, TM, w), lambda b, i: (b, i, 0))
    mod = pl.BlockSpec((1, 1, 6, D), lambda b, i: (b, jnp.minimum(i, 1), 0, 0))
    full = lambda a: pl.BlockSpec(a.shape, lambda b, i: (0,) * a.ndim)
    return row, mod, full


def _ffn_prep_outs(B, T, D):
    row, _, _ = _row_specs(D)
    shapes = (jax.ShapeDtypeStruct((B, T, D), F32), jax.ShapeDtypeStruct((B, T, D), BF16),
              jax.ShapeDtypeStruct((B, T, LANES), F32))
    return shapes, (row(D), row(D), row(LANES))


def _attn_out(oa, ob, wa, wb, x, mods, g, rwh, rwl):
    B, T, D = x.shape
    row, mod, full = _row_specs(D)
    shapes, specs = _ffn_prep_outs(B, T, D)
    return pl.pallas_call(
        _attn_out_kernel,
        out_shape=shapes,
        grid=(B, T // TM),
        in_specs=[row(oa.shape[-1]), row(ob.shape[-1]), full(wa), full(wb), row(D), mod, full(g),
                  full(rwh), full(rwl)],
        out_specs=specs,
        compiler_params=_cparams(("parallel", "parallel")),
        name="attn_out",
    )(oa, ob, wa, wb, x, mods, g, rwh, rwl)


def _gmm_kernel(be_ref, nu_ref, x_ref, w1_ref, w3_ref, w2_ref, o_ref):
    i = pl.program_id(0)

    @pl.when(i < nu_ref[0])
    def _():
        x = x_ref[...]
        a = _dot(x, w1_ref[0])
        b = _dot(x, w3_ref[0])
        mid = (a * _sigmoid(a)) * b
        o_ref[...] = _dot(mid.astype(BF16), w2_ref[0])

    @pl.when(i >= nu_ref[0])
    def _():
        o_ref[...] = jnp.zeros(o_ref.shape, o_ref.dtype)


def _gmm(block_expert, n_used, xs, w1, w3, w2):
    n_slots, D = xs.shape
    F = w1.shape[-1]
    nblk = n_slots // MOE_ROWS
    return pl.pallas_call(
        _gmm_kernel,
        out_shape=jax.ShapeDtypeStruct((n_slots, D), F32),
        grid_spec=pltpu.PrefetchScalarGridSpec(
            num_scalar_prefetch=2,
            grid=(nblk,),
            in_specs=[
                pl.BlockSpec((MOE_ROWS, D), lambda i, be, nu: (i, 0)),
                pl.BlockSpec((1, D, F), lambda i, be, nu: (be[i], 0, 0)),
                pl.BlockSpec((1, D, F), lambda i, be, nu: (be[i], 0, 0)),
                pl.BlockSpec((1, F, D), lambda i, be, nu: (be[i], 0, 0)),
            ],
            out_specs=pl.BlockSpec((MOE_ROWS, D), lambda i, be, nu: (i, 0)),
        ),
        compiler_params=_cparams(("arbitrary",)),
        name="moe_gmm",
    )(block_expert, n_used, xs, w1, w3, w2)


def _route(logits, router_bias):
    n = logits.shape[0]
    probs = jax.nn.softmax(logits, axis=-1)
    sel = (probs + router_bias.astype(F32)).reshape(n, N_GROUPS, EXPERTS_PER_GROUP)
    group_score = jnp.sum(lax.top_k(sel, TOP_K)[0], axis=-1)
    g_idx = jnp.argmax(group_score, axis=-1)
    in_group = jnp.take_along_axis(sel, g_idx[:, None, None], axis=1)[:, 0]
    local = lax.top_k(in_group, TOP_K)[1]
    expert_idx = g_idx[:, None] * EXPERTS_PER_GROUP + local
    w = jnp.take_along_axis(probs, expert_idx, axis=1)
    return expert_idx.astype(jnp.int32), w / jnp.sum(w, axis=-1, keepdims=True)


def _moe(h, logits, router_bias, w1, w3, w2):
    N, D = h.shape
    expert_idx, gate_w = _route(logits, router_bias)
    NK = N * TOP_K
    flat_e = expert_idx.reshape(NK)
    onehot = (flat_e[:, None] == jnp.arange(N_EXPERTS, dtype=jnp.int32)[None, :]).astype(jnp.int32)
    csum = jnp.cumsum(onehot, axis=0)
    counts = csum[-1]
    rank = jnp.take_along_axis(csum, flat_e[:, None], axis=1)[:, 0] - 1
    padded = (counts + MOE_ROWS - 1) // MOE_ROWS * MOE_ROWS
    pad_end = jnp.cumsum(padded)
    pad_start = pad_end - padded
    dest =pad_start[flat_e] + rank
    nblk = -(-NK // MOE_ROWS) + N_EXPERTS
    n_slots = nblk * MOE_ROWS
    n_used = (pad_end[-1] // MOE_ROWS).astype(jnp.int32)
    blk = jnp.arange(nblk, dtype=jnp.int32)
    be = jnp.minimum(jnp.searchsorted(pad_end, blk * MOE_ROWS, side="right"), N_EXPERTS - 1).astype(jnp.int32)
    be = jnp.where(blk < n_used, be, be[jnp.maximum(n_used - 1, 0)])
    flat_tok = jnp.arange(NK, dtype=jnp.int32) // TOP_K
    slot_tok = jnp.zeros((n_slots,), jnp.int32).at[dest].set(flat_tok, unique_indices=True)
    xs = h[slot_tok]
    ys = _gmm(be, n_used.reshape(1), xs, w1, w3, w2)
    pos = dest.reshape(N, TOP_K)
    return ys[pos[:, 0]] * gate_w[:, 0:1] + ys[pos[:, 1]] * gate_w[:, 1:2]


def _residual_kernel(x_ref, f_ref, mod_ref, o_ref):
    o_ref[0] = x_ref[0] + mod_ref[0, 0][5:6] * f_ref[0]


def _final_kernel(x_ref, f_ref, mod_ref, g_ref, o_ref):
    x = x_ref[0] + mod_ref[0, 0][5:6] * f_ref[0]
    ms = jnp.mean(x * x, axis=-1, keepdims=True)
    o_ref[0] = x * lax.rsqrt(ms + RMS_EPS) * g_ref[...]


def _residual(x, f, mods):
    B, T, D = x.shape
    row, mod, _ = _row_specs(D)
    return pl.pallas_call(
        _residual_kernel,
        out_shape=jax.ShapeDtypeStruct((B, T, D), F32),
        grid=(B, T // TM),
        in_specs=[row(D), row(D), mod],
        out_specs=row(D),
        compiler_params=_cparams(("parallel", "parallel")),
        name="residual",
    )(x, f, mods)


def _final(x, f, mods, g, L):
    B, S, D = f.shape
    off = L // TM
    mod = pl.BlockSpec((1, 1, 6, D), lambda b, i: (b, 1, 0, 0))
    return pl.pallas_call(
        _final_kernel,
        out_shape=jax.ShapeDtypeStruct((B, S, D), F32),
        grid=(B, S // TM),
        in_specs=[pl.BlockSpec((1, TM, D), lambda b, i: (b, i + off, 0)),
                  pl.BlockSpec((1, TM, D), lambda b, i: (b, i, 0)), mod,
                  pl.BlockSpec(g.shape, lambda b, i: (0, 0))],
        out_specs=pl.BlockSpec((1, TM, D), lambda b, i: (b, i, 0)),
        compiler_params=_cparams(("parallel", "parallel")),
        name="final_norm",
    )(x, f, mods, g)


def _rwkv_proj_kernel(x_ref, xp_ref, xn_ref, mod_ref, g_ref, xmix_ref, wr_ref, wk_ref, wv_ref,
                      dw1_ref, dw2_ref, da1_ref, da2_ref, g1_ref, g2_ref, vec_ref, ones_ref,
                      r_ref, v_ref, kk_ref, bv_ref, gate_ref, w0_ref, w1_ref, kd0_ref, kd1_ref, bd0_ref, bd1_ref,
                      *, nt):
    i = pl.program_id(1)
    mod = mod_ref[0, 0]
    g = g_ref[...]
    nm = lambda x: _norm_mod(x, g, mod[0:1], mod[1:2])
    h = nm(x_ref[0])
    hp = nm(xp_ref[0])[7:8] * jnp.where(i >= 2, 1.0, 0.0)
    hn = nm(xn_ref[0])[0:1] * jnp.where((i >= 1) & (i < nt - 1), 1.0, 0.0)
    ridx = lax.broadcasted_iota(jnp.int32, h.shape, 0)
    h_dn = jnp.where(ridx == 0, hp, pltpu.roll(h, 1, axis=0))
    h_up = jnp.where(ridx == TM - 1, hn, pltpu.roll(h, TM - 1, axis=0))
    xx = 0.5 * (h_dn + h_up) - h
    xmix = xmix_ref[...]
    mix = lambda j: (h + xx * xmix[j:j + 1]).astype(BF16)
    vec = vec_ref[...]
    ones = ones_ref[...]

    r = _dot(mix(0), wr_ref[...])
    k = _dot(mix(2), wk_ref[...])
    v = _dot(mix(3), wv_ref[...])
    gate_ref[0] = _dot(_sigmoid(_dot(mix(5), g1_ref[...])).astype(BF16), g2_ref[...])
    kk = k * vec[0:1]
    kk = kk * lax.rsqrt(jnp.maximum(_segsum_wide(kk * kk, ones), 1e-24))
    lw = jnp.tanh(_dot(mix(1), dw1_ref[...])).astype(BF16)
    la = _dot(mix(4), da1_ref[...]).astype(BF16)
    r_ref[0] = r
    v_ref[0] = v
    kk_ref[0] = kk
    bonus = jnp.zeros_like(r)
    lora = DECAY_LORA
    for d, (w_ref, kd_ref, bd_ref) in enumerate(((w0_ref, kd0_ref, bd0_ref), (w1_ref, kd1_ref, bd1_ref))):
        z = -(vec[3 + d:4 + d] + _dot(lw[:, d * lora:(d + 1) * lora], dw2_ref[d]))
        softplus = jnp.maximum(z, 0.0) + jnp.log(1.0 + jnp.exp(-jnp.abs(z)))
        w_ref[0] = jnp.exp(-jnp.exp(-softplus - 0.5))
        iclr = _sigmoid(vec[5 + d:6 + d] + _dot(la[:, d * lora:(d + 1) * lora], da2_ref[d]))
        kd = k * (1.0 + (iclr - 1.0) * vec[1:2])
        kd_ref[0] = kd
        bd_ref[0] = kk * iclr
        bonus = bonus + _segsum_wide(r * kd * vec[2:3], ones)
    bv_ref[0] = bonus * v


DECAY_LORA = 64


def _rwkv_proj(x, mods, g, xmix, wr, wk, wv, dw1, dw2, da1, da2, g1, g2, vec, ones):
    B, T, D = x.shape
    nt = T // TM
    row, mod, full = _row_specs(D)
    r8 = TM // 8
    prev = pl.BlockSpec((1, 8, D), lambda b, i: (b, jnp.maximum(i * r8 - 1, 0), 0))
    nxt = pl.BlockSpec((1, 8, D), lambda b, i: (b, jnp.minimum((i + 1) * r8, T // 8 - 1), 0))
    out = jax.ShapeDtypeStruct((B, T, D), F32)
    return pl.pallas_call(
        functools.partial(_rwkv_proj_kernel, nt=nt),
        out_shape=(out,) * 11,
        grid=(B, nt),
        in_specs=[row(D), prev, nxt, mod, full(g), full(xmix), full(wr), full(wk), full(wv),
                  full(dw1), full(dw2), full(da1), full(da2), full(g1), full(g2), full(vec), full(ones)],
        out_specs=(row(D),) * 11,
        compiler_params=_cparams(("parallel", "parallel")),
        name="rwkv_proj",
    )(x, x, x, mods, g, xmix, wr, wk, wv, dw1, dw2, da1, da2, g1, g2, vec, ones)


def _lane_tiles_to_rows(x):
    n = x.shape[1] // LANES
    return jnp.concatenate([x[:, j * LANES:(j + 1) * LANES] for j in range(n)], axis=0)


def _rows_to_lane_tiles(x, n):
    m = x.shape[0] // n
    return jnp.concatenate([x[j * m:(j + 1) * m] for j in range(n)], axis=1)


def _scan_kernel(rf, rb, vf, vb, kkf, kkb, wf, wb, kdf, kdb, bdf, bdb, eye_ref, ones_ref, yf, yb, st, *, tc, nb):
    n = pl.program_id(0)

    @pl.when(n == 0)
    def _():
        st[...] = jnp.zeros(st.shape, F32)

    ones = ones_ref[...]
    N = HEAD_DIM
    nl = st.shape[-1] // LANES
    dirs = ((rf, vf, kkf, wf, kdf, bdf, yf), (rb, vb, kkb, wb, kdb, bdb, yb))

    def seg(x):
        res = _dot(_lane_tiles_to_rows(x.astype(BF16)), ones)
        return _rows_to_lane_tiles(res, nl)

    def extract(ybc):
        return jnp.sum(ybc * eye_ref[...], axis=0, keepdims=True)

    def step(i, carry):
        ip = jnp.maximum(i - 1, 0)
        for d, (R, V, KK, W, KD, BD, Y) in enumerate(dirs):
            row = i if d == 0 else tc - 1 - i
            rowp = ip if d == 0 else tc - 1 - ip
            for b in range(nb):
                gi = d * nb + b
                S = st[gi]
                cur = pl.ds(row, 1)
                a = -KK[b, cur, :]
                x = jnp.concatenate([S * a, S * R[b, pl.ds(rowp, 1), :], eye_ref[...] * V[b, cur, :]], axis=0)
                res = seg(x)
                sa, ybc, vcol = res[0:N], res[N:2 * N], res[2 * N:3 * N]
                st[gi] = S * W[b, cur, :] + sa * BD[b, cur, :] + vcol * KD[b, cur, :]
                Y[b, pl.ds(rowp, 1), :] = extract(ybc)
        return carry

    lax.fori_loop(0, tc, step, 0)
    for d, (R, V, KK, W, KD, BD, Y) in enumerate(dirs):
        last = tc - 1 if d == 0 else 0
        for b in range(nb):
            Y[b, last:last + 1, :] = extract(seg(st[d * nb + b] * R[b, last:last + 1, :]))


def _rwkv_scan(r, v, kk, w0, w1, kd0, kd1, bd0, bd1, eye, ones, L):
    B, T, D = r.shape
    tc = SCAN_CHUNK
    nc, nchunks = L // tc, T // tc
    fwd = pl.BlockSpec((B, tc, D), lambda n: (0, n, 0))
    rev_idx = lambda n: jnp.where(n < nc, nc - 1 - n, nchunks - 1 - (n - nc))
    rev = pl.BlockSpec((B, tc, D), lambda n: (0, rev_idx(n), 0))
    full = lambda a: pl.BlockSpec(a.shape, lambda n: (0,) * a.ndim)
    out = jax.ShapeDtypeStruct((B, T, D), F32)
    return pl.pallas_call(
        functools.partial(_scan_kernel, tc=tc, nb=B),
        out_shape=(out, out),
        grid=(nchunks,),
        in_specs=[fwd, rev, fwd, rev, fwd, rev, fwd, rev, fwd, rev, fwd, rev, full(eye), full(ones)],
        out_specs=(fwd, rev),
        scratch_shapes=[pltpu.VMEM((2 * B, HEAD_DIM, D), F32)],
        compiler_params=_cparams(("arbitrary",)),
        name="rwkv_scan",
    )(r, r, v, v, kk, kk, w0, w1, kd0, kd1, bd0, bd1, eye, ones)


def _rwkv_out_kernel(yf_ref, yb_ref, bv_ref, gate_ref, ln_ref, wo_ref, ones_ref, x_ref, mod_ref, g_ref,
                     rwh_ref, rwl_ref, xo_ref, h_ref, lg_ref):
    ones = ones_ref[...]
    y = yf_ref[0] + yb_ref[0]
    inv = 1.0 / HEAD_DIM
    dlt = y - _segsum_wide(y, ones) * inv
    yn = dlt * lax.rsqrt(_segsum_wide(dlt * dlt, ones) * inv + GN_EPS)
    ln = ln_ref[...]
    o = (yn * ln[0:1] + ln[1:2] + bv_ref[0]) * gate_ref[0]
    yl = _dot(o.astype(BF16), wo_ref[...])
    _ffn_prep(x_ref[0], yl, mod_ref[0, 0], g_ref[...], rwh_ref, rwl_ref, xo_ref, h_ref, lg_ref)


def _rwkv_out(yf, yb, bv, gate, ln, wo, ones, x, mods, g, rwh, rwl):
    B, T, D = x.shape
    row, mod, full = _row_specs(D)
    shapes, specs = _ffn_prep_outs(B, T, D)
    return pl.pallas_call(
        _rwkv_out_kernel,
        out_shape=shapes,
        grid=(B, T // TM),
        in_specs=[row(D), row(D), row(D), row(D), full(ln), full(wo), full(ones), row(D), mod, full(g),
                  full(rwh), full(rwl)],
        out_specs=specs,
        compiler_params=_cparams(("parallel", "parallel")),
        name="rwkv_out",
    )(yf, yb, bv, gate, ln, wo, ones, x, mods, g, rwh, rwl)


def _rope_tables(S, L):
    rows = S // GRID_W
    row = jnp.repeat(jnp.arange(rows, dtype=F32), GRID_W)
    col = (jnp.arange(rows * GRID_W) % GRID_W).astype(F32)
    n_freq = HEAD_DIM // 4
    inv = ROPE_THETA ** (-jnp.arange(n_freq, dtype=F32) / n_freq)
    lane = np.arange(LANES) % HEAD_DIM
    axis, half, freq = lane // 32, (lane % 32) // 16, lane % 16
    pos = jnp.where(jnp.asarray(axis == 0)[None, :], row[:, None], col[:, None])
    ang = pos * inv[freq][None, :]
    sgn = jnp.asarray(np.where(half == 0, -1.0, 1.0), dtype=F32)
    cos = jnp.concatenate([jnp.ones((L, LANES), F32), jnp.cos(ang)], axis=0)
    sin = jnp.concatenate([jnp.zeros((L, LANES), F32), jnp.sin(ang) * sgn[None, :]], axis=0)
    return cos, sin


def kernel(x, c, ctx, c_ctx, ada_w, ada_b, norm_mix_g, norm_ffn_g, attn_w_in, attn_w_out, attn_sink,
           attn_q_norm_g, attn_k_norm_g, rwkv_x_mix, rwkv_w_r, rwkv_w_k, rwkv_w_v, rwkv_w_o,
           rwkv_decay_w0, rwkv_decay_w1, rwkv_decay_w2, rwkv_iclr_a0, rwkv_iclr_a1, rwkv_iclr_a2,
           rwkv_gate_g1, rwkv_gate_g2, rwkv_k_k, rwkv_k_a, rwkv_r_k, rwkv_ln_g, rwkv_ln_b,
           router_w, router_bias, moe_w1, moe_w3, moe_w2, final_norm_g):
    B, S, D = x.shape
    L = ctx.shape[1]
    T = L + S
    depth = ada_w.shape[0]
    assert D == D_MODEL and L == TM and S % TM == 0 and B == 2 and depth == 2
    ones = _seg_ones()
    bf = lambda a: a.astype(BF16)

    cs = jnp.zeros((8, D), F32).at[:B].set(c).at[B].set(c_ctx)
    ada = _ada(cs, ada_w, ada_b).reshape(depth, 8, 6, D)
    mods = [jnp.stack([jnp.broadcast_to(ada[i, B], (B, 6, D)), ada[i, :B]], axis=1) for i in range(depth)]

    xa = jnp.concatenate([ctx, x], axis=1)
    rw = jnp.zeros((D, LANES), F32).at[:, :N_EXPERTS].set(router_w)
    rwh, rwl = _split(rw)

    w_in = attn_w_in[0]
    roped = np.concatenate([np.arange(0, 640), np.arange(768, 1408)])
    w_rot = w_in[:, roped ^ 16]
    cos, sin = _rope_tables(S, L)
    lane = np.arange(LANES) % HEAD_DIM
    gains = lambda g: jnp.stack([g[lane], g[lane ^ 16]], axis=0)
    qa, ka, va, qb, kb, vb = _inproj(xa, mods[0], norm_mix_g[0].reshape(1, D), bf(w_in), bf(w_rot), cos, sin,
                                     gains(attn_q_norm_g[0]), gains(attn_k_norm_g[0]), ones)
    grouped = lambda q: q.reshape(B, A_KV_HEADS, GROUP, T, HEAD_DIM)
    qa, qb = grouped(qa), grouped(qb)
    sink = attn_sink[0].astype(F32)
    nosink = jnp.full((B_Q_HEADS,), NEG, F32)
    oa_l = _window_attn(sink, qa, ka, va, L, S)
    oa_c = _flash(sink, qa, ka, va, q_rows=L, q_off=0, k_rows=L, tq=L, tk=L)
    ob_l = _flash(nosink, qb, kb, vb, q_rows=S, q_off=L, k_rows=T, tq=256, tk=_key_tile(T))
    ob_c = _flash(nosink, qb, kb, vb, q_rows=L, q_off=0, k_rows=L, tq=L, tk=L)
    oa = jnp.concatenate([oa_c, oa_l], axis=1)
    ob = jnp.concatenate([ob_c, ob_l], axis=1)
    w_out = bf(attn_w_out[0])
    na = A_Q_HEADS * HEAD_DIM
    xa, h, lg = _attn_out(oa, ob, w_out[:na], w_out[na:], xa, mods[0], norm_ffn_g[0].reshape(1, D), rwh, rwl)
    f = _moe(h.reshape(B * T, D), lg.reshape(B * T, LANES)[:, :N_EXPERTS], router_bias,
             bf(moe_w1[0]), bf(moe_w3[0]), bf(moe_w2[0]))
    xa = _residual(xa, f.reshape(B, T, D), mods[0])

    cat2 = lambda a: jnp.concatenate([a[0], a[1]], axis=1)
    vec = jnp.stack([rwkv_k_k[0], rwkv_k_a[0], rwkv_r_k[0].reshape(D), rwkv_decay_w0[0, 0], rwkv_decay_w0[0, 1],
                     rwkv_iclr_a0[0, 0], rwkv_iclr_a0[0, 1], jnp.zeros((D,), F32)], axis=0)
    outs = _rwkv_proj(xa, mods[1], norm_mix_g[1].reshape(1, D), jnp.pad(rwkv_x_mix[0], ((0, 2), (0, 0))),
                      bf(rwkv_w_r[0]), bf(rwkv_w_k[0]), bf(rwkv_w_v[0]),
                      bf(cat2(rwkv_decay_w1[0])), bf(rwkv_decay_w2[0]),
                      bf(cat2(rwkv_iclr_a1[0])), bf(rwkv_iclr_a2[0]),
                      bf(rwkv_gate_g1[0]), bf(rwkv_gate_g2[0]), vec, ones)
    r, v, kk, bv, gate, w0, w1, kd0, kd1, bd0, bd1 = outs
    vi = np.arange(HEAD_DIM)
    eye = jnp.asarray(vi[:, None] == (np.arange(D) % HEAD_DIM)[None, :], dtype=F32)
    yf, yb = _rwkv_scan(r, v, kk, w0, w1, kd0, kd1, bd0, bd1, eye, ones, L)
    ln = jnp.stack([rwkv_ln_g[0], rwkv_ln_b[0]] + [jnp.zeros((D,), F32)] * 6, axis=0)
    xa, h, lg = _rwkv_out(yf, yb, bv, gate, ln, bf(rwkv_w_o[0]), ones, xa, mods[1],
                          norm_ffn_g[1].reshape(1, D), rwh, rwl)
    f = _moe(h[:, L:].reshape(B * S, D), lg[:, L:].reshape(B * S, LANES)[:, :N_EXPERTS], router_bias,
             bf(moe_w1[1]), bf(moe_w3[1]), bf(moe_w2[1]))
    return _final(xa, f.reshape(B, S, D), mods[1], final_norm_g.reshape(1, D), L)


def _key_tile(T):
    for tk in (1280, 1024, 768, 512, 256):
        if T % tk == 0:
            return tk
    raise ValueError(T)
```

```python
import functools

import numpy as np
import jax
import jax.numpy as jnp
from jax import lax
from jax.experimental import pallas as pl
from jax.experimental.pallas import tpu as pltpu

F32 = jnp.float32
BF16 = jnp.bfloat16

D_MODEL = 1024
HEAD_DIM = 64
GRID_W = 64
ROPE_THETA = 10000.0
RMS_EPS = 1e-6
GN_EPS = 64e-5
A_Q_HEADS = 8
A_KV_HEADS = 2
B_Q_HEADS = 8
B_KV_HEADS = 2
GROUP = 4
WINDOW = 128
N_EXPERTS = 16
N_GROUPS = 4
EXPERTS_PER_GROUP = 4
TOP_K = 2
LANES = 128
TM = 256
MOE_ROWS = 512
SCAN_CHUNK = 64
VMEM_LIMIT = 56 * 1024 * 1024
NEG = -1e30


def _cparams(sem):
    return pltpu.CompilerParams(dimension_semantics=sem, vmem_limit_bytes=VMEM_LIMIT)


def _dot(a, b):
    return jnp.dot(a, b, preferred_element_type=F32)


def _dot_nt(a, b):
    return lax.dot_general(a, b, (((1,), (1,)), ((), ())), preferred_element_type=F32)


def _split(x):
    hi = x.astype(BF16)
    lo = (x - hi.astype(F32)).astype(BF16)
    return hi, lo


def _dot3(x, w):
    xh, xl = _split(x)
    wh, wl = _split(w)
    return _dot(xh, wh) + _dot(xh, wl) + _dot(xl, wh)


def _segsum(v, ones):
    hi, lo = _split(v)
    return _dot(hi, ones) + _dot(lo, ones)


def _segsum_wide(v, ones):
    n = v.shape[1] // LANES
    return jnp.concatenate([_segsum(v[:, j * LANES:(j + 1) * LANES], ones) for j in range(n)], axis=1)


def _norm_mod(x, g, shift, scale):
    ms = jnp.mean(x * x, axis=-1, keepdims=True)
    return (x * lax.rsqrt(ms + RMS_EPS) * g) * (1.0 + scale) + shift


def _sigmoid(x):
    return 1.0 / (1.0 + jnp.exp(-x))


def _seg_ones():
    i = np.arange(LANES)
    return jnp.asarray((i[:, None] // HEAD_DIM) == (i[None, :] // HEAD_DIM), dtype=BF16)


def _ada_kernel(c_ref, w_ref, b_ref, o_ref):
    c = c_ref[...]
    s = c * _sigmoid(c)
    o_ref[0] = _dot3(s, w_ref[0]) + b_ref[0]


def _ada(cs, ada_w, ada_b):
    depth, d, n = ada_w.shape
    tn = 1536
    return pl.pallas_call(
        _ada_kernel,
        out_shape=jax.ShapeDtypeStruct((depth, 8, n), F32),
        grid=(depth, n // tn),
        in_specs=[
            pl.BlockSpec((8, d), lambda l, j: (0, 0)),
            pl.BlockSpec((1, d, tn), lambda l, j: (l, 0, j)),
            pl.BlockSpec((1, 1, tn), lambda l, j: (l, 0, j)),
        ],
        out_specs=pl.BlockSpec((1, 8, tn), lambda l, j: (l, 0, j)),
        compiler_params=_cparams(("arbitrary", "arbitrary")),
        name="ada",
    )(cs, ada_w, ada_b.reshape(depth, 1, n))


def _inproj_kernel(x_ref, mod_ref, g_ref, w_ref, wrot_ref, cos_ref, sin_ref, gq_ref, gk_ref, ones_ref,
                   qa_ref, ka_ref, va_ref, qb_ref, kb_ref, vb_ref):
    mod = mod_ref[0, 0]
    h = _norm_mod(x_ref[0], g_ref[...], mod[0:1], mod[1:2]).astype(BF16)
    y = _dot(h, w_ref[...])
    yr = _dot(h, wrot_ref[...])
    cos = cos_ref[...]
    sin = sin_ref[...]
    ones = ones_ref[...]
    qscale = HEAD_DIM ** -0.5

    def put(ref, tile, val):
        ref[0, 2 * tile] = val[:, :HEAD_DIM].astype(ref.dtype)
        ref[0, 2 * tile + 1] = val[:, HEAD_DIM:].astype(ref.dtype)

    def chunk(a, c):
        return a[:, c * LANES:(c + 1) * LANES]

    for c in range(4):
        put(qa_ref, c, (chunk(y, c) * cos + chunk(yr, c) * sin) * qscale)
    put(ka_ref, 0, chunk(y, 4) * cos + chunk(yr, 4) * sin)
    put(va_ref, 0, chunk(y, 5))

    def normed(c, cr, gain_ref):
        v = chunk(y, c)
        rs = lax.rsqrt(_segsum(v * v, ones) * (1.0 / HEAD_DIM) + RMS_EPS)
        return (v * rs * gain_ref[0:1]) * cos + (chunk(yr, cr) * rs * gain_ref[1:2]) * sin

    for c in range(4):
        put(qb_ref, c, normed(6 + c, 5 + c, gq_ref) * qscale)
    put(kb_ref, 0, normed(10, 9, gk_ref))
    put(vb_ref, 0, chunk(y, 11))


def _inproj(x, mods, g, w_in, w_rot, cos, sin, gq2, gk2, ones):
    B, T, D = x.shape
    nt = T // TM
    heads = lambda n: jax.ShapeDtypeStruct((B, n, T, HEAD_DIM), BF16)
    hspec = lambda n: pl.BlockSpec((1, n, TM, HEAD_DIM), lambda b, i: (b, 0, i, 0))
    full = lambda a: pl.BlockSpec(a.shape, lambda b, i: (0,) * a.ndim)
    return pl.pallas_call(
        _inproj_kernel,
        out_shape=(heads(8), heads(2), heads(2), heads(8), heads(2), heads(2)),
        grid=(B, nt),
        in_specs=[
            pl.BlockSpec((1, TM, D), lambda b, i: (b, i, 0)),
            pl.BlockSpec((1, 1, 6, D), lambda b, i: (b, jnp.minimum(i, 1), 0, 0)),
            full(g), full(w_in), full(w_rot),
            pl.BlockSpec((TM, LANES), lambda b, i: (i, 0)),
            pl.BlockSpec((TM, LANES), lambda b, i: (i, 0)),
            full(gq2), full(gk2), full(ones),
        ],
        out_specs=(hspec(8), hspec(2), hspec(2), hspec(8), hspec(2), hspec(2)),
        compiler_params=_cparams(("parallel", "parallel")),
        name="attn_inproj",
    )(x, mods, g, w_in, w_rot, cos, sin, gq2, gk2, ones)


def _flash_kernel(sink_ref, q_ref, k_ref, v_ref, o_ref, m_scr, l_scr, acc_scr, *, tq, nk):
    h = pl.program_id(1)
    kj = pl.program_id(3)

    @pl.when(kj == 0)
    def _():
        m_scr[...] = jnp.full(m_scr.shape, NEG, F32)
        l_scr[...] = jnp.zeros(l_scr.shape, F32)
        acc_scr[...] = jnp.zeros(acc_scr.shape, F32)

    q = q_ref[0, 0].reshape(GROUP * tq, HEAD_DIM)
    s = _dot_nt(q, k_ref[0, 0])
    m_prev = m_scr[...]
    m_new = jnp.maximum(m_prev, jnp.max(s, axis=-1, keepdims=True))
    alpha = jnp.exp(m_prev - m_new)
    p = jnp.exp(s - m_new)
    l_scr[...] = alpha * l_scr[...] + jnp.sum(p, axis=-1, keepdims=True)
    acc_scr[...] = alpha * acc_scr[...] + _dot(p.astype(BF16), v_ref[0, 0])
    m_scr[...] = m_new

    @pl.when(kj == nk - 1)
    def _():
        for g in range(GROUP):
            rows = slice(g * tq, (g + 1) * tq)
            l = l_scr[rows] + jnp.exp(sink_ref[h * GROUP + g] - m_scr[rows])
            o_ref[0, :, g * HEAD_DIM:(g + 1) * HEAD_DIM] = (acc_scr[rows] / l).astype(o_ref.dtype)


def _flash(sink, q, k, v, *, q_rows, q_off, k_rows, tq, tk):
    B, Hkv = k.shape[:2]
    nq, nk = q_rows // tq, k_rows // tk
    qo = q_off // tq
    return pl.pallas_call(
        functools.partial(_flash_kernel, tq=tq, nk=nk),
        out_shape=jax.ShapeDtypeStruct((B, q_rows, Hkv * GROUP * HEAD_DIM), BF16),
        grid=(B, Hkv, nq, nk),
        in_specs=[
            pl.BlockSpec(memory_space=pltpu.SMEM),
            pl.BlockSpec((1, 1, GROUP, tq, HEAD_DIM), lambda b, h, i, j: (b, h, 0, i + qo, 0)),
            pl.BlockSpec((1, 1, tk, HEAD_DIM), lambda b, h, i, j: (b, h, j, 0)),
            pl.BlockSpec((1, 1, tk, HEAD_DIM), lambda b, h, i, j: (b, h, j, 0)),
        ],
        out_specs=pl.BlockSpec((1, tq, GROUP * HEAD_DIM), lambda b, h, i, j: (b, i, h)),
        scratch_shapes=[
            pltpu.VMEM((GROUP * tq, 1), F32),
            pltpu.VMEM((GROUP * tq, 1), F32),
            pltpu.VMEM((GROUP * tq, HEAD_DIM), F32),
        ],
        compiler_params=_cparams(("parallel", "parallel", "parallel", "arbitrary")),
        name="flash_attn",
    )(sink, q, k, v)


def _window_kernel(sink_ref, q_ref, kc_ref, vc_ref, k0_ref, k1_ref, k2_ref, v0_ref, v1_ref, v2_ref, o_ref, *, nb):
    h = pl.program_id(1)
    i = pl.program_id(2)
    rows = GROUP * WINDOW
    q = q_ref[0, 0].reshape(rows, HEAD_DIM)
    r = lax.broadcasted_iota(jnp.int32, (rows, WINDOW), 0) & (WINDOW - 1)
    c = lax.broadcasted_iota(jnp.int32, (rows, WINDOW), 1)
    sc = _dot_nt(q, kc_ref[0, 0])
    s0 = jnp.where((c >= r) & (i > 0), _dot_nt(q, k0_ref[0, 0]), NEG)
    s1 = _dot_nt(q, k1_ref[0, 0])
    s2 = jnp.where((c <= r) & (i < nb - 1), _dot_nt(q, k2_ref[0, 0]), NEG)
    sink = jnp.concatenate(
        [jnp.full((WINDOW, 1), sink_ref[h * GROUP + g], F32) for g in range(GROUP)], axis=0)
    rowmax = lambda s: jnp.max(s, axis=-1, keepdims=True)
    m = jnp.maximum(jnp.maximum(rowmax(sc), rowmax(s0)), jnp.maximum(rowmax(s1), rowmax(s2)))
    m = jnp.maximum(m, sink)
    pc, p0, p1, p2 = (jnp.exp(s - m) for s in (sc, s0, s1, s2))
    rowsum = lambda p: jnp.sum(p, axis=-1, keepdims=True)
    l = rowsum(pc) + rowsum(p0) + rowsum(p1) + rowsum(p2) + jnp.exp(sink - m)
    acc = (_dot(pc.astype(BF16), vc_ref[0, 0]) + _dot(p0.astype(BF16), v0_ref[0, 0])
           + _dot(p1.astype(BF16), v1_ref[0, 0]) + _dot(p2.astype(BF16), v2_ref[0, 0]))
    out = acc / l
    for g in range(GROUP):
        o_ref[0, :, g * HEAD_DIM:(g + 1) * HEAD_DIM] = out[g * WINDOW:(g + 1) * WINDOW].astype(o_ref.dtype)


def _window_attn(sink, q, k, v, L, S):
    B, Hkv = k.shape[:2]
    nb = S // WINDOW
    pad = ((0, 0), (0, 0), (WINDOW, WINDOW), (0, 0))
    kp = jnp.pad(k[:, :, L:], pad)
    vp = jnp.pad(v[:, :, L:], pad)
    qo = L // WINDOW
    band = lambda j: pl.BlockSpec((1, 1, WINDOW, HEAD_DIM), lambda b, h, i: (b, h, i + j, 0))
    ctx = pl.BlockSpec((1, 1, L, HEAD_DIM), lambda b, h, i: (b, h, 0, 0))
    return pl.pallas_call(
        functools.partial(_window_kernel, nb=nb),
        out_shape=jax.ShapeDtypeStruct((B, S, Hkv * GROUP * HEAD_DIM), BF16),
        grid=(B, Hkv, nb),
        in_specs=[
            pl.BlockSpec(memory_space=pltpu.SMEM),
            pl.BlockSpec((1, 1, GROUP, WINDOW, HEAD_DIM), lambda b, h, i: (b, h, 0, i + qo, 0)),
            ctx, ctx, band(0), band(1), band(2), band(0), band(1), band(2),
        ],
        out_specs=pl.BlockSpec((1, WINDOW, GROUP * HEAD_DIM), lambda b, h, i: (b, i, h)),
        compiler_params=_cparams(("parallel", "parallel", "parallel")),
        name="window_attn",
    )(sink, q, k, v, kp, kp, kp, vp, vp, vp)


def _ffn_prep(x, y, mod, gffn, rwh_ref, rwl_ref, xo_ref, h_ref, lg_ref):
    xn = x + mod[2:3] * y
    h = _norm_mod(xn, gffn, mod[3:4], mod[4:5])
    xo_ref[0] = xn
    hh, hl = _split(h)
    h_ref[0] = hh
    rwh = rwh_ref[...]
    lg_ref[0] = _dot(hh, rwh) + _dot(hl, rwh) + _dot(hh, rwl_ref[...])


def _attn_out_kernel(oa_ref, ob_ref, wa_ref, wb_ref, x_ref, mod_ref, g_ref, rwh_ref, rwl_ref,
                     xo_ref, h_ref, lg_ref):
    y = _dot(oa_ref[0], wa_ref[...]) + _dot(ob_ref[0], wb_ref[...])
    _ffn_prep(x_ref[0], y, mod_ref[0, 0], g_ref[...], rwh_ref, rwl_ref, xo_ref, h_ref, lg_ref)


def _row_specs(D):
    row = lambda w: pl.BlockSpec((1, TM, w), lambda b, i: (b, i, 0))
    mod = pl.BlockSpec((1, 1, 6, D), lambda b, i: (b, jnp.minimum(i, 1), 0, 0))
    full = lambda a: pl.BlockSpec(a.shape, lambda b, i: (0,) * a.ndim)
    return row, mod, full


def _ffn_prep_outs(B, T, D):
    row, _, _ = _row_specs(D)
    shapes = (jax.ShapeDtypeStruct((B, T, D), F32), jax.ShapeDtypeStruct((B, T, D), BF16),
              jax.ShapeDtypeStruct((B, T, LANES), F32))
    return shapes, (row(D), row(D), row(LANES))


def _attn_out(oa, ob, wa, wb, x, mods, g, rwh, rwl):
    B, T, D = x.shape
    row, mod, full = _row_specs(D)
    shapes, specs = _ffn_prep_outs(B, T, D)
    return pl.pallas_call(
        _attn_out_kernel,
        out_shape=shapes,
        grid=(B, T // TM),
        in_specs=[row(oa.shape[-1]), row(ob.shape[-1]), full(wa), full(wb), row(D), mod, full(g),
                  full(rwh), full(rwl)],
        out_specs=specs,
        compiler_params=_cparams(("parallel", "parallel")),
        name="attn_out",
    )(oa, ob, wa, wb, x, mods, g, rwh, rwl)


def _gmm_kernel(be_ref, nu_ref, x_ref, w1_ref, w3_ref, w2_ref, o_ref):
    i = pl.program_id(0)

    @pl.when(i < nu_ref[0])
    def _():
        x = x_ref[...]
        a = _dot(x, w1_ref[0])
        b = _dot(x, w3_ref[0])
        mid = (a * _sigmoid(a)) * b
        o_ref[...] = _dot(mid.astype(BF16), w2_ref[0])

    @pl.when(i >= nu_ref[0])
    def _():
        o_ref[...] = jnp.zeros(o_ref.shape, o_ref.dtype)


def _gmm(block_expert, n_used, xs, w1, w3, w2):
    n_slots, D = xs.shape
    F = w1.shape[-1]
    nblk = n_slots // MOE_ROWS
    return pl.pallas_call(
        _gmm_kernel,
        out_shape=jax.ShapeDtypeStruct((n_slots, D), F32),
        grid_spec=pltpu.PrefetchScalarGridSpec(
            num_scalar_prefetch=2,
            grid=(nblk,),
            in_specs=[
                pl.BlockSpec((MOE_ROWS, D), lambda i, be, nu: (i, 0)),
                pl.BlockSpec((1, D, F), lambda i, be, nu: (be[i], 0, 0)),
                pl.BlockSpec((1, D, F), lambda i, be, nu: (be[i], 0, 0)),
                pl.BlockSpec((1, F, D), lambda i, be, nu: (be[i], 0, 0)),
            ],
            out_specs=pl.BlockSpec((MOE_ROWS, D), lambda i, be, nu: (i, 0)),
        ),
        compiler_params=_cparams(("arbitrary",)),
        name="moe_gmm",
    )(block_expert, n_used, xs, w1, w3, w2)


ROUTE_ROWS = 512


def _route_kernel(lg_ref, bias_ref, idx_ref, w_ref):
    x = lg_ref[...].T[:N_EXPERTS]
    m = jnp.max(x, axis=0, keepdims=True)
    e = jnp.exp(x - m)
    probs = e / jnp.sum(e, axis=0, keepdims=True)
    sel = probs + bias_ref[...][:, 0:1]
    row = lambda a, i: a[i:i + 1, :]
    G = EXPERTS_PER_GROUP
    scores = []
    for g in range(N_GROUPS):
        s = [row(sel, g * G + i) for i in range(G)]
        best = None
        for i in range(G):
            for j in range(i + 1, G):
                best = s[i] + s[j] if best is None else jnp.maximum(best, s[i] + s[j])
        scores.append(best)
    top = functools.reduce(jnp.maximum, scores)
    gi = jnp.full(top.shape, N_GROUPS - 1, jnp.int32)
    for g in range(N_GROUPS - 2, -1, -1):
        gi = jnp.where(scores[g] == top, g, gi)

    def pick(a, i):
        out = row(a, (N_GROUPS - 1) * G + i)
        for g in range(N_GROUPS - 2, -1, -1):
            out = jnp.where(gi == g, row(a, g * G + i), out)
        return out

    c = [pick(sel, i) for i in range(G)]
    pc = [pick(probs, i) for i in range(G)]

    def first_argmax(vals):
        mx = functools.reduce(jnp.maximum, vals)
        idx = jnp.full(mx.shape, G - 1, jnp.int32)
        for i in range(G - 2, -1, -1):
            idx = jnp.where(vals[i] == mx, i, idx)
        return idx

    i1 = first_argmax(c)
    i2 = first_argmax([jnp.where(i1 == i, -jnp.inf, c[i]) for i in range(G)])
    take = lambda vals, idx: functools.reduce(
        lambda acc, i: jnp.where(idx == i, vals[i], acc), range(G - 2, -1, -1), vals[G - 1])
    w1, w2 = take(pc, i1), take(pc, i2)
    tot = w1 + w2
    zi = jnp.zeros((6,) + top.shape[1:], jnp.int32)
    idx_ref[...] = jnp.concatenate([gi * G + i1, gi * G + i2, zi], axis=0)
    w_ref[...] = jnp.concatenate([w1 / tot, w2 / tot, zi.astype(F32)], axis=0)


def _route(logits, router_bias):
    N = logits.shape[0]
    bias = jnp.broadcast_to(router_bias.astype(F32)[:, None], (N_EXPERTS, LANES))
    idx, w = pl.pallas_call(
        _route_kernel,
        out_shape=(jax.ShapeDtypeStruct((8, N), jnp.int32), jax.ShapeDtypeStruct((8, N), F32)),
        grid=(N // ROUTE_ROWS,),
        in_specs=[pl.BlockSpec((ROUTE_ROWS, LANES), lambda i: (i, 0)),
                  pl.BlockSpec((N_EXPERTS, LANES), lambda i: (0, 0))],
        out_specs=(pl.BlockSpec((8, ROUTE_ROWS), lambda i: (0, i)),
                   pl.BlockSpec((8, ROUTE_ROWS), lambda i: (0, i))),
        compiler_params=_cparams(("parallel",)),
        name="route",
    )(logits, bias)
    return idx[:TOP_K].T, w[:TOP_K].T


def _moe(h, logits, router_bias, w1, w3, w2):
    N, D = h.shape
    expert_idx, gate_w = _route(logits, router_bias)
    NK = N * TOP_K
    flat_e = expert_idx.reshape(NK)
    onehot = (flat_e[:, None] == jnp.arange(N_EXPERTS, dtype=jnp.int32)[None, :]).astype(jnp.int32)
    csum = jnp.cumsum(onehot, axis=0)
    counts = csum[-1]
    rank = jnp.take_along_axis(csum, flat_e[:, None], axis=1)[:, 0] - 1
    padded = (counts + MOE_ROWS - 1) // MOE_ROWS * MOE_ROWS
    pad_end = jnp.cumsum(padded)
    pad_start = pad_end - padded
    dest =pad_start[flat_e] + rank
    nblk = -(-NK // MOE_ROWS) + N_EXPERTS
    n_slots = nblk * MOE_ROWS
    n_used = (pad_end[-1] // MOE_ROWS).astype(jnp.int32)
    blk = jnp.arange(nblk, dtype=jnp.int32)
    be = jnp.minimum(jnp.searchsorted(pad_end, blk * MOE_ROWS, side="right"), N_EXPERTS - 1).astype(jnp.int32)
    be = jnp.where(blk < n_used, be, be[jnp.maximum(n_used - 1, 0)])
    flat_tok = jnp.arange(NK, dtype=jnp.int32) // TOP_K
    slot_tok = jnp.zeros((n_slots,), jnp.int32).at[dest].set(flat_tok, unique_indices=True)
    xs = h[slot_tok]
    ys = _gmm(be, n_used.reshape(1), xs, w1, w3, w2)
    pos = dest.reshape(N, TOP_K)
    return ys[pos[:, 0]] * gate_w[:, 0:1] + ys[pos[:, 1]] * gate_w[:, 1:2]


def _residual_kernel(x_ref, f_ref, mod_ref, o_ref):
    o_ref[0] = x_ref[0] + mod_ref[0, 0][5:6] * f_ref[0]


def _final_kernel(x_ref, f_ref, mod_ref, g_ref, o_ref):
    x = x_ref[0] + mod_ref[0, 0][5:6] * f_ref[0]
    ms = jnp.mean(x * x, axis=-1, keepdims=True)
    o_ref[0] = x * lax.rsqrt(ms + RMS_EPS) * g_ref[...]


def _residual(x, f, mods):
    B, T, D = x.shape
    row, mod, _ = _row_specs(D)
    return pl.pallas_call(
        _residual_kernel,
        out_shape=jax.ShapeDtypeStruct((B, T, D), F32),
        grid=(B, T // TM),
        in_specs=[row(D), row(D), mod],
        out_specs=row(D),
        compiler_params=_cparams(("parallel", "parallel")),
        name="residual",
    )(x, f, mods)


def _final(x, f, mods, g, L):
    B, S, D = f.shape
    off = L // TM
    mod = pl.BlockSpec((1, 1, 6, D), lambda b, i: (b, 1, 0, 0))
    return pl.pallas_call(
        _final_kernel,
        out_shape=jax.ShapeDtypeStruct((B, S, D), F32),
        grid=(B, S // TM),
        in_specs=[pl.BlockSpec((1, TM, D), lambda b, i: (b, i + off, 0)),
                  pl.BlockSpec((1, TM, D), lambda b, i: (b, i, 0)), mod,
                  pl.BlockSpec(g.shape, lambda b, i: (0, 0))],
        out_specs=pl.BlockSpec((1, TM, D), lambda b, i: (b, i, 0)),
        compiler_params=_cparams(("parallel", "parallel")),
        name="final_norm",
    )(x, f, mods, g)


def _rwkv_proj_kernel(x_ref, xp_ref, xn_ref, mod_ref, g_ref, xmix_ref, wr_ref, wk_ref, wv_ref,
                      dw1_ref, dw2_ref, da1_ref, da2_ref, g1_ref, g2_ref, vec_ref, ones_ref,
                      r_ref, v_ref, kk_ref, bv_ref, gate_ref, w0_ref, w1_ref, kd0_ref, kd1_ref, bd0_ref, bd1_ref,
                      *, nt):
    i = pl.program_id(1)
    mod = mod_ref[0, 0]
    g = g_ref[...]
    nm = lambda x: _norm_mod(x, g, mod[0:1], mod[1:2])
    h = nm(x_ref[0])
    hp = nm(xp_ref[0])[7:8] * jnp.where(i >= 2, 1.0, 0.0)
    hn = nm(xn_ref[0])[0:1] * jnp.where((i >= 1) & (i < nt - 1), 1.0, 0.0)
    ridx = lax.broadcasted_iota(jnp.int32, h.shape, 0)
    h_dn = jnp.where(ridx == 0, hp, pltpu.roll(h, 1, axis=0))
    h_up = jnp.where(ridx == TM - 1, hn, pltpu.roll(h, TM - 1, axis=0))
    xx = 0.5 * (h_dn + h_up) - h
    xmix = xmix_ref[...]
    mix = lambda j: (h + xx * xmix[j:j + 1]).astype(BF16)
    vec = vec_ref[...]
    ones = ones_ref[...]

    r = _dot(mix(0), wr_ref[...])
    k = _dot(mix(2), wk_ref[...])
    v = _dot(mix(3), wv_ref[...])
    gate_ref[0] = _dot(_sigmoid(_dot(mix(5), g1_ref[...])).astype(BF16), g2_ref[...])
    kk = k * vec[0:1]
    kk = kk * lax.rsqrt(jnp.maximum(_segsum_wide(kk * kk, ones), 1e-24))
    lw = jnp.tanh(_dot(mix(1), dw1_ref[...])).astype(BF16)
    la = _dot(mix(4), da1_ref[...]).astype(BF16)
    r_ref[0] = r
    v_ref[0] = v
    kk_ref[0] = kk
    bonus = jnp.zeros_like(r)
    lora = DECAY_LORA
    for d, (w_ref, kd_ref, bd_ref) in enumerate(((w0_ref, kd0_ref, bd0_ref), (w1_ref, kd1_ref, bd1_ref))):
        z = -(vec[3 + d:4 + d] + _dot(lw[:, d * lora:(d + 1) * lora], dw2_ref[d]))
        softplus = jnp.maximum(z, 0.0) + jnp.log(1.0 + jnp.exp(-jnp.abs(z)))
        w_ref[0] = jnp.exp(-jnp.exp(-softplus - 0.5))
        iclr = _sigmoid(vec[5 + d:6 + d] + _dot(la[:, d * lora:(d + 1) * lora], da2_ref[d]))
        kd = k * (1.0 + (iclr - 1.0) * vec[1:2])
        kd_ref[0] = kd
        bd_ref[0] = kk * iclr
        bonus = bonus + _segsum_wide(r * kd * vec[2:3], ones)
    bv_ref[0] = bonus * v


DECAY_LORA = 64


def _rwkv_proj(x, mods, g, xmix, wr, wk, wv, dw1, dw2, da1, da2, g1, g2, vec, ones):
    B, T, D = x.shape
    nt = T // TM
    row, mod, full = _row_specs(D)
    r8 = TM // 8
    prev = pl.BlockSpec((1, 8, D), lambda b, i: (b, jnp.maximum(i * r8 - 1, 0), 0))
    nxt = pl.BlockSpec((1, 8, D), lambda b, i: (b, jnp.minimum((i + 1) * r8, T // 8 - 1), 0))
    out = jax.ShapeDtypeStruct((B, T, D), F32)
    return pl.pallas_call(
        functools.partial(_rwkv_proj_kernel, nt=nt),
        out_shape=(out,) * 11,
        grid=(B, nt),
        in_specs=[row(D), prev, nxt, mod, full(g), full(xmix), full(wr), full(wk), full(wv),
                  full(dw1), full(dw2), full(da1), full(da2), full(g1), full(g2), full(vec), full(ones)],
        out_specs=(row(D),) * 11,
        compiler_params=_cparams(("parallel", "parallel")),
        name="rwkv_proj",
    )(x, x, x, mods, g, xmix, wr, wk, wv, dw1, dw2, da1, da2, g1, g2, vec, ones)


def _lane_tiles_to_rows(x):
    n = x.shape[1] // LANES
    return jnp.concatenate([x[:, j * LANES:(j + 1) * LANES] for j in range(n)], axis=0)


def _rows_to_lane_tiles(x, n):
    m = x.shape[0] // n
    return jnp.concatenate([x[j * m:(j + 1) * m] for j in range(n)], axis=1)


def _scan_kernel(rf, rb, vf, vb, kkf, kkb, wf, wb, kdf, kdb, bdf, bdb, eye_ref, ones_ref, yf, yb, st, *, tc, nb):
    n = pl.program_id(0)

    @pl.when(n == 0)
    def _():
        st[...] = jnp.zeros(st.shape, F32)

    ones = ones_ref[...]
    N = HEAD_DIM
    nl = st.shape[-1] // LANES
    dirs = ((rf, vf, kkf, wf, kdf, bdf, yf), (rb, vb, kkb, wb, kdb, bdb, yb))

    def seg(x):
        res = _dot(_lane_tiles_to_rows(x.astype(BF16)), ones)
        return _rows_to_lane_tiles(res, nl)

    def extract(ybc):
        return jnp.sum(ybc * eye_ref[...], axis=0, keepdims=True)

    def step(i, carry):
        ip = jnp.maximum(i - 1, 0)
        for d, (R, V, KK, W, KD, BD, Y) in enumerate(dirs):
            row = i if d == 0 else tc - 1 - i
            rowp = ip if d == 0 else tc - 1 - ip
            for b in range(nb):
                gi = d * nb + b
                S = st[gi]
                cur = pl.ds(row, 1)
                a = -KK[b, cur, :]
                x = jnp.concatenate([S * a, S * R[b, pl.ds(rowp, 1), :], eye_ref[...] * V[b, cur, :]], axis=0)
                res = seg(x)
                sa, ybc, vcol = res[0:N], res[N:2 * N], res[2 * N:3 * N]
                st[gi] = S * W[b, cur, :] + sa * BD[b, cur, :] + vcol * KD[b, cur, :]
                Y[b, pl.ds(rowp, 1), :] = extract(ybc)
        return carry

    lax.fori_loop(0, tc, step, 0)
    for d, (R, V, KK, W, KD, BD, Y) in enumerate(dirs):
        last = tc - 1 if d == 0 else 0
        for b in range(nb):
            Y[b, last:last + 1, :] = extract(seg(st[d * nb + b] * R[b, last:last + 1, :]))


def _rwkv_scan(r, v, kk, w0, w1, kd0, kd1, bd0, bd1, eye, ones, L):
    B, T, D = r.shape
    tc = SCAN_CHUNK
    nc, nchunks = L // tc, T // tc
    fwd = pl.BlockSpec((B, tc, D), lambda n: (0, n, 0))
    rev_idx = lambda n: jnp.where(n < nc, nc - 1 - n, nchunks - 1 - (n - nc))
    rev = pl.BlockSpec((B, tc, D), lambda n: (0, rev_idx(n), 0))
    full = lambda a: pl.BlockSpec(a.shape, lambda n: (0,) * a.ndim)
    out = jax.ShapeDtypeStruct((B, T, D), F32)
    return pl.pallas_call(
        functools.partial(_scan_kernel, tc=tc, nb=B),
        out_shape=(out, out),
        grid=(nchunks,),
        in_specs=[fwd, rev, fwd, rev, fwd, rev, fwd, rev, fwd, rev, fwd, rev, full(eye), full(ones)],
        out_specs=(fwd, rev),
        scratch_shapes=[pltpu.VMEM((2 * B, HEAD_DIM, D), F32)],
        compiler_params=_cparams(("arbitrary",)),
        name="rwkv_scan",
    )(r, r, v, v, kk, kk, w0, w1, kd0, kd1, bd0, bd1, eye, ones)


def _rwkv_out_kernel(yf_ref, yb_ref, bv_ref, gate_ref, ln_ref, wo_ref, ones_ref, x_ref, mod_ref, g_ref,
                     rwh_ref, rwl_ref, xo_ref, h_ref, lg_ref):
    ones = ones_ref[...]
    y = yf_ref[0] + yb_ref[0]
    inv = 1.0 / HEAD_DIM
    dlt = y - _segsum_wide(y, ones) * inv
    yn = dlt * lax.rsqrt(_segsum_wide(dlt * dlt, ones) * inv + GN_EPS)
    ln = ln_ref[...]
    o = (yn * ln[0:1] + ln[1:2] + bv_ref[0]) * gate_ref[0]
    yl = _dot(o.astype(BF16), wo_ref[...])
    _ffn_prep(x_ref[0], yl, mod_ref[0, 0], g_ref[...], rwh_ref, rwl_ref, xo_ref, h_ref, lg_ref)


def _rwkv_out(yf, yb, bv, gate, ln, wo, ones, x, mods, g, rwh, rwl):
    B, T, D = x.shape
    row, mod, full = _row_specs(D)
    shapes, specs = _ffn_prep_outs(B, T, D)
    return pl.pallas_call(
        _rwkv_out_kernel,
        out_shape=shapes,
        grid=(B, T // TM),
        in_specs=[row(D), row(D), row(D), row(D), full(ln), full(wo), full(ones), row(D), mod, full(g),
                  full(rwh), full(rwl)],
        out_specs=specs,
        compiler_params=_cparams(("parallel", "parallel")),
        name="rwkv_out",
    )(yf, yb, bv, gate, ln, wo, ones, x, mods, g, rwh, rwl)


def _rope_tables(S, L):
    rows = S // GRID_W
    row = jnp.repeat(jnp.arange(rows, dtype=F32), GRID_W)
    col = (jnp.arange(rows * GRID_W) % GRID_W).astype(F32)
    n_freq = HEAD_DIM // 4
    inv = ROPE_THETA ** (-jnp.arange(n_freq, dtype=F32) / n_freq)
    lane = np.arange(LANES) % HEAD_DIM
    axis, half, freq = lane // 32, (lane % 32) // 16, lane % 16
    pos = jnp.where(jnp.asarray(axis == 0)[None, :], row[:, None], col[:, None])
    ang = pos * inv[freq][None, :]
    sgn = jnp.asarray(np.where(half == 0, -1.0, 1.0), dtype=F32)
    cos = jnp.concatenate([jnp.ones((L, LANES), F32), jnp.cos(ang)], axis=0)
    sin = jnp.concatenate([jnp.zeros((L, LANES), F32), jnp.sin(ang) * sgn[None, :]], axis=0)
    return cos, sin


def kernel(x, c, ctx, c_ctx, ada_w, ada_b, norm_mix_g, norm_ffn_g, attn_w_in, attn_w_out, attn_sink,
           attn_q_norm_g, attn_k_norm_g, rwkv_x_mix, rwkv_w_r, rwkv_w_k, rwkv_w_v, rwkv_w_o,
           rwkv_decay_w0, rwkv_decay_w1, rwkv_decay_w2, rwkv_iclr_a0, rwkv_iclr_a1, rwkv_iclr_a2,
           rwkv_gate_g1, rwkv_gate_g2, rwkv_k_k, rwkv_k_a, rwkv_r_k, rwkv_ln_g, rwkv_ln_b,
           router_w, router_bias, moe_w1, moe_w3, moe_w2, final_norm_g):
    B, S, D = x.shape
    L = ctx.shape[1]
    T = L + S
    depth = ada_w.shape[0]
    assert D == D_MODEL and L == TM and S % TM == 0 and B == 2 and depth == 2
    ones = _seg_ones()
    bf = lambda a: a.astype(BF16)

    cs = jnp.zeros((8, D), F32).at[:B].set(c).at[B].set(c_ctx)
    ada = _ada(cs, ada_w, ada_b).reshape(depth, 8, 6, D)
    mods = [jnp.stack([jnp.broadcast_to(ada[i, B], (B, 6, D)), ada[i, :B]], axis=1) for i in range(depth)]

    xa = jnp.concatenate([ctx, x], axis=1)
    rw = jnp.zeros((D, LANES), F32).at[:, :N_EXPERTS].set(router_w)
    rwh, rwl = _split(rw)

    w_in = attn_w_in[0]
    roped = np.concatenate([np.arange(0, 640), np.arange(768, 1408)])
    w_rot = w_in[:, roped ^ 16]
    cos, sin = _rope_tables(S, L)
    lane = np.arange(LANES) % HEAD_DIM
    gains = lambda g: jnp.stack([g[lane], g[lane ^ 16]], axis=0)
    qa, ka, va, qb, kb, vb = _inproj(xa, mods[0], norm_mix_g[0].reshape(1, D), bf(w_in), bf(w_rot), cos, sin,
                                     gains(attn_q_norm_g[0]), gains(attn_k_norm_g[0]), ones)
    grouped = lambda q: q.reshape(B, A_KV_HEADS, GROUP, T, HEAD_DIM)
    qa, qb = grouped(qa), grouped(qb)
    sink = attn_sink[0].astype(F32)
    nosink = jnp.full((B_Q_HEADS,), NEG, F32)
    oa_l = _window_attn(sink, qa, ka, va, L, S)
    oa_c = _flash(sink, qa, ka, va, q_rows=L, q_off=0, k_rows=L, tq=L, tk=L)
    ob_l = _flash(nosink, qb, kb, vb, q_rows=S, q_off=L, k_rows=T, tq=256, tk=_key_tile(T))
    ob_c = _flash(nosink, qb, kb, vb, q_rows=L, q_off=0, k_rows=L, tq=L, tk=L)
    oa = jnp.concatenate([oa_c, oa_l], axis=1)
    ob = jnp.concatenate([ob_c, ob_l], axis=1)
    w_out = bf(attn_w_out[0])
    na = A_Q_HEADS * HEAD_DIM
    xa, h, lg = _attn_out(oa, ob, w_out[:na], w_out[na:], xa, mods[0], norm_ffn_g[0].reshape(1, D), rwh, rwl)
    f = _moe(h.reshape(B * T, D), lg.reshape(B * T, LANES), router_bias,
             bf(moe_w1[0]), bf(moe_w3[0]), bf(moe_w2[0]))
    xa = _residual(xa, f.reshape(B, T, D), mods[0])

    cat2 = lambda a: jnp.concatenate([a[0], a[1]], axis=1)
    vec = jnp.stack([rwkv_k_k[0], rwkv_k_a[0], rwkv_r_k[0].reshape(D), rwkv_decay_w0[0, 0], rwkv_decay_w0[0, 1],
                     rwkv_iclr_a0[0, 0], rwkv_iclr_a0[0, 1], jnp.zeros((D,), F32)], axis=0)
    outs = _rwkv_proj(xa, mods[1], norm_mix_g[1].reshape(1, D), jnp.pad(rwkv_x_mix[0], ((0, 2), (0, 0))),
                      bf(rwkv_w_r[0]), bf(rwkv_w_k[0]), bf(rwkv_w_v[0]),
                      bf(cat2(rwkv_decay_w1[0])), bf(rwkv_decay_w2[0]),
                      bf(cat2(rwkv_iclr_a1[0])), bf(rwkv_iclr_a2[0]),
                      bf(rwkv_gate_g1[0]), bf(rwkv_gate_g2[0]), vec, ones)
    r, v, kk, bv, gate, w0, w1, kd0, kd1, bd0, bd1 = outs
    vi = np.arange(HEAD_DIM)
    eye = jnp.asarray(vi[:, None] == (np.arange(D) % HEAD_DIM)[None, :], dtype=F32)
    yf, yb = _rwkv_scan(r, v, kk, w0, w1, kd0, kd1, bd0, bd1, eye, ones, L)
    ln = jnp.stack([rwkv_ln_g[0], rwkv_ln_b[0]] + [jnp.zeros((D,), F32)] * 6, axis=0)
    xa, h, lg = _rwkv_out(yf, yb, bv, gate, ln, bf(rwkv_w_o[0]), ones, xa, mods[1],
                          norm_ffn_g[1].reshape(1, D), rwh, rwl)
    f = _moe(h[:, L:].reshape(B * S, D), lg[:, L:].reshape(B * S, LANES), router_bias,
             bf(moe_w1[1]), bf(moe_w3[1]), bf(moe_w2[1]))
    return _final(xa, f.reshape(B, S, D), mods[1], final_norm_g.reshape(1, D), L)


def _key_tile(T):
    for tk in (1280, 1024, 768, 512, 256):
        if T % tk == 0:
            return tk
    raise ValueError(T)
```

```python
import functools

import numpy as np
import jax
import jax.numpy as jnp
from jax import lax
from jax.experimental import pallas as pl
from jax.experimental.pallas import tpu as pltpu

F32 = jnp.float32
BF16 = jnp.bfloat16

D_MODEL = 1024
HEAD_DIM = 64
GRID_W = 64
ROPE_THETA = 10000.0
RMS_EPS = 1e-6
GN_EPS = 64e-5
A_Q_HEADS = 8
A_KV_HEADS = 2
B_Q_HEADS = 8
B_KV_HEADS = 2
GROUP = 4
WINDOW = 128
N_EXPERTS = 16
N_GROUPS = 4
EXPERTS_PER_GROUP = 4
TOP_K = 2
LANES = 128
TM = 256
MOE_ROWS = 512
SCAN_CHUNK = 64
VMEM_LIMIT = 56 * 1024 * 1024
NEG = -1e30
LOG2E = 1.4426950408889634


def _cparams(sem):
    return pltpu.CompilerParams(dimension_semantics=sem, vmem_limit_bytes=VMEM_LIMIT)


def _dot(a, b):
    return jnp.dot(a, b, preferred_element_type=F32)


def _dot_nt(a, b):
    return lax.dot_general(a, b, (((1,), (1,)), ((), ())), preferred_element_type=F32)


def _split(x):
    hi = x.astype(BF16)
    lo = (x - hi.astype(F32)).astype(BF16)
    return hi, lo


def _dot3(x, w):
    xh, xl = _split(x)
    wh, wl = _split(w)
    return _dot(xh, wh) + _dot(xh, wl) + _dot(xl, wh)


def _segsum(v, ones):
    hi, lo = _split(v)
    return _dot(hi, ones) + _dot(lo, ones)


def _segsum_wide(v, ones):
    n = v.shape[1] // LANES
    return jnp.concatenate([_segsum(v[:, j * LANES:(j + 1) * LANES], ones) for j in range(n)], axis=1)


def _norm_mod(x, g, shift, scale):
    ms = jnp.mean(x * x, axis=-1, keepdims=True)
    return (x * lax.rsqrt(ms + RMS_EPS) * g) * (1.0 + scale) + shift


def _sigmoid(x):
    return 1.0 / (1.0 + jnp.exp(-x))


def _seg_ones():
    i = np.arange(LANES)
    return jnp.asarray((i[:, None] // HEAD_DIM) == (i[None, :] // HEAD_DIM), dtype=BF16)


def _ada_kernel(c_ref, w_ref, b_ref, o_ref):
    c = c_ref[...]
    s = c * _sigmoid(c)
    o_ref[0] = _dot3(s, w_ref[0]) + b_ref[0]


def _ada(cs, ada_w, ada_b):
    depth, d, n = ada_w.shape
    tn = 1536
    return pl.pallas_call(
        _ada_kernel,
        out_shape=jax.ShapeDtypeStruct((depth, 8, n), F32),
        grid=(depth, n // tn),
        in_specs=[
            pl.BlockSpec((8, d), lambda l, j: (0, 0)),
            pl.BlockSpec((1, d, tn), lambda l, j: (l, 0, j)),
            pl.BlockSpec((1, 1, tn), lambda l, j: (l, 0, j)),
        ],
        out_specs=pl.BlockSpec((1, 8, tn), lambda l, j: (l, 0, j)),
        compiler_params=_cparams(("arbitrary", "arbitrary")),
        name="ada",
    )(cs, ada_w, ada_b.reshape(depth, 1, n))


def _inproj_kernel(x_ref, mod_ref, g_ref, w_ref, wrot_ref, cos_ref, sin_ref, gq_ref, gk_ref, ones_ref,
                   qa_ref, ka_ref, va_ref, qb_ref, kb_ref, vb_ref):
    mod = mod_ref[0, 0]
    h = _norm_mod(x_ref[0], g_ref[...], mod[0:1], mod[1:2]).astype(BF16)
    y = _dot(h, w_ref[...])
    yr = _dot(h, wrot_ref[...])
    cos = cos_ref[...]
    sin = sin_ref[...]
    ones = ones_ref[...]
    qscale = HEAD_DIM ** -0.5 * LOG2E

    def put(ref, tile, val):
        ref[0, 2 * tile] = val[:, :HEAD_DIM].astype(ref.dtype)
        ref[0, 2 * tile + 1] = val[:, HEAD_DIM:].astype(ref.dtype)

    def chunk(a, c):
        return a[:, c * LANES:(c + 1) * LANES]

    for c in range(4):
        put(qa_ref, c, (chunk(y, c) * cos + chunk(yr, c) * sin) * qscale)
    put(ka_ref, 0, chunk(y, 4) * cos + chunk(yr, 4) * sin)
    put(va_ref, 0, chunk(y, 5))

    def normed(c, cr, gain_ref):
        v = chunk(y, c)
        rs = lax.rsqrt(_segsum(v * v, ones) * (1.0 / HEAD_DIM) + RMS_EPS)
        return (v * rs * gain_ref[0:1]) * cos + (chunk(yr, cr) * rs * gain_ref[1:2]) * sin

    for c in range(4):
        put(qb_ref, c, normed(6 + c, 5 + c, gq_ref) * qscale)
    put(kb_ref, 0, normed(10, 9, gk_ref))
    put(vb_ref, 0, chunk(y, 11))


def _inproj(x, mods, g, w_in, w_rot, cos, sin, gq2, gk2, ones):
    B, T, D = x.shape
    nt = T // TM
    heads = lambda n: jax.ShapeDtypeStruct((B, n, T, HEAD_DIM), BF16)
    hspec = lambda n: pl.BlockSpec((1, n, TM, HEAD_DIM), lambda b, i: (b, 0, i, 0))
    full = lambda a: pl.BlockSpec(a.shape, lambda b, i: (0,) * a.ndim)
    return pl.pallas_call(
        _inproj_kernel,
        out_shape=(heads(8), heads(2), heads(2), heads(8), heads(2), heads(2)),
        grid=(B, nt),
        in_specs=[
            pl.BlockSpec((1, TM, D), lambda b, i: (b, i, 0)),
            pl.BlockSpec((1, 1, 6, D), lambda b, i: (b, jnp.minimum(i, 1), 0, 0)),
            full(g), full(w_in), full(w_rot),
            pl.BlockSpec((TM, LANES), lambda b, i: (i, 0)),
            pl.BlockSpec((TM, LANES), lambda b, i: (i, 0)),
            full(gq2), full(gk2), full(ones),
        ],
        out_specs=(hspec(8), hspec(2), hspec(2), hspec(8), hspec(2), hspec(2)),
        compiler_params=_cparams(("parallel", "parallel")),
        name="attn_inproj",
    )(x, mods, g, w_in, w_rot, cos, sin, gq2, gk2, ones)


LOOKAHEAD = 2


def _flash_kernel(sink_ref, q_ref, k_ref, v_ref, o_ref, m_scr, acc_scr, s_scr, *, tk, nk):
    h = pl.program_id(1)
    m_scr[...] = jnp.full(m_scr.shape, NEG, F32)
    acc_scr[...] = jnp.zeros(acc_scr.shape, F32)

    def scores(j, g):
        return _dot_nt(k_ref[0, 0, pl.ds(pl.multiple_of(j * tk, tk), tk), :], q_ref[0, 0, g])

    for g in range(LOOKAHEAD):
        s_scr[g] = scores(0, g)

    def body(j, carry):
        vt = v_ref[0, 0, :, pl.ds(pl.multiple_of(j * tk, tk), tk)]
        jn = jnp.minimum(j + 1, nk - 1)
        ahead = {}
        for g in range(GROUP):
            st = s_scr[g] if g < LOOKAHEAD else ahead.pop(g)
            if g + LOOKAHEAD < GROUP:
                ahead[g + LOOKAHEAD] = scores(j, g + LOOKAHEAD)
            m_prev = m_scr[g]
            m_new = jnp.maximum(m_prev, jnp.max(st, axis=0, keepdims=True))
            p = jnp.exp2(st - m_new).astype(BF16)
            if g + LOOKAHEAD >= GROUP:
                s_scr[g + LOOKAHEAD - GROUP] = scores(jn, g + LOOKAHEAD - GROUP)
            acc_scr[g] = jnp.exp2(m_prev - m_new) * acc_scr[g] + _dot(vt, p)
            m_scr[g] = m_new
        return carry

    lax.fori_loop(0, nk, body, 0)
    outs = []
    for g in range(GROUP):
        acc = acc_scr[g]
        l = acc[HEAD_DIM:HEAD_DIM + 1] + jnp.exp2(sink_ref[h * GROUP + g] - m_scr[g])
        outs.append(acc[:HEAD_DIM] / l)
    o_ref[0] = jnp.concatenate(outs, axis=0).T.astype(o_ref.dtype)


def _flash(sink, q, k, v, *, q_rows, q_off, k_rows, tq, tk):
    B, Hkv = k.shape[:2]
    nq, nk = q_rows // tq, k_rows // tk
    qo = q_off // tq
    return pl.pallas_call(
        functools.partial(_flash_kernel, tk=tk, nk=nk),
        out_shape=jax.ShapeDtypeStruct((B, q_rows, Hkv * GROUP * HEAD_DIM), BF16),
        grid=(B, Hkv, nq),
        in_specs=[
            pl.BlockSpec(memory_space=pltpu.SMEM),
            pl.BlockSpec((1, 1, GROUP, tq, HEAD_DIM), lambda b, h, i: (b, h, 0, i + qo, 0)),
            pl.BlockSpec((1, 1, k_rows, HEAD_DIM), lambda b, h, i: (b, h, 0, 0)),
            pl.BlockSpec((1, 1, LANES, k_rows), lambda b, h, i: (b, h, 0, 0)),
        ],
        out_specs=pl.BlockSpec((1, tq, GROUP * HEAD_DIM), lambda b, h, i: (b, i, h)),
        scratch_shapes=[
            pltpu.VMEM((GROUP, 1, tq), F32),
            pltpu.VMEM((GROUP, LANES, tq), F32),
            pltpu.VMEM((LOOKAHEAD, tk, tq), F32),
        ],
        compiler_params=_cparams(("parallel", "parallel", "arbitrary")),
        name="flash_attn",
    )(sink, q, k, v)


def _window_kernel(sink_ref, q_ref, kc_ref, vc_ref, k0_ref, k1_ref, k2_ref, v0_ref, v1_ref, v2_ref, o_ref, *, nb):
    h = pl.program_id(1)
    i = pl.program_id(2)
    rows = GROUP * WINDOW
    q = q_ref[0, 0].reshape(rows, HEAD_DIM)
    r = lax.broadcasted_iota(jnp.int32, (rows, WINDOW), 0) & (WINDOW - 1)
    c = lax.broadcasted_iota(jnp.int32, (rows, WINDOW), 1)
    sc = _dot_nt(q, kc_ref[0, 0])
    s0 = jnp.where((c >= r) & (i > 0), _dot_nt(q, k0_ref[0, 0]), NEG)
    s1 = _dot_nt(q, k1_ref[0, 0])
    s2 = jnp.where((c <= r) & (i < nb - 1), _dot_nt(q, k2_ref[0, 0]), NEG)
    sink = jnp.concatenate(
        [jnp.full((WINDOW, 1), sink_ref[h * GROUP + g], F32) for g in range(GROUP)], axis=0)
    rowmax = lambda s: jnp.max(s, axis=-1, keepdims=True)
    m = jnp.maximum(jnp.maximum(rowmax(sc), rowmax(s0)), jnp.maximum(rowmax(s1), rowmax(s2)))
    m = jnp.maximum(m, sink)
    pc, p0, p1, p2 = (jnp.exp2(s - m) for s in (sc, s0, s1, s2))
    rowsum = lambda p: jnp.sum(p, axis=-1, keepdims=True)
    l = rowsum(pc) + rowsum(p0) + rowsum(p1) + rowsum(p2) + jnp.exp2(sink - m)
    acc = (_dot(pc.astype(BF16), vc_ref[0, 0]) + _dot(p0.astype(BF16), v0_ref[0, 0])
           + _dot(p1.astype(BF16), v1_ref[0, 0]) + _dot(p2.astype(BF16), v2_ref[0, 0]))
    out = acc / l
    for g in range(GROUP):
        o_ref[0, :, g * HEAD_DIM:(g + 1) * HEAD_DIM] = out[g * WINDOW:(g + 1) * WINDOW].astype(o_ref.dtype)


def _window_attn(sink, q, k, v, L, S):
    B, Hkv = k.shape[:2]
    nb = S // WINDOW
    pad = ((0, 0), (0, 0), (WINDOW, WINDOW), (0, 0))
    kp = jnp.pad(k[:, :, L:], pad)
    vp = jnp.pad(v[:, :, L:], pad)
    qo = L // WINDOW
    band = lambda j: pl.BlockSpec((1, 1, WINDOW, HEAD_DIM), lambda b, h, i: (b, h, i + j, 0))
    ctx = pl.BlockSpec((1, 1, L, HEAD_DIM), lambda b, h, i: (b, h, 0, 0))
    return pl.pallas_call(
        functools.partial(_window_kernel, nb=nb),
        out_shape=jax.ShapeDtypeStruct((B, S, Hkv * GROUP * HEAD_DIM), BF16),
        grid=(B, Hkv, nb),
        in_specs=[
            pl.BlockSpec(memory_space=pltpu.SMEM),
            pl.BlockSpec((1, 1, GROUP, WINDOW, HEAD_DIM), lambda b, h, i: (b, h, 0, i + qo, 0)),
            ctx, ctx, band(0), band(1), band(2), band(0), band(1), band(2),
        ],
        out_specs=pl.BlockSpec((1, WINDOW, GROUP * HEAD_DIM), lambda b, h, i: (b, i, h)),
        compiler_params=_cparams(("parallel", "parallel", "parallel")),
        name="window_attn",
    )(sink, q, k, v, kp, kp, kp, vp, vp, vp)


def _ffn_prep(x, y, mod, gffn, rwh_ref, rwl_ref, xo_ref, h_ref, lg_ref):
    xn = x + mod[2:3] * y
    h = _norm_mod(xn, gffn, mod[3:4], mod[4:5])
    xo_ref[0] = xn
    hh, hl = _split(h)
    h_ref[0] = hh
    rwh = rwh_ref[...]
    lg_ref[0] = _dot(hh, rwh) + _dot(hl, rwh) + _dot(hh, rwl_ref[...])


def _attn_out_kernel(oa_ref, ob_ref, wa_ref, wb_ref, x_ref, mod_ref, g_ref, rwh_ref, rwl_ref,
                     xo_ref, h_ref, lg_ref):
    y = _dot(oa_ref[0], wa_ref[...]) + _dot(ob_ref[0], wb_ref[...])
    _ffn_prep(x_ref[0], y, mod_ref[0, 0], g_ref[...], rwh_ref, rwl_ref, xo_ref, h_ref, lg_ref)


def _row_specs(D):
    row = lambda w: pl.BlockSpec((1, TM, w), lambda b, i: (b, i, 0))
    mod = pl.BlockSpec((1, 1, 6, D), lambda b, i: (b, jnp.minimum(i, 1), 0, 0))
    full = lambda a: pl.BlockSpec(a.shape, lambda b, i: (0,) * a.ndim)
    return row, mod, full


def _ffn_prep_outs(B, T, D):
    row, _, _ = _row_specs(D)
    shapes = (jax.ShapeDtypeStruct((B, T, D), F32), jax.ShapeDtypeStruct((B, T, D), BF16),
              jax.ShapeDtypeStruct((B, T, LANES), F32))
    return shapes, (row(D), row(D), row(LANES))


def _attn_out(oa, ob, wa, wb, x, mods, g, rwh, rwl):
    B, T, D = x.shape
    row, mod, full = _row_specs(D)
    shapes, specs = _ffn_prep_outs(B, T, D)
    return pl.pallas_call(
        _attn_out_kernel,
        out_shape=shapes,
        grid=(B, T // TM),
        in_specs=[row(oa.shape[-1]), row(ob.shape[-1]), full(wa), full(wb), row(D), mod, full(g),
                  full(rwh), full(rwl)],
        out_specs=specs,
        compiler_params=_cparams(("parallel", "parallel")),
        name="attn_out",
    )(oa, ob, wa, wb, x, mods, g, rwh, rwl)


def _gmm_kernel(be_ref, nu_ref, x_ref, w1_ref, w3_ref, w2_ref, o_ref):
    i = pl.program_id(0)

    @pl.when(i < nu_ref[0])
    def _():
        x = x_ref[...]
        a = _dot(x, w1_ref[0])
        b = _dot(x, w3_ref[0])
        mid = (a * _sigmoid(a)) * b
        o_ref[...] = _dot(mid.astype(BF16), w2_ref[0])

    @pl.when(i >= nu_ref[0])
    def _():
        o_ref[...] = jnp.zeros(o_ref.shape, o_ref.dtype)


def _gmm(block_expert, n_used, xs, w1, w3, w2):
    n_slots, D = xs.shape
    F = w1.shape[-1]
    nblk = n_slots // MOE_ROWS
    return pl.pallas_call(
        _gmm_kernel,
        out_shape=jax.ShapeDtypeStruct((n_slots, D), F32),
        grid_spec=pltpu.PrefetchScalarGridSpec(
            num_scalar_prefetch=2,
            grid=(nblk,),
            in_specs=[
                pl.BlockSpec((MOE_ROWS, D), lambda i, be, nu: (i, 0)),
                pl.BlockSpec((1, D, F), lambda i, be, nu: (be[i], 0, 0)),
                pl.BlockSpec((1, D, F), lambda i, be, nu: (be[i], 0, 0)),
                pl.BlockSpec((1, F, D), lambda i, be, nu: (be[i], 0, 0)),
            ],
            out_specs=pl.BlockSpec((MOE_ROWS, D), lambda i, be, nu: (i, 0)),
        ),
        compiler_params=_cparams(("arbitrary",)),
        name="moe_gmm",
    )(block_expert, n_used, xs, w1, w3, w2)


ROUTE_ROWS = 512


def _route_kernel(lg_ref, bias_ref, idx_ref, w_ref):
    x = lg_ref[...].T[:N_EXPERTS]
    m = jnp.max(x, axis=0, keepdims=True)
    e = jnp.exp(x - m)
    probs = e / jnp.sum(e, axis=0, keepdims=True)
    sel = probs + bias_ref[...][:, 0:1]
    row = lambda a, i: a[i:i + 1, :]
    G = EXPERTS_PER_GROUP
    scores = []
    for g in range(N_GROUPS):
        s = [row(sel, g * G + i) for i in range(G)]
        best = None
        for i in range(G):
            for j in range(i + 1, G):
                best = s[i] + s[j] if best is None else jnp.maximum(best, s[i] + s[j])
        scores.append(best)
    top = functools.reduce(jnp.maximum, scores)
    gi = jnp.full(top.shape, N_GROUPS - 1, jnp.int32)
    for g in range(N_GROUPS - 2, -1, -1):
        gi = jnp.where(scores[g] == top, g, gi)

    def pick(a, i):
        out = row(a, (N_GROUPS - 1) * G + i)
        for g in range(N_GROUPS - 2, -1, -1):
            out = jnp.where(gi == g, row(a, g * G + i), out)
        return out

    c = [pick(sel, i) for i in range(G)]
    pc = [pick(probs, i) for i in range(G)]

    def first_argmax(vals):
        mx = functools.reduce(jnp.maximum, vals)
        idx = jnp.full(mx.shape, G - 1, jnp.int32)
        for i in range(G - 2, -1, -1):
            idx = jnp.where(vals[i] == mx, i, idx)
        return idx

    i1 = first_argmax(c)
    i2 = first_argmax([jnp.where(i1 == i, -jnp.inf, c[i]) for i in range(G)])
    take = lambda vals, idx: functools.reduce(
        lambda acc, i: jnp.where(idx == i, vals[i], acc), range(G - 2, -1, -1), vals[G - 1])
    w1, w2 = take(pc, i1), take(pc, i2)
    tot = w1 + w2
    zi = jnp.zeros((6,) + top.shape[1:], jnp.int32)
    idx_ref[...] = jnp.concatenate([gi * G + i1, gi * G + i2, zi], axis=0)
    w_ref[...] = jnp.concatenate([w1 / tot, w2 / tot, zi.astype(F32)], axis=0)


def _route(logits, router_bias):
    N = logits.shape[0]
    bias = jnp.broadcast_to(router_bias.astype(F32)[:, None], (N_EXPERTS, LANES))
    idx, w = pl.pallas_call(
        _route_kernel,
        out_shape=(jax.ShapeDtypeStruct((8, N), jnp.int32), jax.ShapeDtypeStruct((8, N), F32)),
        grid=(N // ROUTE_ROWS,),
        in_specs=[pl.BlockSpec((ROUTE_ROWS, LANES), lambda i: (i, 0)),
                  pl.BlockSpec((N_EXPERTS, LANES), lambda i: (0, 0))],
        out_specs=(pl.BlockSpec((8, ROUTE_ROWS), lambda i: (0, i)),
                   pl.BlockSpec((8, ROUTE_ROWS), lambda i: (0, i))),
        compiler_params=_cparams(("parallel",)),
        name="route",
    )(logits, bias)
    return idx[:TOP_K].T, w[:TOP_K].T


def _moe(h, logits, router_bias, w1, w3, w2):
    N, D = h.shape
    expert_idx, gate_w = _route(logits, router_bias)
    NK = N * TOP_K
    flat_e = expert_idx.reshape(NK)
    onehot = (flat_e[:, None] == jnp.arange(N_EXPERTS, dtype=jnp.int32)[None, :]).astype(jnp.int32)
    csum = jnp.cumsum(onehot, axis=0)
    counts = csum[-1]
    rank = jnp.take_along_axis(csum, flat_e[:, None], axis=1)[:, 0] - 1
    padded = (counts + MOE_ROWS - 1) // MOE_ROWS * MOE_ROWS
    pad_end = jnp.cumsum(padded)
    pad_start = pad_end - padded
    dest =pad_start[flat_e] + rank
    nblk = -(-NK // MOE_ROWS) + N_EXPERTS
    n_slots = nblk * MOE_ROWS
    n_used = (pad_end[-1] // MOE_ROWS).astype(jnp.int32)
    blk = jnp.arange(nblk, dtype=jnp.int32)
    be = jnp.sum((pad_end[None, :] <= (blk * MOE_ROWS)[:, None]).astype(jnp.int32), axis=1)
    be = jnp.minimum(be, N_EXPERTS - 1)
    be = jnp.where(blk < n_used, be, be[jnp.maximum(n_used - 1, 0)])
    flat_tok = jnp.arange(NK, dtype=jnp.int32) // TOP_K
    slot_tok = jnp.zeros((n_slots,), jnp.int32).at[dest].set(flat_tok, unique_indices=True)
    xs = h[slot_tok]
    ys = _gmm(be, n_used.reshape(1), xs, w1, w3, w2)
    pos = dest.reshape(N, TOP_K)
    return ys[pos[:, 0]] * gate_w[:, 0:1] + ys[pos[:, 1]] * gate_w[:, 1:2]


def _residual_kernel(x_ref, f_ref, mod_ref, o_ref):
    o_ref[0] = x_ref[0] + mod_ref[0, 0][5:6] * f_ref[0]


def _final_kernel(x_ref, f_ref, mod_ref, g_ref, o_ref):
    x = x_ref[0] + mod_ref[0, 0][5:6] * f_ref[0]
    ms = jnp.mean(x * x, axis=-1, keepdims=True)
    o_ref[0] = x * lax.rsqrt(ms + RMS_EPS) * g_ref[...]


def _residual(x, f, mods):
    B, T, D = x.shape
    row, mod, _ = _row_specs(D)
    return pl.pallas_call(
        _residual_kernel,
        out_shape=jax.ShapeDtypeStruct((B, T, D), F32),
        grid=(B, T // TM),
        in_specs=[row(D), row(D), mod],
        out_specs=row(D),
        compiler_params=_cparams(("parallel", "parallel")),
        name="residual",
    )(x, f, mods)


def _final(x, f, mods, g, L):
    B, S, D = f.shape
    off = L // TM
    mod = pl.BlockSpec((1, 1, 6, D), lambda b, i: (b, 1, 0, 0))
    return pl.pallas_call(
        _final_kernel,
        out_shape=jax.ShapeDtypeStruct((B, S, D), F32),
        grid=(B, S // TM),
        in_specs=[pl.BlockSpec((1, TM, D), lambda b, i: (b, i + off, 0)),
                  pl.BlockSpec((1, TM, D), lambda b, i: (b, i, 0)), mod,
                  pl.BlockSpec(g.shape, lambda b, i: (0, 0))],
        out_specs=pl.BlockSpec((1, TM, D), lambda b, i: (b, i, 0)),
        compiler_params=_cparams(("parallel", "parallel")),
        name="final_norm",
    )(x, f, mods, g)


def _rwkv_proj_kernel(x_ref, xp_ref, xn_ref, mod_ref, g_ref, xmix_ref, wr_ref, wk_ref, wv_ref,
                      dw1_ref, dw2_ref, da1_ref, da2_ref, g1_ref, g2_ref, vec_ref, ones_ref,
                      r_ref, v_ref, kk_ref, bv_ref, gate_ref, w0_ref, w1_ref, kd0_ref, kd1_ref, bd0_ref, bd1_ref,
                      *, nt):
    i = pl.program_id(1)
    mod = mod_ref[0, 0]
    g = g_ref[...]
    nm = lambda x: _norm_mod(x, g, mod[0:1], mod[1:2])
    h = nm(x_ref[0])
    hp = nm(xp_ref[0])[7:8] * jnp.where(i >= 2, 1.0, 0.0)
    hn = nm(xn_ref[0])[0:1] * jnp.where((i >= 1) & (i < nt - 1), 1.0, 0.0)
    ridx = lax.broadcasted_iota(jnp.int32, h.shape, 0)
    h_dn = jnp.where(ridx == 0, hp, pltpu.roll(h, 1, axis=0))
    h_up = jnp.where(ridx == TM - 1, hn, pltpu.roll(h, TM - 1, axis=0))
    xx = 0.5 * (h_dn + h_up) - h
    xmix = xmix_ref[...]
    mix = lambda j: (h + xx * xmix[j:j + 1]).astype(BF16)
    vec = vec_ref[...]
    ones = ones_ref[...]

    r = _dot(mix(0), wr_ref[...])
    k = _dot(mix(2), wk_ref[...])
    v = _dot(mix(3), wv_ref[...])
    gate_ref[0] = _dot(_sigmoid(_dot(mix(5), g1_ref[...])).astype(BF16), g2_ref[...])
    kk = k * vec[0:1]
    kk = kk * lax.rsqrt(jnp.maximum(_segsum_wide(kk * kk, ones), 1e-24))
    lw = jnp.tanh(_dot(mix(1), dw1_ref[...])).astype(BF16)
    la = _dot(mix(4), da1_ref[...]).astype(BF16)
    r_ref[0] = r
    v_ref[0] = v
    kk_ref[0] = kk
    bonus = jnp.zeros_like(r)
    lora = DECAY_LORA
    for d, (w_ref, kd_ref, bd_ref) in enumerate(((w0_ref, kd0_ref, bd0_ref), (w1_ref, kd1_ref, bd1_ref))):
        z = -(vec[3 + d:4 + d] + _dot(lw[:, d * lora:(d + 1) * lora], dw2_ref[d]))
        softplus = jnp.maximum(z, 0.0) + jnp.log(1.0 + jnp.exp(-jnp.abs(z)))
        w_ref[0] = jnp.exp(-jnp.exp(-softplus - 0.5))
        iclr = _sigmoid(vec[5 + d:6 + d] + _dot(la[:, d * lora:(d + 1) * lora], da2_ref[d]))
        kd = k * (1.0 + (iclr - 1.0) * vec[1:2])
        kd_ref[0] = kd
        bd_ref[0] = kk * iclr
        bonus = bonus + _segsum_wide(r * kd * vec[2:3], ones)
    bv_ref[0] = bonus * v


DECAY_LORA = 64


def _rwkv_proj(x, mods, g, xmix, wr, wk, wv, dw1, dw2, da1, da2, g1, g2, vec, ones):
    B, T, D = x.shape
    nt = T // TM
    row, mod, full = _row_specs(D)
    r8 = TM // 8
    prev = pl.BlockSpec((1, 8, D), lambda b, i: (b, jnp.maximum(i * r8 - 1, 0), 0))
    nxt = pl.BlockSpec((1, 8, D), lambda b, i: (b, jnp.minimum((i + 1) * r8, T // 8 - 1), 0))
    out = jax.ShapeDtypeStruct((B, T, D), F32)
    return pl.pallas_call(
        functools.partial(_rwkv_proj_kernel, nt=nt),
        out_shape=(out,) * 11,
        grid=(B, nt),
        in_specs=[row(D), prev, nxt, mod, full(g), full(xmix), full(wr), full(wk), full(wv),
                  full(dw1), full(dw2), full(da1), full(da2), full(g1), full(g2), full(vec), full(ones)],
        out_specs=(row(D),) * 11,
        compiler_params=_cparams(("parallel", "parallel")),
        name="rwkv_proj",
    )(x, x, x, mods, g, xmix, wr, wk, wv, dw1, dw2, da1, da2, g1, g2, vec, ones)


def _lane_tiles_to_rows(x):
    n = x.shape[1] // LANES
    return jnp.concatenate([x[:, j * LANES:(j + 1) * LANES] for j in range(n)], axis=0)


def _rows_to_lane_tiles(x, n):
    m = x.shape[0] // n
    return jnp.concatenate([x[j * m:(j + 1) * m] for j in range(n)], axis=1)


def _scan_kernel(rf, rb, vf, vb, kkf, kkb, wf, wb, kdf, kdb, bdf, bdb, eye_ref, ones_ref, yf, yb, st, *, tc, nb):
    n = pl.program_id(0)

    @pl.when(n == 0)
    def _():
        st[...] = jnp.zeros(st.shape, F32)

    ones = ones_ref[...]
    N = HEAD_DIM
    nl = st.shape[-1] // LANES
    dirs = ((rf, vf, kkf, wf, kdf, bdf, yf), (rb, vb, kkb, wb, kdb, bdb, yb))

    def seg(x):
        res = _dot(_lane_tiles_to_rows(x.astype(BF16)), ones)
        return _rows_to_lane_tiles(res, nl)

    def extract(ybc):
        return jnp.sum(ybc * eye_ref[...], axis=0, keepdims=True)

    def step(i, carry):
        ip = jnp.maximum(i - 1, 0)
        for d, (R, V, KK, W, KD, BD, Y) in enumerate(dirs):
            row = i if d == 0 else tc - 1 - i
            rowp = ip if d == 0 else tc - 1 - ip
            for b in range(nb):
                gi = d * nb + b
                S = st[gi]
                cur = pl.ds(row, 1)
                a = -KK[b, cur, :]
                x = jnp.concatenate([S * a, S * R[b, pl.ds(rowp, 1), :], eye_ref[...] * V[b, cur, :]], axis=0)
                res = seg(x)
                sa, ybc, vcol = res[0:N], res[N:2 * N], res[2 * N:3 * N]
                st[gi] = S * W[b, cur, :] + sa * BD[b, cur, :] + vcol * KD[b, cur, :]
                Y[b, pl.ds(rowp, 1), :] = extract(ybc)
        return carry

    lax.fori_loop(0, tc, step, 0)
    for d, (R, V, KK, W, KD, BD, Y) in enumerate(dirs):
        last = tc - 1 if d == 0 else 0
        for b in range(nb):
            Y[b, last:last + 1, :] = extract(seg(st[d * nb + b] * R[b, last:last + 1, :]))


def _rwkv_scan(r, v, kk, w0, w1, kd0, kd1, bd0, bd1, eye, ones, L):
    B, T, D = r.shape
    tc = SCAN_CHUNK
    nc, nchunks = L // tc, T // tc
    fwd = pl.BlockSpec((B, tc, D), lambda n: (0, n, 0))
    rev_idx = lambda n: jnp.where(n < nc, nc - 1 - n, nchunks - 1 - (n - nc))
    rev = pl.BlockSpec((B, tc, D), lambda n: (0, rev_idx(n), 0))
    full = lambda a: pl.BlockSpec(a.shape, lambda n: (0,) * a.ndim)
    out = jax.ShapeDtypeStruct((B, T, D), F32)
    return pl.pallas_call(
        functools.partial(_scan_kernel, tc=tc, nb=B),
        out_shape=(out, out),
        grid=(nchunks,),
        in_specs=[fwd, rev, fwd, rev, fwd, rev, fwd, rev, fwd, rev, fwd, rev, full(eye), full(ones)],
        out_specs=(fwd, rev),
        scratch_shapes=[pltpu.VMEM((2 * B, HEAD_DIM, D), F32)],
        compiler_params=_cparams(("arbitrary",)),
        name="rwkv_scan",
    )(r, r, v, v, kk, kk, w0, w1, kd0, kd1, bd0, bd1, eye, ones)


def _rwkv_out_kernel(yf_ref, yb_ref, bv_ref, gate_ref, ln_ref, wo_ref, ones_ref, x_ref, mod_ref, g_ref,
                     rwh_ref, rwl_ref, xo_ref, h_ref, lg_ref):
    ones = ones_ref[...]
    y = yf_ref[0] + yb_ref[0]
    inv = 1.0 / HEAD_DIM
    dlt = y - _segsum_wide(y, ones) * inv
    yn = dlt * lax.rsqrt(_segsum_wide(dlt * dlt, ones) * inv + GN_EPS)
    ln = ln_ref[...]
    o = (yn * ln[0:1] + ln[1:2] + bv_ref[0]) * gate_ref[0]
    yl = _dot(o.astype(BF16), wo_ref[...])
    _ffn_prep(x_ref[0], yl, mod_ref[0, 0], g_ref[...], rwh_ref, rwl_ref, xo_ref, h_ref, lg_ref)


def _rwkv_out(yf, yb, bv, gate, ln, wo, ones, x, mods, g, rwh, rwl):
    B, T, D = x.shape
    row, mod, full = _row_specs(D)
    shapes, specs = _ffn_prep_outs(B, T, D)
    return pl.pallas_call(
        _rwkv_out_kernel,
        out_shape=shapes,
        grid=(B, T // TM),
        in_specs=[row(D), row(D), row(D), row(D), full(ln), full(wo), full(ones), row(D), mod, full(g),
                  full(rwh), full(rwl)],
        out_specs=specs,
        compiler_params=_cparams(("parallel", "parallel")),
        name="rwkv_out",
    )(yf, yb, bv, gate, ln, wo, ones, x, mods, g, rwh, rwl)


def _rope_tables(S, L):
    rows = S // GRID_W
    row = jnp.repeat(jnp.arange(rows, dtype=F32), GRID_W)
    col = (jnp.arange(rows * GRID_W) % GRID_W).astype(F32)
    n_freq = HEAD_DIM // 4
    inv = ROPE_THETA ** (-jnp.arange(n_freq, dtype=F32) / n_freq)
    lane = np.arange(LANES) % HEAD_DIM
    axis, half, freq = lane // 32, (lane % 32) // 16, lane % 16
    pos = jnp.where(jnp.asarray(axis == 0)[None, :], row[:, None], col[:, None])
    ang = pos * inv[freq][None, :]
    sgn = jnp.asarray(np.where(half == 0, -1.0, 1.0), dtype=F32)
    cos = jnp.concatenate([jnp.ones((L, LANES), F32), jnp.cos(ang)], axis=0)
    sin = jnp.concatenate([jnp.zeros((L, LANES), F32), jnp.sin(ang) * sgn[None, :]], axis=0)
    return cos, sin


def kernel(x, c, ctx, c_ctx, ada_w, ada_b, norm_mix_g, norm_ffn_g, attn_w_in, attn_w_out, attn_sink,
           attn_q_norm_g, attn_k_norm_g, rwkv_x_mix, rwkv_w_r, rwkv_w_k, rwkv_w_v, rwkv_w_o,
           rwkv_decay_w0, rwkv_decay_w1, rwkv_decay_w2, rwkv_iclr_a0, rwkv_iclr_a1, rwkv_iclr_a2,
           rwkv_gate_g1, rwkv_gate_g2, rwkv_k_k, rwkv_k_a, rwkv_r_k, rwkv_ln_g, rwkv_ln_b,
           router_w, router_bias, moe_w1, moe_w3, moe_w2, final_norm_g):
    B, S, D = x.shape
    L = ctx.shape[1]
    T = L + S
    depth = ada_w.shape[0]
    assert D == D_MODEL and L == TM and S % TM == 0 and B == 2 and depth == 2
    ones = _seg_ones()
    bf = lambda a: a.astype(BF16)

    cs = jnp.zeros((8, D), F32).at[:B].set(c).at[B].set(c_ctx)
    ada = _ada(cs, ada_w, ada_b).reshape(depth, 8, 6, D)
    mods = [jnp.stack([jnp.broadcast_to(ada[i, B], (B, 6, D)), ada[i, :B]], axis=1) for i in range(depth)]

    xa = jnp.concatenate([ctx, x], axis=1)
    rw = jnp.zeros((D, LANES), F32).at[:, :N_EXPERTS].set(router_w)
    rwh, rwl = _split(rw)

    w_in = attn_w_in[0]
    roped = np.concatenate([np.arange(0, 640), np.arange(768, 1408)])
    w_rot = w_in[:, roped ^ 16]
    cos, sin = _rope_tables(S, L)
    lane = np.arange(LANES) % HEAD_DIM
    gains = lambda g: jnp.stack([g[lane], g[lane ^ 16]], axis=0)
    qa, ka, va, qb, kb, vb = _inproj(xa, mods[0], norm_mix_g[0].reshape(1, D), bf(w_in), bf(w_rot), cos, sin,
                                     gains(attn_q_norm_g[0]), gains(attn_k_norm_g[0]), ones)
    grouped = lambda q: q.reshape(B, A_KV_HEADS, GROUP, T, HEAD_DIM)
    qa, qb = grouped(qa), grouped(qb)
    sink = attn_sink[0].astype(F32) * LOG2E
    ext = lambda v: jnp.swapaxes(
        jnp.concatenate([v, jnp.ones_like(v[..., :1]), jnp.zeros_like(v[..., :HEAD_DIM - 1])], axis=-1), 2, 3)
    va_x, vb_x = ext(va), ext(vb)
    nosink = jnp.full((B_Q_HEADS,), NEG, F32)
    oa_l = _window_attn(sink, qa, ka, va, L, S)
    oa_c = _flash(sink, qa, ka, va_x, q_rows=L, q_off=0, k_rows=L, tq=L, tk=L)
    ob_l = _flash(nosink, qb, kb, vb_x, q_rows=S, q_off=L, k_rows=T, tq=256, tk=_key_tile(T))
    ob_c = _flash(nosink, qb, kb, vb_x, q_rows=L, q_off=0, k_rows=L, tq=L, tk=L)
    oa = jnp.concatenate([oa_c, oa_l], axis=1)
    ob = jnp.concatenate([ob_c, ob_l], axis=1)
    w_out = bf(attn_w_out[0])
    na = A_Q_HEADS * HEAD_DIM
    xa, h, lg = _attn_out(oa, ob, w_out[:na], w_out[na:], xa, mods[0], norm_ffn_g[0].reshape(1, D), rwh, rwl)
    f = _moe(h.reshape(B * T, D), lg.reshape(B * T, LANES), router_bias,
             bf(moe_w1[0]), bf(moe_w3[0]), bf(moe_w2[0]))
    xa = _residual(xa, f.reshape(B, T, D), mods[0])

    cat2 = lambda a: jnp.concatenate([a[0], a[1]], axis=1)
    vec = jnp.stack([rwkv_k_k[0], rwkv_k_a[0], rwkv_r_k[0].reshape(D), rwkv_decay_w0[0, 0], rwkv_decay_w0[0, 1],
                     rwkv_iclr_a0[0, 0], rwkv_iclr_a0[0, 1], jnp.zeros((D,), F32)], axis=0)
    outs = _rwkv_proj(xa, mods[1], norm_mix_g[1].reshape(1, D), jnp.pad(rwkv_x_mix[0], ((0, 2), (0, 0))),
                      bf(rwkv_w_r[0]), bf(rwkv_w_k[0]), bf(rwkv_w_v[0]),
                      bf(cat2(rwkv_decay_w1[0])), bf(rwkv_decay_w2[0]),
                      bf(cat2(rwkv_iclr_a1[0])), bf(rwkv_iclr_a2[0]),
                      bf(rwkv_gate_g1[0]), bf(rwkv_gate_g2[0]), vec, ones)
    r, v, kk, bv, gate, w0, w1, kd0, kd1, bd0, bd1 = outs
    vi = np.arange(HEAD_DIM)
    eye = jnp.asarray(vi[:, None] == (np.arange(D) % HEAD_DIM)[None, :], dtype=F32)
    yf, yb = _rwkv_scan(r, v, kk, w0, w1, kd0, kd1, bd0, bd1, eye, ones, L)
    ln = jnp.stack([rwkv_ln_g[0], rwkv_ln_b[0]] + [jnp.zeros((D,), F32)] * 6, axis=0)
    xa, h, lg = _rwkv_out(yf, yb, bv, gate, ln, bf(rwkv_w_o[0]), ones, xa, mods[1],
                          norm_ffn_g[1].reshape(1, D), rwh, rwl)
    f = _moe(h[:, L:].reshape(B * S, D), lg[:, L:].reshape(B * S, LANES), router_bias,
             bf(moe_w1[1]), bf(moe_w3[1]), bf(moe_w2[1]))
    return _final(xa, f.reshape(B, S, D), mods[1], final_norm_g.reshape(1, D), L)


def _key_tile(T):
    for tk in (1280, 1024, 768, 512, 256):
        if T % tk == 0:
            return tk
    raise ValueError(T)
```

```python
import functools

import numpy as np
import jax
import jax.numpy as jnp
from jax import lax
from jax.experimental import pallas as pl
from jax.experimental.pallas import tpu as pltpu

F32 = jnp.float32
BF16 = jnp.bfloat16

D_MODEL = 1024
HEAD_DIM = 64
GRID_W = 64
ROPE_THETA = 10000.0
RMS_EPS = 1e-6
GN_EPS = 64e-5
A_Q_HEADS = 8
A_KV_HEADS = 2
B_Q_HEADS = 8
B_KV_HEADS = 2
GROUP = 4
WINDOW = 128
N_EXPERTS = 16
N_GROUPS = 4
EXPERTS_PER_GROUP = 4
TOP_K = 2
LANES = 128
TM = 256
MOE_ROWS = 512
SCAN_CHUNK = 64
VMEM_LIMIT = 56 * 1024 * 1024
NEG = -1e30
LOG2E = 1.4426950408889634


def _cparams(sem):
    return pltpu.CompilerParams(dimension_semantics=sem, vmem_limit_bytes=VMEM_LIMIT)


def _dot(a, b):
    return jnp.dot(a, b, preferred_element_type=F32)


def _dot_nt(a, b):
    return lax.dot_general(a, b, (((1,), (1,)), ((), ())), preferred_element_type=F32)


def _split(x):
    hi = x.astype(BF16)
    lo = (x - hi.astype(F32)).astype(BF16)
    return hi, lo


def _dot3(x, w):
    xh, xl = _split(x)
    wh, wl = _split(w)
    return _dot(xh, wh) + _dot(xh, wl) + _dot(xl, wh)


def _segsum(v, ones):
    hi, lo = _split(v)
    return _dot(hi, ones) + _dot(lo, ones)


def _segsum_wide(v, ones):
    n = v.shape[1] // LANES
    return jnp.concatenate([_segsum(v[:, j * LANES:(j + 1) * LANES], ones) for j in range(n)], axis=1)


def _norm_mod(x, g, shift, scale):
    ms = jnp.mean(x * x, axis=-1, keepdims=True)
    return (x * lax.rsqrt(ms + RMS_EPS) * g) * (1.0 + scale) + shift


def _sigmoid(x):
    return 1.0 / (1.0 + jnp.exp(-x))


def _seg_ones():
    i = np.arange(LANES)
    return jnp.asarray((i[:, None] // HEAD_DIM) == (i[None, :] // HEAD_DIM), dtype=BF16)


def _ada_kernel(c_ref, w_ref, b_ref, o_ref):
    c = c_ref[...]
    s = c * _sigmoid(c)
    o_ref[0] = _dot3(s, w_ref[0]) + b_ref[0]


def _ada(cs, ada_w, ada_b):
    depth, d, n = ada_w.shape
    tn = 1536
    return pl.pallas_call(
        _ada_kernel,
        out_shape=jax.ShapeDtypeStruct((depth, 8, n), F32),
        grid=(depth, n // tn),
        in_specs=[
            pl.BlockSpec((8, d), lambda l, j: (0, 0)),
            pl.BlockSpec((1, d, tn), lambda l, j: (l, 0, j)),
            pl.BlockSpec((1, 1, tn), lambda l, j: (l, 0, j)),
        ],
        out_specs=pl.BlockSpec((1, 8, tn), lambda l, j: (l, 0, j)),
        compiler_params=_cparams(("arbitrary", "arbitrary")),
        name="ada",
    )(cs, ada_w, ada_b.reshape(depth, 1, n))


def _inproj_kernel(x_ref, mod_ref, g_ref, w_ref, wrot_ref, cos_ref, sin_ref, gq_ref, gk_ref, ones_ref,
                   qa_ref, ka_ref, va_ref, qb_ref, kb_ref, vb_ref):
    mod = mod_ref[0, 0]
    h = _norm_mod(x_ref[0], g_ref[...], mod[0:1], mod[1:2]).astype(BF16)
    y = _dot(h, w_ref[...])
    yr = _dot(h, wrot_ref[...])
    cos = cos_ref[...]
    sin = sin_ref[...]
    ones = ones_ref[...]
    qscale = HEAD_DIM ** -0.5 * LOG2E

    def put(ref, tile, val):
        ref[0, 2 * tile] = val[:, :HEAD_DIM].astype(ref.dtype)
        ref[0, 2 * tile + 1] = val[:, HEAD_DIM:].astype(ref.dtype)

    def chunk(a, c):
        return a[:, c * LANES:(c + 1) * LANES]

    for c in range(4):
        put(qa_ref, c, (chunk(y, c) * cos + chunk(yr, c) * sin) * qscale)
    put(ka_ref, 0, chunk(y, 4) * cos + chunk(yr, 4) * sin)
    put(va_ref, 0, chunk(y, 5))

    def normed(c, cr, gain_ref):
        v = chunk(y, c)
        rs = lax.rsqrt(_segsum(v * v, ones) * (1.0 / HEAD_DIM) + RMS_EPS)
        return (v * rs * gain_ref[0:1]) * cos + (chunk(yr, cr) * rs * gain_ref[1:2]) * sin

    for c in range(4):
        put(qb_ref, c, normed(6 + c, 5 + c, gq_ref) * qscale)
    put(kb_ref, 0, normed(10, 9, gk_ref))
    put(vb_ref, 0, chunk(y, 11))


def _inproj(x, mods, g, w_in, w_rot, cos, sin, gq2, gk2, ones):
    B, T, D = x.shape
    nt = T // TM
    heads = lambda n: jax.ShapeDtypeStruct((B, n, T, HEAD_DIM), BF16)
    hspec = lambda n: pl.BlockSpec((1, n, TM, HEAD_DIM), lambda b, i: (b, 0, i, 0))
    full = lambda a: pl.BlockSpec(a.shape, lambda b, i: (0,) * a.ndim)
    return pl.pallas_call(
        _inproj_kernel,
        out_shape=(heads(8), heads(2), heads(2), heads(8), heads(2), heads(2)),
        grid=(B, nt),
        in_specs=[
            pl.BlockSpec((1, TM, D), lambda b, i: (b, i, 0)),
            pl.BlockSpec((1, 1, 6, D), lambda b, i: (b, jnp.minimum(i, 1), 0, 0)),
            full(g), full(w_in), full(w_rot),
            pl.BlockSpec((TM, LANES), lambda b, i: (i, 0)),
            pl.BlockSpec((TM, LANES), lambda b, i: (i, 0)),
            full(gq2), full(gk2), full(ones),
        ],
        out_specs=(hspec(8), hspec(2), hspec(2), hspec(8), hspec(2), hspec(2)),
        compiler_params=_cparams(("parallel", "parallel")),
        name="attn_inproj",
    )(x, mods, g, w_in, w_rot, cos, sin, gq2, gk2, ones)


LOOKAHEAD = 2


def _flash_kernel(sink_ref, q_ref, k_ref, v_ref, o_ref, m_scr, acc_scr, s_scr, *, tk, nk):
    h = pl.program_id(1)
    m_scr[...] = jnp.full(m_scr.shape, NEG, F32)
    acc_scr[...] = jnp.zeros(acc_scr.shape, F32)

    def scores(j, g):
        return _dot_nt(k_ref[0, 0, pl.ds(pl.multiple_of(j * tk, tk), tk), :], q_ref[0, 0, g])

    for g in range(LOOKAHEAD):
        s_scr[g] = scores(0, g)

    def body(j, carry):
        vt = v_ref[0, 0, :, pl.ds(pl.multiple_of(j * tk, tk), tk)]
        jn = jnp.minimum(j + 1, nk - 1)
        ahead = {}
        for g in range(GROUP):
            st = s_scr[g] if g < LOOKAHEAD else ahead.pop(g)
            if g + LOOKAHEAD < GROUP:
                ahead[g + LOOKAHEAD] = scores(j, g + LOOKAHEAD)
            m_prev = m_scr[g]
            m_new = jnp.maximum(m_prev, jnp.max(st, axis=0, keepdims=True))
            p = jnp.exp2(st - m_new).astype(BF16)
            if g + LOOKAHEAD >= GROUP:
                s_scr[g + LOOKAHEAD - GROUP] = scores(jn, g + LOOKAHEAD - GROUP)
            acc_scr[g] = jnp.exp2(m_prev - m_new) * acc_scr[g] + _dot(vt, p)
            m_scr[g] = m_new
        return carry

    lax.fori_loop(0, nk, body, 0)
    outs = []
    for g in range(GROUP):
        acc = acc_scr[g]
        l = acc[HEAD_DIM:HEAD_DIM + 1] + jnp.exp2(sink_ref[h * GROUP + g] - m_scr[g])
        outs.append(acc[:HEAD_DIM] / l)
    o_ref[0] = jnp.concatenate(outs, axis=0).T.astype(o_ref.dtype)


def _flash(sink, q, k, v, *, q_rows, q_off, k_rows, tq, tk):
    B, Hkv = k.shape[:2]
    nq, nk = q_rows // tq, k_rows // tk
    qo = q_off // tq
    return pl.pallas_call(
        functools.partial(_flash_kernel, tk=tk, nk=nk),
        out_shape=jax.ShapeDtypeStruct((B, q_rows, Hkv * GROUP * HEAD_DIM), BF16),
        grid=(B, Hkv, nq),
        in_specs=[
            pl.BlockSpec(memory_space=pltpu.SMEM),
            pl.BlockSpec((1, 1, GROUP, tq, HEAD_DIM), lambda b, h, i: (b, h, 0, i + qo, 0)),
            pl.BlockSpec((1, 1, k_rows, HEAD_DIM), lambda b, h, i: (b, h, 0, 0)),
            pl.BlockSpec((1, 1, LANES, k_rows), lambda b, h, i: (b, h, 0, 0)),
        ],
        out_specs=pl.BlockSpec((1, tq, GROUP * HEAD_DIM), lambda b, h, i: (b, i, h)),
        scratch_shapes=[
            pltpu.VMEM((GROUP, 1, tq), F32),
            pltpu.VMEM((GROUP, LANES, tq), F32),
            pltpu.VMEM((LOOKAHEAD, tk, tq), F32),
        ],
        compiler_params=_cparams(("parallel", "parallel", "arbitrary")),
        name="flash_attn",
    )(sink, q, k, v)


def _window_kernel(sink_ref, q_ref, kc_ref, vc_ref, k0_ref, k1_ref, k2_ref, v0_ref, v1_ref, v2_ref, o_ref, *, nb):
    h = pl.program_id(1)
    i = pl.program_id(2)
    rows = GROUP * WINDOW
    q = q_ref[0, 0].reshape(rows, HEAD_DIM)
    r = lax.broadcasted_iota(jnp.int32, (rows, WINDOW), 0) & (WINDOW - 1)
    c = lax.broadcasted_iota(jnp.int32, (rows, WINDOW), 1)
    sc = _dot_nt(q, kc_ref[0, 0])
    s0 = jnp.where((c >= r) & (i > 0), _dot_nt(q, k0_ref[0, 0]), NEG)
    s1 = _dot_nt(q, k1_ref[0, 0])
    s2 = jnp.where((c <= r) & (i < nb - 1), _dot_nt(q, k2_ref[0, 0]), NEG)
    sink = jnp.concatenate(
        [jnp.full((WINDOW, 1), sink_ref[h * GROUP + g], F32) for g in range(GROUP)], axis=0)
    rowmax = lambda s: jnp.max(s, axis=-1, keepdims=True)
    m = jnp.maximum(jnp.maximum(rowmax(sc), rowmax(s0)), jnp.maximum(rowmax(s1), rowmax(s2)))
    m = jnp.maximum(m, sink)
    pc, p0, p1, p2 = (jnp.exp2(s - m) for s in (sc, s0, s1, s2))
    rowsum = lambda p: jnp.sum(p, axis=-1, keepdims=True)
    l = rowsum(pc) + rowsum(p0) + rowsum(p1) + rowsum(p2) + jnp.exp2(sink - m)
    acc = (_dot(pc.astype(BF16), vc_ref[0, 0]) + _dot(p0.astype(BF16), v0_ref[0, 0])
           + _dot(p1.astype(BF16), v1_ref[0, 0]) + _dot(p2.astype(BF16), v2_ref[0, 0]))
    out = acc / l
    for g in range(GROUP):
        o_ref[0, :, g * HEAD_DIM:(g + 1) * HEAD_DIM] = out[g * WINDOW:(g + 1) * WINDOW].astype(o_ref.dtype)


def _window_attn(sink, q, k, v, L, S):
    B, Hkv = k.shape[:2]
    nb = S // WINDOW
    pad = ((0, 0), (0, 0), (WINDOW, WINDOW), (0, 0))
    kp = jnp.pad(k[:, :, L:], pad)
    vp = jnp.pad(v[:, :, L:], pad)
    qo = L // WINDOW
    band = lambda j: pl.BlockSpec((1, 1, WINDOW, HEAD_DIM), lambda b, h, i: (b, h, i + j, 0))
    ctx = pl.BlockSpec((1, 1, L, HEAD_DIM), lambda b, h, i: (b, h, 0, 0))
    return pl.pallas_call(
        functools.partial(_window_kernel, nb=nb),
        out_shape=jax.ShapeDtypeStruct((B, S, Hkv * GROUP * HEAD_DIM), BF16),
        grid=(B, Hkv, nb),
        in_specs=[
            pl.BlockSpec(memory_space=pltpu.SMEM),
            pl.BlockSpec((1, 1, GROUP, WINDOW, HEAD_DIM), lambda b, h, i: (b, h, 0, i + qo, 0)),
            ctx, ctx, band(0), band(1), band(2), band(0), band(1), band(2),
        ],
        out_specs=pl.BlockSpec((1, WINDOW, GROUP * HEAD_DIM), lambda b, h, i: (b, i, h)),
        compiler_params=_cparams(("parallel", "parallel", "parallel")),
        name="window_attn",
    )(sink, q, k, v, kp, kp, kp, vp, vp, vp)


def _ffn_prep(x, y, mod, gffn, rwh_ref, rwl_ref, xo_ref, h_ref, lg_ref):
    xn = x + mod[2:3] * y
    h = _norm_mod(xn, gffn, mod[3:4], mod[4:5])
    xo_ref[0] = xn
    hh, hl = _split(h)
    h_ref[0] = hh
    rwh = rwh_ref[...]
    lg_ref[0] = _dot(hh, rwh) + _dot(hl, rwh) + _dot(hh, rwl_ref[...])


def _attn_out_kernel(oa_ref, ob_ref, wa_ref, wb_ref, x_ref, mod_ref, g_ref, rwh_ref, rwl_ref,
                     xo_ref, h_ref, lg_ref):
    y = _dot(oa_ref[0], wa_ref[...]) + _dot(ob_ref[0], wb_ref[...])
    _ffn_prep(x_ref[0], y, mod_ref[0, 0], g_ref[...], rwh_ref, rwl_ref, xo_ref, h_ref, lg_ref)


def _row_specs(D):
    row = lambda w: pl.BlockSpec((1, TM, w), lambda b, i: (b, i, 0))
    mod = pl.BlockSpec((1, 1, 6, D), lambda b, i: (b, jnp.minimum(i, 1), 0, 0))
    full = lambda a: pl.BlockSpec(a.shape, lambda b, i: (0,) * a.ndim)
    return row, mod, full


def _ffn_prep_outs(B, T, D):
    row, _, _ = _row_specs(D)
    shapes = (jax.ShapeDtypeStruct((B, T, D), F32), jax.ShapeDtypeStruct((B, T, D), BF16),
              jax.ShapeDtypeStruct((B, T, LANES), F32))
    return shapes, (row(D), row(D), row(LANES))


def _attn_out(oa, ob, wa, wb, x, mods, g, rwh, rwl):
    B, T, D = x.shape
    row, mod, full = _row_specs(D)
    shapes, specs = _ffn_prep_outs(B, T, D)
    return pl.pallas_call(
        _attn_out_kernel,
        out_shape=shapes,
        grid=(B, T // TM),
        in_specs=[row(oa.shape[-1]), row(ob.shape[-1]), full(wa), full(wb), row(D), mod, full(g),
                  full(rwh), full(rwl)],
        out_specs=specs,
        compiler_params=_cparams(("parallel", "parallel")),
        name="attn_out",
    )(oa, ob, wa, wb, x, mods, g, rwh, rwl)


def _gmm_kernel(be_ref, nu_ref, x_ref, w1_ref, w3_ref, w2_ref, o_ref):
    i = pl.program_id(0)

    @pl.when(i < nu_ref[0])
    def _():
        x = x_ref[...]
        a = _dot(x, w1_ref[0])
        b = _dot(x, w3_ref[0])
        mid = (a * _sigmoid(a)) * b
        o_ref[...] = _dot(mid.astype(BF16), w2_ref[0])

    @pl.when(i >= nu_ref[0])
    def _():
        o_ref[...] = jnp.zeros(o_ref.shape, o_ref.dtype)


def _gmm(block_expert, n_used, xs, w1, w3, w2):
    n_slots, D = xs.shape
    F = w1.shape[-1]
    nblk = n_slots // MOE_ROWS
    return pl.pallas_call(
        _gmm_kernel,
        out_shape=jax.ShapeDtypeStruct((n_slots, D), F32),
        grid_spec=pltpu.PrefetchScalarGridSpec(
            num_scalar_prefetch=2,
            grid=(nblk,),
            in_specs=[
                pl.BlockSpec((MOE_ROWS, D), lambda i, be, nu: (i, 0)),
                pl.BlockSpec((1, D, F), lambda i, be, nu: (be[i], 0, 0)),
                pl.BlockSpec((1, D, F), lambda i, be, nu: (be[i], 0, 0)),
                pl.BlockSpec((1, F, D), lambda i, be, nu: (be[i], 0, 0)),
            ],
            out_specs=pl.BlockSpec((MOE_ROWS, D), lambda i, be, nu: (i, 0)),
        ),
        compiler_params=_cparams(("arbitrary",)),
        name="moe_gmm",
    )(block_expert, n_used, xs, w1, w3, w2)


ROUTE_ROWS = 512


def _route_kernel(lg_ref, bias_ref, idx_ref, w_ref):
    x = lg_ref[...].T[:N_EXPERTS]
    m = jnp.max(x, axis=0, keepdims=True)
    e = jnp.exp(x - m)
    probs = e / jnp.sum(e, axis=0, keepdims=True)
    sel = probs + bias_ref[...][:, 0:1]
    row = lambda a, i: a[i:i + 1, :]
    G = EXPERTS_PER_GROUP
    scores = []
    for g in range(N_GROUPS):
        s = [row(sel, g * G + i) for i in range(G)]
        best = None
        for i in range(G):
            for j in range(i + 1, G):
                best = s[i] + s[j] if best is None else jnp.maximum(best, s[i] + s[j])
        scores.append(best)
    top = functools.reduce(jnp.maximum, scores)
    gi = jnp.full(top.shape, N_GROUPS - 1, jnp.int32)
    for g in range(N_GROUPS - 2, -1, -1):
        gi = jnp.where(scores[g] == top, g, gi)

    def pick(a, i):
        out = row(a, (N_GROUPS - 1) * G + i)
        for g in range(N_GROUPS - 2, -1, -1):
            out = jnp.where(gi == g, row(a, g * G + i), out)
        return out

    c = [pick(sel, i) for i in range(G)]
    pc = [pick(probs, i) for i in range(G)]

    def first_argmax(vals):
        mx = functools.reduce(jnp.maximum, vals)
        idx = jnp.full(mx.shape, G - 1, jnp.int32)
        for i in range(G - 2, -1, -1):
            idx = jnp.where(vals[i] == mx, i, idx)
        return idx

    i1 = first_argmax(c)
    i2 = first_argmax([jnp.where(i1 == i, -jnp.inf, c[i]) for i in range(G)])
    take = lambda vals, idx: functools.reduce(
        lambda acc, i: jnp.where(idx == i, vals[i], acc), range(G - 2, -1, -1), vals[G - 1])
    w1, w2 = take(pc, i1), take(pc, i2)
    tot = w1 + w2
    zi = jnp.zeros((6,) + top.shape[1:], jnp.int32)
    idx_ref[...] = jnp.concatenate([gi * G + i1, gi * G + i2, zi], axis=0)
    w_ref[...] = jnp.concatenate([w1 / tot, w2 / tot, zi.astype(F32)], axis=0)


def _route(logits, router_bias):
    N = logits.shape[0]
    bias = jnp.broadcast_to(router_bias.astype(F32)[:, None], (N_EXPERTS, LANES))
    idx, w = pl.pallas_call(
        _route_kernel,
        out_shape=(jax.ShapeDtypeStruct((8, N), jnp.int32), jax.ShapeDtypeStruct((8, N), F32)),
        grid=(N // ROUTE_ROWS,),
        in_specs=[pl.BlockSpec((ROUTE_ROWS, LANES), lambda i: (i, 0)),
                  pl.BlockSpec((N_EXPERTS, LANES), lambda i: (0, 0))],
        out_specs=(pl.BlockSpec((8, ROUTE_ROWS), lambda i: (0, i)),
                   pl.BlockSpec((8, ROUTE_ROWS), lambda i: (0, i))),
        compiler_params=_cparams(("parallel",)),
        name="route",
    )(logits, bias)
    return idx[:TOP_K].T, w[:TOP_K].T


def _moe(h, logits, router_bias, w1, w3, w2):
    N, D = h.shape
    expert_idx, gate_w = _route(logits, router_bias)
    NK = N * TOP_K
    flat_e = expert_idx.reshape(NK)
    onehot = (flat_e[:, None] == jnp.arange(N_EXPERTS, dtype=jnp.int32)[None, :]).astype(jnp.int32)
    csum = jnp.cumsum(onehot, axis=0)
    counts = csum[-1]
    rank = jnp.take_along_axis(csum, flat_e[:, None], axis=1)[:, 0] - 1
    padded = (counts + MOE_ROWS - 1) // MOE_ROWS * MOE_ROWS
    pad_end = jnp.cumsum(padded)
    pad_start = pad_end - padded
    dest =pad_start[flat_e] + rank
    nblk = -(-NK // MOE_ROWS) + N_EXPERTS
    n_slots = nblk * MOE_ROWS
    n_used = (pad_end[-1] // MOE_ROWS).astype(jnp.int32)
    blk = jnp.arange(nblk, dtype=jnp.int32)
    be = jnp.sum((pad_end[None, :] <= (blk * MOE_ROWS)[:, None]).astype(jnp.int32), axis=1)
    be = jnp.minimum(be, N_EXPERTS - 1)
    be = jnp.where(blk < n_used, be, be[jnp.maximum(n_used - 1, 0)])
    flat_tok = jnp.arange(NK, dtype=jnp.int32) // TOP_K
    slot_tok = jnp.zeros((n_slots,), jnp.int32).at[dest].set(flat_tok, unique_indices=True)
    xs = h[slot_tok]
    ys = _gmm(be, n_used.reshape(1), xs, w1, w3, w2)
    pos = dest.reshape(N, TOP_K)
    return ys[pos[:, 0]] * gate_w[:, 0:1] + ys[pos[:, 1]] * gate_w[:, 1:2]


def _residual_kernel(x_ref, f_ref, mod_ref, o_ref):
    o_ref[0] = x_ref[0] + mod_ref[0, 0][5:6] * f_ref[0]


def _final_kernel(x_ref, f_ref, mod_ref, g_ref, o_ref):
    x = x_ref[0] + mod_ref[0, 0][5:6] * f_ref[0]
    ms = jnp.mean(x * x, axis=-1, keepdims=True)
    o_ref[0] = x * lax.rsqrt(ms + RMS_EPS) * g_ref[...]


def _residual(x, f, mods):
    B, T, D = x.shape
    row, mod, _ = _row_specs(D)
    return pl.pallas_call(
        _residual_kernel,
        out_shape=jax.ShapeDtypeStruct((B, T, D), F32),
        grid=(B, T // TM),
        in_specs=[row(D), row(D), mod],
        out_specs=row(D),
        compiler_params=_cparams(("parallel", "parallel")),
        name="residual",
    )(x, f, mods)


def _final(x, f, mods, g, L):
    B, S, D = f.shape
    off = L // TM
    mod = pl.BlockSpec((1, 1, 6, D), lambda b, i: (b, 1, 0, 0))
    return pl.pallas_call(
        _final_kernel,
        out_shape=jax.ShapeDtypeStruct((B, S, D), F32),
        grid=(B, S // TM),
        in_specs=[pl.BlockSpec((1, TM, D), lambda b, i: (b, i + off, 0)),
                  pl.BlockSpec((1, TM, D), lambda b, i: (b, i, 0)), mod,
                  pl.BlockSpec(g.shape, lambda b, i: (0, 0))],
        out_specs=pl.BlockSpec((1, TM, D), lambda b, i: (b, i, 0)),
        compiler_params=_cparams(("parallel", "parallel")),
        name="final_norm",
    )(x, f, mods, g)


def _rwkv_proj_kernel(x_ref, xp_ref, xn_ref, mod_ref, g_ref, xmix_ref, wr_ref, wk_ref, wv_ref,
                      dw1_ref, dw2_ref, da1_ref, da2_ref, g1_ref, g2_ref, vec_ref, ones_ref,
                      r_ref, v_ref, kk_ref, bv_ref, gate_ref, w0_ref, w1_ref, kd0_ref, kd1_ref, bd0_ref, bd1_ref,
                      *, nt):
    i = pl.program_id(1)
    mod = mod_ref[0, 0]
    g = g_ref[...]
    nm = lambda x: _norm_mod(x, g, mod[0:1], mod[1:2])
    h = nm(x_ref[0])
    hp = nm(xp_ref[0])[7:8] * jnp.where(i >= 2, 1.0, 0.0)
    hn = nm(xn_ref[0])[0:1] * jnp.where((i >= 1) & (i < nt - 1), 1.0, 0.0)
    ridx = lax.broadcasted_iota(jnp.int32, h.shape, 0)
    h_dn = jnp.where(ridx == 0, hp, pltpu.roll(h, 1, axis=0))
    h_up = jnp.where(ridx == TM - 1, hn, pltpu.roll(h, TM - 1, axis=0))
    xx = 0.5 * (h_dn + h_up) - h
    xmix = xmix_ref[...]
    mix = lambda j: (h + xx * xmix[j:j + 1]).astype(BF16)
    vec = vec_ref[...]
    ones = ones_ref[...]

    r = _dot(mix(0), wr_ref[...])
    k = _dot(mix(2), wk_ref[...])
    v = _dot(mix(3), wv_ref[...])
    gate_ref[0] = _dot(_sigmoid(_dot(mix(5), g1_ref[...])).astype(BF16), g2_ref[...])
    kk = k * vec[0:1]
    kk = kk * lax.rsqrt(jnp.maximum(_segsum_wide(kk * kk, ones), 1e-24))
    lw = jnp.tanh(_dot(mix(1), dw1_ref[...])).astype(BF16)
    la = _dot(mix(4), da1_ref[...]).astype(BF16)
    r_ref[0] = r
    v_ref[0] = v
    kk_ref[0] = kk
    bonus = jnp.zeros_like(r)
    lora = DECAY_LORA
    for d, (w_ref, kd_ref, bd_ref) in enumerate(((w0_ref, kd0_ref, bd0_ref), (w1_ref, kd1_ref, bd1_ref))):
        z = -(vec[3 + d:4 + d] + _dot(lw[:, d * lora:(d + 1) * lora], dw2_ref[d]))
        softplus = jnp.maximum(z, 0.0) + jnp.log(1.0 + jnp.exp(-jnp.abs(z)))
        w_ref[0] = jnp.exp(-jnp.exp(-softplus - 0.5))
        iclr = _sigmoid(vec[5 + d:6 + d] + _dot(la[:, d * lora:(d + 1) * lora], da2_ref[d]))
        kd = k * (1.0 + (iclr - 1.0) * vec[1:2])
        kd_ref[0] = kd
        bd_ref[0] = kk * iclr
        bonus = bonus + _segsum_wide(r * kd * vec[2:3], ones)
    bv_ref[0] = bonus * v


DECAY_LORA = 64


def _rwkv_proj(x, mods, g, xmix, wr, wk, wv, dw1, dw2, da1, da2, g1, g2, vec, ones):
    B, T, D = x.shape
    nt = T // TM
    row, mod, full = _row_specs(D)
    r8 = TM // 8
    prev = pl.BlockSpec((1, 8, D), lambda b, i: (b, jnp.maximum(i * r8 - 1, 0), 0))
    nxt = pl.BlockSpec((1, 8, D), lambda b, i: (b, jnp.minimum((i + 1) * r8, T // 8 - 1), 0))
    out = jax.ShapeDtypeStruct((B, T, D), F32)
    return pl.pallas_call(
        functools.partial(_rwkv_proj_kernel, nt=nt),
        out_shape=(out,) * 11,
        grid=(B, nt),
        in_specs=[row(D), prev, nxt, mod, full(g), full(xmix), full(wr), full(wk), full(wv),
                  full(dw1), full(dw2), full(da1), full(da2), full(g1), full(g2), full(vec), full(ones)],
        out_specs=(row(D),) * 11,
        compiler_params=_cparams(("parallel", "parallel")),
        name="rwkv_proj",
    )(x, x, x, mods, g, xmix, wr, wk, wv, dw1, dw2, da1, da2, g1, g2, vec, ones)


SCAN_COLS = 48


def _scan_kernel(rf, rb, vf, vb, kkf, kkb, wf, wb, kdf, kdb, bdf, bdb, mask_ref, eye_ref, yf, yb, st, *, tc, nb):
    n = pl.program_id(0)

    @pl.when(n == 0)
    def _():
        st[...] = jnp.zeros(st.shape, F32)

    N = HEAD_DIM
    H = mask_ref.shape[0]
    dirs = ((rf, vf, kkf, wf, kdf, bdf, yf), (rb, vb, kkb, wb, kdb, bdb, yb))
    lane = lax.broadcasted_iota(jnp.int32, (N, SCAN_COLS), 1)
    zeros = jnp.zeros((H, st.shape[-1]), BF16)
    spread = lambda x: (mask_ref[...] * x).astype(BF16)

    def step(i, carry):
        ip = jnp.maximum(i - 1, 0)
        work = []
        for d, (R, V, KK, W, KD, BD, Y) in enumerate(dirs):
            cur = pl.ds(i if d == 0 else tc - 1 - i, 1)
            prev = pl.ds(ip if d == 0 else tc - 1 - ip, 1)
            for b in range(nb):
                work.append((d * nb + b, b, cur, prev, R, V, KK, W, KD, BD, Y))
        firsts = []
        for gi, b, cur, prev, R, V, KK, W, KD, BD, Y in work:
            lhs = jnp.concatenate([st[gi].astype(BF16), eye_ref[...]], axis=0)
            w1 = jnp.concatenate([spread(-KK[b, cur, :]), spread(V[b, cur, :]), spread(R[b, prev, :])], axis=0)
            firsts.append(_dot_nt(lhs, w1))
        for (gi, b, cur, prev, R, V, KK, W, KD, BD, Y), out in zip(work, firsts):
            top = out[:N]
            Y[b, prev] = top[None]
            lhs = jnp.where(lane < H, top, out[N:]).astype(BF16)
            w2 = jnp.concatenate([spread(BD[b, cur, :]), spread(KD[b, cur, :]), zeros], axis=0)
            st[gi] = st[gi] * W[b, cur, :] + _dot(lhs, w2)
        return carry

    lax.fori_loop(0, tc, step, 0)
    for d, (R, V, KK, W, KD, BD, Y) in enumerate(dirs):
        last = tc - 1 if d == 0 else 0
        for b in range(nb):
            w1 = jnp.concatenate([zeros, zeros, spread(R[b, last:last + 1, :])], axis=0)
            Y[b, last] = _dot_nt(st[d * nb + b].astype(BF16), w1)


def _rwkv_scan(r, v, kk, w0, w1, kd0, kd1, bd0, bd1, mask, eye, L):
    B, T, D = r.shape
    tc = SCAN_CHUNK
    nc, nchunks = L // tc, T // tc
    fwd_idx = lambda n: n
    rev_idx = lambda n: jnp.where(n < nc, nc - 1 - n, nchunks - 1 - (n - nc))
    fwd = pl.BlockSpec((B, tc, D), lambda n: (0, n, 0))
    rev = pl.BlockSpec((B, tc, D), lambda n: (0, rev_idx(n), 0))
    yspec = lambda idx: pl.BlockSpec((B, tc, HEAD_DIM, SCAN_COLS), lambda n: (0, idx(n), 0, 0))
    full = lambda a: pl.BlockSpec(a.shape, lambda n: (0,) * a.ndim)
    out = jax.ShapeDtypeStruct((B, T, HEAD_DIM, SCAN_COLS), F32)
    return pl.pallas_call(
        functools.partial(_scan_kernel, tc=tc, nb=B),
        out_shape=(out, out),
        grid=(nchunks,),
        in_specs=[fwd, rev, fwd, rev, fwd, rev, fwd, rev, fwd, rev, fwd, rev, full(mask), full(eye)],
        out_specs=(yspec(fwd_idx), yspec(rev_idx)),
        scratch_shapes=[pltpu.VMEM((2 * B, HEAD_DIM, D), F32)],
        compiler_params=_cparams(("arbitrary",)),
        name="rwkv_scan",
    )(r, r, v, v, kk, kk, w0, w1, kd0, kd1, bd0, bd1, mask, eye)


def _rwkv_out_kernel(y_ref, bv_ref, gate_ref, ln_ref, wo_ref, ones_ref, x_ref, mod_ref, g_ref,
                     rwh_ref, rwl_ref, xo_ref, h_ref, lg_ref):
    ones = ones_ref[...]
    y = y_ref[0]
    inv = 1.0 / HEAD_DIM
    dlt = y - _segsum_wide(y, ones) * inv
    yn = dlt * lax.rsqrt(_segsum_wide(dlt * dlt, ones) * inv + GN_EPS)
    ln = ln_ref[...]
    o = (yn * ln[0:1] + ln[1:2] + bv_ref[0]) * gate_ref[0]
    yl = _dot(o.astype(BF16), wo_ref[...])
    _ffn_prep(x_ref[0], yl, mod_ref[0, 0], g_ref[...], rwh_ref, rwl_ref, xo_ref, h_ref, lg_ref)


def _rwkv_out(y, bv, gate, ln, wo, ones, x, mods, g, rwh, rwl):
    B, T, D = x.shape
    row, mod, full = _row_specs(D)
    shapes, specs = _ffn_prep_outs(B, T, D)
    return pl.pallas_call(
        _rwkv_out_kernel,
        out_shape=shapes,
        grid=(B, T // TM),
        in_specs=[row(D), row(D), row(D), full(ln), full(wo), full(ones), row(D), mod, full(g),
                  full(rwh), full(rwl)],
        out_specs=specs,
        compiler_params=_cparams(("parallel", "parallel")),
        name="rwkv_out",
    )(y, bv, gate, ln, wo, ones, x, mods, g, rwh, rwl)


def _rope_tables(S, L):
    rows = S // GRID_W
    row = jnp.repeat(jnp.arange(rows, dtype=F32), GRID_W)
    col = (jnp.arange(rows * GRID_W) % GRID_W).astype(F32)
    n_freq = HEAD_DIM // 4
    inv = ROPE_THETA ** (-jnp.arange(n_freq, dtype=F32) / n_freq)
    lane = np.arange(LANES) % HEAD_DIM
    axis, half, freq = lane // 32, (lane % 32) // 16, lane % 16
    pos = jnp.where(jnp.asarray(axis == 0)[None, :], row[:, None], col[:, None])
    ang = pos * inv[freq][None, :]
    sgn = jnp.asarray(np.where(half == 0, -1.0, 1.0), dtype=F32)
    cos = jnp.concatenate([jnp.ones((L, LANES), F32), jnp.cos(ang)], axis=0)
    sin = jnp.concatenate([jnp.zeros((L, LANES), F32), jnp.sin(ang) * sgn[None, :]], axis=0)
    return cos, sin


def kernel(x, c, ctx, c_ctx, ada_w, ada_b, norm_mix_g, norm_ffn_g, attn_w_in, attn_w_out, attn_sink,
           attn_q_norm_g, attn_k_norm_g, rwkv_x_mix, rwkv_w_r, rwkv_w_k, rwkv_w_v, rwkv_w_o,
           rwkv_decay_w0, rwkv_decay_w1, rwkv_decay_w2, rwkv_iclr_a0, rwkv_iclr_a1, rwkv_iclr_a2,
           rwkv_gate_g1, rwkv_gate_g2, rwkv_k_k, rwkv_k_a, rwkv_r_k, rwkv_ln_g, rwkv_ln_b,
           router_w, router_bias, moe_w1, moe_w3, moe_w2, final_norm_g):
    B, S, D = x.shape
    L = ctx.shape[1]
    T = L + S
    depth = ada_w.shape[0]
    assert D == D_MODEL and L == TM and S % TM == 0 and B == 2 and depth == 2
    ones = _seg_ones()
    bf = lambda a: a.astype(BF16)

    cs = jnp.zeros((8, D), F32).at[:B].set(c).at[B].set(c_ctx)
    ada = _ada(cs, ada_w, ada_b).reshape(depth, 8, 6, D)
    mods = [jnp.stack([jnp.broadcast_to(ada[i, B], (B, 6, D)), ada[i, :B]], axis=1) for i in range(depth)]

    xa = jnp.concatenate([ctx, x], axis=1)
    rw = jnp.zeros((D, LANES), F32).at[:, :N_EXPERTS].set(router_w)
    rwh, rwl = _split(rw)

    w_in = attn_w_in[0]
    roped = np.concatenate([np.arange(0, 640), np.arange(768, 1408)])
    w_rot = w_in[:, roped ^ 16]
    cos, sin = _rope_tables(S, L)
    lane = np.arange(LANES) % HEAD_DIM
    gains = lambda g: jnp.stack([g[lane], g[lane ^ 16]], axis=0)
    qa, ka, va, qb, kb, vb = _inproj(xa, mods[0], norm_mix_g[0].reshape(1, D), bf(w_in), bf(w_rot), cos, sin,
                                     gains(attn_q_norm_g[0]), gains(attn_k_norm_g[0]), ones)
    grouped = lambda q: q.reshape(B, A_KV_HEADS, GROUP, T, HEAD_DIM)
    qa, qb = grouped(qa), grouped(qb)
    sink = attn_sink[0].astype(F32) * LOG2E
    ext = lambda v: jnp.swapaxes(
        jnp.concatenate([v, jnp.ones_like(v[..., :1]), jnp.zeros_like(v[..., :HEAD_DIM - 1])], axis=-1), 2, 3)
    va_x, vb_x = ext(va), ext(vb)
    nosink = jnp.full((B_Q_HEADS,), NEG, F32)
    oa_l = _window_attn(sink, qa, ka, va, L, S)
    oa_c = _flash(sink, qa, ka, va_x, q_rows=L, q_off=0, k_rows=L, tq=L, tk=L)
    ob_l = _flash(nosink, qb, kb, vb_x, q_rows=S, q_off=L, k_rows=T, tq=256, tk=_key_tile(T))
    ob_c = _flash(nosink, qb, kb, vb_x, q_rows=L, q_off=0, k_rows=L, tq=L, tk=L)
    oa = jnp.concatenate([oa_c, oa_l], axis=1)
    ob = jnp.concatenate([ob_c, ob_l], axis=1)
    w_out = bf(attn_w_out[0])
    na = A_Q_HEADS * HEAD_DIM
    xa, h, lg = _attn_out(oa, ob, w_out[:na], w_out[na:], xa, mods[0], norm_ffn_g[0].reshape(1, D), rwh, rwl)
    f = _moe(h.reshape(B * T, D), lg.reshape(B * T, LANES), router_bias,
             bf(moe_w1[0]), bf(moe_w3[0]), bf(moe_w2[0]))
    xa = _residual(xa, f.reshape(B, T, D), mods[0])

    cat2 = lambda a: jnp.concatenate([a[0], a[1]], axis=1)
    vec = jnp.stack([rwkv_k_k[0], rwkv_k_a[0], rwkv_r_k[0].reshape(D), rwkv_decay_w0[0, 0], rwkv_decay_w0[0, 1],
                     rwkv_iclr_a0[0, 0], rwkv_iclr_a0[0, 1], jnp.zeros((D,), F32)], axis=0)
    outs = _rwkv_proj(xa, mods[1], norm_mix_g[1].reshape(1, D), jnp.pad(rwkv_x_mix[0], ((0, 2), (0, 0))),
                      bf(rwkv_w_r[0]), bf(rwkv_w_k[0]), bf(rwkv_w_v[0]),
                      bf(cat2(rwkv_decay_w1[0])), bf(rwkv_decay_w2[0]),
                      bf(cat2(rwkv_iclr_a1[0])), bf(rwkv_iclr_a2[0]),
                      bf(rwkv_gate_g1[0]), bf(rwkv_gate_g2[0]), vec, ones)
    r, v, kk, bv, gate, w0, w1, kd0, kd1, bd0, bd1 = outs
    lane_id = np.arange(D)
    eye = jnp.asarray(np.arange(HEAD_DIM)[:, None] == (lane_id % HEAD_DIM)[None, :], dtype=BF16)
    n_heads = D // HEAD_DIM
    head_mask = jnp.asarray(np.arange(n_heads)[:, None] == (lane_id // HEAD_DIM)[None, :], dtype=F32)
    yf, yb = _rwkv_scan(r, v, kk, w0, w1, kd0, kd1, bd0, bd1, head_mask, eye, L)
    ycols = yf[..., 2 * n_heads:3 * n_heads] + yb[..., 2 * n_heads:3 * n_heads]
    y = jnp.swapaxes(ycols, 2, 3).reshape(B, T, D)
    ln = jnp.stack([rwkv_ln_g[0], rwkv_ln_b[0]] + [jnp.zeros((D,), F32)] * 6, axis=0)
    xa, h, lg = _rwkv_out(y, bv, gate, ln, bf(rwkv_w_o[0]), ones, xa, mods[1],
                          norm_ffn_g[1].reshape(1, D), rwh, rwl)
    f = _moe(h[:, L:].reshape(B * S, D), lg[:, L:].reshape(B * S, LANES), router_bias,
             bf(moe_w1[1]), bf(moe_w3[1]), bf(moe_w2[1]))
    return _final(xa, f.reshape(B, S, D), mods[1], final_norm_g.reshape(1, D), L)


def _key_tile(T):
    for tk in (1280, 1024, 768, 512, 256):
        if T % tk == 0:
            return tk
    raise ValueError(T)
```

```python
import functools

import numpy as np
import jax
import jax.numpy as jnp
from jax import lax
from jax.experimental import pallas as pl
from jax.experimental.pallas import tpu as pltpu

F32 = jnp.float32
BF16 = jnp.bfloat16

D_MODEL = 1024
HEAD_DIM = 64
GRID_W = 64
ROPE_THETA = 10000.0
RMS_EPS = 1e-6
GN_EPS = 64e-5
A_Q_HEADS = 8
A_KV_HEADS = 2
B_Q_HEADS = 8
B_KV_HEADS = 2
GROUP = 4
WINDOW = 128
N_EXPERTS = 16
N_GROUPS = 4
EXPERTS_PER_GROUP = 4
TOP_K = 2
LANES = 128
TM = 256
MOE_ROWS = 512
SCAN_CHUNK = 64
VMEM_LIMIT = 56 * 1024 * 1024
NEG = -1e30
LOG2E = 1.4426950408889634


def _cparams(sem):
    return pltpu.CompilerParams(dimension_semantics=sem, vmem_limit_bytes=VMEM_LIMIT)


def _dot(a, b):
    return jnp.dot(a, b, preferred_element_type=F32)


def _dot_nt(a, b):
    return lax.dot_general(a, b, (((1,), (1,)), ((), ())), preferred_element_type=F32)


def _split(x):
    hi = x.astype(BF16)
    lo = (x - hi.astype(F32)).astype(BF16)
    return hi, lo


def _dot3(x, w):
    xh, xl = _split(x)
    wh, wl = _split(w)
    return _dot(xh, wh) + _dot(xh, wl) + _dot(xl, wh)


def _segsum(v, ones):
    hi, lo = _split(v)
    return _dot(hi, ones) + _dot(lo, ones)


def _segsum_wide(v, ones):
    n = v.shape[1] // LANES
    return jnp.concatenate([_segsum(v[:, j * LANES:(j + 1) * LANES], ones) for j in range(n)], axis=1)


def _norm_mod(x, g, shift, scale):
    ms = jnp.mean(x * x, axis=-1, keepdims=True)
    return (x * lax.rsqrt(ms + RMS_EPS) * g) * (1.0 + scale) + shift


def _sigmoid(x):
    return 1.0 / (1.0 + jnp.exp(-x))


def _seg_ones():
    i = np.arange(LANES)
    return jnp.asarray((i[:, None] // HEAD_DIM) == (i[None, :] // HEAD_DIM), dtype=BF16)


def _ada_kernel(c_ref, w_ref, b_ref, o_ref):
    c = c_ref[...]
    s = c * _sigmoid(c)
    o_ref[0] = _dot3(s, w_ref[0]) + b_ref[0]


def _ada(cs, ada_w, ada_b):
    depth, d, n = ada_w.shape
    tn = 1536
    return pl.pallas_call(
        _ada_kernel,
        out_shape=jax.ShapeDtypeStruct((depth, 8, n), F32),
        grid=(depth, n // tn),
        in_specs=[
            pl.BlockSpec((8, d), lambda l, j: (0, 0)),
            pl.BlockSpec((1, d, tn), lambda l, j: (l, 0, j)),
            pl.BlockSpec((1, 1, tn), lambda l, j: (l, 0, j)),
        ],
        out_specs=pl.BlockSpec((1, 8, tn), lambda l, j: (l, 0, j)),
        compiler_params=_cparams(("arbitrary", "arbitrary")),
        name="ada",
    )(cs, ada_w, ada_b.reshape(depth, 1, n))


def _inproj_kernel(x_ref, mod_ref, g_ref, w_ref, wrot_ref, cos_ref, sin_ref, gq_ref, gk_ref, ones_ref,
                   qa_ref, ka_ref, va_ref, qb_ref, kb_ref, vb_ref):
    mod = mod_ref[0, 0]
    h = _norm_mod(x_ref[0], g_ref[...], mod[0:1], mod[1:2]).astype(BF16)
    y = _dot(h, w_ref[...])
    yr = _dot(h, wrot_ref[...])
    cos = cos_ref[...]
    sin = sin_ref[...]
    ones = ones_ref[...]
    qscale = HEAD_DIM ** -0.5 * LOG2E

    def put(ref, tile, val):
        ref[0, 2 * tile] = val[:, :HEAD_DIM].astype(ref.dtype)
        ref[0, 2 * tile + 1] = val[:, HEAD_DIM:].astype(ref.dtype)

    def chunk(a, c):
        return a[:, c * LANES:(c + 1) * LANES]

    for c in range(4):
        put(qa_ref, c, (chunk(y, c) * cos + chunk(yr, c) * sin) * qscale)
    put(ka_ref, 0, chunk(y, 4) * cos + chunk(yr, 4) * sin)
    put(va_ref, 0, chunk(y, 5))

    def normed(c, cr, gain_ref):
        v = chunk(y, c)
        rs = lax.rsqrt(_segsum(v * v, ones) * (1.0 / HEAD_DIM) + RMS_EPS)
        return (v * rs * gain_ref[0:1]) * cos + (chunk(yr, cr) * rs * gain_ref[1:2]) * sin

    for c in range(4):
        put(qb_ref, c, normed(6 + c, 5 + c, gq_ref) * qscale)
    put(kb_ref, 0, normed(10, 9, gk_ref))
    put(vb_ref, 0, chunk(y, 11))


def _inproj(x, mods, g, w_in, w_rot, cos, sin, gq2, gk2, ones):
    B, T, D = x.shape
    nt = T // TM
    heads = lambda n: jax.ShapeDtypeStruct((B, n, T, HEAD_DIM), BF16)
    hspec = lambda n: pl.BlockSpec((1, n, TM, HEAD_DIM), lambda b, i: (b, 0, i, 0))
    full = lambda a: pl.BlockSpec(a.shape, lambda b, i: (0,) * a.ndim)
    return pl.pallas_call(
        _inproj_kernel,
        out_shape=(heads(8), heads(2), heads(2), heads(8), heads(2), heads(2)),
        grid=(B, nt),
        in_specs=[
            pl.BlockSpec((1, TM, D), lambda b, i: (b, i, 0)),
            pl.BlockSpec((1, 1, 6, D), lambda b, i: (b, jnp.minimum(i, 1), 0, 0)),
            full(g), full(w_in), full(w_rot),
            pl.BlockSpec((TM, LANES), lambda b, i: (i, 0)),
            pl.BlockSpec((TM, LANES), lambda b, i: (i, 0)),
            full(gq2), full(gk2), full(ones),
        ],
        out_specs=(hspec(8), hspec(2), hspec(2), hspec(8), hspec(2), hspec(2)),
        compiler_params=_cparams(("parallel", "parallel")),
        name="attn_inproj",
    )(x, mods, g, w_in, w_rot, cos, sin, gq2, gk2, ones)


LOOKAHEAD = 2


def _flash_kernel(sink_ref, q_ref, k_ref, v_ref, o_ref, m_scr, acc_scr, s_scr, *, tk, nk):
    h = pl.program_id(1)
    m_scr[...] = jnp.full(m_scr.shape, NEG, F32)
    acc_scr[...] = jnp.zeros(acc_scr.shape, F32)

    def scores(j, g):
        return _dot_nt(k_ref[0, 0, pl.ds(pl.multiple_of(j * tk, tk), tk), :], q_ref[0, 0, g])

    for g in range(LOOKAHEAD):
        s_scr[g] = scores(0, g)

    def body(j, carry):
        vt = v_ref[0, 0, :, pl.ds(pl.multiple_of(j * tk, tk), tk)]
        jn = jnp.minimum(j + 1, nk - 1)
        ahead = {}
        for g in range(GROUP):
            st = s_scr[g] if g < LOOKAHEAD else ahead.pop(g)
            if g + LOOKAHEAD < GROUP:
                ahead[g + LOOKAHEAD] = scores(j, g + LOOKAHEAD)
            m_prev = m_scr[g]
            m_new = jnp.maximum(m_prev, jnp.max(st, axis=0, keepdims=True))
            p = jnp.exp2(st - m_new).astype(BF16)
            if g + LOOKAHEAD >= GROUP:
                s_scr[g + LOOKAHEAD - GROUP] = scores(jn, g + LOOKAHEAD - GROUP)
            acc_scr[g] = jnp.exp2(m_prev - m_new) * acc_scr[g] + _dot(vt, p)
            m_scr[g] = m_new
        return carry

    lax.fori_loop(0, nk, body, 0)
    outs = []
    for g in range(GROUP):
        acc = acc_scr[g]
        l = acc[HEAD_DIM:HEAD_DIM + 1] + jnp.exp2(sink_ref[h * GROUP + g] - m_scr[g])
        outs.append(acc[:HEAD_DIM] / l)
    o_ref[0] = jnp.concatenate(outs, axis=0).T.astype(o_ref.dtype)


def _flash(sink, q, k, v, *, q_rows, q_off, k_rows, tq, tk):
    B, Hkv = k.shape[:2]
    nq, nk = q_rows // tq, k_rows // tk
    qo = q_off // tq
    return pl.pallas_call(
        functools.partial(_flash_kernel, tk=tk, nk=nk),
        out_shape=jax.ShapeDtypeStruct((B, q_rows, Hkv * GROUP * HEAD_DIM), BF16),
        grid=(B, Hkv, nq),
        in_specs=[
            pl.BlockSpec(memory_space=pltpu.SMEM),
            pl.BlockSpec((1, 1, GROUP, tq, HEAD_DIM), lambda b, h, i: (b, h, 0, i + qo, 0)),
            pl.BlockSpec((1, 1, k_rows, HEAD_DIM), lambda b, h, i: (b, h, 0, 0)),
            pl.BlockSpec((1, 1, LANES, k_rows), lambda b, h, i: (b, h, 0, 0)),
        ],
        out_specs=pl.BlockSpec((1, tq, GROUP * HEAD_DIM), lambda b, h, i: (b, i, h)),
        scratch_shapes=[
            pltpu.VMEM((GROUP, 1, tq), F32),
            pltpu.VMEM((GROUP, LANES, tq), F32),
            pltpu.VMEM((LOOKAHEAD, tk, tq), F32),
        ],
        compiler_params=_cparams(("parallel", "parallel", "arbitrary")),
        name="flash_attn",
    )(sink, q, k, v)


def _window_kernel(sink_ref, q_ref, kc_ref, vc_ref, k0_ref, k1_ref, k2_ref, v0_ref, v1_ref, v2_ref, o_ref, *, nb):
    h = pl.program_id(1)
    i = pl.program_id(2)
    rows = GROUP * WINDOW
    q = q_ref[0, 0].reshape(rows, HEAD_DIM)
    r = lax.broadcasted_iota(jnp.int32, (rows, WINDOW), 0) & (WINDOW - 1)
    c = lax.broadcasted_iota(jnp.int32, (rows, WINDOW), 1)
    sc = _dot_nt(q, kc_ref[0, 0])
    s0 = jnp.where((c >= r) & (i > 0), _dot_nt(q, k0_ref[0, 0]), NEG)
    s1 = _dot_nt(q, k1_ref[0, 0])
    s2 = jnp.where((c <= r) & (i < nb - 1), _dot_nt(q, k2_ref[0, 0]), NEG)
    sink = jnp.concatenate(
        [jnp.full((WINDOW, 1), sink_ref[h * GROUP + g], F32) for g in range(GROUP)], axis=0)
    rowmax = lambda s: jnp.max(s, axis=-1, keepdims=True)
    m = jnp.maximum(jnp.maximum(rowmax(sc), rowmax(s0)), jnp.maximum(rowmax(s1), rowmax(s2)))
    m = jnp.maximum(m, sink)
    pc, p0, p1, p2 = (jnp.exp2(s - m) for s in (sc, s0, s1, s2))
    rowsum = lambda p: jnp.sum(p, axis=-1, keepdims=True)
    l = rowsum(pc) + rowsum(p0) + rowsum(p1) + rowsum(p2) + jnp.exp2(sink - m)
    acc = (_dot(pc.astype(BF16), vc_ref[0, 0]) + _dot(p0.astype(BF16), v0_ref[0, 0])
           + _dot(p1.astype(BF16), v1_ref[0, 0]) + _dot(p2.astype(BF16), v2_ref[0, 0]))
    out = acc / l
    for g in range(GROUP):
        o_ref[0, :, g * HEAD_DIM:(g + 1) * HEAD_DIM] = out[g * WINDOW:(g + 1) * WINDOW].astype(o_ref.dtype)


def _window_attn(sink, q, k, v, L, S):
    B, Hkv = k.shape[:2]
    nb = S // WINDOW
    pad = ((0, 0), (0, 0), (WINDOW, WINDOW), (0, 0))
    kp = jnp.pad(k[:, :, L:], pad)
    vp = jnp.pad(v[:, :, L:], pad)
    qo = L // WINDOW
    band = lambda j: pl.BlockSpec((1, 1, WINDOW, HEAD_DIM), lambda b, h, i: (b, h, i + j, 0))
    ctx = pl.BlockSpec((1, 1, L, HEAD_DIM), lambda b, h, i: (b, h, 0, 0))
    return pl.pallas_call(
        functools.partial(_window_kernel, nb=nb),
        out_shape=jax.ShapeDtypeStruct((B, S, Hkv * GROUP * HEAD_DIM), BF16),
        grid=(B, Hkv, nb),
        in_specs=[
            pl.BlockSpec(memory_space=pltpu.SMEM),
            pl.BlockSpec((1, 1, GROUP, WINDOW, HEAD_DIM), lambda b, h, i: (b, h, 0, i + qo, 0)),
            ctx, ctx, band(0), band(1), band(2), band(0), band(1), band(2),
        ],
        out_specs=pl.BlockSpec((1, WINDOW, GROUP * HEAD_DIM), lambda b, h, i: (b, i, h)),
        compiler_params=_cparams(("parallel", "parallel", "parallel")),
        name="window_attn",
    )(sink, q, k, v, kp, kp, kp, vp, vp, vp)


def _ffn_prep(x, y, mod, gffn, rwh_ref, rwl_ref, xo_ref, h_ref, lg_ref):
    xn = x + mod[2:3] * y
    h = _norm_mod(xn, gffn, mod[3:4], mod[4:5])
    xo_ref[0] = xn
    hh, hl = _split(h)
    h_ref[0] = hh
    rwh = rwh_ref[...]
    lg_ref[0] = _dot(hh, rwh) + _dot(hl, rwh) + _dot(hh, rwl_ref[...])


def _attn_out_kernel(oa_ref, ob_ref, wa_ref, wb_ref, x_ref, mod_ref, g_ref, rwh_ref, rwl_ref,
                     xo_ref, h_ref, lg_ref):
    y = _dot(oa_ref[0], wa_ref[...]) + _dot(ob_ref[0], wb_ref[...])
    _ffn_prep(x_ref[0], y, mod_ref[0, 0], g_ref[...], rwh_ref, rwl_ref, xo_ref, h_ref, lg_ref)


def _row_specs(D):
    row = lambda w: pl.BlockSpec((1, TM, w), lambda b, i: (b, i, 0))
    mod = pl.BlockSpec((1, 1, 6, D), lambda b, i: (b, jnp.minimum(i, 1), 0, 0))
    full = lambda a: pl.BlockSpec(a.shape, lambda b, i: (0,) * a.ndim)
    return row, mod, full


def _ffn_prep_outs(B, T, D):
    row, _, _ = _row_specs(D)
    shapes = (jax.ShapeDtypeStruct((B, T, D), F32), jax.ShapeDtypeStruct((B, T, D), BF16),
              jax.ShapeDtypeStruct((B, T, LANES), F32))
    return shapes, (row(D), row(D), row(LANES))


def _attn_out(oa, ob, wa, wb, x, mods, g, rwh, rwl):
    B, T, D = x.shape
    row, mod, full = _row_specs(D)
    shapes, specs = _ffn_prep_outs(B, T, D)
    return pl.pallas_call(
        _attn_out_kernel,
        out_shape=shapes,
        grid=(B, T // TM),
        in_specs=[row(oa.shape[-1]), row(ob.shape[-1]), full(wa), full(wb), row(D), mod, full(g),
                  full(rwh), full(rwl)],
        out_specs=specs,
        compiler_params=_cparams(("parallel", "parallel")),
        name="attn_out",
    )(oa, ob, wa, wb, x, mods, g, rwh, rwl)


def _gmm_kernel(be_ref, nu_ref, x_ref, w1_ref, w3_ref, w2_ref, o_ref):
    i = pl.program_id(0)

    @pl.when(i < nu_ref[0])
    def _():
        x = x_ref[...]
        a = _dot(x, w1_ref[0])
        b = _dot(x, w3_ref[0])
        mid = (a * _sigmoid(a)) * b
        o_ref[...] = _dot(mid.astype(BF16), w2_ref[0])

    @pl.when(i >= nu_ref[0])
    def _():
        o_ref[...] = jnp.zeros(o_ref.shape, o_ref.dtype)


def _gmm(block_expert, n_used, xs, w1, w3, w2):
    n_slots, D = xs.shape
    F = w1.shape[-1]
    nblk = n_slots // MOE_ROWS
    return pl.pallas_call(
        _gmm_kernel,
        out_shape=jax.ShapeDtypeStruct((n_slots, D), F32),
        grid_spec=pltpu.PrefetchScalarGridSpec(
            num_scalar_prefetch=2,
            grid=(nblk,),
            in_specs=[
                pl.BlockSpec((MOE_ROWS, D), lambda i, be, nu: (i, 0)),
                pl.BlockSpec((1, D, F), lambda i, be, nu: (be[i], 0, 0)),
                pl.BlockSpec((1, D, F), lambda i, be, nu: (be[i], 0, 0)),
                pl.BlockSpec((1, F, D), lambda i, be, nu: (be[i], 0, 0)),
            ],
            out_specs=pl.BlockSpec((MOE_ROWS, D), lambda i, be, nu: (i, 0)),
        ),
        compiler_params=_cparams(("arbitrary",)),
        name="moe_gmm",
    )(block_expert, n_used, xs, w1, w3, w2)


ROUTE_ROWS = 512


def _route_kernel(lg_ref, bias_ref, idx_ref, w_ref):
    x = lg_ref[...].T[:N_EXPERTS]
    m = jnp.max(x, axis=0, keepdims=True)
    e = jnp.exp(x - m)
    probs = e / jnp.sum(e, axis=0, keepdims=True)
    sel = probs + bias_ref[...][:, 0:1]
    row = lambda a, i: a[i:i + 1, :]
    G = EXPERTS_PER_GROUP
    scores = []
    for g in range(N_GROUPS):
        s = [row(sel, g * G + i) for i in range(G)]
        best = None
        for i in range(G):
            for j in range(i + 1, G):
                best = s[i] + s[j] if best is None else jnp.maximum(best, s[i] + s[j])
        scores.append(best)
    top = functools.reduce(jnp.maximum, scores)
    gi = jnp.full(top.shape, N_GROUPS - 1, jnp.int32)
    for g in range(N_GROUPS - 2, -1, -1):
        gi = jnp.where(scores[g] == top, g, gi)

    def pick(a, i):
        out = row(a, (N_GROUPS - 1) * G + i)
        for g in range(N_GROUPS - 2, -1, -1):
            out = jnp.where(gi == g, row(a, g * G + i), out)
        return out

    c = [pick(sel, i) for i in range(G)]
    pc = [pick(probs, i) for i in range(G)]

    def first_argmax(vals):
        mx = functools.reduce(jnp.maximum, vals)
        idx = jnp.full(mx.shape, G - 1, jnp.int32)
        for i in range(G - 2, -1, -1):
            idx = jnp.where(vals[i] == mx, i, idx)
        return idx

    i1 = first_argmax(c)
    i2 = first_argmax([jnp.where(i1 == i, -jnp.inf, c[i]) for i in range(G)])
    take = lambda vals, idx: functools.reduce(
        lambda acc, i: jnp.where(idx == i, vals[i], acc), range(G - 2, -1, -1), vals[G - 1])
    w1, w2 = take(pc, i1), take(pc, i2)
    tot = w1 + w2
    zi = jnp.zeros((6,) + top.shape[1:], jnp.int32)
    idx_ref[...] = jnp.concatenate([gi * G + i1, gi * G + i2, zi], axis=0)
    w_ref[...] = jnp.concatenate([w1 / tot, w2 / tot, zi.astype(F32)], axis=0)


def _route(logits, router_bias):
    N = logits.shape[0]
    bias = jnp.broadcast_to(router_bias.astype(F32)[:, None], (N_EXPERTS, LANES))
    idx, w = pl.pallas_call(
        _route_kernel,
        out_shape=(jax.ShapeDtypeStruct((8, N), jnp.int32), jax.ShapeDtypeStruct((8, N), F32)),
        grid=(N // ROUTE_ROWS,),
        in_specs=[pl.BlockSpec((ROUTE_ROWS, LANES), lambda i: (i, 0)),
                  pl.BlockSpec((N_EXPERTS, LANES), lambda i: (0, 0))],
        out_specs=(pl.BlockSpec((8, ROUTE_ROWS), lambda i: (0, i)),
                   pl.BlockSpec((8, ROUTE_ROWS), lambda i: (0, i))),
        compiler_params=_cparams(("parallel",)),
        name="route",
    )(logits, bias)
    return idx[:TOP_K].T, w[:TOP_K].T


def _moe(h, logits, router_bias, w1, w3, w2):
    N, D = h.shape
    expert_idx, gate_w = _route(logits, router_bias)
    NK = N * TOP_K
    flat_e = expert_idx.reshape(NK)
    onehot = (flat_e[:, None] == jnp.arange(N_EXPERTS, dtype=jnp.int32)[None, :]).astype(jnp.int32)
    csum = jnp.cumsum(onehot, axis=0)
    counts = csum[-1]
    rank = jnp.take_along_axis(csum, flat_e[:, None], axis=1)[:, 0] - 1
    padded = (counts + MOE_ROWS - 1) // MOE_ROWS * MOE_ROWS
    pad_end = jnp.cumsum(padded)
    pad_start = pad_end - padded
    dest =pad_start[flat_e] + rank
    nblk = -(-NK // MOE_ROWS) + N_EXPERTS
    n_slots = nblk * MOE_ROWS
    n_used = (pad_end[-1] // MOE_ROWS).astype(jnp.int32)
    blk = jnp.arange(nblk, dtype=jnp.int32)
    be = jnp.sum((pad_end[None, :] <= (blk * MOE_ROWS)[:, None]).astype(jnp.int32), axis=1)
    be = jnp.minimum(be, N_EXPERTS - 1)
    be = jnp.where(blk < n_used, be, be[jnp.maximum(n_used - 1, 0)])
    flat_tok = jnp.arange(NK, dtype=jnp.int32) // TOP_K
    slot_tok = jnp.zeros((n_slots,), jnp.int32).at[dest].set(flat_tok, unique_indices=True)
    xs = h[slot_tok]
    ys = _gmm(be, n_used.reshape(1), xs, w1, w3, w2)
    pos = dest.reshape(N, TOP_K)
    return ys[pos[:, 0]] * gate_w[:, 0:1] + ys[pos[:, 1]] * gate_w[:, 1:2]


def _residual_kernel(x_ref, f_ref, mod_ref, o_ref):
    o_ref[0] = x_ref[0] + mod_ref[0, 0][5:6] * f_ref[0]


def _final_kernel(x_ref, f_ref, mod_ref, g_ref, o_ref):
    x = x_ref[0] + mod_ref[0, 0][5:6] * f_ref[0]
    ms = jnp.mean(x * x, axis=-1, keepdims=True)
    o_ref[0] = x * lax.rsqrt(ms + RMS_EPS) * g_ref[...]


def _residual(x, f, mods):
    B, T, D = x.shape
    row, mod, _ = _row_specs(D)
    return pl.pallas_call(
        _residual_kernel,
        out_shape=jax.ShapeDtypeStruct((B, T, D), F32),
        grid=(B, T // TM),
        in_specs=[row(D), row(D), mod],
        out_specs=row(D),
        compiler_params=_cparams(("parallel", "parallel")),
        name="residual",
    )(x, f, mods)


def _final(x, f, mods, g, L):
    B, S, D = f.shape
    off = L // TM
    mod = pl.BlockSpec((1, 1, 6, D), lambda b, i: (b, 1, 0, 0))
    return pl.pallas_call(
        _final_kernel,
        out_shape=jax.ShapeDtypeStruct((B, S, D), F32),
        grid=(B, S // TM),
        in_specs=[pl.BlockSpec((1, TM, D), lambda b, i: (b, i + off, 0)),
                  pl.BlockSpec((1, TM, D), lambda b, i: (b, i, 0)), mod,
                  pl.BlockSpec(g.shape, lambda b, i: (0, 0))],
        out_specs=pl.BlockSpec((1, TM, D), lambda b, i: (b, i, 0)),
        compiler_params=_cparams(("parallel", "parallel")),
        name="final_norm",
    )(x, f, mods, g)


def _rwkv_proj_kernel(x_ref, xp_ref, xn_ref, mod_ref, g_ref, xmix_ref, wr_ref, wk_ref, wv_ref,
                      dw1_ref, dw2_ref, da1_ref, da2_ref, g1_ref, g2_ref, vec_ref, ones_ref,
                      r_ref, v_ref, kk_ref, bv_ref, gate_ref, w0_ref, w1_ref, kd0_ref, kd1_ref, bd0_ref, bd1_ref,
                      *, nt):
    i = pl.program_id(1)
    mod = mod_ref[0, 0]
    g = g_ref[...]
    nm = lambda x: _norm_mod(x, g, mod[0:1], mod[1:2])
    h = nm(x_ref[0])
    hp = nm(xp_ref[0])[7:8] * jnp.where(i >= 2, 1.0, 0.0)
    hn = nm(xn_ref[0])[0:1] * jnp.where((i >= 1) & (i < nt - 1), 1.0, 0.0)
    ridx = lax.broadcasted_iota(jnp.int32, h.shape, 0)
    h_dn = jnp.where(ridx == 0, hp, pltpu.roll(h, 1, axis=0))
    h_up = jnp.where(ridx == TM - 1, hn, pltpu.roll(h, TM - 1, axis=0))
    xx = 0.5 * (h_dn + h_up) - h
    xmix = xmix_ref[...]
    mix = lambda j: (h + xx * xmix[j:j + 1]).astype(BF16)
    vec = vec_ref[...]
    ones = ones_ref[...]

    r = _dot(mix(0), wr_ref[...])
    k = _dot(mix(2), wk_ref[...])
    v = _dot(mix(3), wv_ref[...])
    gate_ref[0] = _dot(_sigmoid(_dot(mix(5), g1_ref[...])).astype(BF16), g2_ref[...])
    kk = k * vec[0:1]
    kk = kk * lax.rsqrt(jnp.maximum(_segsum_wide(kk * kk, ones), 1e-24))
    lw = jnp.tanh(_dot(mix(1), dw1_ref[...])).astype(BF16)
    la = _dot(mix(4), da1_ref[...]).astype(BF16)
    r_ref[0] = r
    v_ref[0] = v
    kk_ref[0] = kk
    bonus = jnp.zeros_like(r)
    lora = DECAY_LORA
    for d, (w_ref, kd_ref, bd_ref) in enumerate(((w0_ref, kd0_ref, bd0_ref), (w1_ref, kd1_ref, bd1_ref))):
        z = -(vec[3 + d:4 + d] + _dot(lw[:, d * lora:(d + 1) * lora], dw2_ref[d]))
        softplus = jnp.maximum(z, 0.0) + jnp.log(1.0 + jnp.exp(-jnp.abs(z)))
        w_ref[0] = jnp.exp(-jnp.exp(-softplus - 0.5))
        iclr = _sigmoid(vec[5 + d:6 + d] + _dot(la[:, d * lora:(d + 1) * lora], da2_ref[d]))
        kd = k * (1.0 + (iclr - 1.0) * vec[1:2])
        kd_ref[0] = kd
        bd_ref[0] = kk * iclr
        bonus = bonus + _segsum_wide(r * kd * vec[2:3], ones)
    bv_ref[0] = bonus * v


DECAY_LORA = 64


def _rwkv_proj(x, mods, g, xmix, wr, wk, wv, dw1, dw2, da1, da2, g1, g2, vec, ones):
    B, T, D = x.shape
    nt = T // TM
    row, mod, full = _row_specs(D)
    r8 = TM // 8
    prev = pl.BlockSpec((1, 8, D), lambda b, i: (b, jnp.maximum(i * r8 - 1, 0), 0))
    nxt = pl.BlockSpec((1, 8, D), lambda b, i: (b, jnp.minimum((i + 1) * r8, T // 8 - 1), 0))
    out = jax.ShapeDtypeStruct((B, T, D), F32)
    return pl.pallas_call(
        functools.partial(_rwkv_proj_kernel, nt=nt),
        out_shape=(out,) * 11,
        grid=(B, nt),
        in_specs=[row(D), prev, nxt, mod, full(g), full(xmix), full(wr), full(wk), full(wv),
                  full(dw1), full(dw2), full(da1), full(da2), full(g1), full(g2), full(vec), full(ones)],
        out_specs=(row(D),) * 11,
        compiler_params=_cparams(("parallel", "parallel")),
        name="rwkv_proj",
    )(x, x, x, mods, g, xmix, wr, wk, wv, dw1, dw2, da1, da2, g1, g2, vec, ones)


CHUNK = 4
N_HEADS = D_MODEL // HEAD_DIM
_AB, _AK, _RB, _RK = 0, 3, 6, 10


def _coef_kernel(r_ref, kk_ref, w_ref, kd_ref, bd_ref, sel_ref, at_ref, rt_ref, bh_ref, kh_ref, gc_ref, c_ref,
                 *, reverse):
    r, a, w, kd, bd = r_ref[0], -kk_ref[0], w_ref[0], kd_ref[0], bd_ref[0]
    rows = r.shape[0]
    p = lax.broadcasted_iota(jnp.int32, r.shape, 0) & (CHUNK - 1)
    s = (CHUNK - 1 - p) if reverse else p
    back = lambda x, k: pltpu.roll(x, (rows - k) if reverse else k, axis=0)
    ahead = lambda x, k: pltpu.roll(x, k if reverse else (rows - k), axis=0)
    wb = [None] + [back(w, k) for k in range(1, CHUNK)]
    excl = jnp.ones_like(w)
    rest = jnp.ones_like(w)
    for k in range(1, CHUNK):
        excl = excl * jnp.where(s >= k, wb[k], 1.0)
        rest = rest * jnp.where(s + k <= CHUNK - 1, ahead(w, k), 1.0)
    at_ref[0] = a * excl
    rt_ref[0] = r * (excl * w)
    bh_ref[0] = bd * rest
    kh_ref[0] = kd * rest
    gc_ref[0] = excl * w * rest
    between = [None, None, wb[1], wb[1] * wb[2]]
    rw = r * w
    prods = []
    for y in (bd, kd):
        for dist in range(1, CHUNK):
            e = back(y, dist) if between[dist] is None else between[dist] * back(y, dist)
            prods.append(a * e)
    for y in (bd, kd):
        prods.append(r * y)
        for dist in range(1, CHUNK):
            e = back(y, dist) if between[dist] is None else between[dist] * back(y, dist)
            prods.append(rw * e)
    halves = []
    for half in range(2):
        acc = None
        for blk in range(8):
            i = half * 8 + blk
            if i < len(prods):
                t = _dot(prods[i].astype(BF16), sel_ref[blk])
                acc = t if acc is None else acc + t
        halves.append(acc)
    c_ref[0] = jnp.concatenate(halves, axis=1)


def _rwkv_coef(r, kk, w, kd, bd, sel, reverse):
    B, T, D = r.shape
    row, _, full = _row_specs(D)
    out = jax.ShapeDtypeStruct((B, T, D), F32)
    return pl.pallas_call(
        functools.partial(_coef_kernel, reverse=reverse),
        out_shape=(out,) * 5 + (jax.ShapeDtypeStruct((B, T, 2 * LANES), F32),),
        grid=(B, T // TM),
        in_specs=[row(D)] * 5 + [full(sel)],
        out_specs=(row(D),) * 5 + (row(2 * LANES),),
        compiler_params=_cparams(("parallel", "parallel")),
        name="rwkv_coef",
    )(r, kk, w, kd, bd, sel)


def _mid_rows(c, reverse):
    B, T, _ = c.shape
    nc = T // CHUNK
    cc = c.reshape(B, nc, CHUNK, 16, N_HEADS)
    pos = lambda s: CHUNK - 1 - s if reverse else s

    def pair(kind, s, j):
        return cc[:, :, pos(s), kind + (s - j) - (1 if kind in (_AB, _AK) else 0), :]

    C = range(CHUNK)
    lab = {(s, j): pair(_AB, s, j) for s in C for j in range(s)}
    lak = {(s, j): pair(_AK, s, j) for s in C for j in range(s)}
    rb = {(s, j): pair(_RB, s, j) for s in C for j in range(s + 1)}
    rk = {(s, j): pair(_RK, s, j) for s in C for j in range(s + 1)}
    one = jnp.ones_like(rb[0, 0])
    zero = jnp.zeros_like(one)
    m = {(s, s): one for s in C}
    for j in C:
        for s in range(j + 1, CHUNK):
            m[s, j] = sum(lab[s, i] * m[i, j] for i in range(j, s))
    g = {(s, j): sum(m[s, i] * lak[i, j] for i in range(j + 1, s + 1)) for s in C for j in range(s)}
    yz = {(s, j): sum(rb[s, i] * m[i, j] for i in range(j, s + 1)) for s in C for j in range(s + 1)}
    yv = {(s, j): rk[s, j] + sum((rb[s, i] * g[i, j] for i in range(j + 1, s + 1)), zero)
          for s in C for j in range(s + 1)}
    rows = []
    for j in C:
        rows.append([m.get((s, j), zero) for s in C] + [yz.get((s, j), zero) for s in C])
    for j in C:
        rows.append([g.get((s, j), zero) for s in C] + [yv.get((s, j), zero) for s in C])
    for j in C:
        rows.append([zero] * CHUNK + [one if s == j else zero for s in C])
    rows += [[zero] * (2 * CHUNK)] * (16 - len(rows))
    out = jnp.stack([jnp.stack(rw, axis=2) for rw in rows], axis=2)
    return out.reshape(B, nc, 16, 2 * CHUNK * N_HEADS)


def _scan_kernel(atf, atb, rtf, rtb, vf, vb, bhf, bhb, khf, khb, gcf, gcb, cff, cfb, mask_ref, eye_ref, e16_ref,
                 yf, yb, st, *, tc, nb):
    n = pl.program_id(0)

    @pl.when(n == 0)
    def _():
        st[...] = jnp.zeros(st.shape, F32)

    N = HEAD_DIM
    W = CHUNK * N_HEADS
    nch = tc // CHUNK
    dirs = ((atf, rtf, vf, bhf, khf, gcf, cff, yf), (atb, rtb, vb, bhb, khb, gcb, cfb, yb))
    lane1 = lax.broadcasted_iota(jnp.int32, (N, 3 * W), 1)
    lane2 = lax.broadcasted_iota(jnp.int32, (N, 2 * W), 1)
    spread = lambda x: (mask_ref[...] * x).astype(BF16)

    def chunk(ci, carry):
        work = []
        for d, refs in enumerate(dirs):
            cc = ci if d == 0 else nch - 1 - ci
            rows = [pl.ds(cc * CHUNK + (s if d == 0 else CHUNK - 1 - s), 1) for s in range(CHUNK)]
            for b in range(nb):
                work.append((d * nb + b, b, cc, rows, refs))
        firsts = []
        for gi, b, cc, rows, (AT, RT, V, BH, KH, GC, CF, Y) in work:
            lhs = jnp.concatenate([st[gi].astype(BF16), eye_ref[...]], axis=0)
            w1 = jnp.concatenate([spread(X[b, rw, :]) for X in (AT, V, RT) for rw in rows], axis=0)
            firsts.append(_dot_nt(lhs, w1))
        mids = []
        for (gi, b, cc, rows, (AT, RT, V, BH, KH, GC, CF, Y)), out in zip(work, firsts):
            zvq = jnp.where((lane1 >= W) & (lane1 < 2 * W), out[N:], out[:N])
            cf = CF[b, cc]
            wm = jnp.concatenate([e16_ref[...] * cf[i:i + 1, :] for i in range(3 * CHUNK)], axis=0)
            uy = _dot(zvq.astype(BF16), wm.astype(BF16))
            Y[b, cc] = uy
            mids.append((zvq, uy))
        for (gi, b, cc, rows, (AT, RT, V, BH, KH, GC, CF, Y)), (zvq, uy) in zip(work, mids):
            uv = jnp.where(lane2 < W, uy, zvq[:, :2 * W]).astype(BF16)
            w2 = jnp.concatenate([spread(X[b, rw, :]) for X in (BH, KH) for rw in rows], axis=0)
            st[gi] = st[gi] * GC[b, rows[0], :] + _dot(uv, w2)
        return carry

    lax.fori_loop(0, nch, chunk, 0)


def _rwkv_scan(ins_f, ins_b, v, mask, eye, e16, L):
    B, T, D = v.shape
    tc = SCAN_CHUNK
    nch = tc // CHUNK
    nc, nchunks = L // tc, T // tc
    fwd_idx = lambda n: n
    rev_idx = lambda n: jnp.where(n < nc, nc - 1 - n, nchunks - 1 - (n - nc))
    tok = lambda idx: pl.BlockSpec((B, tc, D), lambda n: (0, idx(n), 0))
    cfs = lambda idx: pl.BlockSpec((B, nch, 16, LANES), lambda n: (0, idx(n), 0, 0))
    ys = lambda idx: pl.BlockSpec((B, nch, HEAD_DIM, LANES), lambda n: (0, idx(n), 0, 0))
    full = lambda a: pl.BlockSpec(a.shape, lambda n: (0,) * a.ndim)
    out = jax.ShapeDtypeStruct((B, T // CHUNK, HEAD_DIM, LANES), F32)
    atf, rtf, bhf, khf, gcf, cff = ins_f
    atb, rtb, bhb, khb, gcb, cfb = ins_b
    f, r_ = tok(fwd_idx), tok(rev_idx)
    return pl.pallas_call(
        functools.partial(_scan_kernel, tc=tc, nb=B),
        out_shape=(out, out),
        grid=(nchunks,),
        in_specs=[f, r_, f, r_, f, r_, f, r_, f, r_, f, r_, cfs(fwd_idx), cfs(rev_idx),
                  full(mask), full(eye), full(e16)],
        out_specs=(ys(fwd_idx), ys(rev_idx)),
        scratch_shapes=[pltpu.VMEM((2 * B, HEAD_DIM, D), F32)],
        compiler_params=_cparams(("arbitrary",)),
        name="rwkv_scan",
    )(atf, atb, rtf, rtb, v, v, bhf, bhb, khf, khb, gcf, gcb, cff, cfb, mask, eye, e16)


def _rwkv_out_kernel(y_ref, bv_ref, gate_ref, ln_ref, wo_ref, ones_ref, x_ref, mod_ref, g_ref,
                     rwh_ref, rwl_ref, xo_ref, h_ref, lg_ref):
    ones = ones_ref[...]
    y = y_ref[0]
    inv = 1.0 / HEAD_DIM
    dlt = y - _segsum_wide(y, ones) * inv
    yn = dlt * lax.rsqrt(_segsum_wide(dlt * dlt, ones) * inv + GN_EPS)
    ln = ln_ref[...]
    o = (yn * ln[0:1] + ln[1:2] + bv_ref[0]) * gate_ref[0]
    yl = _dot(o.astype(BF16), wo_ref[...])
    _ffn_prep(x_ref[0], yl, mod_ref[0, 0], g_ref[...], rwh_ref, rwl_ref, xo_ref, h_ref, lg_ref)


def _rwkv_out(y, bv, gate, ln, wo, ones, x, mods, g, rwh, rwl):
    B, T, D = x.shape
    row, mod, full = _row_specs(D)
    shapes, specs = _ffn_prep_outs(B, T, D)
    return pl.pallas_call(
        _rwkv_out_kernel,
        out_shape=shapes,
        grid=(B, T // TM),
        in_specs=[row(D), row(D), row(D), full(ln), full(wo), full(ones), row(D), mod, full(g),
                  full(rwh), full(rwl)],
        out_specs=specs,
        compiler_params=_cparams(("parallel", "parallel")),
        name="rwkv_out",
    )(y, bv, gate, ln, wo, ones, x, mods, g, rwh, rwl)


def _rope_tables(S, L):
    rows = S // GRID_W
    row = jnp.repeat(jnp.arange(rows, dtype=F32), GRID_W)
    col = (jnp.arange(rows * GRID_W) % GRID_W).astype(F32)
    n_freq = HEAD_DIM // 4
    inv = ROPE_THETA ** (-jnp.arange(n_freq, dtype=F32) / n_freq)
    lane = np.arange(LANES) % HEAD_DIM
    axis, half, freq = lane // 32, (lane % 32) // 16, lane % 16
    pos = jnp.where(jnp.asarray(axis == 0)[None, :], row[:, None], col[:, None])
    ang = pos * inv[freq][None, :]
    sgn = jnp.asarray(np.where(half == 0, -1.0, 1.0), dtype=F32)
    cos = jnp.concatenate([jnp.ones((L, LANES), F32), jnp.cos(ang)], axis=0)
    sin = jnp.concatenate([jnp.zeros((L, LANES), F32), jnp.sin(ang) * sgn[None, :]], axis=0)
    return cos, sin


def kernel(x, c, ctx, c_ctx, ada_w, ada_b, norm_mix_g, norm_ffn_g, attn_w_in, attn_w_out, attn_sink,
           attn_q_norm_g, attn_k_norm_g, rwkv_x_mix, rwkv_w_r, rwkv_w_k, rwkv_w_v, rwkv_w_o,
           rwkv_decay_w0, rwkv_decay_w1, rwkv_decay_w2, rwkv_iclr_a0, rwkv_iclr_a1, rwkv_iclr_a2,
           rwkv_gate_g1, rwkv_gate_g2, rwkv_k_k, rwkv_k_a, rwkv_r_k, rwkv_ln_g, rwkv_ln_b,
           router_w, router_bias, moe_w1, moe_w3, moe_w2, final_norm_g):
    B, S, D = x.shape
    L = ctx.shape[1]
    T = L + S
    depth = ada_w.shape[0]
    assert D == D_MODEL and L == TM and S % TM == 0 and B == 2 and depth == 2
    ones = _seg_ones()
    bf = lambda a: a.astype(BF16)

    cs = jnp.zeros((8, D), F32).at[:B].set(c).at[B].set(c_ctx)
    ada = _ada(cs, ada_w, ada_b).reshape(depth, 8, 6, D)
    mods = [jnp.stack([jnp.broadcast_to(ada[i, B], (B, 6, D)), ada[i, :B]], axis=1) for i in range(depth)]

    xa = jnp.concatenate([ctx, x], axis=1)
    rw = jnp.zeros((D, LANES), F32).at[:, :N_EXPERTS].set(router_w)
    rwh, rwl = _split(rw)

    w_in = attn_w_in[0]
    roped = np.concatenate([np.arange(0, 640), np.arange(768, 1408)])
    w_rot = w_in[:, roped ^ 16]
    cos, sin = _rope_tables(S, L)
    lane = np.arange(LANES) % HEAD_DIM
    gains = lambda g: jnp.stack([g[lane], g[lane ^ 16]], axis=0)
    qa, ka, va, qb, kb, vb = _inproj(xa, mods[0], norm_mix_g[0].reshape(1, D), bf(w_in), bf(w_rot), cos, sin,
                                     gains(attn_q_norm_g[0]), gains(attn_k_norm_g[0]), ones)
    grouped = lambda q: q.reshape(B, A_KV_HEADS, GROUP, T, HEAD_DIM)
    qa, qb = grouped(qa), grouped(qb)
    sink = attn_sink[0].astype(F32) * LOG2E
    ext = lambda v: jnp.swapaxes(
        jnp.concatenate([v, jnp.ones_like(v[..., :1]), jnp.zeros_like(v[..., :HEAD_DIM - 1])], axis=-1), 2, 3)
    va_x, vb_x = ext(va), ext(vb)
    nosink = jnp.full((B_Q_HEADS,), NEG, F32)
    oa_l = _window_attn(sink, qa, ka, va, L, S)
    oa_c = _flash(sink, qa, ka, va_x, q_rows=L, q_off=0, k_rows=L, tq=L, tk=L)
    ob_l = _flash(nosink, qb, kb, vb_x, q_rows=S, q_off=L, k_rows=T, tq=256, tk=_key_tile(T))
    ob_c = _flash(nosink, qb, kb, vb_x, q_rows=L, q_off=0, k_rows=L, tq=L, tk=L)
    oa = jnp.concatenate([oa_c, oa_l], axis=1)
    ob = jnp.concatenate([ob_c, ob_l], axis=1)
    w_out = bf(attn_w_out[0])
    na = A_Q_HEADS * HEAD_DIM
    xa, h, lg = _attn_out(oa, ob, w_out[:na], w_out[na:], xa, mods[0], norm_ffn_g[0].reshape(1, D), rwh, rwl)
    f = _moe(h.reshape(B * T, D), lg.reshape(B * T, LANES), router_bias,
             bf(moe_w1[0]), bf(moe_w3[0]), bf(moe_w2[0]))
    xa = _residual(xa, f.reshape(B, T, D), mods[0])

    cat2 = lambda a: jnp.concatenate([a[0], a[1]], axis=1)
    vec = jnp.stack([rwkv_k_k[0], rwkv_k_a[0], rwkv_r_k[0].reshape(D), rwkv_decay_w0[0, 0], rwkv_decay_w0[0, 1],
                     rwkv_iclr_a0[0, 0], rwkv_iclr_a0[0, 1], jnp.zeros((D,), F32)], axis=0)
    outs = _rwkv_proj(xa, mods[1], norm_mix_g[1].reshape(1, D), jnp.pad(rwkv_x_mix[0], ((0, 2), (0, 0))),
                      bf(rwkv_w_r[0]), bf(rwkv_w_k[0]), bf(rwkv_w_v[0]),
                      bf(cat2(rwkv_decay_w1[0])), bf(rwkv_decay_w2[0]),
                      bf(cat2(rwkv_iclr_a1[0])), bf(rwkv_iclr_a2[0]),
                      bf(rwkv_gate_g1[0]), bf(rwkv_gate_g2[0]), vec, ones)
    r, v, kk, bv, gate, w0, w1, kd0, kd1, bd0, bd1 = outs
    lane_id = np.arange(D)
    eye = jnp.asarray(np.arange(HEAD_DIM)[:, None] == (lane_id % HEAD_DIM)[None, :], dtype=BF16)
    n_heads = D // HEAD_DIM
    head_mask = jnp.asarray(np.arange(n_heads)[:, None] == (lane_id // HEAD_DIM)[None, :], dtype=F32)
    blk = np.arange(8)[:, None, None]
    sel = jnp.asarray(np.arange(LANES)[None, None, :] == blk * n_heads + (lane_id // HEAD_DIM)[None, :, None],
                      dtype=BF16)
    e16 = jnp.asarray(np.arange(n_heads)[:, None] == (np.arange(LANES) % n_heads)[None, :], dtype=F32)
    scan_ins = []
    for d, (w_d, kd_d, bd_d) in enumerate(((w0, kd0, bd0), (w1, kd1, bd1))):
        at, rt, bh, kh, gc, pairs = _rwkv_coef(r, kk, w_d, kd_d, bd_d, sel, reverse=d == 1)
        scan_ins.append((at, rt, bh, kh, gc, _mid_rows(pairs, reverse=d == 1)))
    yf, yb = _rwkv_scan(scan_ins[0], scan_ins[1], v, head_mask, eye, e16, L)
    ycols = lambda t: t[..., CHUNK * n_heads:].reshape(B, T // CHUNK, HEAD_DIM, CHUNK, n_heads)
    ysum = ycols(yf) + jnp.flip(ycols(yb), axis=3)
    y = jnp.transpose(ysum, (0, 1, 3, 4, 2)).reshape(B, T, D)
    ln = jnp.stack([rwkv_ln_g[0], rwkv_ln_b[0]] + [jnp.zeros((D,), F32)] * 6, axis=0)
    xa, h, lg = _rwkv_out(y, bv, gate, ln, bf(rwkv_w_o[0]), ones, xa, mods[1],
                          norm_ffn_g[1].reshape(1, D), rwh, rwl)
    f = _moe(h[:, L:].reshape(B * S, D), lg[:, L:].reshape(B * S, LANES), router_bias,
             bf(moe_w1[1]), bf(moe_w3[1]), bf(moe_w2[1]))
    return _final(xa, f.reshape(B, S, D), mods[1], final_norm_g.reshape(1, D), L)


def _key_tile(T):
    for tk in (1280, 1024, 768, 512, 256):
        if T % tk == 0:
            return tk
    raise ValueError(T)
```

```python
import functools

import numpy as np
import jax
import jax.numpy as jnp
from jax import lax
from jax.experimental import pallas as pl
from jax.experimental.pallas import tpu as pltpu

F32 = jnp.float32
BF16 = jnp.bfloat16

D_MODEL = 1024
HEAD_DIM = 64
GRID_W = 64
ROPE_THETA = 10000.0
RMS_EPS = 1e-6
GN_EPS = 64e-5
A_Q_HEADS = 8
A_KV_HEADS = 2
B_Q_HEADS = 8
B_KV_HEADS = 2
GROUP = 4
WINDOW = 128
N_EXPERTS = 16
N_GROUPS = 4
EXPERTS_PER_GROUP = 4
TOP_K = 2
LANES = 128
TM = 256
MOE_ROWS = 512
SCAN_CHUNK = 64
VMEM_LIMIT = 56 * 1024 * 1024
NEG = -1e30
LOG2E = 1.4426950408889634


def _cparams(sem):
    return pltpu.CompilerParams(dimension_semantics=sem, vmem_limit_bytes=VMEM_LIMIT)


def _dot(a, b):
    return jnp.dot(a, b, preferred_element_type=F32)


def _dot_nt(a, b):
    return lax.dot_general(a, b, (((1,), (1,)), ((), ())), preferred_element_type=F32)


def _split(x):
    hi = x.astype(BF16)
    lo = (x - hi.astype(F32)).astype(BF16)
    return hi, lo


def _dot3(x, w):
    xh, xl = _split(x)
    wh, wl = _split(w)
    return _dot(xh, wh) + _dot(xh, wl) + _dot(xl, wh)


def _segsum(v, ones):
    hi, lo = _split(v)
    return _dot(hi, ones) + _dot(lo, ones)


def _segsum_wide(v, ones):
    n = v.shape[1] // LANES
    return jnp.concatenate([_segsum(v[:, j * LANES:(j + 1) * LANES], ones) for j in range(n)], axis=1)


def _norm_mod(x, g, shift, scale):
    ms = jnp.mean(x * x, axis=-1, keepdims=True)
    return (x * lax.rsqrt(ms + RMS_EPS) * g) * (1.0 + scale) + shift


def _sigmoid(x):
    return 1.0 / (1.0 + jnp.exp(-x))


def _seg_ones():
    i = np.arange(LANES)
    return jnp.asarray((i[:, None] // HEAD_DIM) == (i[None, :] // HEAD_DIM), dtype=BF16)


def _ada_kernel(c_ref, w_ref, b_ref, o_ref):
    c = c_ref[...]
    s = c * _sigmoid(c)
    o_ref[0] = _dot3(s, w_ref[0]) + b_ref[0]


def _ada(cs, ada_w, ada_b):
    depth, d, n = ada_w.shape
    tn = 1536
    return pl.pallas_call(
        _ada_kernel,
        out_shape=jax.ShapeDtypeStruct((depth, 8, n), F32),
        grid=(depth, n // tn),
        in_specs=[
            pl.BlockSpec((8, d), lambda l, j: (0, 0)),
            pl.BlockSpec((1, d, tn), lambda l, j: (l, 0, j)),
            pl.BlockSpec((1, 1, tn), lambda l, j: (l, 0, j)),
        ],
        out_specs=pl.BlockSpec((1, 8, tn), lambda l, j: (l, 0, j)),
        compiler_params=_cparams(("arbitrary", "arbitrary")),
        name="ada",
    )(cs, ada_w, ada_b.reshape(depth, 1, n))


def _inproj_kernel(x_ref, mod_ref, g_ref, w_ref, wrot_ref, cos_ref, sin_ref, gq_ref, gk_ref, ones_ref,
                   qa_ref, ka_ref, va_ref, qb_ref, kb_ref, vb_ref):
    mod = mod_ref[0, 0]
    h = _norm_mod(x_ref[0], g_ref[...], mod[0:1], mod[1:2]).astype(BF16)
    y = _dot(h, w_ref[...])
    yr = _dot(h, wrot_ref[...])
    cos = cos_ref[...]
    sin = sin_ref[...]
    ones = ones_ref[...]
    qscale = HEAD_DIM ** -0.5 * LOG2E

    def put(ref, tile, val):
        ref[0, 2 * tile] = val[:, :HEAD_DIM].astype(ref.dtype)
        ref[0, 2 * tile + 1] = val[:, HEAD_DIM:].astype(ref.dtype)

    def chunk(a, c):
        return a[:, c * LANES:(c + 1) * LANES]

    for c in range(4):
        put(qa_ref, c, (chunk(y, c) * cos + chunk(yr, c) * sin) * qscale)
    put(ka_ref, 0, chunk(y, 4) * cos + chunk(yr, 4) * sin)
    put(va_ref, 0, chunk(y, 5))

    def normed(c, cr, gain_ref):
        v = chunk(y, c)
        rs = lax.rsqrt(_segsum(v * v, ones) * (1.0 / HEAD_DIM) + RMS_EPS)
        return (v * rs * gain_ref[0:1]) * cos + (chunk(yr, cr) * rs * gain_ref[1:2]) * sin

    for c in range(4):
        put(qb_ref, c, normed(6 + c, 5 + c, gq_ref) * qscale)
    put(kb_ref, 0, normed(10, 9, gk_ref))
    put(vb_ref, 0, chunk(y, 11))


def _inproj(x, mods, g, w_in, w_rot, cos, sin, gq2, gk2, ones):
    B, T, D = x.shape
    nt = T // TM
    heads = lambda n: jax.ShapeDtypeStruct((B, n, T, HEAD_DIM), BF16)
    hspec = lambda n: pl.BlockSpec((1, n, TM, HEAD_DIM), lambda b, i: (b, 0, i, 0))
    full = lambda a: pl.BlockSpec(a.shape, lambda b, i: (0,) * a.ndim)
    return pl.pallas_call(
        _inproj_kernel,
        out_shape=(heads(8), heads(2), heads(2), heads(8), heads(2), heads(2)),
        grid=(B, nt),
        in_specs=[
            pl.BlockSpec((1, TM, D), lambda b, i: (b, i, 0)),
            pl.BlockSpec((1, 1, 6, D), lambda b, i: (b, jnp.minimum(i, 1), 0, 0)),
            full(g), full(w_in), full(w_rot),
            pl.BlockSpec((TM, LANES), lambda b, i: (i, 0)),
            pl.BlockSpec((TM, LANES), lambda b, i: (i, 0)),
            full(gq2), full(gk2), full(ones),
        ],
        out_specs=(hspec(8), hspec(2), hspec(2), hspec(8), hspec(2), hspec(2)),
        compiler_params=_cparams(("parallel", "parallel")),
        name="attn_inproj",
    )(x, mods, g, w_in, w_rot, cos, sin, gq2, gk2, ones)


LOOKAHEAD = 2


def _flash_kernel(sink_ref, q_ref, k_ref, v_ref, o_ref, m_scr, acc_scr, s_scr, *, tk, nk):
    h = pl.program_id(1)
    m_scr[...] = jnp.full(m_scr.shape, NEG, F32)
    acc_scr[...] = jnp.zeros(acc_scr.shape, F32)

    def scores(j, g):
        return _dot_nt(k_ref[0, 0, pl.ds(pl.multiple_of(j * tk, tk), tk), :], q_ref[0, 0, g])

    for g in range(LOOKAHEAD):
        s_scr[g] = scores(0, g)

    def body(j, carry):
        vt = v_ref[0, 0, :, pl.ds(pl.multiple_of(j * tk, tk), tk)]
        jn = jnp.minimum(j + 1, nk - 1)
        ahead = {}
        for g in range(GROUP):
            st = s_scr[g] if g < LOOKAHEAD else ahead.pop(g)
            if g + LOOKAHEAD < GROUP:
                ahead[g + LOOKAHEAD] = scores(j, g + LOOKAHEAD)
            m_prev = m_scr[g]
            m_new = jnp.maximum(m_prev, jnp.max(st, axis=0, keepdims=True))
            p = jnp.exp2(st - m_new).astype(BF16)
            if g + LOOKAHEAD >= GROUP:
                s_scr[g + LOOKAHEAD - GROUP] = scores(jn, g + LOOKAHEAD - GROUP)
            acc_scr[g] = jnp.exp2(m_prev - m_new) * acc_scr[g] + _dot(vt, p)
            m_scr[g] = m_new
        return carry

    lax.fori_loop(0, nk, body, 0)
    outs = []
    for g in range(GROUP):
        acc = acc_scr[g]
        l = acc[HEAD_DIM:HEAD_DIM + 1] + jnp.exp2(sink_ref[h * GROUP + g] - m_scr[g])
        outs.append(acc[:HEAD_DIM] / l)
    o_ref[0] = jnp.concatenate(outs, axis=0).T.astype(o_ref.dtype)


def _flash(sink, q, k, v, *, q_rows, q_off, k_rows, tq, tk):
    B, Hkv = k.shape[:2]
    nq, nk = q_rows // tq, k_rows // tk
    qo = q_off // tq
    return pl.pallas_call(
        functools.partial(_flash_kernel, tk=tk, nk=nk),
        out_shape=jax.ShapeDtypeStruct((B, q_rows, Hkv * GROUP * HEAD_DIM), BF16),
        grid=(B, Hkv, nq),
        in_specs=[
            pl.BlockSpec(memory_space=pltpu.SMEM),
            pl.BlockSpec((1, 1, GROUP, tq, HEAD_DIM), lambda b, h, i: (b, h, 0, i + qo, 0)),
            pl.BlockSpec((1, 1, k_rows, HEAD_DIM), lambda b, h, i: (b, h, 0, 0)),
            pl.BlockSpec((1, 1, LANES, k_rows), lambda b, h, i: (b, h, 0, 0)),
        ],
        out_specs=pl.BlockSpec((1, tq, GROUP * HEAD_DIM), lambda b, h, i: (b, i, h)),
        scratch_shapes=[
            pltpu.VMEM((GROUP, 1, tq), F32),
            pltpu.VMEM((GROUP, LANES, tq), F32),
            pltpu.VMEM((LOOKAHEAD, tk, tq), F32),
        ],
        compiler_params=_cparams(("parallel", "parallel", "arbitrary")),
        name="flash_attn",
    )(sink, q, k, v)


def _window_kernel(sink_ref, q_ref, kc_ref, vc_ref, k0_ref, k1_ref, k2_ref, v0_ref, v1_ref, v2_ref, o_ref, *, nb):
    h = pl.program_id(1)
    i = pl.program_id(2)
    rows = GROUP * WINDOW
    q = q_ref[0, 0].reshape(rows, HEAD_DIM)
    r = lax.broadcasted_iota(jnp.int32, (rows, WINDOW), 0) & (WINDOW - 1)
    c = lax.broadcasted_iota(jnp.int32, (rows, WINDOW), 1)
    sc = _dot_nt(q, kc_ref[0, 0])
    s0 = jnp.where((c >= r) & (i > 0), _dot_nt(q, k0_ref[0, 0]), NEG)
    s1 = _dot_nt(q, k1_ref[0, 0])
    s2 = jnp.where((c <= r) & (i < nb - 1), _dot_nt(q, k2_ref[0, 0]), NEG)
    sink = jnp.concatenate(
        [jnp.full((WINDOW, 1), sink_ref[h * GROUP + g], F32) for g in range(GROUP)], axis=0)
    rowmax = lambda s: jnp.max(s, axis=-1, keepdims=True)
    m = jnp.maximum(jnp.maximum(rowmax(sc), rowmax(s0)), jnp.maximum(rowmax(s1), rowmax(s2)))
    m = jnp.maximum(m, sink)
    pc, p0, p1, p2 = (jnp.exp2(s - m) for s in (sc, s0, s1, s2))
    rowsum = lambda p: jnp.sum(p, axis=-1, keepdims=True)
    l = rowsum(pc) + rowsum(p0) + rowsum(p1) + rowsum(p2) + jnp.exp2(sink - m)
    acc = (_dot(pc.astype(BF16), vc_ref[0, 0]) + _dot(p0.astype(BF16), v0_ref[0, 0])
           + _dot(p1.astype(BF16), v1_ref[0, 0]) + _dot(p2.astype(BF16), v2_ref[0, 0]))
    out = acc / l
    for g in range(GROUP):
        o_ref[0, :, g * HEAD_DIM:(g + 1) * HEAD_DIM] = out[g * WINDOW:(g + 1) * WINDOW].astype(o_ref.dtype)


def _window_attn(sink, q, k, v, L, S):
    B, Hkv = k.shape[:2]
    nb = S // WINDOW
    pad = ((0, 0), (0, 0), (WINDOW, WINDOW), (0, 0))
    kp = jnp.pad(k[:, :, L:], pad)
    vp = jnp.pad(v[:, :, L:], pad)
    qo = L // WINDOW
    band = lambda j: pl.BlockSpec((1, 1, WINDOW, HEAD_DIM), lambda b, h, i: (b, h, i + j, 0))
    ctx = pl.BlockSpec((1, 1, L, HEAD_DIM), lambda b, h, i: (b, h, 0, 0))
    return pl.pallas_call(
        functools.partial(_window_kernel, nb=nb),
        out_shape=jax.ShapeDtypeStruct((B, S, Hkv * GROUP * HEAD_DIM), BF16),
        grid=(B, Hkv, nb),
        in_specs=[
            pl.BlockSpec(memory_space=pltpu.SMEM),
            pl.BlockSpec((1, 1, GROUP, WINDOW, HEAD_DIM), lambda b, h, i: (b, h, 0, i + qo, 0)),
            ctx, ctx, band(0), band(1), band(2), band(0), band(1), band(2),
        ],
        out_specs=pl.BlockSpec((1, WINDOW, GROUP * HEAD_DIM), lambda b, h, i: (b, i, h)),
        compiler_params=_cparams(("parallel", "parallel", "parallel")),
        name="window_attn",
    )(sink, q, k, v, kp, kp, kp, vp, vp, vp)


def _ffn_prep(x, y, mod, gffn, rwh_ref, rwl_ref, xo_ref, h_ref, lg_ref):
    xn = x + mod[2:3] * y
    h = _norm_mod(xn, gffn, mod[3:4], mod[4:5])
    xo_ref[0] = xn
    hh, hl = _split(h)
    h_ref[0] = hh
    rwh = rwh_ref[...]
    lg_ref[0] = _dot(hh, rwh) + _dot(hl, rwh) + _dot(hh, rwl_ref[...])


def _attn_out_kernel(oa_ref, ob_ref, wa_ref, wb_ref, x_ref, mod_ref, g_ref, rwh_ref, rwl_ref,
                     xo_ref, h_ref, lg_ref):
    y = _dot(oa_ref[0], wa_ref[...]) + _dot(ob_ref[0], wb_ref[...])
    _ffn_prep(x_ref[0], y, mod_ref[0, 0], g_ref[...], rwh_ref, rwl_ref, xo_ref, h_ref, lg_ref)


def _row_specs(D):
    row = lambda w: pl.BlockSpec((1, TM, w), lambda b, i: (b, i, 0))
    mod = pl.BlockSpec((1, 1, 6, D), lambda b, i: (b, jnp.minimum(i, 1), 0, 0))
    full = lambda a: pl.BlockSpec(a.shape, lambda b, i: (0,) * a.ndim)
    return row, mod, full


def _ffn_prep_outs(B, T, D):
    row, _, _ = _row_specs(D)
    shapes = (jax.ShapeDtypeStruct((B, T, D), F32), jax.ShapeDtypeStruct((B, T, D), BF16),
              jax.ShapeDtypeStruct((B, T, LANES), F32))
    return shapes, (row(D), row(D), row(LANES))


def _attn_out(oa, ob, wa, wb, x, mods, g, rwh, rwl):
    B, T, D = x.shape
    row, mod, full = _row_specs(D)
    shapes, specs = _ffn_prep_outs(B, T, D)
    return pl.pallas_call(
        _attn_out_kernel,
        out_shape=shapes,
        grid=(B, T // TM),
        in_specs=[row(oa.shape[-1]), row(ob.shape[-1]), full(wa), full(wb), row(D), mod, full(g),
                  full(rwh), full(rwl)],
        out_specs=specs,
        compiler_params=_cparams(("parallel", "parallel")),
        name="attn_out",
    )(oa, ob, wa, wb, x, mods, g, rwh, rwl)


def _gmm_kernel(be_ref, nu_ref, x_ref, w1_ref, w3_ref, w2_ref, o_ref):
    i = pl.program_id(0)

    @pl.when(i < nu_ref[0])
    def _():
        x = x_ref[...]
        a = _dot(x, w1_ref[0].astype(BF16))
        b = _dot(x, w3_ref[0].astype(BF16))
        mid = (a * _sigmoid(a)) * b
        o_ref[...] = _dot(mid.astype(BF16), w2_ref[0].astype(BF16))

    @pl.when(i >= nu_ref[0])
    def _():
        o_ref[...] = jnp.zeros(o_ref.shape, o_ref.dtype)


def _gmm(block_expert, n_used, xs, w1, w3, w2):
    n_slots, D = xs.shape
    F = w1.shape[-1]
    nblk = n_slots // MOE_ROWS
    return pl.pallas_call(
        _gmm_kernel,
        out_shape=jax.ShapeDtypeStruct((n_slots, D), F32),
        grid_spec=pltpu.PrefetchScalarGridSpec(
            num_scalar_prefetch=2,
            grid=(nblk,),
            in_specs=[
                pl.BlockSpec((MOE_ROWS, D), lambda i, be, nu: (i, 0)),
                pl.BlockSpec((1, D, F), lambda i, be, nu: (be[i], 0, 0)),
                pl.BlockSpec((1, D, F), lambda i, be, nu: (be[i], 0, 0)),
                pl.BlockSpec((1, F, D), lambda i, be, nu: (be[i], 0, 0)),
            ],
            out_specs=pl.BlockSpec((MOE_ROWS, D), lambda i, be, nu: (i, 0)),
        ),
        compiler_params=_cparams(("arbitrary",)),
        name="moe_gmm",
    )(block_expert, n_used, xs, w1, w3, w2)


ROUTE_ROWS = 512


def _route_kernel(lg_ref, bias_ref, idx_ref, w_ref):
    x = lg_ref[...].T[:N_EXPERTS]
    m = jnp.max(x, axis=0, keepdims=True)
    e = jnp.exp(x - m)
    probs = e / jnp.sum(e, axis=0, keepdims=True)
    sel = probs + bias_ref[...][:, 0:1]
    row = lambda a, i: a[i:i + 1, :]
    G = EXPERTS_PER_GROUP
    scores = []
    for g in range(N_GROUPS):
        s = [row(sel, g * G + i) for i in range(G)]
        best = None
        for i in range(G):
            for j in range(i + 1, G):
                best = s[i] + s[j] if best is None else jnp.maximum(best, s[i] + s[j])
        scores.append(best)
    top = functools.reduce(jnp.maximum, scores)
    gi = jnp.full(top.shape, N_GROUPS - 1, jnp.int32)
    for g in range(N_GROUPS - 2, -1, -1):
        gi = jnp.where(scores[g] == top, g, gi)

    def pick(a, i):
        out = row(a, (N_GROUPS - 1) * G + i)
        for g in range(N_GROUPS - 2, -1, -1):
            out = jnp.where(gi == g, row(a, g * G + i), out)
        return out

    c = [pick(sel, i) for i in range(G)]
    pc = [pick(probs, i) for i in range(G)]

    def first_argmax(vals):
        mx = functools.reduce(jnp.maximum, vals)
        idx = jnp.full(mx.shape, G - 1, jnp.int32)
        for i in range(G - 2, -1, -1):
            idx = jnp.where(vals[i] == mx, i, idx)
        return idx

    i1 = first_argmax(c)
    i2 = first_argmax([jnp.where(i1 == i, -jnp.inf, c[i]) for i in range(G)])
    take = lambda vals, idx: functools.reduce(
        lambda acc, i: jnp.where(idx == i, vals[i], acc), range(G - 2, -1, -1), vals[G - 1])
    w1, w2 = take(pc, i1), take(pc, i2)
    tot = w1 + w2
    zi = jnp.zeros((6,) + top.shape[1:], jnp.int32)
    idx_ref[...] = jnp.concatenate([gi * G + i1, gi * G + i2, zi], axis=0)
    w_ref[...] = jnp.concatenate([w1 / tot, w2 / tot, zi.astype(F32)], axis=0)


def _route(logits, router_bias):
    N = logits.shape[0]
    bias = jnp.broadcast_to(router_bias.astype(F32)[:, None], (N_EXPERTS, LANES))
    idx, w = pl.pallas_call(
        _route_kernel,
        out_shape=(jax.ShapeDtypeStruct((8, N), jnp.int32), jax.ShapeDtypeStruct((8, N), F32)),
        grid=(N // ROUTE_ROWS,),
        in_specs=[pl.BlockSpec((ROUTE_ROWS, LANES), lambda i: (i, 0)),
                  pl.BlockSpec((N_EXPERTS, LANES), lambda i: (0, 0))],
        out_specs=(pl.BlockSpec((8, ROUTE_ROWS), lambda i: (0, i)),
                   pl.BlockSpec((8, ROUTE_ROWS), lambda i: (0, i))),
        compiler_params=_cparams(("parallel",)),
        name="route",
    )(logits, bias)
    return idx[:TOP_K].T, w[:TOP_K].T


def _moe(h, logits, router_bias, w1, w3, w2):
    N, D = h.shape
    expert_idx, gate_w = _route(logits, router_bias)
    NK = N * TOP_K
    flat_e = expert_idx.reshape(NK)
    onehot = (flat_e[:, None] == jnp.arange(N_EXPERTS, dtype=jnp.int32)[None, :]).astype(jnp.int32)
    csum = jnp.cumsum(onehot, axis=0)
    counts = csum[-1]
    rank = jnp.take_along_axis(csum, flat_e[:, None], axis=1)[:, 0] - 1
    padded = (counts + MOE_ROWS - 1) // MOE_ROWS * MOE_ROWS
    pad_end = jnp.cumsum(padded)
    pad_start = pad_end - padded
    dest =pad_start[flat_e] + rank
    nblk = -(-NK // MOE_ROWS) + N_EXPERTS
    n_slots = nblk * MOE_ROWS
    n_used = (pad_end[-1] // MOE_ROWS).astype(jnp.int32)
    blk = jnp.arange(nblk, dtype=jnp.int32)
    be = jnp.sum((pad_end[None, :] <= (blk * MOE_ROWS)[:, None]).astype(jnp.int32), axis=1)
    be = jnp.minimum(be, N_EXPERTS - 1)
    be = jnp.where(blk < n_used, be, be[jnp.maximum(n_used - 1, 0)])
    flat_tok = jnp.arange(NK, dtype=jnp.int32) // TOP_K
    slot_tok = jnp.zeros((n_slots,), jnp.int32).at[dest].set(flat_tok, unique_indices=True)
    xs = h[slot_tok]
    ys = _gmm(be, n_used.reshape(1), xs, w1, w3, w2)
    pos = dest.reshape(N, TOP_K)
    return ys[pos[:, 0]] * gate_w[:, 0:1] + ys[pos[:, 1]] * gate_w[:, 1:2]


def _residual_kernel(x_ref, f_ref, mod_ref, o_ref):
    o_ref[0] = x_ref[0] + mod_ref[0, 0][5:6] * f_ref[0]


def _final_kernel(x_ref, f_ref, mod_ref, g_ref, o_ref):
    x = x_ref[0] + mod_ref[0, 0][5:6] * f_ref[0]
    ms = jnp.mean(x * x, axis=-1, keepdims=True)
    o_ref[0] = x * lax.rsqrt(ms + RMS_EPS) * g_ref[...]


def _residual(x, f, mods):
    B, T, D = x.shape
    row, mod, _ = _row_specs(D)
    return pl.pallas_call(
        _residual_kernel,
        out_shape=jax.ShapeDtypeStruct((B, T, D), F32),
        grid=(B, T // TM),
        in_specs=[row(D), row(D), mod],
        out_specs=row(D),
        compiler_params=_cparams(("parallel", "parallel")),
        name="residual",
    )(x, f, mods)


def _final(x, f, mods, g, L):
    B, S, D = f.shape
    off = L // TM
    mod = pl.BlockSpec((1, 1, 6, D), lambda b, i: (b, 1, 0, 0))
    return pl.pallas_call(
        _final_kernel,
        out_shape=jax.ShapeDtypeStruct((B, S, D), F32),
        grid=(B, S // TM),
        in_specs=[pl.BlockSpec((1, TM, D), lambda b, i: (b, i + off, 0)),
                  pl.BlockSpec((1, TM, D), lambda b, i: (b, i, 0)), mod,
                  pl.BlockSpec(g.shape, lambda b, i: (0, 0))],
        out_specs=pl.BlockSpec((1, TM, D), lambda b, i: (b, i, 0)),
        compiler_params=_cparams(("parallel", "parallel")),
        name="final_norm",
    )(x, f, mods, g)


def _rwkv_proj_kernel(x_ref, xp_ref, xn_ref, mod_ref, g_ref, xmix_ref, wr_ref, wk_ref, wv_ref,
                      dw1_ref, dw2_ref, da1_ref, da2_ref, g1_ref, g2_ref, vec_ref, ones_ref,
                      r_ref, v_ref, kk_ref, bv_ref, gate_ref, w0_ref, w1_ref, kd0_ref, kd1_ref, bd0_ref, bd1_ref,
                      *, nt):
    i = pl.program_id(1)
    mod = mod_ref[0, 0]
    g = g_ref[...]
    nm = lambda x: _norm_mod(x, g, mod[0:1], mod[1:2])
    h = nm(x_ref[0])
    hp = nm(xp_ref[0])[7:8] * jnp.where(i >= 2, 1.0, 0.0)
    hn = nm(xn_ref[0])[0:1] * jnp.where((i >= 1) & (i < nt - 1), 1.0, 0.0)
    ridx = lax.broadcasted_iota(jnp.int32, h.shape, 0)
    h_dn = jnp.where(ridx == 0, hp, pltpu.roll(h, 1, axis=0))
    h_up = jnp.where(ridx == TM - 1, hn, pltpu.roll(h, TM - 1, axis=0))
    xx = 0.5 * (h_dn + h_up) - h
    xmix = xmix_ref[...]
    mix = lambda j: (h + xx * xmix[j:j + 1]).astype(BF16)
    vec = vec_ref[...]
    ones = ones_ref[...]

    r = _dot(mix(0), wr_ref[...])
    k = _dot(mix(2), wk_ref[...])
    v = _dot(mix(3), wv_ref[...])
    gate_ref[0] = _dot(_sigmoid(_dot(mix(5), g1_ref[...])).astype(BF16), g2_ref[...])
    kk = k * vec[0:1]
    kk = kk * lax.rsqrt(jnp.maximum(_segsum_wide(kk * kk, ones), 1e-24))
    lw = jnp.tanh(_dot(mix(1), dw1_ref[...])).astype(BF16)
    la = _dot(mix(4), da1_ref[...]).astype(BF16)
    r_ref[0] = r
    v_ref[0] = v
    kk_ref[0] = kk
    bonus = jnp.zeros_like(r)
    lora = DECAY_LORA
    for d, (w_ref, kd_ref, bd_ref) in enumerate(((w0_ref, kd0_ref, bd0_ref), (w1_ref, kd1_ref, bd1_ref))):
        z = -(vec[3 + d:4 + d] + _dot(lw[:, d * lora:(d + 1) * lora], dw2_ref[d]))
        softplus = jnp.maximum(z, 0.0) + jnp.log(1.0 + jnp.exp(-jnp.abs(z)))
        w_ref[0] = jnp.exp(-jnp.exp(-softplus - 0.5))
        iclr = _sigmoid(vec[5 + d:6 + d] + _dot(la[:, d * lora:(d + 1) * lora], da2_ref[d]))
        kd = k * (1.0 + (iclr - 1.0) * vec[1:2])
        kd_ref[0] = kd
        bd_ref[0] = kk * iclr
        bonus = bonus + _segsum_wide(r * kd * vec[2:3], ones)
    bv_ref[0] = bonus * v


DECAY_LORA = 64


def _rwkv_proj(x, mods, g, xmix, wr, wk, wv, dw1, dw2, da1, da2, g1, g2, vec, ones):
    B, T, D = x.shape
    nt = T // TM
    row, mod, full = _row_specs(D)
    r8 = TM // 8
    prev = pl.BlockSpec((1, 8, D), lambda b, i: (b, jnp.maximum(i * r8 - 1, 0), 0))
    nxt = pl.BlockSpec((1, 8, D), lambda b, i: (b, jnp.minimum((i + 1) * r8, T // 8 - 1), 0))
    out = jax.ShapeDtypeStruct((B, T, D), F32)
    return pl.pallas_call(
        functools.partial(_rwkv_proj_kernel, nt=nt),
        out_shape=(out,) * 11,
        grid=(B, nt),
        in_specs=[row(D), prev, nxt, mod, full(g), full(xmix), full(wr), full(wk), full(wv),
                  full(dw1), full(dw2), full(da1), full(da2), full(g1), full(g2), full(vec), full(ones)],
        out_specs=(row(D),) * 11,
        compiler_params=_cparams(("parallel", "parallel")),
        name="rwkv_proj",
    )(x, x, x, mods, g, xmix, wr, wk, wv, dw1, dw2, da1, da2, g1, g2, vec, ones)


CHUNK = 4
N_HEADS = D_MODEL // HEAD_DIM
_AB, _AK, _RB, _RK = 0, 3, 6, 10


def _coef_kernel(r_ref, kk_ref, w_ref, kd_ref, bd_ref, sel_ref, at_ref, rt_ref, bh_ref, kh_ref, gc_ref, c_ref,
                 *, reverse):
    r, a, w, kd, bd = r_ref[0], -kk_ref[0], w_ref[0], kd_ref[0], bd_ref[0]
    rows = r.shape[0]
    p = lax.broadcasted_iota(jnp.int32, r.shape, 0) & (CHUNK - 1)
    s = (CHUNK - 1 - p) if reverse else p
    back = lambda x, k: pltpu.roll(x, (rows - k) if reverse else k, axis=0)
    ahead = lambda x, k: pltpu.roll(x, k if reverse else (rows - k), axis=0)
    wb = [None] + [back(w, k) for k in range(1, CHUNK)]
    excl = jnp.ones_like(w)
    rest = jnp.ones_like(w)
    for k in range(1, CHUNK):
        excl = excl * jnp.where(s >= k, wb[k], 1.0)
        rest = rest * jnp.where(s + k <= CHUNK - 1, ahead(w, k), 1.0)
    at_ref[0] = a * excl
    rt_ref[0] = r * (excl * w)
    bh_ref[0] = bd * rest
    kh_ref[0] = kd * rest
    gc_ref[0] = excl * w * rest
    between = [None, None, wb[1], wb[1] * wb[2]]
    rw = r * w
    prods = []
    for y in (bd, kd):
        for dist in range(1, CHUNK):
            e = back(y, dist) if between[dist] is None else between[dist] * back(y, dist)
            prods.append(a * e)
    for y in (bd, kd):
        prods.append(r * y)
        for dist in range(1, CHUNK):
            e = back(y, dist) if between[dist] is None else between[dist] * back(y, dist)
            prods.append(rw * e)
    halves = []
    for half in range(2):
        acc = None
        for blk in range(8):
            i = half * 8 + blk
            if i < len(prods):
                t = _dot(prods[i].astype(BF16), sel_ref[blk])
                acc = t if acc is None else acc + t
        halves.append(acc)
    c_ref[0] = jnp.concatenate(halves, axis=1)


def _rwkv_coef(r, kk, w, kd, bd, sel, reverse):
    B, T, D = r.shape
    row, _, full = _row_specs(D)
    out = jax.ShapeDtypeStruct((B, T, D), F32)
    return pl.pallas_call(
        functools.partial(_coef_kernel, reverse=reverse),
        out_shape=(out,) * 5 + (jax.ShapeDtypeStruct((B, T, 2 * LANES), F32),),
        grid=(B, T // TM),
        in_specs=[row(D)] * 5 + [full(sel)],
        out_specs=(row(D),) * 5 + (row(2 * LANES),),
        compiler_params=_cparams(("parallel", "parallel")),
        name="rwkv_coef",
    )(r, kk, w, kd, bd, sel)


def _mid_rows(c, reverse):
    B, T, _ = c.shape
    nc = T // CHUNK
    cc = c.reshape(B, nc, CHUNK, 16, N_HEADS).transpose(2, 3, 0, 1, 4).reshape(CHUNK, 16, B, nc * N_HEADS)
    pos = lambda s: CHUNK - 1 - s if reverse else s

    def pair(kind, s, j):
        return cc[pos(s), kind + (s - j) - (1 if kind in (_AB, _AK) else 0)]

    C = range(CHUNK)
    lab = {(s, j): pair(_AB, s, j) for s in C for j in range(s)}
    lak = {(s, j): pair(_AK, s, j) for s in C for j in range(s)}
    rb = {(s, j): pair(_RB, s, j) for s in C for j in range(s + 1)}
    rk = {(s, j): pair(_RK, s, j) for s in C for j in range(s + 1)}
    one = jnp.ones_like(rb[0, 0])
    zero = jnp.zeros_like(one)
    m = {(s, s): one for s in C}
    for j in C:
        for s in range(j + 1, CHUNK):
            m[s, j] = sum(lab[s, i] * m[i, j] for i in range(j, s))
    g = {(s, j): sum(m[s, i] * lak[i, j] for i in range(j + 1, s + 1)) for s in C for j in range(s)}
    yz = {(s, j): sum(rb[s, i] * m[i, j] for i in range(j, s + 1)) for s in C for j in range(s + 1)}
    yv = {(s, j): rk[s, j] + sum((rb[s, i] * g[i, j] for i in range(j + 1, s + 1)), zero)
          for s in C for j in range(s + 1)}
    to_u = ([[m.get((s, j), zero) for s in C] for j in C] + [[g.get((s, j), zero) for s in C] for j in C]
            + [[zero] * CHUNK for j in C])
    to_y = ([[yz.get((s, j), zero) for s in C] for j in C] + [[yv.get((s, j), zero) for s in C] for j in C]
            + [[one if s == j else zero for s in C] for j in C])
    n_in = len(to_u)
    planes = lambda rows: jnp.stack([jnp.stack(rw, axis=0) for rw in rows], axis=0)
    u_rows = planes(to_u).reshape(n_in, CHUNK, B, nc, N_HEADS).transpose(2, 3, 0, 1, 4)
    u_rows = jnp.pad(u_rows.reshape(B, nc, n_in, CHUNK * N_HEADS),
                     ((0, 0), (0, 0), (0, 16 - n_in), (0, LANES - CHUNK * N_HEADS)))
    y_rows = planes(to_y).reshape(n_in, CHUNK, B, nc, N_HEADS).transpose(2, 3, 1, 0, 4)
    if reverse:
        y_rows = jnp.flip(y_rows, axis=2)
    return u_rows, y_rows.reshape(B, T, n_in * N_HEADS)


def _scan_kernel(atf, atb, rtf, rtb, vf, vb, bhf, bhb, khf, khb, gcf, gcb, cuf, cub, cyf, cyb,
                 mask_ref, eye_ref, e16_ref, yf, yb, st, *, tc, nb):
    n = pl.program_id(0)

    @pl.when(n == 0)
    def _():
        st[...] = jnp.zeros(st.shape, F32)

    N = HEAD_DIM
    W = CHUNK * N_HEADS
    nch = tc // CHUNK
    dirs = ((atf, rtf, vf, bhf, khf, gcf, cuf, cyf, yf), (atb, rtb, vb, bhb, khb, gcb, cub, cyb, yb))
    lane1 = lax.broadcasted_iota(jnp.int32, (N, 3 * W), 1)
    lane2 = lax.broadcasted_iota(jnp.int32, (N, 2 * W), 1)
    spread = lambda x: (mask_ref[...] * x).astype(BF16)

    def chunk(ci, carry):
        work = []
        for d, refs in enumerate(dirs):
            cc = ci if d == 0 else nch - 1 - ci
            rows = [pl.ds(cc * CHUNK + (s if d == 0 else CHUNK - 1 - s), 1) for s in range(CHUNK)]
            for b in range(nb):
                work.append((d * nb + b, b, cc, rows, refs))
        firsts = []
        for gi, b, cc, rows, (AT, RT, V, BH, KH, GC, CU, CY, Y) in work:
            lhs = jnp.concatenate([st[gi].astype(BF16), eye_ref[...]], axis=0)
            w1 = jnp.concatenate([spread(X[b, rw, :]) for X in (AT, V, RT) for rw in rows], axis=0)
            firsts.append(_dot_nt(lhs, w1))
        mids = []
        for (gi, b, cc, rows, (AT, RT, V, BH, KH, GC, CU, CY, Y)), out in zip(work, firsts):
            zvq = jnp.where((lane1 >= W) & (lane1 < 2 * W), out[N:], out[:N])
            zvq16 = zvq.astype(BF16)
            cu = CU[b, cc]
            wu = jnp.concatenate([e16_ref[:, :LANES] * cu[i:i + 1, :] for i in range(3 * CHUNK)], axis=0)
            u = _dot(zvq16, wu.astype(BF16))
            wy = jnp.concatenate([e16_ref[...] * CY[b, rw, :] for rw in rows], axis=0)
            yt = _dot_nt(wy.astype(BF16), zvq16)
            for s, rw in enumerate(rows):
                Y[b, rw] = yt[s * N_HEADS:(s + 1) * N_HEADS][None]
            mids.append((zvq, u))
        for (gi, b, cc, rows, (AT, RT, V, BH, KH, GC, CU, CY, Y)), (zvq, u) in zip(work, mids):
            uv = jnp.where(lane2 < W, u, zvq[:, :2 * W]).astype(BF16)
            w2 = jnp.concatenate([spread(X[b, rw, :]) for X in (BH, KH) for rw in rows], axis=0)
            st[gi] = st[gi] * GC[b, rows[0], :] + _dot(uv, w2)
        return carry

    lax.fori_loop(0, nch, chunk, 0)


def _rwkv_scan(ins_f, ins_b, v, mask, eye, e16, L):
    B, T, D = v.shape
    tc = SCAN_CHUNK
    nch = tc // CHUNK
    nc, nchunks = L // tc, T // tc
    fwd_idx = lambda n: n
    rev_idx = lambda n: jnp.where(n < nc, nc - 1 - n, nchunks - 1 - (n - nc))
    tok = lambda idx, w: pl.BlockSpec((B, tc, w), lambda n: (0, idx(n), 0))
    cus = lambda idx: pl.BlockSpec((B, nch, 16, LANES), lambda n: (0, idx(n), 0, 0))
    ys = lambda idx: pl.BlockSpec((B, tc, N_HEADS, HEAD_DIM), lambda n: (0, idx(n), 0, 0))
    full = lambda a: pl.BlockSpec(a.shape, lambda n: (0,) * a.ndim)
    out = jax.ShapeDtypeStruct((B, T, N_HEADS, HEAD_DIM), F32)
    atf, rtf, bhf, khf, gcf, cuf, cyf = ins_f
    atb, rtb, bhb, khb, gcb, cub, cyb = ins_b
    f, r_ = tok(fwd_idx, D), tok(rev_idx, D)
    wy = cyf.shape[-1]
    return pl.pallas_call(
        functools.partial(_scan_kernel, tc=tc, nb=B),
        out_shape=(out, out),
        grid=(nchunks,),
        in_specs=[f, r_, f, r_, f, r_, f, r_, f, r_, f, r_, cus(fwd_idx), cus(rev_idx),
                  tok(fwd_idx, wy), tok(rev_idx, wy), full(mask), full(eye), full(e16)],
        out_specs=(ys(fwd_idx), ys(rev_idx)),
        scratch_shapes=[pltpu.VMEM((2 * B, HEAD_DIM, D), F32)],
        compiler_params=_cparams(("arbitrary",)),
        name="rwkv_scan",
    )(atf, atb, rtf, rtb, v, v, bhf, bhb, khf, khb, gcf, gcb, cuf, cub, cyf, cyb, mask, eye, e16)


def _rwkv_out_kernel(yf_ref, yb_ref, bv_ref, gate_ref, ln_ref, wo_ref, ones_ref, x_ref, mod_ref, g_ref,
                     rwh_ref, rwl_ref, xo_ref, h_ref, lg_ref):
    ones = ones_ref[...]
    y = yf_ref[0] + yb_ref[0]
    inv = 1.0 / HEAD_DIM
    dlt = y - _segsum_wide(y, ones) * inv
    yn = dlt * lax.rsqrt(_segsum_wide(dlt * dlt, ones) * inv + GN_EPS)
    ln = ln_ref[...]
    o = (yn * ln[0:1] + ln[1:2] + bv_ref[0]) * gate_ref[0]
    yl = _dot(o.astype(BF16), wo_ref[...])
    _ffn_prep(x_ref[0], yl, mod_ref[0, 0], g_ref[...], rwh_ref, rwl_ref, xo_ref, h_ref, lg_ref)


def _rwkv_out(yf, yb, bv, gate, ln, wo, ones, x, mods, g, rwh, rwl):
    B, T, D = x.shape
    row, mod, full = _row_specs(D)
    shapes, specs = _ffn_prep_outs(B, T, D)
    return pl.pallas_call(
        _rwkv_out_kernel,
        out_shape=shapes,
        grid=(B, T // TM),
        in_specs=[row(D), row(D), row(D), row(D), full(ln), full(wo), full(ones), row(D), mod, full(g),
                  full(rwh), full(rwl)],
        out_specs=specs,
        compiler_params=_cparams(("parallel", "parallel")),
        name="rwkv_out",
    )(yf, yb, bv, gate, ln, wo, ones, x, mods, g, rwh, rwl)


def _rope_tables(S, L):
    rows = S // GRID_W
    row = jnp.repeat(jnp.arange(rows, dtype=F32), GRID_W)
    col = (jnp.arange(rows * GRID_W) % GRID_W).astype(F32)
    n_freq = HEAD_DIM // 4
    inv = ROPE_THETA ** (-jnp.arange(n_freq, dtype=F32) / n_freq)
    lane = np.arange(LANES) % HEAD_DIM
    axis, half, freq = lane // 32, (lane % 32) // 16, lane % 16
    pos = jnp.where(jnp.asarray(axis == 0)[None, :], row[:, None], col[:, None])
    ang = pos * inv[freq][None, :]
    sgn = jnp.asarray(np.where(half == 0, -1.0, 1.0), dtype=F32)
    cos = jnp.concatenate([jnp.ones((L, LANES), F32), jnp.cos(ang)], axis=0)
    sin = jnp.concatenate([jnp.zeros((L, LANES), F32), jnp.sin(ang) * sgn[None, :]], axis=0)
    return cos, sin


def kernel(x, c, ctx, c_ctx, ada_w, ada_b, norm_mix_g, norm_ffn_g, attn_w_in, attn_w_out, attn_sink,
           attn_q_norm_g, attn_k_norm_g, rwkv_x_mix, rwkv_w_r, rwkv_w_k, rwkv_w_v, rwkv_w_o,
           rwkv_decay_w0, rwkv_decay_w1, rwkv_decay_w2, rwkv_iclr_a0, rwkv_iclr_a1, rwkv_iclr_a2,
           rwkv_gate_g1, rwkv_gate_g2, rwkv_k_k, rwkv_k_a, rwkv_r_k, rwkv_ln_g, rwkv_ln_b,
           router_w, router_bias, moe_w1, moe_w3, moe_w2, final_norm_g):
    B, S, D = x.shape
    L = ctx.shape[1]
    T = L + S
    depth = ada_w.shape[0]
    assert D == D_MODEL and L == TM and S % TM == 0 and B == 2 and depth == 2
    ones = _seg_ones()
    bf = lambda a: a.astype(BF16)

    cs = jnp.zeros((8, D), F32).at[:B].set(c).at[B].set(c_ctx)
    ada = _ada(cs, ada_w, ada_b).reshape(depth, 8, 6, D)
    mods = [jnp.stack([jnp.broadcast_to(ada[i, B], (B, 6, D)), ada[i, :B]], axis=1) for i in range(depth)]

    xa = jnp.concatenate([ctx, x], axis=1)
    rw = jnp.zeros((D, LANES), F32).at[:, :N_EXPERTS].set(router_w)
    rwh, rwl = _split(rw)

    w_in = attn_w_in[0]
    roped = np.concatenate([np.arange(0, 640), np.arange(768, 1408)])
    w_rot = w_in[:, roped ^ 16]
    cos, sin = _rope_tables(S, L)
    lane = np.arange(LANES) % HEAD_DIM
    gains = lambda g: jnp.stack([g[lane], g[lane ^ 16]], axis=0)
    qa, ka, va, qb, kb, vb = _inproj(xa, mods[0], norm_mix_g[0].reshape(1, D), bf(w_in), bf(w_rot), cos, sin,
                                     gains(attn_q_norm_g[0]), gains(attn_k_norm_g[0]), ones)
    grouped = lambda q: q.reshape(B, A_KV_HEADS, GROUP, T, HEAD_DIM)
    qa, qb = grouped(qa), grouped(qb)
    sink = attn_sink[0].astype(F32) * LOG2E
    ext = lambda v: jnp.swapaxes(
        jnp.concatenate([v, jnp.ones_like(v[..., :1]), jnp.zeros_like(v[..., :HEAD_DIM - 1])], axis=-1), 2, 3)
    va_x, vb_x = ext(va), ext(vb)
    nosink = jnp.full((B_Q_HEADS,), NEG, F32)
    oa_l = _window_attn(sink, qa, ka, va, L, S)
    oa_c = _flash(sink, qa, ka, va_x, q_rows=L, q_off=0, k_rows=L, tq=L, tk=L)
    ob_l = _flash(nosink, qb, kb, vb_x, q_rows=S, q_off=L, k_rows=T, tq=256, tk=_key_tile(T))
    ob_c = _flash(nosink, qb, kb, vb_x, q_rows=L, q_off=0, k_rows=L, tq=L, tk=L)
    oa = jnp.concatenate([oa_c, oa_l], axis=1)
    ob = jnp.concatenate([ob_c, ob_l], axis=1)
    w_out = bf(attn_w_out[0])
    na = A_Q_HEADS * HEAD_DIM
    xa, h, lg = _attn_out(oa, ob, w_out[:na], w_out[na:], xa, mods[0], norm_ffn_g[0].reshape(1, D), rwh, rwl)
    f = _moe(h.reshape(B * T, D), lg.reshape(B * T, LANES), router_bias,
             moe_w1[0], moe_w3[0], moe_w2[0])
    xa = _residual(xa, f.reshape(B, T, D), mods[0])

    cat2 = lambda a: jnp.concatenate([a[0], a[1]], axis=1)
    vec = jnp.stack([rwkv_k_k[0], rwkv_k_a[0], rwkv_r_k[0].reshape(D), rwkv_decay_w0[0, 0], rwkv_decay_w0[0, 1],
                     rwkv_iclr_a0[0, 0], rwkv_iclr_a0[0, 1], jnp.zeros((D,), F32)], axis=0)
    outs = _rwkv_proj(xa, mods[1], norm_mix_g[1].reshape(1, D), jnp.pad(rwkv_x_mix[0], ((0, 2), (0, 0))),
                      bf(rwkv_w_r[0]), bf(rwkv_w_k[0]), bf(rwkv_w_v[0]),
                      bf(cat2(rwkv_decay_w1[0])), bf(rwkv_decay_w2[0]),
                      bf(cat2(rwkv_iclr_a1[0])), bf(rwkv_iclr_a2[0]),
                      bf(rwkv_gate_g1[0]), bf(rwkv_gate_g2[0]), vec, ones)
    r, v, kk, bv, gate, w0, w1, kd0, kd1, bd0, bd1 = outs
    lane_id = np.arange(D)
    eye = jnp.asarray(np.arange(HEAD_DIM)[:, None] == (lane_id % HEAD_DIM)[None, :], dtype=BF16)
    n_heads = D // HEAD_DIM
    head_mask = jnp.asarray(np.arange(n_heads)[:, None] == (lane_id // HEAD_DIM)[None, :], dtype=F32)
    blk = np.arange(8)[:, None, None]
    sel = jnp.asarray(np.arange(LANES)[None, None, :] == blk * n_heads + (lane_id // HEAD_DIM)[None, :, None],
                      dtype=BF16)
    e16 = jnp.asarray(np.arange(n_heads)[:, None] == (np.arange(3 * CHUNK * n_heads) % n_heads)[None, :], dtype=F32)
    scan_ins = []
    for d, (w_d, kd_d, bd_d) in enumerate(((w0, kd0, bd0), (w1, kd1, bd1))):
        at, rt, bh, kh, gc, pairs = _rwkv_coef(r, kk, w_d, kd_d, bd_d, sel, reverse=d == 1)
        scan_ins.append((at, rt, bh, kh, gc) + _mid_rows(pairs, reverse=d == 1))
    yf, yb = _rwkv_scan(scan_ins[0], scan_ins[1], v, head_mask, eye, e16, L)
    ln = jnp.stack([rwkv_ln_g[0], rwkv_ln_b[0]] + [jnp.zeros((D,), F32)] * 6, axis=0)
    xa, h, lg = _rwkv_out(yf.reshape(B, T, D), yb.reshape(B, T, D), bv, gate, ln, bf(rwkv_w_o[0]), ones, xa, mods[1],
                          norm_ffn_g[1].reshape(1, D), rwh, rwl)
    f = _moe(h[:, L:].reshape(B * S, D), lg[:, L:].reshape(B * S, LANES), router_bias,
             moe_w1[1], moe_w3[1], moe_w2[1])
    return _final(xa, f.reshape(B, S, D), mods[1], final_norm_g.reshape(1, D), L)


def _key_tile(T):
    for tk in (1280, 1024, 768, 512, 256):
        if T % tk == 0:
            return tk
    raise ValueError(T)
```

```python
import functools

import numpy as np
import jax
import jax.numpy as jnp
from jax import lax
from jax.experimental import pallas as pl
from jax.experimental.pallas import tpu as pltpu

F32 = jnp.float32
BF16 = jnp.bfloat16

D_MODEL = 1024
HEAD_DIM = 64
GRID_W = 64
ROPE_THETA = 10000.0
RMS_EPS = 1e-6
GN_EPS = 64e-5
A_Q_HEADS = 8
A_KV_HEADS = 2
B_Q_HEADS = 8
B_KV_HEADS = 2
GROUP = 4
WINDOW = 128
N_EXPERTS = 16
N_GROUPS = 4
EXPERTS_PER_GROUP = 4
TOP_K = 2
LANES = 128
TM = 256
MOE_ROWS = 512
SCAN_CHUNK = 64
VMEM_LIMIT = 56 * 1024 * 1024
NEG = -1e30
LOG2E = 1.4426950408889634


def _cparams(sem):
    return pltpu.CompilerParams(dimension_semantics=sem, vmem_limit_bytes=VMEM_LIMIT)


def _dot(a, b):
    return jnp.dot(a, b, preferred_element_type=F32)


def _dot_nt(a, b):
    return lax.dot_general(a, b, (((1,), (1,)), ((), ())), preferred_element_type=F32)


def _split(x):
    hi = x.astype(BF16)
    lo = (x - hi.astype(F32)).astype(BF16)
    return hi, lo


def _dot3(x, w):
    xh, xl = _split(x)
    wh, wl = _split(w)
    return _dot(xh, wh) + _dot(xh, wl) + _dot(xl, wh)


def _segsum(v, ones):
    hi, lo = _split(v)
    return _dot(hi, ones) + _dot(lo, ones)


def _segsum_wide(v, ones):
    n = v.shape[1] // LANES
    return jnp.concatenate([_segsum(v[:, j * LANES:(j + 1) * LANES], ones) for j in range(n)], axis=1)


def _norm_mod(x, g, shift, scale):
    ms = jnp.mean(x * x, axis=-1, keepdims=True)
    return (x * lax.rsqrt(ms + RMS_EPS) * g) * (1.0 + scale) + shift


def _sigmoid(x):
    return 1.0 / (1.0 + jnp.exp(-x))


def _seg_ones():
    i = np.arange(LANES)
    return jnp.asarray((i[:, None] // HEAD_DIM) == (i[None, :] // HEAD_DIM), dtype=BF16)


def _ada_kernel(c_ref, w_ref, b_ref, o_ref):
    c = c_ref[...]
    s = c * _sigmoid(c)
    o_ref[0] = _dot3(s, w_ref[0]) + b_ref[0]


def _ada(cs, ada_w, ada_b):
    depth, d, n = ada_w.shape
    tn = 1536
    return pl.pallas_call(
        _ada_kernel,
        out_shape=jax.ShapeDtypeStruct((depth, 8, n), F32),
        grid=(depth, n // tn),
        in_specs=[
            pl.BlockSpec((8, d), lambda l, j: (0, 0)),
            pl.BlockSpec((1, d, tn), lambda l, j: (l, 0, j)),
            pl.BlockSpec((1, 1, tn), lambda l, j: (l, 0, j)),
        ],
        out_specs=pl.BlockSpec((1, 8, tn), lambda l, j: (l, 0, j)),
        compiler_params=_cparams(("arbitrary", "arbitrary")),
        name="ada",
    )(cs, ada_w, ada_b.reshape(depth, 1, n))


def _inproj_kernel(x_ref, mod_ref, g_ref, w_ref, wrot_ref, cos_ref, sin_ref, gq_ref, gk_ref, ones_ref,
                   qa_ref, ka_ref, va_ref, qb_ref, kb_ref, vb_ref):
    mod = mod_ref[0, 0]
    h = _norm_mod(x_ref[0], g_ref[...], mod[0:1], mod[1:2]).astype(BF16)
    y = _dot(h, w_ref[...])
    yr = _dot(h, wrot_ref[...])
    cos = cos_ref[...]
    sin = sin_ref[...]
    ones = ones_ref[...]
    qscale = HEAD_DIM ** -0.5 * LOG2E

    def put(ref, tile, val):
        ref[0, 2 * tile] = val[:, :HEAD_DIM].astype(ref.dtype)
        ref[0, 2 * tile + 1] = val[:, HEAD_DIM:].astype(ref.dtype)

    def chunk(a, c):
        return a[:, c * LANES:(c + 1) * LANES]

    for c in range(4):
        put(qa_ref, c, (chunk(y, c) * cos + chunk(yr, c) * sin) * qscale)
    put(ka_ref, 0, chunk(y, 4) * cos + chunk(yr, 4) * sin)
    put(va_ref, 0, chunk(y, 5))

    def normed(c, cr, gain_ref):
        v = chunk(y, c)
        rs = lax.rsqrt(_segsum(v * v, ones) * (1.0 / HEAD_DIM) + RMS_EPS)
        return (v * rs * gain_ref[0:1]) * cos + (chunk(yr, cr) * rs * gain_ref[1:2]) * sin

    for c in range(4):
        put(qb_ref, c, normed(6 + c, 5 + c, gq_ref) * qscale)
    put(kb_ref, 0, normed(10, 9, gk_ref))
    put(vb_ref, 0, chunk(y, 11))


def _inproj(x, mods, g, w_in, w_rot, cos, sin, gq2, gk2, ones):
    B, T, D = x.shape
    nt = T // TM
    heads = lambda n: jax.ShapeDtypeStruct((B, n, T, HEAD_DIM), BF16)
    hspec = lambda n: pl.BlockSpec((1, n, TM, HEAD_DIM), lambda b, i: (b, 0, i, 0))
    full = lambda a: pl.BlockSpec(a.shape, lambda b, i: (0,) * a.ndim)
    return pl.pallas_call(
        _inproj_kernel,
        out_shape=(heads(8), heads(2), heads(2), heads(8), heads(2), heads(2)),
        grid=(B, nt),
        in_specs=[
            pl.BlockSpec((1, TM, D), lambda b, i: (b, i, 0)),
            pl.BlockSpec((1, 1, 6, D), lambda b, i: (b, jnp.minimum(i, 1), 0, 0)),
            full(g), full(w_in), full(w_rot),
            pl.BlockSpec((TM, LANES), lambda b, i: (i, 0)),
            pl.BlockSpec((TM, LANES), lambda b, i: (i, 0)),
            full(gq2), full(gk2), full(ones),
        ],
        out_specs=(hspec(8), hspec(2), hspec(2), hspec(8), hspec(2), hspec(2)),
        compiler_params=_cparams(("parallel", "parallel")),
        name="attn_inproj",
    )(x, mods, g, w_in, w_rot, cos, sin, gq2, gk2, ones)


LOOKAHEAD = 2


def _flash_kernel(sink_ref, q_ref, k_ref, v_ref, o_ref, m_scr, acc_scr, s_scr, *, tk, nk):
    h = pl.program_id(1)
    m_scr[...] = jnp.full(m_scr.shape, NEG, F32)
    acc_scr[...] = jnp.zeros(acc_scr.shape, F32)

    def scores(j, g):
        return _dot_nt(k_ref[0, 0, pl.ds(pl.multiple_of(j * tk, tk), tk), :], q_ref[0, 0, g])

    for g in range(LOOKAHEAD):
        s_scr[g] = scores(0, g)

    def body(j, carry):
        vt = v_ref[0, 0, :, pl.ds(pl.multiple_of(j * tk, tk), tk)]
        jn = jnp.minimum(j + 1, nk - 1)
        ahead = {}
        for g in range(GROUP):
            st = s_scr[g] if g < LOOKAHEAD else ahead.pop(g)
            if g + LOOKAHEAD < GROUP:
                ahead[g + LOOKAHEAD] = scores(j, g + LOOKAHEAD)
            m_prev = m_scr[g]
            m_new = jnp.maximum(m_prev, jnp.max(st, axis=0, keepdims=True))
            p = jnp.exp2(st - m_new).astype(BF16)
            if g + LOOKAHEAD >= GROUP:
                s_scr[g + LOOKAHEAD - GROUP] = scores(jn, g + LOOKAHEAD - GROUP)
            acc_scr[g] = jnp.exp2(m_prev - m_new) * acc_scr[g] + _dot(vt, p)
            m_scr[g] = m_new
        return carry

    lax.fori_loop(0, nk, body, 0)
    outs = []
    for g in range(GROUP):
        acc = acc_scr[g]
        l = acc[HEAD_DIM:HEAD_DIM + 1] + jnp.exp2(sink_ref[h * GROUP + g] - m_scr[g])
        outs.append(acc[:HEAD_DIM] / l)
    o_ref[0] = jnp.concatenate(outs, axis=0).T.astype(o_ref.dtype)


def _flash(sink, q, k, v, *, q_rows, q_off, k_rows, tq, tk):
    B, Hkv = k.shape[:2]
    nq, nk = q_rows // tq, k_rows // tk
    qo = q_off // tq
    return pl.pallas_call(
        functools.partial(_flash_kernel, tk=tk, nk=nk),
        out_shape=jax.ShapeDtypeStruct((B, q_rows, Hkv * GROUP * HEAD_DIM), BF16),
        grid=(B, Hkv, nq),
        in_specs=[
            pl.BlockSpec(memory_space=pltpu.SMEM),
            pl.BlockSpec((1, 1, GROUP, tq, HEAD_DIM), lambda b, h, i: (b, h, 0, i + qo, 0)),
            pl.BlockSpec((1, 1, k_rows, HEAD_DIM), lambda b, h, i: (b, h, 0, 0)),
            pl.BlockSpec((1, 1, LANES, k_rows), lambda b, h, i: (b, h, 0, 0)),
        ],
        out_specs=pl.BlockSpec((1, tq, GROUP * HEAD_DIM), lambda b, h, i: (b, i, h)),
        scratch_shapes=[
            pltpu.VMEM((GROUP, 1, tq), F32),
            pltpu.VMEM((GROUP, LANES, tq), F32),
            pltpu.VMEM((LOOKAHEAD, tk, tq), F32),
        ],
        compiler_params=_cparams(("parallel", "parallel", "arbitrary")),
        name="flash_attn",
    )(sink, q, k, v)


def _window_kernel(sink_ref, q_ref, kc_ref, vc_ref, k0_ref, k1_ref, k2_ref, v0_ref, v1_ref, v2_ref, o_ref, *, nb):
    h = pl.program_id(1)
    i = pl.program_id(2)
    rows = GROUP * WINDOW
    q = q_ref[0, 0].reshape(rows, HEAD_DIM)
    r = lax.broadcasted_iota(jnp.int32, (rows, WINDOW), 0) & (WINDOW - 1)
    c = lax.broadcasted_iota(jnp.int32, (rows, WINDOW), 1)
    sc = _dot_nt(q, kc_ref[0, 0])
    s0 = jnp.where((c >= r) & (i > 0), _dot_nt(q, k0_ref[0, 0]), NEG)
    s1 = _dot_nt(q, k1_ref[0, 0])
    s2 = jnp.where((c <= r) & (i < nb - 1), _dot_nt(q, k2_ref[0, 0]), NEG)
    sink = jnp.concatenate(
        [jnp.full((WINDOW, 1), sink_ref[h * GROUP + g], F32) for g in range(GROUP)], axis=0)
    rowmax = lambda s: jnp.max(s, axis=-1, keepdims=True)
    m = jnp.maximum(jnp.maximum(rowmax(sc), rowmax(s0)), jnp.maximum(rowmax(s1), rowmax(s2)))
    m = jnp.maximum(m, sink)
    pc, p0, p1, p2 = (jnp.exp2(s - m) for s in (sc, s0, s1, s2))
    rowsum = lambda p: jnp.sum(p, axis=-1, keepdims=True)
    l = rowsum(pc) + rowsum(p0) + rowsum(p1) + rowsum(p2) + jnp.exp2(sink - m)
    acc = (_dot(pc.astype(BF16), vc_ref[0, 0]) + _dot(p0.astype(BF16), v0_ref[0, 0])
           + _dot(p1.astype(BF16), v1_ref[0, 0]) + _dot(p2.astype(BF16), v2_ref[0, 0]))
    out = acc / l
    for g in range(GROUP):
        o_ref[0, :, g * HEAD_DIM:(g + 1) * HEAD_DIM] = out[g * WINDOW:(g + 1) * WINDOW].astype(o_ref.dtype)


def _window_attn(sink, q, k, v, L, S):
    B, Hkv = k.shape[:2]
    nb = S // WINDOW
    pad = ((0, 0), (0, 0), (WINDOW, WINDOW), (0, 0))
    kp = jnp.pad(k[:, :, L:], pad)
    vp = jnp.pad(v[:, :, L:], pad)
    qo = L // WINDOW
    band = lambda j: pl.BlockSpec((1, 1, WINDOW, HEAD_DIM), lambda b, h, i: (b, h, i + j, 0))
    ctx = pl.BlockSpec((1, 1, L, HEAD_DIM), lambda b, h, i: (b, h, 0, 0))
    return pl.pallas_call(
        functools.partial(_window_kernel, nb=nb),
        out_shape=jax.ShapeDtypeStruct((B, S, Hkv * GROUP * HEAD_DIM), BF16),
        grid=(B, Hkv, nb),
        in_specs=[
            pl.BlockSpec(memory_space=pltpu.SMEM),
            pl.BlockSpec((1, 1, GROUP, WINDOW, HEAD_DIM), lambda b, h, i: (b, h, 0, i + qo, 0)),
            ctx, ctx, band(0), band(1), band(2), band(0), band(1), band(2),
        ],
        out_specs=pl.BlockSpec((1, WINDOW, GROUP * HEAD_DIM), lambda b, h, i: (b, i, h)),
        compiler_params=_cparams(("parallel", "parallel", "parallel")),
        name="window_attn",
    )(sink, q, k, v, kp, kp, kp, vp, vp, vp)


def _ffn_prep(x, y, mod, gffn, rwh_ref, rwl_ref, xo_ref, h_ref, lg_ref):
    xn = x + mod[2:3] * y
    h = _norm_mod(xn, gffn, mod[3:4], mod[4:5])
    xo_ref[0] = xn
    hh, hl = _split(h)
    h_ref[0] = hh
    rwh = rwh_ref[...]
    lg_ref[0] = _dot(hh, rwh) + _dot(hl, rwh) + _dot(hh, rwl_ref[...])


def _attn_out_kernel(oa_ref, ob_ref, wa_ref, wb_ref, x_ref, mod_ref, g_ref, rwh_ref, rwl_ref,
                     xo_ref, h_ref, lg_ref):
    y = _dot(oa_ref[0], wa_ref[...]) + _dot(ob_ref[0], wb_ref[...])
    _ffn_prep(x_ref[0], y, mod_ref[0, 0], g_ref[...], rwh_ref, rwl_ref, xo_ref, h_ref, lg_ref)


def _row_specs(D):
    row = lambda w: pl.BlockSpec((1, TM, w), lambda b, i: (b, i, 0))
    mod = pl.BlockSpec((1, 1, 6, D), lambda b, i: (b, jnp.minimum(i, 1), 0, 0))
    full = lambda a: pl.BlockSpec(a.shape, lambda b, i: (0,) * a.ndim)
    return row, mod, full


def _ffn_prep_outs(B, T, D):
    row, _, _ = _row_specs(D)
    shapes = (jax.ShapeDtypeStruct((B, T, D), F32), jax.ShapeDtypeStruct((B, T, D), BF16),
              jax.ShapeDtypeStruct((B, T, LANES), F32))
    return shapes, (row(D), row(D), row(LANES))


def _attn_out(oa, ob, wa, wb, x, mods, g, rwh, rwl):
    B, T, D = x.shape
    row, mod, full = _row_specs(D)
    shapes, specs = _ffn_prep_outs(B, T, D)
    return pl.pallas_call(
        _attn_out_kernel,
        out_shape=shapes,
        grid=(B, T // TM),
        in_specs=[row(oa.shape[-1]), row(ob.shape[-1]), full(wa), full(wb), row(D), mod, full(g),
                  full(rwh), full(rwl)],
        out_specs=specs,
        compiler_params=_cparams(("parallel", "parallel")),
        name="attn_out",
    )(oa, ob, wa, wb, x, mods, g, rwh, rwl)


def _gmm_kernel(be_ref, nu_ref, x_ref, w1_ref, w3_ref, w2_ref, o_ref):
    i = pl.program_id(0)

    @pl.when(i < nu_ref[0])
    def _():
        x = x_ref[...]
        a = _dot(x, w1_ref[0].astype(BF16))
        b = _dot(x, w3_ref[0].astype(BF16))
        mid = (a * _sigmoid(a)) * b
        o_ref[...] = _dot(mid.astype(BF16), w2_ref[0].astype(BF16))

    @pl.when(i >= nu_ref[0])
    def _():
        o_ref[...] = jnp.zeros(o_ref.shape, o_ref.dtype)


def _gmm(block_expert, n_used, xs, w1, w3, w2):
    n_slots, D = xs.shape
    F = w1.shape[-1]
    nblk = n_slots // MOE_ROWS
    return pl.pallas_call(
        _gmm_kernel,
        out_shape=jax.ShapeDtypeStruct((n_slots, D), F32),
        grid_spec=pltpu.PrefetchScalarGridSpec(
            num_scalar_prefetch=2,
            grid=(nblk,),
            in_specs=[
                pl.BlockSpec((MOE_ROWS, D), lambda i, be, nu: (i, 0)),
                pl.BlockSpec((1, D, F), lambda i, be, nu: (be[i], 0, 0)),
                pl.BlockSpec((1, D, F), lambda i, be, nu: (be[i], 0, 0)),
                pl.BlockSpec((1, F, D), lambda i, be, nu: (be[i], 0, 0)),
            ],
            out_specs=pl.BlockSpec((MOE_ROWS, D), lambda i, be, nu: (i, 0)),
        ),
        compiler_params=_cparams(("arbitrary",)),
        name="moe_gmm",
    )(block_expert, n_used, xs, w1, w3, w2)


ROUTE_ROWS = 512


def _route_kernel(lg_ref, bias_ref, idx_ref, w_ref):
    x = lg_ref[...].T[:N_EXPERTS]
    m = jnp.max(x, axis=0, keepdims=True)
    e = jnp.exp(x - m)
    probs = e / jnp.sum(e, axis=0, keepdims=True)
    sel = probs + bias_ref[...][:, 0:1]
    row = lambda a, i: a[i:i + 1, :]
    G = EXPERTS_PER_GROUP
    scores = []
    for g in range(N_GROUPS):
        s = [row(sel, g * G + i) for i in range(G)]
        best = None
        for i in range(G):
            for j in range(i + 1, G):
                best = s[i] + s[j] if best is None else jnp.maximum(best, s[i] + s[j])
        scores.append(best)
    top = functools.reduce(jnp.maximum, scores)
    gi = jnp.full(top.shape, N_GROUPS - 1, jnp.int32)
    for g in range(N_GROUPS - 2, -1, -1):
        gi = jnp.where(scores[g] == top, g, gi)

    def pick(a, i):
        out = row(a, (N_GROUPS - 1) * G + i)
        for g in range(N_GROUPS - 2, -1, -1):
            out = jnp.where(gi == g, row(a, g * G + i), out)
        return out

    c = [pick(sel, i) for i in range(G)]
    pc = [pick(probs, i) for i in range(G)]

    def first_argmax(vals):
        mx = functools.reduce(jnp.maximum, vals)
        idx = jnp.full(mx.shape, G - 1, jnp.int32)
        for i in range(G - 2, -1, -1):
            idx = jnp.where(vals[i] == mx, i, idx)
        return idx

    i1 = first_argmax(c)
    i2 = first_argmax([jnp.where(i1 == i, -jnp.inf, c[i]) for i in range(G)])
    take = lambda vals, idx: functools.reduce(
        lambda acc, i: jnp.where(idx == i, vals[i], acc), range(G - 2, -1, -1), vals[G - 1])
    w1, w2 = take(pc, i1), take(pc, i2)
    tot = w1 + w2
    zi = jnp.zeros((6,) + top.shape[1:], jnp.int32)
    idx_ref[...] = jnp.concatenate([gi * G + i1, gi * G + i2, zi], axis=0)
    w_ref[...] = jnp.concatenate([w1 / tot, w2 / tot, zi.astype(F32)], axis=0)


def _route(logits, router_bias):
    N = logits.shape[0]
    bias = jnp.broadcast_to(router_bias.astype(F32)[:, None], (N_EXPERTS, LANES))
    idx, w = pl.pallas_call(
        _route_kernel,
        out_shape=(jax.ShapeDtypeStruct((8, N), jnp.int32), jax.ShapeDtypeStruct((8, N), F32)),
        grid=(N // ROUTE_ROWS,),
        in_specs=[pl.BlockSpec((ROUTE_ROWS, LANES), lambda i: (i, 0)),
                  pl.BlockSpec((N_EXPERTS, LANES), lambda i: (0, 0))],
        out_specs=(pl.BlockSpec((8, ROUTE_ROWS), lambda i: (0, i)),
                   pl.BlockSpec((8, ROUTE_ROWS), lambda i: (0, i))),
        compiler_params=_cparams(("parallel",)),
        name="route",
    )(logits, bias)
    return idx[:TOP_K].T, w[:TOP_K].T


def _moe(h, logits, router_bias, w1, w3, w2):
    N, D = h.shape
    expert_idx, gate_w = _route(logits, router_bias)
    NK = N * TOP_K
    flat_e = expert_idx.reshape(NK)
    onehot = (flat_e[:, None] == jnp.arange(N_EXPERTS, dtype=jnp.int32)[None, :]).astype(jnp.int32)
    csum = jnp.cumsum(onehot, axis=0)
    counts = csum[-1]
    rank = jnp.take_along_axis(csum, flat_e[:, None], axis=1)[:, 0] - 1
    padded = (counts + MOE_ROWS - 1) // MOE_ROWS * MOE_ROWS
    pad_end = jnp.cumsum(padded)
    pad_start = pad_end - padded
    dest =pad_start[flat_e] + rank
    nblk = -(-NK // MOE_ROWS) + N_EXPERTS
    n_slots = nblk * MOE_ROWS
    n_used = (pad_end[-1] // MOE_ROWS).astype(jnp.int32)
    blk = jnp.arange(nblk, dtype=jnp.int32)
    be = jnp.sum((pad_end[None, :] <= (blk * MOE_ROWS)[:, None]).astype(jnp.int32), axis=1)
    be = jnp.minimum(be, N_EXPERTS - 1)
    be = jnp.where(blk < n_used, be, be[jnp.maximum(n_used - 1, 0)])
    flat_tok = jnp.arange(NK, dtype=jnp.int32) // TOP_K
    slot_tok = jnp.zeros((n_slots,), jnp.int32).at[dest].set(flat_tok, unique_indices=True)
    xs = h[slot_tok]
    ys = _gmm(be, n_used.reshape(1), xs, w1, w3, w2)
    pos = dest.reshape(N, TOP_K)
    return ys[pos[:, 0]] * gate_w[:, 0:1] + ys[pos[:, 1]] * gate_w[:, 1:2]


def _residual_kernel(x_ref, f_ref, mod_ref, o_ref):
    o_ref[0] = x_ref[0] + mod_ref[0, 0][5:6] * f_ref[0]


def _final_kernel(x_ref, f_ref, mod_ref, g_ref, o_ref):
    x = x_ref[0] + mod_ref[0, 0][5:6] * f_ref[0]
    ms = jnp.mean(x * x, axis=-1, keepdims=True)
    o_ref[0] = x * lax.rsqrt(ms + RMS_EPS) * g_ref[...]


def _residual(x, f, mods):
    B, T, D = x.shape
    row, mod, _ = _row_specs(D)
    return pl.pallas_call(
        _residual_kernel,
        out_shape=jax.ShapeDtypeStruct((B, T, D), F32),
        grid=(B, T // TM),
        in_specs=[row(D), row(D), mod],
        out_specs=row(D),
        compiler_params=_cparams(("parallel", "parallel")),
        name="residual",
    )(x, f, mods)


def _final(x, f, mods, g, L):
    B, S, D = f.shape
    off = L // TM
    mod = pl.BlockSpec((1, 1, 6, D), lambda b, i: (b, 1, 0, 0))
    return pl.pallas_call(
        _final_kernel,
        out_shape=jax.ShapeDtypeStruct((B, S, D), F32),
        grid=(B, S // TM),
        in_specs=[pl.BlockSpec((1, TM, D), lambda b, i: (b, i + off, 0)),
                  pl.BlockSpec((1, TM, D), lambda b, i: (b, i, 0)), mod,
                  pl.BlockSpec(g.shape, lambda b, i: (0, 0))],
        out_specs=pl.BlockSpec((1, TM, D), lambda b, i: (b, i, 0)),
        compiler_params=_cparams(("parallel", "parallel")),
        name="final_norm",
    )(x, f, mods, g)


def _rwkv_proj_kernel(x_ref, xp_ref, xn_ref, mod_ref, g_ref, xmix_ref, wr_ref, wk_ref, wv_ref,
                      dw1_ref, dw2_ref, da1_ref, da2_ref, g1_ref, g2_ref, vec_ref, ones_ref,
                      r_ref, v_ref, kk_ref, bv_ref, gate_ref, w0_ref, w1_ref, kd0_ref, kd1_ref, bd0_ref, bd1_ref,
                      *, nt):
    i = pl.program_id(1)
    mod = mod_ref[0, 0]
    g = g_ref[...]
    nm = lambda x: _norm_mod(x, g, mod[0:1], mod[1:2])
    h = nm(x_ref[0])
    hp = nm(xp_ref[0])[7:8] * jnp.where(i >= 2, 1.0, 0.0)
    hn = nm(xn_ref[0])[0:1] * jnp.where((i >= 1) & (i < nt - 1), 1.0, 0.0)
    ridx = lax.broadcasted_iota(jnp.int32, h.shape, 0)
    h_dn = jnp.where(ridx == 0, hp, pltpu.roll(h, 1, axis=0))
    h_up = jnp.where(ridx == TM - 1, hn, pltpu.roll(h, TM - 1, axis=0))
    xx = 0.5 * (h_dn + h_up) - h
    xmix = xmix_ref[...]
    mix = lambda j: (h + xx * xmix[j:j + 1]).astype(BF16)
    vec = vec_ref[...]
    ones = ones_ref[...]

    r = _dot(mix(0), wr_ref[...])
    k = _dot(mix(2), wk_ref[...])
    v = _dot(mix(3), wv_ref[...])
    gate_ref[0] = _dot(_sigmoid(_dot(mix(5), g1_ref[...])).astype(BF16), g2_ref[...])
    kk = k * vec[0:1]
    kk = kk * lax.rsqrt(jnp.maximum(_segsum_wide(kk * kk, ones), 1e-24))
    lw = jnp.tanh(_dot(mix(1), dw1_ref[...])).astype(BF16)
    la = _dot(mix(4), da1_ref[...]).astype(BF16)
    r_ref[0] = r
    v_ref[0] = v
    kk_ref[0] = kk
    bonus = jnp.zeros_like(r)
    lora = DECAY_LORA
    for d, (w_ref, kd_ref, bd_ref) in enumerate(((w0_ref, kd0_ref, bd0_ref), (w1_ref, kd1_ref, bd1_ref))):
        z = -(vec[3 + d:4 + d] + _dot(lw[:, d * lora:(d + 1) * lora], dw2_ref[d]))
        softplus = jnp.maximum(z, 0.0) + jnp.log(1.0 + jnp.exp(-jnp.abs(z)))
        w_ref[0] = jnp.exp(-jnp.exp(-softplus - 0.5))
        iclr = _sigmoid(vec[5 + d:6 + d] + _dot(la[:, d * lora:(d + 1) * lora], da2_ref[d]))
        kd = k * (1.0 + (iclr - 1.0) * vec[1:2])
        kd_ref[0] = kd
        bd_ref[0] = kk * iclr
        bonus = bonus + _segsum_wide(r * kd * vec[2:3], ones)
    bv_ref[0] = bonus * v


DECAY_LORA = 64


def _rwkv_proj(x, mods, g, xmix, wr, wk, wv, dw1, dw2, da1, da2, g1, g2, vec, ones):
    B, T, D = x.shape
    nt = T // TM
    row, mod, full = _row_specs(D)
    r8 = TM // 8
    prev = pl.BlockSpec((1, 8, D), lambda b, i: (b, jnp.maximum(i * r8 - 1, 0), 0))
    nxt = pl.BlockSpec((1, 8, D), lambda b, i: (b, jnp.minimum((i + 1) * r8, T // 8 - 1), 0))
    out = jax.ShapeDtypeStruct((B, T, D), F32)
    return pl.pallas_call(
        functools.partial(_rwkv_proj_kernel, nt=nt),
        out_shape=(out,) * 11,
        grid=(B, nt),
        in_specs=[row(D), prev, nxt, mod, full(g), full(xmix), full(wr), full(wk), full(wv),
                  full(dw1), full(dw2), full(da1), full(da2), full(g1), full(g2), full(vec), full(ones)],
        out_specs=(row(D),) * 11,
        compiler_params=_cparams(("parallel", "parallel")),
        name="rwkv_proj",
    )(x, x, x, mods, g, xmix, wr, wk, wv, dw1, dw2, da1, da2, g1, g2, vec, ones)


CHUNK = 4
N_HEADS = D_MODEL // HEAD_DIM
MAP_LANES = 3 * CHUNK * N_HEADS


def _coef_kernel(r_ref, kk_ref, w_ref, kd_ref, bd_ref, sel_ref, at_ref, rt_ref, bh_ref, kh_ref, gc_ref,
                 cu_ref, cy_ref, *, reverse):
    r, a, w, kd, bd = r_ref[0], -kk_ref[0], w_ref[0], kd_ref[0], bd_ref[0]
    rows = r.shape[0]
    p = lax.broadcasted_iota(jnp.int32, r.shape, 0) & (CHUNK - 1)
    s = (CHUNK - 1 - p) if reverse else p
    back = lambda x, k: pltpu.roll(x, (rows - k) if reverse else k, axis=0)
    ahead = lambda x, k: pltpu.roll(x, k if reverse else (rows - k), axis=0)
    wb = [None] + [back(w, k) for k in range(1, CHUNK)]
    excl = jnp.ones_like(w)
    rest = jnp.ones_like(w)
    for k in range(1, CHUNK):
        excl = excl * jnp.where(s >= k, wb[k], 1.0)
        rest = rest * jnp.where(s + k <= CHUNK - 1, ahead(w, k), 1.0)
    at_ref[0] = a * excl
    rt_ref[0] = r * (excl * w)
    bh_ref[0] = bd * rest
    kh_ref[0] = kd * rest
    gc_ref[0] = excl * w * rest
    between = [None, None, wb[1], wb[1] * wb[2]]
    rw = r * w
    s1 = s[:, :LANES]
    seg = lambda x: _dot(x.astype(BF16), sel_ref[...])

    def by_dist(lead, y, first):
        out = [None] * CHUNK
        for dist in range(first, CHUNK):
            if dist == 0:
                out[0] = seg(r * y)
                continue
            e = back(y, dist) if between[dist] is None else between[dist] * back(y, dist)
            out[dist] = jnp.where(s1 >= dist, seg(lead * e), 0.0)
        return out

    lab, lak = by_dist(a, bd, 1), by_dist(a, kd, 1)
    rb, rk = by_dist(rw, bd, 0), by_dist(rw, kd, 0)
    one = jnp.ones_like(rb[0])
    bk = lambda x, k: x if k == 0 else pltpu.roll(x, (rows - k) if reverse else k, axis=0)
    md = [one]
    for dist in range(1, CHUNK):
        md.append(sum(lab[e] * (bk(md[dist - e], e) if dist - e else 1.0) for e in range(1, dist + 1)))
    gd = [None] + [sum((md[e] if e else 1.0) * bk(lak[dist - e], e) for e in range(dist)) for dist in range(1, CHUNK)]
    yzd = [sum(rb[e] * (bk(md[dist - e], e) if dist - e else 1.0) for e in range(dist + 1)) for dist in range(CHUNK)]
    yvd = [rk[dist] + sum(rb[e] * bk(gd[dist - e], e) for e in range(dist)) for dist in range(CHUNK)]

    def at_pos(table, j, first):
        out = jnp.zeros_like(one)
        for dist in range(first, CHUNK - j):
            out = jnp.where(s1 == j + dist, table[dist], out)
        return out

    zero = jnp.zeros_like(one)
    u_blocks = ([at_pos(md, j, 0) for j in range(CHUNK)] + [at_pos(gd, j, 1) for j in range(CHUNK)]
                + [zero] * CHUNK)
    y_blocks = ([at_pos(yzd, j, 0) for j in range(CHUNK)] + [at_pos(yvd, j, 0) for j in range(CHUNK)]
                + [jnp.where(s1 == j, 1.0, 0.0) for j in range(CHUNK)])
    lane_blk = lax.broadcasted_iota(jnp.int32, one.shape, 1) // N_HEADS

    def place(blocks):
        per_tile = LANES // N_HEADS
        tiles = []
        for t in range(2):
            acc = zero
            for i in range(t * per_tile, min((t + 1) * per_tile, len(blocks))):
                acc = jnp.where(lane_blk == i - t * per_tile, blocks[i], acc)
            tiles.append(acc)
        return jnp.concatenate(tiles, axis=1)

    cu_ref[0] = place(u_blocks)
    cy_ref[0] = place(y_blocks)


def _rwkv_coef(r, kk, w, kd, bd, sel, reverse):
    B, T, D = r.shape
    row, _, full = _row_specs(D)
    out = jax.ShapeDtypeStruct((B, T, D), F32)
    maps = jax.ShapeDtypeStruct((B, T, 2 * LANES), F32)
    return pl.pallas_call(
        functools.partial(_coef_kernel, reverse=reverse),
        out_shape=(out,) * 5 + (maps, maps),
        grid=(B, T // TM),
        in_specs=[row(D)] * 5 + [full(sel)],
        out_specs=(row(D),) * 5 + (row(2 * LANES), row(2 * LANES)),
        compiler_params=_cparams(("parallel", "parallel")),
        name="rwkv_coef",
    )(r, kk, w, kd, bd, sel)


def _scan_kernel(atf, atb, rtf, rtb, vf, vb, bhf, bhb, khf, khb, gcf, gcb, cuf, cub, cyf, cyb,
                 mask_ref, eye_ref, e16_ref, yf, yb, st, *, tc, nb):
    n = pl.program_id(0)

    @pl.when(n == 0)
    def _():
        st[...] = jnp.zeros(st.shape, F32)

    N = HEAD_DIM
    W = CHUNK * N_HEADS
    nch = tc // CHUNK
    dirs = ((atf, rtf, vf, bhf, khf, gcf, cuf, cyf, yf), (atb, rtb, vb, bhb, khb, gcb, cub, cyb, yb))
    lane1 = lax.broadcasted_iota(jnp.int32, (N, 3 * W), 1)
    lane2 = lax.broadcasted_iota(jnp.int32, (N, 2 * W), 1)
    spread = lambda x: (mask_ref[...] * x).astype(BF16)

    def chunk(ci, carry):
        work = []
        for d, refs in enumerate(dirs):
            cc = ci if d == 0 else nch - 1 - ci
            rows = [pl.ds(cc * CHUNK + (s if d == 0 else CHUNK - 1 - s), 1) for s in range(CHUNK)]
            for b in range(nb):
                work.append((d * nb + b, b, cc, rows, refs))
        firsts = []
        for gi, b, cc, rows, (AT, RT, V, BH, KH, GC, CU, CY, Y) in work:
            lhs = jnp.concatenate([st[gi].astype(BF16), eye_ref[...]], axis=0)
            w1 = jnp.concatenate([spread(X[b, rw, :]) for X in (AT, V, RT) for rw in rows], axis=0)
            firsts.append(_dot_nt(lhs, w1))
        mids = []
        for (gi, b, cc, rows, (AT, RT, V, BH, KH, GC, CU, CY, Y)), out in zip(work, firsts):
            zvq = jnp.where((lane1 >= W) & (lane1 < 2 * W), out[N:], out[:N])
            zvq16 = zvq.astype(BF16)
            maprows = lambda M: [e16_ref[...] * M[b, rw, :MAP_LANES] for rw in rows]
            wu = jnp.concatenate(maprows(CU) + [jnp.zeros((W, MAP_LANES), F32)], axis=0)
            u = _dot_nt(zvq16, wu.astype(BF16))
            wy = jnp.concatenate(maprows(CY), axis=0)
            yt = _dot_nt(wy.astype(BF16), zvq16)
            for s, rw in enumerate(rows):
                Y[b, rw] = yt[s * N_HEADS:(s + 1) * N_HEADS][None]
            mids.append((zvq, u))
        for (gi, b, cc, rows, (AT, RT, V, BH, KH, GC, CU, CY, Y)), (zvq, u) in zip(work, mids):
            uv = jnp.where(lane2 < W, u, zvq[:, :2 * W]).astype(BF16)
            w2 = jnp.concatenate([spread(X[b, rw, :]) for X in (BH, KH) for rw in rows], axis=0)
            st[gi] = st[gi] * GC[b, rows[0], :] + _dot(uv, w2)
        return carry

    lax.fori_loop(0, nch, chunk, 0)


def _rwkv_scan(ins_f, ins_b, v, mask, eye, e16, L):
    B, T, D = v.shape
    tc = SCAN_CHUNK
    nch = tc // CHUNK
    nc, nchunks = L // tc, T // tc
    fwd_idx = lambda n: n
    rev_idx = lambda n: jnp.where(n < nc, nc - 1 - n, nchunks - 1 - (n - nc))
    tok = lambda idx, w: pl.BlockSpec((B, tc, w), lambda n: (0, idx(n), 0))
    ys = lambda idx: pl.BlockSpec((B, tc, N_HEADS, HEAD_DIM), lambda n: (0, idx(n), 0, 0))
    full = lambda a: pl.BlockSpec(a.shape, lambda n: (0,) * a.ndim)
    out = jax.ShapeDtypeStruct((B, T, N_HEADS, HEAD_DIM), F32)
    atf, rtf, bhf, khf, gcf, cuf, cyf = ins_f
    atb, rtb, bhb, khb, gcb, cub, cyb = ins_b
    f, r_ = tok(fwd_idx, D), tok(rev_idx, D)
    wy = cyf.shape[-1]
    return pl.pallas_call(
        functools.partial(_scan_kernel, tc=tc, nb=B),
        out_shape=(out, out),
        grid=(nchunks,),
        in_specs=[f, r_, f, r_, f, r_, f, r_, f, r_, f, r_, tok(fwd_idx, wy), tok(rev_idx, wy),
                  tok(fwd_idx, wy), tok(rev_idx, wy), full(mask), full(eye), full(e16)],
        out_specs=(ys(fwd_idx), ys(rev_idx)),
        scratch_shapes=[pltpu.VMEM((2 * B, HEAD_DIM, D), F32)],
        compiler_params=_cparams(("arbitrary",)),
        name="rwkv_scan",
    )(atf, atb, rtf, rtb, v, v, bhf, bhb, khf, khb, gcf, gcb, cuf, cub, cyf, cyb, mask, eye, e16)


def _rwkv_out_kernel(yf_ref, yb_ref, bv_ref, gate_ref, ln_ref, wo_ref, ones_ref, x_ref, mod_ref, g_ref,
                     rwh_ref, rwl_ref, xo_ref, h_ref, lg_ref):
    ones = ones_ref[...]
    y = yf_ref[0] + yb_ref[0]
    inv = 1.0 / HEAD_DIM
    dlt = y - _segsum_wide(y, ones) * inv
    yn = dlt * lax.rsqrt(_segsum_wide(dlt * dlt, ones) * inv + GN_EPS)
    ln = ln_ref[...]
    o = (yn * ln[0:1] + ln[1:2] + bv_ref[0]) * gate_ref[0]
    yl = _dot(o.astype(BF16), wo_ref[...])
    _ffn_prep(x_ref[0], yl, mod_ref[0, 0], g_ref[...], rwh_ref, rwl_ref, xo_ref, h_ref, lg_ref)


def _rwkv_out(yf, yb, bv, gate, ln, wo, ones, x, mods, g, rwh, rwl):
    B, T, D = x.shape
    row, mod, full = _row_specs(D)
    shapes, specs = _ffn_prep_outs(B, T, D)
    return pl.pallas_call(
        _rwkv_out_kernel,
        out_shape=shapes,
        grid=(B, T // TM),
        in_specs=[row(D), row(D), row(D), row(D), full(ln), full(wo), full(ones), row(D), mod, full(g),
                  full(rwh), full(rwl)],
        out_specs=specs,
        compiler_params=_cparams(("parallel", "parallel")),
        name="rwkv_out",
    )(yf, yb, bv, gate, ln, wo, ones, x, mods, g, rwh, rwl)


def _rope_tables(S, L):
    rows = S // GRID_W
    row = jnp.repeat(jnp.arange(rows, dtype=F32), GRID_W)
    col = (jnp.arange(rows * GRID_W) % GRID_W).astype(F32)
    n_freq = HEAD_DIM // 4
    inv = ROPE_THETA ** (-jnp.arange(n_freq, dtype=F32) / n_freq)
    lane = np.arange(LANES) % HEAD_DIM
    axis, half, freq = lane // 32, (lane % 32) // 16, lane % 16
    pos = jnp.where(jnp.asarray(axis == 0)[None, :], row[:, None], col[:, None])
    ang = pos * inv[freq][None, :]
    sgn = jnp.asarray(np.where(half == 0, -1.0, 1.0), dtype=F32)
    cos = jnp.concatenate([jnp.ones((L, LANES), F32), jnp.cos(ang)], axis=0)
    sin = jnp.concatenate([jnp.zeros((L, LANES), F32), jnp.sin(ang) * sgn[None, :]], axis=0)
    return cos, sin


def kernel(x, c, ctx, c_ctx, ada_w, ada_b, norm_mix_g, norm_ffn_g, attn_w_in, attn_w_out, attn_sink,
           attn_q_norm_g, attn_k_norm_g, rwkv_x_mix, rwkv_w_r, rwkv_w_k, rwkv_w_v, rwkv_w_o,
           rwkv_decay_w0, rwkv_decay_w1, rwkv_decay_w2, rwkv_iclr_a0, rwkv_iclr_a1, rwkv_iclr_a2,
           rwkv_gate_g1, rwkv_gate_g2, rwkv_k_k, rwkv_k_a, rwkv_r_k, rwkv_ln_g, rwkv_ln_b,
           router_w, router_bias, moe_w1, moe_w3, moe_w2, final_norm_g):
    B, S, D = x.shape
    L = ctx.shape[1]
    T = L + S
    depth = ada_w.shape[0]
    assert D == D_MODEL and L == TM and S % TM == 0 and B == 2 and depth == 2
    ones = _seg_ones()
    bf = lambda a: a.astype(BF16)

    cs = jnp.zeros((8, D), F32).at[:B].set(c).at[B].set(c_ctx)
    ada = _ada(cs, ada_w, ada_b).reshape(depth, 8, 6, D)
    mods = [jnp.stack([jnp.broadcast_to(ada[i, B], (B, 6, D)), ada[i, :B]], axis=1) for i in range(depth)]

    xa = jnp.concatenate([ctx, x], axis=1)
    rw = jnp.zeros((D, LANES), F32).at[:, :N_EXPERTS].set(router_w)
    rwh, rwl = _split(rw)

    w_in = attn_w_in[0]
    roped = np.concatenate([np.arange(0, 640), np.arange(768, 1408)])
    w_rot = w_in[:, roped ^ 16]
    cos, sin = _rope_tables(S, L)
    lane = np.arange(LANES) % HEAD_DIM
    gains = lambda g: jnp.stack([g[lane], g[lane ^ 16]], axis=0)
    qa, ka, va, qb, kb, vb = _inproj(xa, mods[0], norm_mix_g[0].reshape(1, D), bf(w_in), bf(w_rot), cos, sin,
                                     gains(attn_q_norm_g[0]), gains(attn_k_norm_g[0]), ones)
    grouped = lambda q: q.reshape(B, A_KV_HEADS, GROUP, T, HEAD_DIM)
    qa, qb = grouped(qa), grouped(qb)
    sink = attn_sink[0].astype(F32) * LOG2E
    ext = lambda v: jnp.swapaxes(
        jnp.concatenate([v, jnp.ones_like(v[..., :1]), jnp.zeros_like(v[..., :HEAD_DIM - 1])], axis=-1), 2, 3)
    va_x, vb_x = ext(va), ext(vb)
    nosink = jnp.full((B_Q_HEADS,), NEG, F32)
    oa_l = _window_attn(sink, qa, ka, va, L, S)
    oa_c = _flash(sink, qa, ka, va_x, q_rows=L, q_off=0, k_rows=L, tq=L, tk=L)
    ob_l = _flash(nosink, qb, kb, vb_x, q_rows=S, q_off=L, k_rows=T, tq=256, tk=_key_tile(T))
    ob_c = _flash(nosink, qb, kb, vb_x, q_rows=L, q_off=0, k_rows=L, tq=L, tk=L)
    oa = jnp.concatenate([oa_c, oa_l], axis=1)
    ob = jnp.concatenate([ob_c, ob_l], axis=1)
    w_out = bf(attn_w_out[0])
    na = A_Q_HEADS * HEAD_DIM
    xa, h, lg = _attn_out(oa, ob, w_out[:na], w_out[na:], xa, mods[0], norm_ffn_g[0].reshape(1, D), rwh, rwl)
    f = _moe(h.reshape(B * T, D), lg.reshape(B * T, LANES), router_bias,
             moe_w1[0], moe_w3[0], moe_w2[0])
    xa = _residual(xa, f.reshape(B, T, D), mods[0])

    cat2 = lambda a: jnp.concatenate([a[0], a[1]], axis=1)
    vec = jnp.stack([rwkv_k_k[0], rwkv_k_a[0], rwkv_r_k[0].reshape(D), rwkv_decay_w0[0, 0], rwkv_decay_w0[0, 1],
                     rwkv_iclr_a0[0, 0], rwkv_iclr_a0[0, 1], jnp.zeros((D,), F32)], axis=0)
    outs = _rwkv_proj(xa, mods[1], norm_mix_g[1].reshape(1, D), jnp.pad(rwkv_x_mix[0], ((0, 2), (0, 0))),
                      bf(rwkv_w_r[0]), bf(rwkv_w_k[0]), bf(rwkv_w_v[0]),
                      bf(cat2(rwkv_decay_w1[0])), bf(rwkv_decay_w2[0]),
                      bf(cat2(rwkv_iclr_a1[0])), bf(rwkv_iclr_a2[0]),
                      bf(rwkv_gate_g1[0]), bf(rwkv_gate_g2[0]), vec, ones)
    r, v, kk, bv, gate, w0, w1, kd0, kd1, bd0, bd1 = outs
    lane_id = np.arange(D)
    eye = jnp.asarray(np.arange(HEAD_DIM)[:, None] == (lane_id % HEAD_DIM)[None, :], dtype=BF16)
    n_heads = D // HEAD_DIM
    head_mask = jnp.asarray(np.arange(n_heads)[:, None] == (lane_id // HEAD_DIM)[None, :], dtype=F32)
    sel = jnp.asarray((lane_id // HEAD_DIM)[:, None] == (np.arange(LANES) % n_heads)[None, :], dtype=BF16)
    e16 = jnp.asarray(np.arange(n_heads)[:, None] == (np.arange(MAP_LANES) % n_heads)[None, :], dtype=F32)
    scan_ins = [_rwkv_coef(r, kk, w_d, kd_d, bd_d, sel, reverse=d == 1)
                for d, (w_d, kd_d, bd_d) in enumerate(((w0, kd0, bd0), (w1, kd1, bd1)))]
    yf, yb = _rwkv_scan(scan_ins[0], scan_ins[1], v, head_mask, eye, e16, L)
    ln = jnp.stack([rwkv_ln_g[0], rwkv_ln_b[0]] + [jnp.zeros((D,), F32)] * 6, axis=0)
    xa, h, lg = _rwkv_out(yf.reshape(B, T, D), yb.reshape(B, T, D), bv, gate, ln, bf(rwkv_w_o[0]), ones, xa, mods[1],
                          norm_ffn_g[1].reshape(1, D), rwh, rwl)
    f = _moe(h[:, L:].reshape(B * S, D), lg[:, L:].reshape(B * S, LANES), router_bias,
             moe_w1[1], moe_w3[1], moe_w2[1])
    return _final(xa, f.reshape(B, S, D), mods[1], final_norm_g.reshape(1, D), L)


def _key_tile(T):
    for tk in (1280, 1024, 768, 512, 256):
        if T % tk == 0:
            return tk
    raise ValueError(T)
```

```python
import functools

import numpy as np
import jax
import jax.numpy as jnp
from jax import lax
from jax.experimental import pallas as pl
from jax.experimental.pallas import tpu as pltpu

F32 = jnp.float32
BF16 = jnp.bfloat16

D_MODEL = 1024
HEAD_DIM = 64
GRID_W = 64
ROPE_THETA = 10000.0
RMS_EPS = 1e-6
GN_EPS = 64e-5
A_Q_HEADS = 8
A_KV_HEADS = 2
B_Q_HEADS = 8
B_KV_HEADS = 2
GROUP = 4
WINDOW = 128
N_EXPERTS = 16
N_GROUPS = 4
EXPERTS_PER_GROUP = 4
TOP_K = 2
LANES = 128
TM = 256
MOE_ROWS = 512
SCAN_CHUNK = 64
VMEM_LIMIT = 56 * 1024 * 1024
NEG = -1e30
LOG2E = 1.4426950408889634


def _cparams(sem):
    return pltpu.CompilerParams(dimension_semantics=sem, vmem_limit_bytes=VMEM_LIMIT)


def _dot(a, b):
    return jnp.dot(a, b, preferred_element_type=F32)


def _dot_nt(a, b):
    return lax.dot_general(a, b, (((1,), (1,)), ((), ())), preferred_element_type=F32)


def _split(x):
    hi = x.astype(BF16)
    lo = (x - hi.astype(F32)).astype(BF16)
    return hi, lo


def _dot3(x, w):
    xh, xl = _split(x)
    wh, wl = _split(w)
    return _dot(xh, wh) + _dot(xh, wl) + _dot(xl, wh)


def _segsum(v, ones):
    hi, lo = _split(v)
    return _dot(hi, ones) + _dot(lo, ones)


def _segsum_wide(v, ones):
    n = v.shape[1] // LANES
    return jnp.concatenate([_segsum(v[:, j * LANES:(j + 1) * LANES], ones) for j in range(n)], axis=1)


def _norm_mod(x, g, shift, scale):
    ms = jnp.mean(x * x, axis=-1, keepdims=True)
    return (x * lax.rsqrt(ms + RMS_EPS) * g) * (1.0 + scale) + shift


def _sigmoid(x):
    return 1.0 / (1.0 + jnp.exp(-x))


def _seg_ones():
    i = np.arange(LANES)
    return jnp.asarray((i[:, None] // HEAD_DIM) == (i[None, :] // HEAD_DIM), dtype=BF16)


def _ada_kernel(c_ref, w_ref, b_ref, o_ref):
    c = c_ref[...]
    s = c * _sigmoid(c)
    o_ref[0] = _dot3(s, w_ref[0]) + b_ref[0]


def _ada(cs, ada_w, ada_b):
    depth, d, n = ada_w.shape
    tn = 1536
    return pl.pallas_call(
        _ada_kernel,
        out_shape=jax.ShapeDtypeStruct((depth, 8, n), F32),
        grid=(depth, n // tn),
        in_specs=[
            pl.BlockSpec((8, d), lambda l, j: (0, 0)),
            pl.BlockSpec((1, d, tn), lambda l, j: (l, 0, j)),
            pl.BlockSpec((1, 1, tn), lambda l, j: (l, 0, j)),
        ],
        out_specs=pl.BlockSpec((1, 8, tn), lambda l, j: (l, 0, j)),
        compiler_params=_cparams(("arbitrary", "arbitrary")),
        name="ada",
    )(cs, ada_w, ada_b.reshape(depth, 1, n))


def _inproj_kernel(x_ref, mod_ref, g_ref, w_ref, wrot_ref, cos_ref, sin_ref, gq_ref, gk_ref, ones_ref,
                   qa_ref, ka_ref, va_ref, qb_ref, kb_ref, vb_ref):
    mod = mod_ref[0, 0]
    h = _norm_mod(x_ref[0], g_ref[...], mod[0:1], mod[1:2]).astype(BF16)
    y = _dot(h, w_ref[...])
    yr = _dot(h, wrot_ref[...])
    cos = cos_ref[...]
    sin = sin_ref[...]
    ones = ones_ref[...]
    qscale = HEAD_DIM ** -0.5 * LOG2E

    def put(ref, tile, val):
        ref[0, 2 * tile] = val[:, :HEAD_DIM].astype(ref.dtype)
        ref[0, 2 * tile + 1] = val[:, HEAD_DIM:].astype(ref.dtype)

    def chunk(a, c):
        return a[:, c * LANES:(c + 1) * LANES]

    for c in range(4):
        put(qa_ref, c, (chunk(y, c) * cos + chunk(yr, c) * sin) * qscale)
    put(ka_ref, 0, chunk(y, 4) * cos + chunk(yr, 4) * sin)
    put(va_ref, 0, chunk(y, 5))

    def normed(c, cr, gain_ref):
        v = chunk(y, c)
        rs = lax.rsqrt(_segsum(v * v, ones) * (1.0 / HEAD_DIM) + RMS_EPS)
        return (v * rs * gain_ref[0:1]) * cos + (chunk(yr, cr) * rs * gain_ref[1:2]) * sin

    for c in range(4):
        put(qb_ref, c, normed(6 + c, 5 + c, gq_ref) * qscale)
    put(kb_ref, 0, normed(10, 9, gk_ref))
    put(vb_ref, 0, chunk(y, 11))


def _inproj(x, mods, g, w_in, w_rot, cos, sin, gq2, gk2, ones):
    B, T, D = x.shape
    nt = T // TM
    heads = lambda n: jax.ShapeDtypeStruct((B, n, T, HEAD_DIM), BF16)
    hspec = lambda n: pl.BlockSpec((1, n, TM, HEAD_DIM), lambda b, i: (b, 0, i, 0))
    full = lambda a: pl.BlockSpec(a.shape, lambda b, i: (0,) * a.ndim)
    return pl.pallas_call(
        _inproj_kernel,
        out_shape=(heads(8), heads(2), heads(2), heads(8), heads(2), heads(2)),
        grid=(B, nt),
        in_specs=[
            pl.BlockSpec((1, TM, D), lambda b, i: (b, i, 0)),
            pl.BlockSpec((1, 1, 6, D), lambda b, i: (b, jnp.minimum(i, 1), 0, 0)),
            full(g), full(w_in), full(w_rot),
            pl.BlockSpec((TM, LANES), lambda b, i: (i, 0)),
            pl.BlockSpec((TM, LANES), lambda b, i: (i, 0)),
            full(gq2), full(gk2), full(ones),
        ],
        out_specs=(hspec(8), hspec(2), hspec(2), hspec(8), hspec(2), hspec(2)),
        compiler_params=_cparams(("parallel", "parallel")),
        name="attn_inproj",
    )(x, mods, g, w_in, w_rot, cos, sin, gq2, gk2, ones)


LOOKAHEAD = 3


def _flash_kernel(sink_ref, q_ref, k_ref, v_ref, o_ref, m_scr, acc_scr, s_scr, *, tk, nk):
    h = pl.program_id(1)
    m_scr[...] = jnp.full(m_scr.shape, NEG, F32)
    acc_scr[...] = jnp.zeros(acc_scr.shape, F32)

    def scores(j, g):
        return _dot_nt(k_ref[0, 0, pl.ds(pl.multiple_of(j * tk, tk), tk), :], q_ref[0, 0, g])

    for g in range(LOOKAHEAD):
        s_scr[g] = scores(0, g)

    def body(j, carry):
        vt = v_ref[0, 0, :, pl.ds(pl.multiple_of(j * tk, tk), tk)]
        jn = jnp.minimum(j + 1, nk - 1)
        ahead = {}
        for g in range(GROUP):
            st = s_scr[g] if g < LOOKAHEAD else ahead.pop(g)
            if g + LOOKAHEAD < GROUP:
                ahead[g + LOOKAHEAD] = scores(j, g + LOOKAHEAD)
            m_prev = m_scr[g]
            m_new = jnp.maximum(m_prev, jnp.max(st, axis=0, keepdims=True))
            p = jnp.exp2(st - m_new).astype(BF16)
            if g + LOOKAHEAD >= GROUP:
                s_scr[g + LOOKAHEAD - GROUP] = scores(jn, g + LOOKAHEAD - GROUP)
            acc_scr[g] = jnp.exp2(m_prev - m_new) * acc_scr[g] + _dot(vt, p)
            m_scr[g] = m_new
        return carry

    lax.fori_loop(0, nk, body, 0)
    outs = []
    for g in range(GROUP):
        acc = acc_scr[g]
        l = acc[HEAD_DIM:HEAD_DIM + 1] + jnp.exp2(sink_ref[h * GROUP + g] - m_scr[g])
        outs.append(acc[:HEAD_DIM] / l)
    o_ref[0] = jnp.concatenate(outs, axis=0).T.astype(o_ref.dtype)


def _flash(sink, q, k, v, *, q_rows, q_off, k_rows, tq, tk):
    B, Hkv = k.shape[:2]
    nq, nk = q_rows // tq, k_rows // tk
    qo = q_off // tq
    return pl.pallas_call(
        functools.partial(_flash_kernel, tk=tk, nk=nk),
        out_shape=jax.ShapeDtypeStruct((B, q_rows, Hkv * GROUP * HEAD_DIM), BF16),
        grid=(B, Hkv, nq),
        in_specs=[
            pl.BlockSpec(memory_space=pltpu.SMEM),
            pl.BlockSpec((1, 1, GROUP, tq, HEAD_DIM), lambda b, h, i: (b, h, 0, i + qo, 0)),
            pl.BlockSpec((1, 1, k_rows, HEAD_DIM), lambda b, h, i: (b, h, 0, 0)),
            pl.BlockSpec((1, 1, LANES, k_rows), lambda b, h, i: (b, h, 0, 0)),
        ],
        out_specs=pl.BlockSpec((1, tq, GROUP * HEAD_DIM), lambda b, h, i: (b, i, h)),
        scratch_shapes=[
            pltpu.VMEM((GROUP, 1, tq), F32),
            pltpu.VMEM((GROUP, LANES, tq), F32),
            pltpu.VMEM((LOOKAHEAD, tk, tq), F32),
        ],
        compiler_params=_cparams(("parallel", "parallel", "arbitrary")),
        name="flash_attn",
    )(sink, q, k, v)


def _window_kernel(sink_ref, q_ref, kc_ref, vc_ref, k0_ref, k1_ref, k2_ref, v0_ref, v1_ref, v2_ref, o_ref, *, nb):
    h = pl.program_id(1)
    i = pl.program_id(2)
    rows = GROUP * WINDOW
    q = q_ref[0, 0].reshape(rows, HEAD_DIM)
    r = lax.broadcasted_iota(jnp.int32, (rows, WINDOW), 0) & (WINDOW - 1)
    c = lax.broadcasted_iota(jnp.int32, (rows, WINDOW), 1)
    sc = _dot_nt(q, kc_ref[0, 0])
    s0 = jnp.where((c >= r) & (i > 0), _dot_nt(q, k0_ref[0, 0]), NEG)
    s1 = _dot_nt(q, k1_ref[0, 0])
    s2 = jnp.where((c <= r) & (i < nb - 1), _dot_nt(q, k2_ref[0, 0]), NEG)
    sink = jnp.concatenate(
        [jnp.full((WINDOW, 1), sink_ref[h * GROUP + g], F32) for g in range(GROUP)], axis=0)
    rowmax = lambda s: jnp.max(s, axis=-1, keepdims=True)
    m = jnp.maximum(jnp.maximum(rowmax(sc), rowmax(s0)), jnp.maximum(rowmax(s1), rowmax(s2)))
    m = jnp.maximum(m, sink)
    pc, p0, p1, p2 = (jnp.exp2(s - m) for s in (sc, s0, s1, s2))
    rowsum = lambda p: jnp.sum(p, axis=-1, keepdims=True)
    l = rowsum(pc) + rowsum(p0) + rowsum(p1) + rowsum(p2) + jnp.exp2(sink - m)
    acc = (_dot(pc.astype(BF16), vc_ref[0, 0]) + _dot(p0.astype(BF16), v0_ref[0, 0])
           + _dot(p1.astype(BF16), v1_ref[0, 0]) + _dot(p2.astype(BF16), v2_ref[0, 0]))
    out = acc / l
    for g in range(GROUP):
        o_ref[0, :, g * HEAD_DIM:(g + 1) * HEAD_DIM] = out[g * WINDOW:(g + 1) * WINDOW].astype(o_ref.dtype)


def _window_attn(sink, q, k, v, L, S):
    B, Hkv = k.shape[:2]
    nb = S // WINDOW
    pad = ((0, 0), (0, 0), (WINDOW, WINDOW), (0, 0))
    kp = jnp.pad(k[:, :, L:], pad)
    vp = jnp.pad(v[:, :, L:], pad)
    qo = L // WINDOW
    band = lambda j: pl.BlockSpec((1, 1, WINDOW, HEAD_DIM), lambda b, h, i: (b, h, i + j, 0))
    ctx = pl.BlockSpec((1, 1, L, HEAD_DIM), lambda b, h, i: (b, h, 0, 0))
    return pl.pallas_call(
        functools.partial(_window_kernel, nb=nb),
        out_shape=jax.ShapeDtypeStruct((B, S, Hkv * GROUP * HEAD_DIM), BF16),
        grid=(B, Hkv, nb),
        in_specs=[
            pl.BlockSpec(memory_space=pltpu.SMEM),
            pl.BlockSpec((1, 1, GROUP, WINDOW, HEAD_DIM), lambda b, h, i: (b, h, 0, i + qo, 0)),
            ctx, ctx, band(0), band(1), band(2), band(0), band(1), band(2),
        ],
        out_specs=pl.BlockSpec((1, WINDOW, GROUP * HEAD_DIM), lambda b, h, i: (b, i, h)),
        compiler_params=_cparams(("parallel", "parallel", "parallel")),
        name="window_attn",
    )(sink, q, k, v, kp, kp, kp, vp, vp, vp)


def _ffn_prep(x, y, mod, gffn, rwh_ref, rwl_ref, xo_ref, h_ref, lg_ref):
    xn = x + mod[2:3] * y
    h = _norm_mod(xn, gffn, mod[3:4], mod[4:5])
    xo_ref[0] = xn
    hh, hl = _split(h)
    h_ref[0] = hh
    rwh = rwh_ref[...]
    lg_ref[0] = _dot(hh, rwh) + _dot(hl, rwh) + _dot(hh, rwl_ref[...])


def _attn_out_kernel(oa_ref, ob_ref, wa_ref, wb_ref, x_ref, mod_ref, g_ref, rwh_ref, rwl_ref,
                     xo_ref, h_ref, lg_ref):
    y = _dot(oa_ref[0], wa_ref[...]) + _dot(ob_ref[0], wb_ref[...])
    _ffn_prep(x_ref[0], y, mod_ref[0, 0], g_ref[...], rwh_ref, rwl_ref, xo_ref, h_ref, lg_ref)


def _row_specs(D):
    row = lambda w: pl.BlockSpec((1, TM, w), lambda b, i: (b, i, 0))
    mod = pl.BlockSpec((1, 1, 6, D), lambda b, i: (b, jnp.minimum(i, 1), 0, 0))
    full = lambda a: pl.BlockSpec(a.shape, lambda b, i: (0,) * a.ndim)
    return row, mod, full


def _ffn_prep_outs(B, T, D):
    row, _, _ = _row_specs(D)
    shapes = (jax.ShapeDtypeStruct((B, T, D), F32), jax.ShapeDtypeStruct((B, T, D), BF16),
              jax.ShapeDtypeStruct((B, T, LANES), F32))
    return shapes, (row(D), row(D), row(LANES))


def _attn_out(oa, ob, wa, wb, x, mods, g, rwh, rwl):
    B, T, D = x.shape
    row, mod, full = _row_specs(D)
    shapes, specs = _ffn_prep_outs(B, T, D)
    return pl.pallas_call(
        _attn_out_kernel,
        out_shape=shapes,
        grid=(B, T // TM),
        in_specs=[row(oa.shape[-1]), row(ob.shape[-1]), full(wa), full(wb), row(D), mod, full(g),
                  full(rwh), full(rwl)],
        out_specs=specs,
        compiler_params=_cparams(("parallel", "parallel")),
        name="attn_out",
    )(oa, ob, wa, wb, x, mods, g, rwh, rwl)


def _gmm_kernel(be_ref, nu_ref, x_ref, w1_ref, w3_ref, w2_ref, o_ref):
    i = pl.program_id(0)

    @pl.when(i < nu_ref[0])
    def _():
        x = x_ref[...]
        a = _dot(x, w1_ref[0].astype(BF16))
        b = _dot(x, w3_ref[0].astype(BF16))
        mid = (a * _sigmoid(a)) * b
        o_ref[...] = _dot(mid.astype(BF16), w2_ref[0].astype(BF16)).astype(o_ref.dtype)

    @pl.when(i >= nu_ref[0])
    def _():
        o_ref[...] = jnp.zeros(o_ref.shape, o_ref.dtype)


def _gmm(block_expert, n_used, xs, w1, w3, w2):
    n_slots, D = xs.shape
    F = w1.shape[-1]
    nblk = n_slots // MOE_ROWS
    return pl.pallas_call(
        _gmm_kernel,
        out_shape=jax.ShapeDtypeStruct((n_slots, D), BF16),
        grid_spec=pltpu.PrefetchScalarGridSpec(
            num_scalar_prefetch=2,
            grid=(nblk,),
            in_specs=[
                pl.BlockSpec((MOE_ROWS, D), lambda i, be, nu: (i, 0)),
                pl.BlockSpec((1, D, F), lambda i, be, nu: (be[i], 0, 0)),
                pl.BlockSpec((1, D, F), lambda i, be, nu: (be[i], 0, 0)),
                pl.BlockSpec((1, F, D), lambda i, be, nu: (be[i], 0, 0)),
            ],
            out_specs=pl.BlockSpec((MOE_ROWS, D), lambda i, be, nu: (i, 0)),
        ),
        compiler_params=_cparams(("arbitrary",)),
        name="moe_gmm",
    )(block_expert, n_used, xs, w1, w3, w2)


ROUTE_ROWS = 512


def _route_kernel(lg_ref, bias_ref, idx_ref, w_ref):
    x = lg_ref[...].T[:N_EXPERTS]
    m = jnp.max(x, axis=0, keepdims=True)
    e = jnp.exp(x - m)
    probs = e / jnp.sum(e, axis=0, keepdims=True)
    sel = probs + bias_ref[...][:, 0:1]
    row = lambda a, i: a[i:i + 1, :]
    G = EXPERTS_PER_GROUP
    scores = []
    for g in range(N_GROUPS):
        s = [row(sel, g * G + i) for i in range(G)]
        best = None
        for i in range(G):
            for j in range(i + 1, G):
                best = s[i] + s[j] if best is None else jnp.maximum(best, s[i] + s[j])
        scores.append(best)
    top = functools.reduce(jnp.maximum, scores)
    gi = jnp.full(top.shape, N_GROUPS - 1, jnp.int32)
    for g in range(N_GROUPS - 2, -1, -1):
        gi = jnp.where(scores[g] == top, g, gi)

    def pick(a, i):
        out = row(a, (N_GROUPS - 1) * G + i)
        for g in range(N_GROUPS - 2, -1, -1):
            out = jnp.where(gi == g, row(a, g * G + i), out)
        return out

    c = [pick(sel, i) for i in range(G)]
    pc = [pick(probs, i) for i in range(G)]

    def first_argmax(vals):
        mx = functools.reduce(jnp.maximum, vals)
        idx = jnp.full(mx.shape, G - 1, jnp.int32)
        for i in range(G - 2, -1, -1):
            idx = jnp.where(vals[i] == mx, i, idx)
        return idx

    i1 = first_argmax(c)
    i2 = first_argmax([jnp.where(i1 == i, -jnp.inf, c[i]) for i in range(G)])
    take = lambda vals, idx: functools.reduce(
        lambda acc, i: jnp.where(idx == i, vals[i], acc), range(G - 2, -1, -1), vals[G - 1])
    w1, w2 = take(pc, i1), take(pc, i2)
    tot = w1 + w2
    zi = jnp.zeros((6,) + top.shape[1:], jnp.int32)
    idx_ref[...] = jnp.concatenate([gi * G + i1, gi * G + i2, zi], axis=0)
    w_ref[...] = jnp.concatenate([w1 / tot, w2 / tot, zi.astype(F32)], axis=0)


def _route(logits, router_bias):
    N = logits.shape[0]
    bias = jnp.broadcast_to(router_bias.astype(F32)[:, None], (N_EXPERTS, LANES))
    idx, w = pl.pallas_call(
        _route_kernel,
        out_shape=(jax.ShapeDtypeStruct((8, N), jnp.int32), jax.ShapeDtypeStruct((8, N), F32)),
        grid=(N // ROUTE_ROWS,),
        in_specs=[pl.BlockSpec((ROUTE_ROWS, LANES), lambda i: (i, 0)),
                  pl.BlockSpec((N_EXPERTS, LANES), lambda i: (0, 0))],
        out_specs=(pl.BlockSpec((8, ROUTE_ROWS), lambda i: (0, i)),
                   pl.BlockSpec((8, ROUTE_ROWS), lambda i: (0, i))),
        compiler_params=_cparams(("parallel",)),
        name="route",
    )(logits, bias)
    return idx[:TOP_K], w[:TOP_K]


def _moe(h, logits, router_bias, w1, w3, w2):
    N, D = h.shape
    expert_idx, gate_w = _route(logits, router_bias)
    NK = N * TOP_K
    flat_e = expert_idx.reshape(NK)
    onehot = (flat_e[None, :] == jnp.arange(N_EXPERTS, dtype=jnp.int32)[:, None]).astype(jnp.int32)
    csum = jnp.cumsum(onehot, axis=1)
    counts = csum[:, -1]
    padded = (counts + MOE_ROWS - 1) // MOE_ROWS * MOE_ROWS
    pad_end = jnp.cumsum(padded)
    pad_start = pad_end - padded
    dest = jnp.sum(onehot * (csum - 1 + pad_start[:, None]), axis=0)
    nblk = -(-NK // MOE_ROWS) + N_EXPERTS
    n_slots = nblk * MOE_ROWS
    n_used = (pad_end[-1] // MOE_ROWS).astype(jnp.int32)
    blk = jnp.arange(nblk, dtype=jnp.int32)
    be = jnp.sum((pad_end[None, :] <= (blk * MOE_ROWS)[:, None]).astype(jnp.int32), axis=1)
    be = jnp.minimum(be, N_EXPERTS - 1)
    be = jnp.where(blk < n_used, be, be[jnp.maximum(n_used - 1, 0)])
    flat_tok = jnp.arange(NK, dtype=jnp.int32) % N
    slot_tok = jnp.zeros((n_slots,), jnp.int32).at[dest].set(flat_tok, unique_indices=True)
    xs = h[slot_tok]
    ys = _gmm(be, n_used.reshape(1), xs, w1, w3, w2)
    return ys[dest[:N]] * gate_w[0][:, None] + ys[dest[N:]] * gate_w[1][:, None]


def _residual_kernel(x_ref, f_ref, mod_ref, o_ref):
    o_ref[0] = x_ref[0] + mod_ref[0, 0][5:6] * f_ref[0]


def _final_kernel(x_ref, f_ref, mod_ref, g_ref, o_ref):
    x = x_ref[0] + mod_ref[0, 0][5:6] * f_ref[0]
    ms = jnp.mean(x * x, axis=-1, keepdims=True)
    o_ref[0] = x * lax.rsqrt(ms + RMS_EPS) * g_ref[...]


def _residual(x, f, mods):
    B, T, D = x.shape
    row, mod, _ = _row_specs(D)
    return pl.pallas_call(
        _residual_kernel,
        out_shape=jax.ShapeDtypeStruct((B, T, D), F32),
        grid=(B, T // TM),
        in_specs=[row(D), row(D), mod],
        out_specs=row(D),
        compiler_params=_cparams(("parallel", "parallel")),
        name="residual",
    )(x, f, mods)


def _final(x, f, mods, g, L):
    B, S, D = f.shape
    off = L // TM
    mod = pl.BlockSpec((1, 1, 6, D), lambda b, i: (b, 1, 0, 0))
    return pl.pallas_call(
        _final_kernel,
        out_shape=jax.ShapeDtypeStruct((B, S, D), F32),
        grid=(B, S // TM),
        in_specs=[pl.BlockSpec((1, TM, D), lambda b, i: (b, i + off, 0)),
                  pl.BlockSpec((1, TM, D), lambda b, i: (b, i, 0)), mod,
                  pl.BlockSpec(g.shape, lambda b, i: (0, 0))],
        out_specs=pl.BlockSpec((1, TM, D), lambda b, i: (b, i, 0)),
        compiler_params=_cparams(("parallel", "parallel")),
        name="final_norm",
    )(x, f, mods, g)


def _rwkv_proj_kernel(x_ref, xp_ref, xn_ref, mod_ref, g_ref, xmix_ref, wr_ref, wk_ref, wv_ref,
                      dw1_ref, dw2_ref, da1_ref, da2_ref, g1_ref, g2_ref, vec_ref, ones_ref,
                      r_ref, v_ref, kk_ref, bv_ref, gate_ref, w0_ref, w1_ref, kd0_ref, kd1_ref, bd0_ref, bd1_ref,
                      *, nt):
    i = pl.program_id(1)
    mod = mod_ref[0, 0]
    g = g_ref[...]
    nm = lambda x: _norm_mod(x, g, mod[0:1], mod[1:2])
    h = nm(x_ref[0])
    hp = nm(xp_ref[0])[7:8] * jnp.where(i >= 2, 1.0, 0.0)
    hn = nm(xn_ref[0])[0:1] * jnp.where((i >= 1) & (i < nt - 1), 1.0, 0.0)
    ridx = lax.broadcasted_iota(jnp.int32, h.shape, 0)
    h_dn = jnp.where(ridx == 0, hp, pltpu.roll(h, 1, axis=0))
    h_up = jnp.where(ridx == TM - 1, hn, pltpu.roll(h, TM - 1, axis=0))
    xx = 0.5 * (h_dn + h_up) - h
    xmix = xmix_ref[...]
    mix = lambda j: (h + xx * xmix[j:j + 1]).astype(BF16)
    vec = vec_ref[...]
    ones = ones_ref[...]

    r = _dot(mix(0), wr_ref[...])
    k = _dot(mix(2), wk_ref[...])
    v = _dot(mix(3), wv_ref[...])
    gate_ref[0] = _dot(_sigmoid(_dot(mix(5), g1_ref[...])).astype(BF16), g2_ref[...])
    kk = k * vec[0:1]
    kk = kk * lax.rsqrt(jnp.maximum(_segsum_wide(kk * kk, ones), 1e-24))
    lw = jnp.tanh(_dot(mix(1), dw1_ref[...])).astype(BF16)
    la = _dot(mix(4), da1_ref[...]).astype(BF16)
    r_ref[0] = r
    v_ref[0] = v
    kk_ref[0] = kk
    bonus = jnp.zeros_like(r)
    lora = DECAY_LORA
    for d, (w_ref, kd_ref, bd_ref) in enumerate(((w0_ref, kd0_ref, bd0_ref), (w1_ref, kd1_ref, bd1_ref))):
        z = -(vec[3 + d:4 + d] + _dot(lw[:, d * lora:(d + 1) * lora], dw2_ref[d]))
        softplus = jnp.maximum(z, 0.0) + jnp.log(1.0 + jnp.exp(-jnp.abs(z)))
        w_ref[0] = jnp.exp(-jnp.exp(-softplus - 0.5))
        iclr = _sigmoid(vec[5 + d:6 + d] + _dot(la[:, d * lora:(d + 1) * lora], da2_ref[d]))
        kd = k * (1.0 + (iclr - 1.0) * vec[1:2])
        kd_ref[0] = kd
        bd_ref[0] = kk * iclr
        bonus = bonus + _segsum_wide(r * kd * vec[2:3], ones)
    bv_ref[0] = bonus * v


DECAY_LORA = 64


def _rwkv_proj(x, mods, g, xmix, wr, wk, wv, dw1, dw2, da1, da2, g1, g2, vec, ones):
    B, T, D = x.shape
    nt = T // TM
    row, mod, full = _row_specs(D)
    r8 = TM // 8
    prev = pl.BlockSpec((1, 8, D), lambda b, i: (b, jnp.maximum(i * r8 - 1, 0), 0))
    nxt = pl.BlockSpec((1, 8, D), lambda b, i: (b, jnp.minimum((i + 1) * r8, T // 8 - 1), 0))
    out = jax.ShapeDtypeStruct((B, T, D), F32)
    return pl.pallas_call(
        functools.partial(_rwkv_proj_kernel, nt=nt),
        out_shape=(out,) * 11,
        grid=(B, nt),
        in_specs=[row(D), prev, nxt, mod, full(g), full(xmix), full(wr), full(wk), full(wv),
                  full(dw1), full(dw2), full(da1), full(da2), full(g1), full(g2), full(vec), full(ones)],
        out_specs=(row(D),) * 11,
        compiler_params=_cparams(("parallel", "parallel")),
        name="rwkv_proj",
    )(x, x, x, mods, g, xmix, wr, wk, wv, dw1, dw2, da1, da2, g1, g2, vec, ones)


CHUNK = 4
N_HEADS = D_MODEL // HEAD_DIM
MAP_LANES = 3 * CHUNK * N_HEADS


def _coef_kernel(r_ref, kk_ref, w_ref, kd_ref, bd_ref, sel_ref, at_ref, rt_ref, bh_ref, kh_ref, gc_ref,
                 cu_ref, cy_ref, *, reverse):
    r, a, w, kd, bd = r_ref[0], -kk_ref[0], w_ref[0], kd_ref[0], bd_ref[0]
    rows = r.shape[0]
    p = lax.broadcasted_iota(jnp.int32, r.shape, 0) & (CHUNK - 1)
    s = (CHUNK - 1 - p) if reverse else p
    back = lambda x, k: pltpu.roll(x, (rows - k) if reverse else k, axis=0)
    ahead = lambda x, k: pltpu.roll(x, k if reverse else (rows - k), axis=0)
    wb = [None] + [back(w, k) for k in range(1, CHUNK)]
    excl = jnp.ones_like(w)
    rest = jnp.ones_like(w)
    for k in range(1, CHUNK):
        excl = excl * jnp.where(s >= k, wb[k], 1.0)
        rest = rest * jnp.where(s + k <= CHUNK - 1, ahead(w, k), 1.0)
    at_ref[0] = a * excl
    rt_ref[0] = r * (excl * w)
    bh_ref[0] = bd * rest
    kh_ref[0] = kd * rest
    gc_ref[0] = excl * w * rest
    between = [None, None, wb[1], wb[1] * wb[2]]
    rw = r * w
    s1 = s[:, :LANES]
    seg = lambda x: _dot(x.astype(BF16), sel_ref[...])

    def by_dist(lead, y, first):
        out = [None] * CHUNK
        for dist in range(first, CHUNK):
            if dist == 0:
                out[0] = seg(r * y)
                continue
            e = back(y, dist) if between[dist] is None else between[dist] * back(y, dist)
            out[dist] = jnp.where(s1 >= dist, seg(lead * e), 0.0)
        return out

    lab, lak = by_dist(a, bd, 1), by_dist(a, kd, 1)
    rb, rk = by_dist(rw, bd, 0), by_dist(rw, kd, 0)
    one = jnp.ones_like(rb[0])
    bk = lambda x, k: x if k == 0 else pltpu.roll(x, (rows - k) if reverse else k, axis=0)
    md = [one]
    for dist in range(1, CHUNK):
        md.append(sum(lab[e] * (bk(md[dist - e], e) if dist - e else 1.0) for e in range(1, dist + 1)))
    gd = [None] + [sum((md[e] if e else 1.0) * bk(lak[dist - e], e) for e in range(dist)) for dist in range(1, CHUNK)]
    yzd = [sum(rb[e] * (bk(md[dist - e], e) if dist - e else 1.0) for e in range(dist + 1)) for dist in range(CHUNK)]
    yvd = [rk[dist] + sum(rb[e] * bk(gd[dist - e], e) for e in range(dist)) for dist in range(CHUNK)]

    def at_pos(table, j, first):
        out = jnp.zeros_like(one)
        for dist in range(first, CHUNK - j):
            out = jnp.where(s1 == j + dist, table[dist], out)
        return out

    zero = jnp.zeros_like(one)
    u_blocks = ([at_pos(md, j, 0) for j in range(CHUNK)] + [at_pos(gd, j, 1) for j in range(CHUNK)]
                + [zero] * CHUNK)
    y_blocks = ([at_pos(yzd, j, 0) for j in range(CHUNK)] + [at_pos(yvd, j, 0) for j in range(CHUNK)]
                + [jnp.where(s1 == j, 1.0, 0.0) for j in range(CHUNK)])
    lane_blk = lax.broadcasted_iota(jnp.int32, one.shape, 1) // N_HEADS

    def place(blocks):
        per_tile = LANES // N_HEADS
        tiles = []
        for t in range(2):
            acc = zero
            for i in range(t * per_tile, min((t + 1) * per_tile, len(blocks))):
                acc = jnp.where(lane_blk == i - t * per_tile, blocks[i], acc)
            tiles.append(acc)
        return jnp.concatenate(tiles, axis=1)

    cu_ref[0] = place(u_blocks)
    cy_ref[0] = place(y_blocks)


def _rwkv_coef(r, kk, w, kd, bd, sel, reverse):
    B, T, D = r.shape
    row, _, full = _row_specs(D)
    out = jax.ShapeDtypeStruct((B, T, D), F32)
    maps = jax.ShapeDtypeStruct((B, T, 2 * LANES), F32)
    return pl.pallas_call(
        functools.partial(_coef_kernel, reverse=reverse),
        out_shape=(out,) * 5 + (maps, maps),
        grid=(B, T // TM),
        in_specs=[row(D)] * 5 + [full(sel)],
        out_specs=(row(D),) * 5 + (row(2 * LANES), row(2 * LANES)),
        compiler_params=_cparams(("parallel", "parallel")),
        name="rwkv_coef",
    )(r, kk, w, kd, bd, sel)


def _scan_kernel(atf, atb, rtf, rtb, vf, vb, bhf, bhb, khf, khb, gcf, gcb, cuf, cub, cyf, cyb,
                 mask_ref, eye_ref, e16_ref, yf, yb, st, *, tc, nb):
    n = pl.program_id(0)

    @pl.when(n == 0)
    def _():
        st[...] = jnp.zeros(st.shape, F32)

    N = HEAD_DIM
    W = CHUNK * N_HEADS
    nch = tc // CHUNK
    dirs = ((atf, rtf, vf, bhf, khf, gcf, cuf, cyf, yf), (atb, rtb, vb, bhb, khb, gcb, cub, cyb, yb))
    lane1 = lax.broadcasted_iota(jnp.int32, (N, 3 * W), 1)
    lane2 = lax.broadcasted_iota(jnp.int32, (N, 2 * W), 1)
    spread = lambda x: (mask_ref[...] * x).astype(BF16)

    def chunk(ci, carry):
        work = []
        for d, refs in enumerate(dirs):
            cc = ci if d == 0 else nch - 1 - ci
            rows = [pl.ds(cc * CHUNK + (s if d == 0 else CHUNK - 1 - s), 1) for s in range(CHUNK)]
            for b in range(nb):
                work.append((d * nb + b, b, cc, rows, refs))
        firsts = []
        for gi, b, cc, rows, (AT, RT, V, BH, KH, GC, CU, CY, Y) in work:
            lhs = jnp.concatenate([st[gi].astype(BF16), eye_ref[...]], axis=0)
            w1 = jnp.concatenate([spread(X[b, rw, :]) for X in (AT, V, RT) for rw in rows], axis=0)
            firsts.append(_dot_nt(lhs, w1))
        mids = []
        for (gi, b, cc, rows, (AT, RT, V, BH, KH, GC, CU, CY, Y)), out in zip(work, firsts):
            zvq = jnp.where((lane1 >= W) & (lane1 < 2 * W), out[N:], out[:N])
            zvq16 = zvq.astype(BF16)
            maprows = lambda M: [e16_ref[...] * M[b, rw, :MAP_LANES] for rw in rows]
            wu = jnp.concatenate(maprows(CU) + [jnp.zeros((W, MAP_LANES), F32)], axis=0)
            u = _dot_nt(zvq16, wu.astype(BF16))
            wy = jnp.concatenate(maprows(CY), axis=0)
            yt = _dot_nt(wy.astype(BF16), zvq16)
            for s, rw in enumerate(rows):
                Y[b, rw] = yt[s * N_HEADS:(s + 1) * N_HEADS][None]
            mids.append((zvq, u))
        for (gi, b, cc, rows, (AT, RT, V, BH, KH, GC, CU, CY, Y)), (zvq, u) in zip(work, mids):
            uv = jnp.where(lane2 < W, u, zvq[:, :2 * W]).astype(BF16)
            w2 = jnp.concatenate([spread(X[b, rw, :]) for X in (BH, KH) for rw in rows], axis=0)
            st[gi] = st[gi] * GC[b, rows[0], :] + _dot(uv, w2)
        return carry

    lax.fori_loop(0, nch, chunk, 0)


def _rwkv_scan(ins_f, ins_b, v, mask, eye, e16, L):
    B, T, D = v.shape
    tc = SCAN_CHUNK
    nch = tc // CHUNK
    nc, nchunks = L // tc, T // tc
    fwd_idx = lambda n: n
    rev_idx = lambda n: jnp.where(n < nc, nc - 1 - n, nchunks - 1 - (n - nc))
    tok = lambda idx, w: pl.BlockSpec((B, tc, w), lambda n: (0, idx(n), 0))
    ys = lambda idx: pl.BlockSpec((B, tc, N_HEADS, HEAD_DIM), lambda n: (0, idx(n), 0, 0))
    full = lambda a: pl.BlockSpec(a.shape, lambda n: (0,) * a.ndim)
    out = jax.ShapeDtypeStruct((B, T, N_HEADS, HEAD_DIM), F32)
    atf, rtf, bhf, khf, gcf, cuf, cyf = ins_f
    atb, rtb, bhb, khb, gcb, cub, cyb = ins_b
    f, r_ = tok(fwd_idx, D), tok(rev_idx, D)
    wy = cyf.shape[-1]
    return pl.pallas_call(
        functools.partial(_scan_kernel, tc=tc, nb=B),
        out_shape=(out, out),
        grid=(nchunks,),
        in_specs=[f, r_, f, r_, f, r_, f, r_, f, r_, f, r_, tok(fwd_idx, wy), tok(rev_idx, wy),
                  tok(fwd_idx, wy), tok(rev_idx, wy), full(mask), full(eye), full(e16)],
        out_specs=(ys(fwd_idx), ys(rev_idx)),
        scratch_shapes=[pltpu.VMEM((2 * B, HEAD_DIM, D), F32)],
        compiler_params=_cparams(("arbitrary",)),
        name="rwkv_scan",
    )(atf, atb, rtf, rtb, v, v, bhf, bhb, khf, khb, gcf, gcb, cuf, cub, cyf, cyb, mask, eye, e16)


def _rwkv_out_kernel(yf_ref, yb_ref, bv_ref, gate_ref, ln_ref, wo_ref, ones_ref, x_ref, mod_ref, g_ref,
                     rwh_ref, rwl_ref, xo_ref, h_ref, lg_ref):
    ones = ones_ref[...]
    y = yf_ref[0] + yb_ref[0]
    inv = 1.0 / HEAD_DIM
    dlt = y - _segsum_wide(y, ones) * inv
    yn = dlt * lax.rsqrt(_segsum_wide(dlt * dlt, ones) * inv + GN_EPS)
    ln = ln_ref[...]
    o = (yn * ln[0:1] + ln[1:2] + bv_ref[0]) * gate_ref[0]
    yl = _dot(o.astype(BF16), wo_ref[...])
    _ffn_prep(x_ref[0], yl, mod_ref[0, 0], g_ref[...], rwh_ref, rwl_ref, xo_ref, h_ref, lg_ref)


def _rwkv_out(yf, yb, bv, gate, ln, wo, ones, x, mods, g, rwh, rwl):
    B, T, D = x.shape
    row, mod, full = _row_specs(D)
    shapes, specs = _ffn_prep_outs(B, T, D)
    return pl.pallas_call(
        _rwkv_out_kernel,
        out_shape=shapes,
        grid=(B, T // TM),
        in_specs=[row(D), row(D), row(D), row(D), full(ln), full(wo), full(ones), row(D), mod, full(g),
                  full(rwh), full(rwl)],
        out_specs=specs,
        compiler_params=_cparams(("parallel", "parallel")),
        name="rwkv_out",
    )(yf, yb, bv, gate, ln, wo, ones, x, mods, g, rwh, rwl)


def _rope_tables(S, L):
    rows = S // GRID_W
    row = jnp.repeat(jnp.arange(rows, dtype=F32), GRID_W)
    col = (jnp.arange(rows * GRID_W) % GRID_W).astype(F32)
    n_freq = HEAD_DIM // 4
    inv = ROPE_THETA ** (-jnp.arange(n_freq, dtype=F32) / n_freq)
    lane = np.arange(LANES) % HEAD_DIM
    axis, half, freq = lane // 32, (lane % 32) // 16, lane % 16
    pos = jnp.where(jnp.asarray(axis == 0)[None, :], row[:, None], col[:, None])
    ang = pos * inv[freq][None, :]
    sgn = jnp.asarray(np.where(half == 0, -1.0, 1.0), dtype=F32)
    cos = jnp.concatenate([jnp.ones((L, LANES), F32), jnp.cos(ang)], axis=0)
    sin = jnp.concatenate([jnp.zeros((L, LANES), F32), jnp.sin(ang) * sgn[None, :]], axis=0)
    return cos, sin


def kernel(x, c, ctx, c_ctx, ada_w, ada_b, norm_mix_g, norm_ffn_g, attn_w_in, attn_w_out, attn_sink,
           attn_q_norm_g, attn_k_norm_g, rwkv_x_mix, rwkv_w_r, rwkv_w_k, rwkv_w_v, rwkv_w_o,
           rwkv_decay_w0, rwkv_decay_w1, rwkv_decay_w2, rwkv_iclr_a0, rwkv_iclr_a1, rwkv_iclr_a2,
           rwkv_gate_g1, rwkv_gate_g2, rwkv_k_k, rwkv_k_a, rwkv_r_k, rwkv_ln_g, rwkv_ln_b,
           router_w, router_bias, moe_w1, moe_w3, moe_w2, final_norm_g):
    B, S, D = x.shape
    L = ctx.shape[1]
    T = L + S
    depth = ada_w.shape[0]
    assert D == D_MODEL and L == TM and S % TM == 0 and B == 2 and depth == 2
    ones = _seg_ones()
    bf = lambda a: a.astype(BF16)

    cs = jnp.zeros((8, D), F32).at[:B].set(c).at[B].set(c_ctx)
    ada = _ada(cs, ada_w, ada_b).reshape(depth, 8, 6, D)
    mods = [jnp.stack([jnp.broadcast_to(ada[i, B], (B, 6, D)), ada[i, :B]], axis=1) for i in range(depth)]

    xa = jnp.concatenate([ctx, x], axis=1)
    rw = jnp.zeros((D, LANES), F32).at[:, :N_EXPERTS].set(router_w)
    rwh, rwl = _split(rw)

    w_in = attn_w_in[0]
    roped = np.concatenate([np.arange(0, 640), np.arange(768, 1408)])
    w_rot = w_in[:, roped ^ 16]
    cos, sin = _rope_tables(S, L)
    lane = np.arange(LANES) % HEAD_DIM
    gains = lambda g: jnp.stack([g[lane], g[lane ^ 16]], axis=0)
    qa, ka, va, qb, kb, vb = _inproj(xa, mods[0], norm_mix_g[0].reshape(1, D), bf(w_in), bf(w_rot), cos, sin,
                                     gains(attn_q_norm_g[0]), gains(attn_k_norm_g[0]), ones)
    grouped = lambda q: q.reshape(B, A_KV_HEADS, GROUP, T, HEAD_DIM)
    qa, qb = grouped(qa), grouped(qb)
    sink = attn_sink[0].astype(F32) * LOG2E
    ext = lambda v: jnp.swapaxes(
        jnp.concatenate([v, jnp.ones_like(v[..., :1]), jnp.zeros_like(v[..., :HEAD_DIM - 1])], axis=-1), 2, 3)
    va_x, vb_x = ext(va), ext(vb)
    nosink = jnp.full((B_Q_HEADS,), NEG, F32)
    oa_l = _window_attn(sink, qa, ka, va, L, S)
    oa_c = _flash(sink, qa, ka, va_x, q_rows=L, q_off=0, k_rows=L, tq=L, tk=L)
    ob_l = _flash(nosink, qb, kb, vb_x, q_rows=S, q_off=L, k_rows=T, tq=256, tk=_key_tile(T))
    ob_c = _flash(nosink, qb, kb, vb_x, q_rows=L, q_off=0, k_rows=L, tq=L, tk=L)
    oa = jnp.concatenate([oa_c, oa_l], axis=1)
    ob = jnp.concatenate([ob_c, ob_l], axis=1)
    w_out = bf(attn_w_out[0])
    na = A_Q_HEADS * HEAD_DIM
    xa, h, lg = _attn_out(oa, ob, w_out[:na], w_out[na:], xa, mods[0], norm_ffn_g[0].reshape(1, D), rwh, rwl)
    f = _moe(h.reshape(B * T, D), lg.reshape(B * T, LANES), router_bias,
             moe_w1[0], moe_w3[0], moe_w2[0])
    xa = _residual(xa, f.reshape(B, T, D), mods[0])

    cat2 = lambda a: jnp.concatenate([a[0], a[1]], axis=1)
    vec = jnp.stack([rwkv_k_k[0], rwkv_k_a[0], rwkv_r_k[0].reshape(D), rwkv_decay_w0[0, 0], rwkv_decay_w0[0, 1],
                     rwkv_iclr_a0[0, 0], rwkv_iclr_a0[0, 1], jnp.zeros((D,), F32)], axis=0)
    outs = _rwkv_proj(xa, mods[1], norm_mix_g[1].reshape(1, D), jnp.pad(rwkv_x_mix[0], ((0, 2), (0, 0))),
                      bf(rwkv_w_r[0]), bf(rwkv_w_k[0]), bf(rwkv_w_v[0]),
                      bf(cat2(rwkv_decay_w1[0])), bf(rwkv_decay_w2[0]),
                      bf(cat2(rwkv_iclr_a1[0])), bf(rwkv_iclr_a2[0]),
                      bf(rwkv_gate_g1[0]), bf(rwkv_gate_g2[0]), vec, ones)
    r, v, kk, bv, gate, w0, w1, kd0, kd1, bd0, bd1 = outs
    lane_id = np.arange(D)
    eye = jnp.asarray(np.arange(HEAD_DIM)[:, None] == (lane_id % HEAD_DIM)[None, :], dtype=BF16)
    n_heads = D // HEAD_DIM
    head_mask = jnp.asarray(np.arange(n_heads)[:, None] == (lane_id // HEAD_DIM)[None, :], dtype=F32)
    sel = jnp.asarray((lane_id // HEAD_DIM)[:, None] == (np.arange(LANES) % n_heads)[None, :], dtype=BF16)
    e16 = jnp.asarray(np.arange(n_heads)[:, None] == (np.arange(MAP_LANES) % n_heads)[None, :], dtype=F32)
    scan_ins = [_rwkv_coef(r, kk, w_d, kd_d, bd_d, sel, reverse=d == 1)
                for d, (w_d, kd_d, bd_d) in enumerate(((w0, kd0, bd0), (w1, kd1, bd1)))]
    yf, yb = _rwkv_scan(scan_ins[0], scan_ins[1], v, head_mask, eye, e16, L)
    ln = jnp.stack([rwkv_ln_g[0], rwkv_ln_b[0]] + [jnp.zeros((D,), F32)] * 6, axis=0)
    xa, h, lg = _rwkv_out(yf.reshape(B, T, D), yb.reshape(B, T, D), bv, gate, ln, bf(rwkv_w_o[0]), ones, xa, mods[1],
                          norm_ffn_g[1].reshape(1, D), rwh, rwl)
    f = _moe(h[:, L:].reshape(B * S, D), lg[:, L:].reshape(B * S, LANES), router_bias,
             moe_w1[1], moe_w3[1], moe_w2[1])
    return _final(xa, f.reshape(B, S, D), mods[1], final_norm_g.reshape(1, D), L)


def _key_tile(T):
    for tk in (1280, 1024, 768, 512, 256):
        if T % tk == 0:
            return tk
    raise ValueError(T)
```

```python
import functools

import numpy as np
import jax
import jax.numpy as jnp
from jax import lax
from jax.experimental import pallas as pl
from jax.experimental.pallas import tpu as pltpu

F32 = jnp.float32
BF16 = jnp.bfloat16

D_MODEL = 1024
HEAD_DIM = 64
GRID_W = 64
ROPE_THETA = 10000.0
RMS_EPS = 1e-6
GN_EPS = 64e-5
A_Q_HEADS = 8
A_KV_HEADS = 2
B_Q_HEADS = 8
B_KV_HEADS = 2
GROUP = 4
WINDOW = 128
N_EXPERTS = 16
N_GROUPS = 4
EXPERTS_PER_GROUP = 4
TOP_K = 2
LANES = 128
TM = 256
MOE_ROWS = 512
SCAN_CHUNK = 64
VMEM_LIMIT = 56 * 1024 * 1024
NEG = -1e30
LOG2E = 1.4426950408889634


def _cparams(sem):
    return pltpu.CompilerParams(dimension_semantics=sem, vmem_limit_bytes=VMEM_LIMIT)


def _dot(a, b):
    return jnp.dot(a, b, preferred_element_type=F32)


def _dot_nt(a, b):
    return lax.dot_general(a, b, (((1,), (1,)), ((), ())), preferred_element_type=F32)


def _split(x):
    hi = x.astype(BF16)
    lo = (x - hi.astype(F32)).astype(BF16)
    return hi, lo


def _dot3(x, w):
    xh, xl = _split(x)
    wh, wl = _split(w)
    return _dot(xh, wh) + _dot(xh, wl) + _dot(xl, wh)


def _segsum(v, ones):
    hi, lo = _split(v)
    return _dot(hi, ones) + _dot(lo, ones)


def _segsum_wide(v, ones):
    n = v.shape[1] // LANES
    return jnp.concatenate([_segsum(v[:, j * LANES:(j + 1) * LANES], ones) for j in range(n)], axis=1)


def _norm_mod(x, g, shift, scale):
    ms = jnp.mean(x * x, axis=-1, keepdims=True)
    return (x * lax.rsqrt(ms + RMS_EPS) * g) * (1.0 + scale) + shift


def _sigmoid(x):
    return 1.0 / (1.0 + jnp.exp(-x))


def _seg_ones():
    i = np.arange(LANES)
    return jnp.asarray((i[:, None] // HEAD_DIM) == (i[None, :] // HEAD_DIM), dtype=BF16)


def _ada_kernel(c_ref, w_ref, b_ref, o_ref):
    c = c_ref[...]
    s = c * _sigmoid(c)
    o_ref[0] = _dot3(s, w_ref[0]) + b_ref[0]


def _ada(cs, ada_w, ada_b):
    depth, d, n = ada_w.shape
    tn = 1536
    return pl.pallas_call(
        _ada_kernel,
        out_shape=jax.ShapeDtypeStruct((depth, 8, n), F32),
        grid=(depth, n // tn),
        in_specs=[
            pl.BlockSpec((8, d), lambda l, j: (0, 0)),
            pl.BlockSpec((1, d, tn), lambda l, j: (l, 0, j)),
            pl.BlockSpec((1, 1, tn), lambda l, j: (l, 0, j)),
        ],
        out_specs=pl.BlockSpec((1, 8, tn), lambda l, j: (l, 0, j)),
        compiler_params=_cparams(("arbitrary", "arbitrary")),
        name="ada",
    )(cs, ada_w, ada_b.reshape(depth, 1, n))


def _inproj_kernel(x_ref, mod_ref, g_ref, w_ref, wrot_ref, cos_ref, sin_ref, gq_ref, gk_ref, ones_ref,
                   qa_ref, ka_ref, va_ref, qb_ref, kb_ref, vb_ref):
    mod = mod_ref[0, 0]
    h = _norm_mod(x_ref[0], g_ref[...], mod[0:1], mod[1:2]).astype(BF16)
    y = _dot(h, w_ref[...])
    yr = _dot(h, wrot_ref[...])
    cos = cos_ref[...]
    sin = sin_ref[...]
    ones = ones_ref[...]
    qscale = HEAD_DIM ** -0.5 * LOG2E

    def put(ref, tile, val):
        ref[0, 2 * tile] = val[:, :HEAD_DIM].astype(ref.dtype)
        ref[0, 2 * tile + 1] = val[:, HEAD_DIM:].astype(ref.dtype)

    def chunk(a, c):
        return a[:, c * LANES:(c + 1) * LANES]

    for c in range(4):
        put(qa_ref, c, (chunk(y, c) * cos + chunk(yr, c) * sin) * qscale)
    put(ka_ref, 0, chunk(y, 4) * cos + chunk(yr, 4) * sin)
    put(va_ref, 0, chunk(y, 5))

    def normed(c, cr, gain_ref):
        v = chunk(y, c)
        rs = lax.rsqrt(_segsum(v * v, ones) * (1.0 / HEAD_DIM) + RMS_EPS)
        return (v * rs * gain_ref[0:1]) * cos + (chunk(yr, cr) * rs * gain_ref[1:2]) * sin

    for c in range(4):
        put(qb_ref, c, normed(6 + c, 5 + c, gq_ref) * qscale)
    put(kb_ref, 0, normed(10, 9, gk_ref))
    put(vb_ref, 0, chunk(y, 11))


def _inproj(x, mods, g, w_in, w_rot, cos, sin, gq2, gk2, ones):
    B, T, D = x.shape
    nt = T // TM
    heads = lambda n: jax.ShapeDtypeStruct((B, n, T, HEAD_DIM), BF16)
    hspec = lambda n: pl.BlockSpec((1, n, TM, HEAD_DIM), lambda b, i: (b, 0, i, 0))
    full = lambda a: pl.BlockSpec(a.shape, lambda b, i: (0,) * a.ndim)
    return pl.pallas_call(
        _inproj_kernel,
        out_shape=(heads(8), heads(2), heads(2), heads(8), heads(2), heads(2)),
        grid=(B, nt),
        in_specs=[
            pl.BlockSpec((1, TM, D), lambda b, i: (b, i, 0)),
            pl.BlockSpec((1, 1, 6, D), lambda b, i: (b, jnp.minimum(i, 1), 0, 0)),
            full(g), full(w_in), full(w_rot),
            pl.BlockSpec((TM, LANES), lambda b, i: (i, 0)),
            pl.BlockSpec((TM, LANES), lambda b, i: (i, 0)),
            full(gq2), full(gk2), full(ones),
        ],
        out_specs=(hspec(8), hspec(2), hspec(2), hspec(8), hspec(2), hspec(2)),
        compiler_params=_cparams(("parallel", "parallel")),
        name="attn_inproj",
    )(x, mods, g, w_in, w_rot, cos, sin, gq2, gk2, ones)


LOOKAHEAD = 3


def _flash_kernel(sink_ref, q_ref, k_ref, v_ref, o_ref, m_scr, acc_scr, s_scr, *, tk, nk):
    h = pl.program_id(1)
    m_scr[...] = jnp.full(m_scr.shape, NEG, F32)
    acc_scr[...] = jnp.zeros(acc_scr.shape, F32)

    def scores(j, g):
        return _dot_nt(k_ref[0, 0, pl.ds(pl.multiple_of(j * tk, tk), tk), :], q_ref[0, 0, g])

    for g in range(LOOKAHEAD):
        s_scr[g] = scores(0, g)

    def body(j, carry):
        vt = v_ref[0, 0, :, pl.ds(pl.multiple_of(j * tk, tk), tk)]
        jn = jnp.minimum(j + 1, nk - 1)
        ahead = {}
        for g in range(GROUP):
            st = s_scr[g] if g < LOOKAHEAD else ahead.pop(g)
            if g + LOOKAHEAD < GROUP:
                ahead[g + LOOKAHEAD] = scores(j, g + LOOKAHEAD)
            m_prev = m_scr[g]
            m_new = jnp.maximum(m_prev, jnp.max(st, axis=0, keepdims=True))
            p = jnp.exp2(st - m_new).astype(BF16)
            if g + LOOKAHEAD >= GROUP:
                s_scr[g + LOOKAHEAD - GROUP] = scores(jn, g + LOOKAHEAD - GROUP)
            acc_scr[g] = jnp.exp2(m_prev - m_new) * acc_scr[g] + _dot(vt, p)
            m_scr[g] = m_new
        return carry

    lax.fori_loop(0, nk, body, 0)
    outs = []
    for g in range(GROUP):
        acc = acc_scr[g]
        l = acc[HEAD_DIM:HEAD_DIM + 1] + jnp.exp2(sink_ref[h * GROUP + g] - m_scr[g])
        outs.append(acc[:HEAD_DIM] / l)
    o_ref[0] = jnp.concatenate(outs, axis=0).T.astype(o_ref.dtype)


def _flash(sink, q, k, v, *, q_rows, q_off, k_rows, tq, tk):
    B, Hkv = k.shape[:2]
    nq, nk = q_rows // tq, k_rows // tk
    qo = q_off // tq
    return pl.pallas_call(
        functools.partial(_flash_kernel, tk=tk, nk=nk),
        out_shape=jax.ShapeDtypeStruct((B, q_rows, Hkv * GROUP * HEAD_DIM), BF16),
        grid=(B, Hkv, nq),
        in_specs=[
            pl.BlockSpec(memory_space=pltpu.SMEM),
            pl.BlockSpec((1, 1, GROUP, tq, HEAD_DIM), lambda b, h, i: (b, h, 0, i + qo, 0)),
            pl.BlockSpec((1, 1, k_rows, HEAD_DIM), lambda b, h, i: (b, h, 0, 0)),
            pl.BlockSpec((1, 1, LANES, k_rows), lambda b, h, i: (b, h, 0, 0)),
        ],
        out_specs=pl.BlockSpec((1, tq, GROUP * HEAD_DIM), lambda b, h, i: (b, i, h)),
        scratch_shapes=[
            pltpu.VMEM((GROUP, 1, tq), F32),
            pltpu.VMEM((GROUP, LANES, tq), F32),
            pltpu.VMEM((LOOKAHEAD, tk, tq), F32),
        ],
        compiler_params=_cparams(("parallel", "parallel", "arbitrary")),
        name="flash_attn",
    )(sink, q, k, v)


def _window_kernel(sink_ref, q_ref, kc_ref, vc_ref, k0_ref, k1_ref, k2_ref, v0_ref, v1_ref, v2_ref, o_ref, *, nb):
    h = pl.program_id(1)
    i = pl.program_id(2)
    rows = GROUP * WINDOW
    q = q_ref[0, 0].reshape(rows, HEAD_DIM)
    r = lax.broadcasted_iota(jnp.int32, (rows, WINDOW), 0) & (WINDOW - 1)
    c = lax.broadcasted_iota(jnp.int32, (rows, WINDOW), 1)
    sc = _dot_nt(q, kc_ref[0, 0])
    s0 = jnp.where((c >= r) & (i > 0), _dot_nt(q, k0_ref[0, 0]), NEG)
    s1 = _dot_nt(q, k1_ref[0, 0])
    s2 = jnp.where((c <= r) & (i < nb - 1), _dot_nt(q, k2_ref[0, 0]), NEG)
    sink = jnp.concatenate(
        [jnp.full((WINDOW, 1), sink_ref[h * GROUP + g], F32) for g in range(GROUP)], axis=0)
    rowmax = lambda s: jnp.max(s, axis=-1, keepdims=True)
    m = jnp.maximum(jnp.maximum(rowmax(sc), rowmax(s0)), jnp.maximum(rowmax(s1), rowmax(s2)))
    m = jnp.maximum(m, sink)
    pc, p0, p1, p2 = (jnp.exp2(s - m) for s in (sc, s0, s1, s2))
    rowsum = lambda p: jnp.sum(p, axis=-1, keepdims=True)
    l = rowsum(pc) + rowsum(p0) + rowsum(p1) + rowsum(p2) + jnp.exp2(sink - m)
    acc = (_dot(pc.astype(BF16), vc_ref[0, 0]) + _dot(p0.astype(BF16), v0_ref[0, 0])
           + _dot(p1.astype(BF16), v1_ref[0, 0]) + _dot(p2.astype(BF16), v2_ref[0, 0]))
    out = acc / l
    for g in range(GROUP):
        o_ref[0, :, g * HEAD_DIM:(g + 1) * HEAD_DIM] = out[g * WINDOW:(g + 1) * WINDOW].astype(o_ref.dtype)


def _window_attn(sink, q, k, v, L, S):
    B, Hkv = k.shape[:2]
    nb = S // WINDOW
    pad = ((0, 0), (0, 0), (WINDOW, WINDOW), (0, 0))
    kp = jnp.pad(k[:, :, L:], pad)
    vp = jnp.pad(v[:, :, L:], pad)
    qo = L // WINDOW
    band = lambda j: pl.BlockSpec((1, 1, WINDOW, HEAD_DIM), lambda b, h, i: (b, h, i + j, 0))
    ctx = pl.BlockSpec((1, 1, L, HEAD_DIM), lambda b, h, i: (b, h, 0, 0))
    return pl.pallas_call(
        functools.partial(_window_kernel, nb=nb),
        out_shape=jax.ShapeDtypeStruct((B, S, Hkv * GROUP * HEAD_DIM), BF16),
        grid=(B, Hkv, nb),
        in_specs=[
            pl.BlockSpec(memory_space=pltpu.SMEM),
            pl.BlockSpec((1, 1, GROUP, WINDOW, HEAD_DIM), lambda b, h, i: (b, h, 0, i + qo, 0)),
            ctx, ctx, band(0), band(1), band(2), band(0), band(1), band(2),
        ],
        out_specs=pl.BlockSpec((1, WINDOW, GROUP * HEAD_DIM), lambda b, h, i: (b, i, h)),
        compiler_params=_cparams(("parallel", "parallel", "parallel")),
        name="window_attn",
    )(sink, q, k, v, kp, kp, kp, vp, vp, vp)


def _ffn_prep(x, y, mod, gffn, rwh_ref, rwl_ref, xo_ref, h_ref, lg_ref):
    xn = x + mod[2:3] * y
    h = _norm_mod(xn, gffn, mod[3:4], mod[4:5])
    xo_ref[0] = xn
    hh, hl = _split(h)
    h_ref[0] = hh
    rwh = rwh_ref[...]
    lg_ref[0] = _dot(hh, rwh) + _dot(hl, rwh) + _dot(hh, rwl_ref[...])


def _attn_out_kernel(oa_ref, ob_ref, wa_ref, wb_ref, x_ref, mod_ref, g_ref, rwh_ref, rwl_ref,
                     xo_ref, h_ref, lg_ref):
    y = _dot(oa_ref[0], wa_ref[...]) + _dot(ob_ref[0], wb_ref[...])
    _ffn_prep(x_ref[0], y, mod_ref[0, 0], g_ref[...], rwh_ref, rwl_ref, xo_ref, h_ref, lg_ref)


def _row_specs(D):
    row = lambda w: pl.BlockSpec((1, TM, w), lambda b, i: (b, i, 0))
    mod = pl.BlockSpec((1, 1, 6, D), lambda b, i: (b, jnp.minimum(i, 1), 0, 0))
    full = lambda a: pl.BlockSpec(a.shape, lambda b, i: (0,) * a.ndim)
    return row, mod, full


def _ffn_prep_outs(B, T, D):
    row, _, _ = _row_specs(D)
    shapes = (jax.ShapeDtypeStruct((B, T, D), F32), jax.ShapeDtypeStruct((B, T, D), BF16),
              jax.ShapeDtypeStruct((B, T, LANES), F32))
    return shapes, (row(D), row(D), row(LANES))


def _attn_out(oa, ob, wa, wb, x, mods, g, rwh, rwl):
    B, T, D = x.shape
    row, mod, full = _row_specs(D)
    shapes, specs = _ffn_prep_outs(B, T, D)
    return pl.pallas_call(
        _attn_out_kernel,
        out_shape=shapes,
        grid=(B, T // TM),
        in_specs=[row(oa.shape[-1]), row(ob.shape[-1]), full(wa), full(wb), row(D), mod, full(g),
                  full(rwh), full(rwl)],
        out_specs=specs,
        compiler_params=_cparams(("parallel", "parallel")),
        name="attn_out",
    )(oa, ob, wa, wb, x, mods, g, rwh, rwl)


def _gmm_kernel(be_ref, nu_ref, x_ref, w1_ref, w3_ref, w2_ref, o_ref):
    i = pl.program_id(0)

    @pl.when(i < nu_ref[0])
    def _():
        x = x_ref[...]
        a = _dot(x, w1_ref[0, 0].astype(BF16))
        b = _dot(x, w3_ref[0, 0].astype(BF16))
        mid = (a * _sigmoid(a)) * b
        o_ref[...] = _dot(mid.astype(BF16), w2_ref[0, 0].astype(BF16)).astype(o_ref.dtype)

    @pl.when(i >= nu_ref[0])
    def _():
        o_ref[...] = jnp.zeros(o_ref.shape, o_ref.dtype)


def _gmm(block_expert, n_used, xs, w1, w3, w2, layer):
    n_slots, D = xs.shape
    F = w1.shape[-1]
    nblk = n_slots // MOE_ROWS
    return pl.pallas_call(
        _gmm_kernel,
        out_shape=jax.ShapeDtypeStruct((n_slots, D), BF16),
        grid_spec=pltpu.PrefetchScalarGridSpec(
            num_scalar_prefetch=2,
            grid=(nblk,),
            in_specs=[
                pl.BlockSpec((MOE_ROWS, D), lambda i, be, nu: (i, 0)),
                pl.BlockSpec((1, 1, D, F), lambda i, be, nu: (layer, be[i], 0, 0)),
                pl.BlockSpec((1, 1, D, F), lambda i, be, nu: (layer, be[i], 0, 0)),
                pl.BlockSpec((1, 1, F, D), lambda i, be, nu: (layer, be[i], 0, 0)),
            ],
            out_specs=pl.BlockSpec((MOE_ROWS, D), lambda i, be, nu: (i, 0)),
        ),
        compiler_params=_cparams(("arbitrary",)),
        name="moe_gmm",
    )(block_expert, n_used, xs, w1, w3, w2)


ROUTE_ROWS = 512


def _route_kernel(lg_ref, bias_ref, idx_ref, w_ref):
    x = lg_ref[...].T[:N_EXPERTS]
    m = jnp.max(x, axis=0, keepdims=True)
    e = jnp.exp(x - m)
    probs = e / jnp.sum(e, axis=0, keepdims=True)
    sel = probs + bias_ref[...][:, 0:1]
    row = lambda a, i: a[i:i + 1, :]
    G = EXPERTS_PER_GROUP
    scores = []
    for g in range(N_GROUPS):
        s = [row(sel, g * G + i) for i in range(G)]
        best = None
        for i in range(G):
            for j in range(i + 1, G):
                best = s[i] + s[j] if best is None else jnp.maximum(best, s[i] + s[j])
        scores.append(best)
    top = functools.reduce(jnp.maximum, scores)
    gi = jnp.full(top.shape, N_GROUPS - 1, jnp.int32)
    for g in range(N_GROUPS - 2, -1, -1):
        gi = jnp.where(scores[g] == top, g, gi)

    def pick(a, i):
        out = row(a, (N_GROUPS - 1) * G + i)
        for g in range(N_GROUPS - 2, -1, -1):
            out = jnp.where(gi == g, row(a, g * G + i), out)
        return out

    c = [pick(sel, i) for i in range(G)]
    pc = [pick(probs, i) for i in range(G)]

    def first_argmax(vals):
        mx = functools.reduce(jnp.maximum, vals)
        idx = jnp.full(mx.shape, G - 1, jnp.int32)
        for i in range(G - 2, -1, -1):
            idx = jnp.where(vals[i] == mx, i, idx)
        return idx

    i1 = first_argmax(c)
    i2 = first_argmax([jnp.where(i1 == i, -jnp.inf, c[i]) for i in range(G)])
    take = lambda vals, idx: functools.reduce(
        lambda acc, i: jnp.where(idx == i, vals[i], acc), range(G - 2, -1, -1), vals[G - 1])
    w1, w2 = take(pc, i1), take(pc, i2)
    tot = w1 + w2
    zi = jnp.zeros((6,) + top.shape[1:], jnp.int32)
    idx_ref[...] = jnp.concatenate([gi * G + i1, gi * G + i2, zi], axis=0)
    w_ref[...] = jnp.concatenate([w1 / tot, w2 / tot, zi.astype(F32)], axis=0)


def _route(logits, router_bias):
    N = logits.shape[0]
    bias = jnp.broadcast_to(router_bias.astype(F32)[:, None], (N_EXPERTS, LANES))
    idx, w = pl.pallas_call(
        _route_kernel,
        out_shape=(jax.ShapeDtypeStruct((8, N), jnp.int32), jax.ShapeDtypeStruct((8, N), F32)),
        grid=(N // ROUTE_ROWS,),
        in_specs=[pl.BlockSpec((ROUTE_ROWS, LANES), lambda i: (i, 0)),
                  pl.BlockSpec((N_EXPERTS, LANES), lambda i: (0, 0))],
        out_specs=(pl.BlockSpec((8, ROUTE_ROWS), lambda i: (0, i)),
                   pl.BlockSpec((8, ROUTE_ROWS), lambda i: (0, i))),
        compiler_params=_cparams(("parallel",)),
        name="route",
    )(logits, bias)
    return idx[:TOP_K], w[:TOP_K]


def _moe(h, logits, router_bias, w1, w3, w2, layer):
    N, D = h.shape
    expert_idx, gate_w = _route(logits, router_bias)
    NK = N * TOP_K
    flat_e = expert_idx.reshape(NK)
    onehot = (flat_e[None, :] == jnp.arange(N_EXPERTS, dtype=jnp.int32)[:, None]).astype(jnp.int32)
    csum = jnp.cumsum(onehot, axis=1)
    counts = csum[:, -1]
    padded = (counts + MOE_ROWS - 1) // MOE_ROWS * MOE_ROWS
    pad_end = jnp.cumsum(padded)
    pad_start = pad_end - padded
    dest = jnp.sum(onehot * (csum - 1 + pad_start[:, None]), axis=0)
    nblk = -(-NK // MOE_ROWS) + N_EXPERTS
    n_slots = nblk * MOE_ROWS
    n_used = (pad_end[-1] // MOE_ROWS).astype(jnp.int32)
    blk = jnp.arange(nblk, dtype=jnp.int32)
    be = jnp.sum((pad_end[None, :] <= (blk * MOE_ROWS)[:, None]).astype(jnp.int32), axis=1)
    be = jnp.minimum(be, N_EXPERTS - 1)
    be = jnp.where(blk < n_used, be, be[jnp.maximum(n_used - 1, 0)])
    flat_tok = jnp.arange(NK, dtype=jnp.int32) % N
    slot_tok = jnp.zeros((n_slots,), jnp.int32).at[dest].set(flat_tok, unique_indices=True)
    xs = h[slot_tok]
    ys = _gmm(be, n_used.reshape(1), xs, w1, w3, w2, layer)
    return ys[dest[:N]], ys[dest[N:]], gate_w.T


def _combine(x_ref, y0_ref, y1_ref, gw_ref, mod_ref):
    gw = gw_ref[0]
    f = y0_ref[0].astype(F32) * gw[:, 0:1] + y1_ref[0].astype(F32) * gw[:, 1:2]
    return x_ref[0] + mod_ref[0, 0][5:6] * f


def _residual_kernel(x_ref, y0_ref, y1_ref, gw_ref, mod_ref, o_ref):
    o_ref[0] = _combine(x_ref, y0_ref, y1_ref, gw_ref, mod_ref)


def _final_kernel(x_ref, y0_ref, y1_ref, gw_ref, mod_ref, g_ref, o_ref):
    x = _combine(x_ref, y0_ref, y1_ref, gw_ref, mod_ref)
    ms = jnp.mean(x * x, axis=-1, keepdims=True)
    o_ref[0] = x * lax.rsqrt(ms + RMS_EPS) * g_ref[...]


def _residual(x, y0, y1, gw, mods):
    B, T, D = x.shape
    row, mod, _ = _row_specs(D)
    return pl.pallas_call(
        _residual_kernel,
        out_shape=jax.ShapeDtypeStruct((B, T, D), F32),
        grid=(B, T // TM),
        in_specs=[row(D), row(D), row(D), row(TOP_K), mod],
        out_specs=row(D),
        compiler_params=_cparams(("parallel", "parallel")),
        name="residual",
    )(x, y0, y1, gw, mods)


def _final(x, y0, y1, gw, mods, g, L):
    B, S, D = y0.shape
    off = L // TM
    mod = pl.BlockSpec((1, 1, 6, D), lambda b, i: (b, 1, 0, 0))
    lat = lambda w: pl.BlockSpec((1, TM, w), lambda b, i: (b, i, 0))
    return pl.pallas_call(
        _final_kernel,
        out_shape=jax.ShapeDtypeStruct((B, S, D), F32),
        grid=(B, S // TM),
        in_specs=[pl.BlockSpec((1, TM, D), lambda b, i: (b, i + off, 0)), lat(D), lat(D), lat(TOP_K), mod,
                  pl.BlockSpec(g.shape, lambda b, i: (0, 0))],
        out_specs=lat(D),
        compiler_params=_cparams(("parallel", "parallel")),
        name="final_norm",
    )(x, y0, y1, gw, mods, g)


def _rwkv_proj_kernel(x_ref, xp_ref, xn_ref, mod_ref, g_ref, xmix_ref, wr_ref, wk_ref, wv_ref,
                      dw1_ref, dw2_ref, da1_ref, da2_ref, g1_ref, g2_ref, vec_ref, ones_ref,
                      r_ref, v_ref, kk_ref, bv_ref, gate_ref, w0_ref, w1_ref, kd0_ref, kd1_ref, bd0_ref, bd1_ref,
                      *, nt):
    i = pl.program_id(1)
    mod = mod_ref[0, 0]
    g = g_ref[...]
    nm = lambda x: _norm_mod(x, g, mod[0:1], mod[1:2])
    h = nm(x_ref[0])
    hp = nm(xp_ref[0])[7:8] * jnp.where(i >= 2, 1.0, 0.0)
    hn = nm(xn_ref[0])[0:1] * jnp.where((i >= 1) & (i < nt - 1), 1.0, 0.0)
    ridx = lax.broadcasted_iota(jnp.int32, h.shape, 0)
    h_dn = jnp.where(ridx == 0, hp, pltpu.roll(h, 1, axis=0))
    h_up = jnp.where(ridx == TM - 1, hn, pltpu.roll(h, TM - 1, axis=0))
    xx = 0.5 * (h_dn + h_up) - h
    xmix = xmix_ref[...]
    mix = lambda j: (h + xx * xmix[j:j + 1]).astype(BF16)
    vec = vec_ref[...]
    ones = ones_ref[...]

    r = _dot(mix(0), wr_ref[...])
    k = _dot(mix(2), wk_ref[...])
    v = _dot(mix(3), wv_ref[...])
    gate_ref[0] = _dot(_sigmoid(_dot(mix(5), g1_ref[...])).astype(BF16), g2_ref[...])
    kk = k * vec[0:1]
    kk = kk * lax.rsqrt(jnp.maximum(_segsum_wide(kk * kk, ones), 1e-24))
    lw = jnp.tanh(_dot(mix(1), dw1_ref[...])).astype(BF16)
    la = _dot(mix(4), da1_ref[...]).astype(BF16)
    r_ref[0] = r
    v_ref[0] = v
    kk_ref[0] = kk
    bonus = jnp.zeros_like(r)
    lora = DECAY_LORA
    for d, (w_ref, kd_ref, bd_ref) in enumerate(((w0_ref, kd0_ref, bd0_ref), (w1_ref, kd1_ref, bd1_ref))):
        z = -(vec[3 + d:4 + d] + _dot(lw[:, d * lora:(d + 1) * lora], dw2_ref[d]))
        softplus = jnp.maximum(z, 0.0) + jnp.log(1.0 + jnp.exp(-jnp.abs(z)))
        w_ref[0] = jnp.exp(-jnp.exp(-softplus - 0.5))
        iclr = _sigmoid(vec[5 + d:6 + d] + _dot(la[:, d * lora:(d + 1) * lora], da2_ref[d]))
        kd = k * (1.0 + (iclr - 1.0) * vec[1:2])
        kd_ref[0] = kd
        bd_ref[0] = kk * iclr
        bonus = bonus + _segsum_wide(r * kd * vec[2:3], ones)
    bv_ref[0] = bonus * v


DECAY_LORA = 64


def _rwkv_proj(x, mods, g, xmix, wr, wk, wv, dw1, dw2, da1, da2, g1, g2, vec, ones):
    B, T, D = x.shape
    nt = T // TM
    row, mod, full = _row_specs(D)
    r8 = TM // 8
    prev = pl.BlockSpec((1, 8, D), lambda b, i: (b, jnp.maximum(i * r8 - 1, 0), 0))
    nxt = pl.BlockSpec((1, 8, D), lambda b, i: (b, jnp.minimum((i + 1) * r8, T // 8 - 1), 0))
    out = jax.ShapeDtypeStruct((B, T, D), F32)
    return pl.pallas_call(
        functools.partial(_rwkv_proj_kernel, nt=nt),
        out_shape=(out,) * 11,
        grid=(B, nt),
        in_specs=[row(D), prev, nxt, mod, full(g), full(xmix), full(wr), full(wk), full(wv),
                  full(dw1), full(dw2), full(da1), full(da2), full(g1), full(g2), full(vec), full(ones)],
        out_specs=(row(D),) * 11,
        compiler_params=_cparams(("parallel", "parallel")),
        name="rwkv_proj",
    )(x, x, x, mods, g, xmix, wr, wk, wv, dw1, dw2, da1, da2, g1, g2, vec, ones)


CHUNK = 4
N_HEADS = D_MODEL // HEAD_DIM
MAP_LANES = 3 * CHUNK * N_HEADS


def _coef_kernel(r_ref, kk_ref, w_ref, kd_ref, bd_ref, sel_ref, at_ref, rt_ref, bh_ref, kh_ref, gc_ref,
                 cu_ref, cy_ref, *, reverse):
    r, a, w, kd, bd = r_ref[0], -kk_ref[0], w_ref[0], kd_ref[0], bd_ref[0]
    rows = r.shape[0]
    p = lax.broadcasted_iota(jnp.int32, r.shape, 0) & (CHUNK - 1)
    s = (CHUNK - 1 - p) if reverse else p
    back = lambda x, k: pltpu.roll(x, (rows - k) if reverse else k, axis=0)
    ahead = lambda x, k: pltpu.roll(x, k if reverse else (rows - k), axis=0)
    wb = [None] + [back(w, k) for k in range(1, CHUNK)]
    excl = jnp.ones_like(w)
    rest = jnp.ones_like(w)
    for k in range(1, CHUNK):
        excl = excl * jnp.where(s >= k, wb[k], 1.0)
        rest = rest * jnp.where(s + k <= CHUNK - 1, ahead(w, k), 1.0)
    at_ref[0] = a * excl
    rt_ref[0] = r * (excl * w)
    bh_ref[0] = bd * rest
    kh_ref[0] = kd * rest
    gc_ref[0] = excl * w * rest
    between = [None, None, wb[1], wb[1] * wb[2]]
    rw = r * w
    s1 = s[:, :LANES]
    seg = lambda x: _dot(x.astype(BF16), sel_ref[...])

    def by_dist(lead, y, first):
        out = [None] * CHUNK
        for dist in range(first, CHUNK):
            if dist == 0:
                out[0] = seg(r * y)
                continue
            e = back(y, dist) if between[dist] is None else between[dist] * back(y, dist)
            out[dist] = jnp.where(s1 >= dist, seg(lead * e), 0.0)
        return out

    lab, lak = by_dist(a, bd, 1), by_dist(a, kd, 1)
    rb, rk = by_dist(rw, bd, 0), by_dist(rw, kd, 0)
    one = jnp.ones_like(rb[0])
    bk = lambda x, k: x if k == 0 else pltpu.roll(x, (rows - k) if reverse else k, axis=0)
    md = [one]
    for dist in range(1, CHUNK):
        md.append(sum(lab[e] * (bk(md[dist - e], e) if dist - e else 1.0) for e in range(1, dist + 1)))
    gd = [None] + [sum((md[e] if e else 1.0) * bk(lak[dist - e], e) for e in range(dist)) for dist in range(1, CHUNK)]
    yzd = [sum(rb[e] * (bk(md[dist - e], e) if dist - e else 1.0) for e in range(dist + 1)) for dist in range(CHUNK)]
    yvd = [rk[dist] + sum(rb[e] * bk(gd[dist - e], e) for e in range(dist)) for dist in range(CHUNK)]

    def at_pos(table, j, first):
        out = jnp.zeros_like(one)
        for dist in range(first, CHUNK - j):
            out = jnp.where(s1 == j + dist, table[dist], out)
        return out

    zero = jnp.zeros_like(one)
    u_blocks = ([at_pos(md, j, 0) for j in range(CHUNK)] + [at_pos(gd, j, 1) for j in range(CHUNK)]
                + [zero] * CHUNK)
    y_blocks = ([at_pos(yzd, j, 0) for j in range(CHUNK)] + [at_pos(yvd, j, 0) for j in range(CHUNK)]
                + [jnp.where(s1 == j, 1.0, 0.0) for j in range(CHUNK)])
    lane_blk = lax.broadcasted_iota(jnp.int32, one.shape, 1) // N_HEADS

    def place(blocks):
        per_tile = LANES // N_HEADS
        tiles = []
        for t in range(2):
            acc = zero
            for i in range(t * per_tile, min((t + 1) * per_tile, len(blocks))):
                acc = jnp.where(lane_blk == i - t * per_tile, blocks[i], acc)
            tiles.append(acc)
        return jnp.concatenate(tiles, axis=1)

    cu_ref[0] = place(u_blocks)
    cy_ref[0] = place(y_blocks)


def _rwkv_coef(r, kk, w, kd, bd, sel, reverse):
    B, T, D = r.shape
    row, _, full = _row_specs(D)
    out = jax.ShapeDtypeStruct((B, T, D), F32)
    maps = jax.ShapeDtypeStruct((B, T, 2 * LANES), F32)
    return pl.pallas_call(
        functools.partial(_coef_kernel, reverse=reverse),
        out_shape=(out,) * 5 + (maps, maps),
        grid=(B, T // TM),
        in_specs=[row(D)] * 5 + [full(sel)],
        out_specs=(row(D),) * 5 + (row(2 * LANES), row(2 * LANES)),
        compiler_params=_cparams(("parallel", "parallel")),
        name="rwkv_coef",
    )(r, kk, w, kd, bd, sel)


def _scan_kernel(atf, atb, rtf, rtb, vf, vb, bhf, bhb, khf, khb, gcf, gcb, cuf, cub, cyf, cyb,
                 mask_ref, eye_ref, e16_ref, yf, yb, st, *, tc, nb):
    n = pl.program_id(0)

    @pl.when(n == 0)
    def _():
        st[...] = jnp.zeros(st.shape, F32)

    N = HEAD_DIM
    W = CHUNK * N_HEADS
    nch = tc // CHUNK
    dirs = ((atf, rtf, vf, bhf, khf, gcf, cuf, cyf, yf), (atb, rtb, vb, bhb, khb, gcb, cub, cyb, yb))
    lane1 = lax.broadcasted_iota(jnp.int32, (N, 3 * W), 1)
    lane2 = lax.broadcasted_iota(jnp.int32, (N, 2 * W), 1)
    spread = lambda x: (mask_ref[...] * x).astype(BF16)

    def chunk(ci, carry):
        work = []
        for d, refs in enumerate(dirs):
            cc = ci if d == 0 else nch - 1 - ci
            rows = [pl.ds(cc * CHUNK + (s if d == 0 else CHUNK - 1 - s), 1) for s in range(CHUNK)]
            for b in range(nb):
                work.append((d * nb + b, b, cc, rows, refs))
        firsts = []
        for gi, b, cc, rows, (AT, RT, V, BH, KH, GC, CU, CY, Y) in work:
            lhs = jnp.concatenate([st[gi].astype(BF16), eye_ref[...]], axis=0)
            w1 = jnp.concatenate([spread(X[b, rw, :]) for X in (AT, V, RT) for rw in rows], axis=0)
            firsts.append(_dot_nt(lhs, w1))
        mids = []
        for (gi, b, cc, rows, (AT, RT, V, BH, KH, GC, CU, CY, Y)), out in zip(work, firsts):
            zvq = jnp.where((lane1 >= W) & (lane1 < 2 * W), out[N:], out[:N])
            zvq16 = zvq.astype(BF16)
            maprows = lambda M: [e16_ref[...] * M[b, rw, :MAP_LANES] for rw in rows]
            wu = jnp.concatenate(maprows(CU) + [jnp.zeros((W, MAP_LANES), F32)], axis=0)
            u = _dot_nt(zvq16, wu.astype(BF16))
            wy = jnp.concatenate(maprows(CY), axis=0)
            yt = _dot_nt(wy.astype(BF16), zvq16)
            for s, rw in enumerate(rows):
                Y[b, rw] = yt[s * N_HEADS:(s + 1) * N_HEADS][None]
            mids.append((zvq, u))
        for (gi, b, cc, rows, (AT, RT, V, BH, KH, GC, CU, CY, Y)), (zvq, u) in zip(work, mids):
            uv = jnp.where(lane2 < W, u, zvq[:, :2 * W]).astype(BF16)
            w2 = jnp.concatenate([spread(X[b, rw, :]) for X in (BH, KH) for rw in rows], axis=0)
            st[gi] = st[gi] * GC[b, rows[0], :] + _dot(uv, w2)
        return carry

    lax.fori_loop(0, nch, chunk, 0)


def _rwkv_scan(ins_f, ins_b, v, mask, eye, e16, L):
    B, T, D = v.shape
    tc = SCAN_CHUNK
    nch = tc // CHUNK
    nc, nchunks = L // tc, T // tc
    fwd_idx = lambda n: n
    rev_idx = lambda n: jnp.where(n < nc, nc - 1 - n, nchunks - 1 - (n - nc))
    tok = lambda idx, w: pl.BlockSpec((B, tc, w), lambda n: (0, idx(n), 0))
    ys = lambda idx: pl.BlockSpec((B, tc, N_HEADS, HEAD_DIM), lambda n: (0, idx(n), 0, 0))
    full = lambda a: pl.BlockSpec(a.shape, lambda n: (0,) * a.ndim)
    out = jax.ShapeDtypeStruct((B, T, N_HEADS, HEAD_DIM), F32)
    atf, rtf, bhf, khf, gcf, cuf, cyf = ins_f
    atb, rtb, bhb, khb, gcb, cub, cyb = ins_b
    f, r_ = tok(fwd_idx, D), tok(rev_idx, D)
    wy = cyf.shape[-1]
    return pl.pallas_call(
        functools.partial(_scan_kernel, tc=tc, nb=B),
        out_shape=(out, out),
        grid=(nchunks,),
        in_specs=[f, r_, f, r_, f, r_, f, r_, f, r_, f, r_, tok(fwd_idx, wy), tok(rev_idx, wy),
                  tok(fwd_idx, wy), tok(rev_idx, wy), full(mask), full(eye), full(e16)],
        out_specs=(ys(fwd_idx), ys(rev_idx)),
        scratch_shapes=[pltpu.VMEM((2 * B, HEAD_DIM, D), F32)],
        compiler_params=_cparams(("arbitrary",)),
        name="rwkv_scan",
    )(atf, atb, rtf, rtb, v, v, bhf, bhb, khf, khb, gcf, gcb, cuf, cub, cyf, cyb, mask, eye, e16)


def _rwkv_out_kernel(yf_ref, yb_ref, bv_ref, gate_ref, ln_ref, wo_ref, ones_ref, x_ref, mod_ref, g_ref,
                     rwh_ref, rwl_ref, xo_ref, h_ref, lg_ref):
    ones = ones_ref[...]
    y = yf_ref[0] + yb_ref[0]
    inv = 1.0 / HEAD_DIM
    dlt = y - _segsum_wide(y, ones) * inv
    yn = dlt * lax.rsqrt(_segsum_wide(dlt * dlt, ones) * inv + GN_EPS)
    ln = ln_ref[...]
    o = (yn * ln[0:1] + ln[1:2] + bv_ref[0]) * gate_ref[0]
    yl = _dot(o.astype(BF16), wo_ref[...])
    _ffn_prep(x_ref[0], yl, mod_ref[0, 0], g_ref[...], rwh_ref, rwl_ref, xo_ref, h_ref, lg_ref)


def _rwkv_out(yf, yb, bv, gate, ln, wo, ones, x, mods, g, rwh, rwl):
    B, T, D = x.shape
    row, mod, full = _row_specs(D)
    shapes, specs = _ffn_prep_outs(B, T, D)
    return pl.pallas_call(
        _rwkv_out_kernel,
        out_shape=shapes,
        grid=(B, T // TM),
        in_specs=[row(D), row(D), row(D), row(D), full(ln), full(wo), full(ones), row(D), mod, full(g),
                  full(rwh), full(rwl)],
        out_specs=specs,
        compiler_params=_cparams(("parallel", "parallel")),
        name="rwkv_out",
    )(yf, yb, bv, gate, ln, wo, ones, x, mods, g, rwh, rwl)


def _rope_tables(S, L):
    rows = S // GRID_W
    row = jnp.repeat(jnp.arange(rows, dtype=F32), GRID_W)
    col = (jnp.arange(rows * GRID_W) % GRID_W).astype(F32)
    n_freq = HEAD_DIM // 4
    inv = ROPE_THETA ** (-jnp.arange(n_freq, dtype=F32) / n_freq)
    lane = np.arange(LANES) % HEAD_DIM
    axis, half, freq = lane // 32, (lane % 32) // 16, lane % 16
    pos = jnp.where(jnp.asarray(axis == 0)[None, :], row[:, None], col[:, None])
    ang = pos * inv[freq][None, :]
    sgn = jnp.asarray(np.where(half == 0, -1.0, 1.0), dtype=F32)
    cos = jnp.concatenate([jnp.ones((L, LANES), F32), jnp.cos(ang)], axis=0)
    sin = jnp.concatenate([jnp.zeros((L, LANES), F32), jnp.sin(ang) * sgn[None, :]], axis=0)
    return cos, sin


def kernel(x, c, ctx, c_ctx, ada_w, ada_b, norm_mix_g, norm_ffn_g, attn_w_in, attn_w_out, attn_sink,
           attn_q_norm_g, attn_k_norm_g, rwkv_x_mix, rwkv_w_r, rwkv_w_k, rwkv_w_v, rwkv_w_o,
           rwkv_decay_w0, rwkv_decay_w1, rwkv_decay_w2, rwkv_iclr_a0, rwkv_iclr_a1, rwkv_iclr_a2,
           rwkv_gate_g1, rwkv_gate_g2, rwkv_k_k, rwkv_k_a, rwkv_r_k, rwkv_ln_g, rwkv_ln_b,
           router_w, router_bias, moe_w1, moe_w3, moe_w2, final_norm_g):
    B, S, D = x.shape
    L = ctx.shape[1]
    T = L + S
    depth = ada_w.shape[0]
    assert D == D_MODEL and L == TM and S % TM == 0 and B == 2 and depth == 2
    ones = _seg_ones()
    bf = lambda a: a.astype(BF16)

    cs = jnp.zeros((8, D), F32).at[:B].set(c).at[B].set(c_ctx)
    ada = _ada(cs, ada_w, ada_b).reshape(depth, 8, 6, D)
    mods = [jnp.stack([jnp.broadcast_to(ada[i, B], (B, 6, D)), ada[i, :B]], axis=1) for i in range(depth)]

    xa = jnp.concatenate([ctx, x], axis=1)
    rw = jnp.zeros((D, LANES), F32).at[:, :N_EXPERTS].set(router_w)
    rwh, rwl = _split(rw)

    w_in = attn_w_in[0]
    roped = np.concatenate([np.arange(0, 640), np.arange(768, 1408)])
    w_rot = w_in[:, roped ^ 16]
    cos, sin = _rope_tables(S, L)
    lane = np.arange(LANES) % HEAD_DIM
    gains = lambda g: jnp.stack([g[lane], g[lane ^ 16]], axis=0)
    qa, ka, va, qb, kb, vb = _inproj(xa, mods[0], norm_mix_g[0].reshape(1, D), bf(w_in), bf(w_rot), cos, sin,
                                     gains(attn_q_norm_g[0]), gains(attn_k_norm_g[0]), ones)
    grouped = lambda q: q.reshape(B, A_KV_HEADS, GROUP, T, HEAD_DIM)
    qa, qb = grouped(qa), grouped(qb)
    sink = attn_sink[0].astype(F32) * LOG2E
    ext = lambda v: jnp.swapaxes(
        jnp.concatenate([v, jnp.ones_like(v[..., :1]), jnp.zeros_like(v[..., :HEAD_DIM - 1])], axis=-1), 2, 3)
    va_x, vb_x = ext(va), ext(vb)
    nosink = jnp.full((B_Q_HEADS,), NEG, F32)
    oa_l = _window_attn(sink, qa, ka, va, L, S)
    oa_c = _flash(sink, qa, ka, va_x, q_rows=L, q_off=0, k_rows=L, tq=L, tk=L)
    ob_l = _flash(nosink, qb, kb, vb_x, q_rows=S, q_off=L, k_rows=T, tq=256, tk=_key_tile(T))
    ob_c = _flash(nosink, qb, kb, vb_x, q_rows=L, q_off=0, k_rows=L, tq=L, tk=L)
    oa = jnp.concatenate([oa_c, oa_l], axis=1)
    ob = jnp.concatenate([ob_c, ob_l], axis=1)
    w_out = bf(attn_w_out[0])
    na = A_Q_HEADS * HEAD_DIM
    xa, h, lg = _attn_out(oa, ob, w_out[:na], w_out[na:], xa, mods[0], norm_ffn_g[0].reshape(1, D), rwh, rwl)
    y0, y1, gw = _moe(h.reshape(B * T, D), lg.reshape(B * T, LANES), router_bias,
                      moe_w1, moe_w3, moe_w2, 0)
    xa = _residual(xa, y0.reshape(B, T, D), y1.reshape(B, T, D), gw.reshape(B, T, TOP_K), mods[0])

    cat2 = lambda a: jnp.concatenate([a[0], a[1]], axis=1)
    vec = jnp.stack([rwkv_k_k[0], rwkv_k_a[0], rwkv_r_k[0].reshape(D), rwkv_decay_w0[0, 0], rwkv_decay_w0[0, 1],
                     rwkv_iclr_a0[0, 0], rwkv_iclr_a0[0, 1], jnp.zeros((D,), F32)], axis=0)
    outs = _rwkv_proj(xa, mods[1], norm_mix_g[1].reshape(1, D), jnp.pad(rwkv_x_mix[0], ((0, 2), (0, 0))),
                      bf(rwkv_w_r[0]), bf(rwkv_w_k[0]), bf(rwkv_w_v[0]),
                      bf(cat2(rwkv_decay_w1[0])), bf(rwkv_decay_w2[0]),
                      bf(cat2(rwkv_iclr_a1[0])), bf(rwkv_iclr_a2[0]),
                      bf(rwkv_gate_g1[0]), bf(rwkv_gate_g2[0]), vec, ones)
    r, v, kk, bv, gate, w0, w1, kd0, kd1, bd0, bd1 = outs
    lane_id = np.arange(D)
    eye = jnp.asarray(np.arange(HEAD_DIM)[:, None] == (lane_id % HEAD_DIM)[None, :], dtype=BF16)
    n_heads = D // HEAD_DIM
    head_mask = jnp.asarray(np.arange(n_heads)[:, None] == (lane_id // HEAD_DIM)[None, :], dtype=F32)
    sel = jnp.asarray((lane_id // HEAD_DIM)[:, None] == (np.arange(LANES) % n_heads)[None, :], dtype=BF16)
    e16 = jnp.asarray(np.arange(n_heads)[:, None] == (np.arange(MAP_LANES) % n_heads)[None, :], dtype=F32)
    scan_ins = [_rwkv_coef(r, kk, w_d, kd_d, bd_d, sel, reverse=d == 1)
                for d, (w_d, kd_d, bd_d) in enumerate(((w0, kd0, bd0), (w1, kd1, bd1)))]
    yf, yb = _rwkv_scan(scan_ins[0], scan_ins[1], v, head_mask, eye, e16, L)
    ln = jnp.stack([rwkv_ln_g[0], rwkv_ln_b[0]] + [jnp.zeros((D,), F32)] * 6, axis=0)
    xa, h, lg = _rwkv_out(yf.reshape(B, T, D), yb.reshape(B, T, D), bv, gate, ln, bf(rwkv_w_o[0]), ones, xa, mods[1],
                          norm_ffn_g[1].reshape(1, D), rwh, rwl)
    y0, y1, gw = _moe(h[:, L:].reshape(B * S, D), lg[:, L:].reshape(B * S, LANES), router_bias,
                      moe_w1, moe_w3, moe_w2, 1)
    return _final(xa, y0.reshape(B, S, D), y1.reshape(B, S, D), gw.reshape(B, S, TOP_K), mods[1],
                  final_norm_g.reshape(1, D), L)


def _key_tile(T):
    for tk in (1280, 1024, 768, 512, 256):
        if T % tk == 0:
            return tk
    raise ValueError(T)
```

```python
import functools

import numpy as np
import jax
import jax.numpy as jnp
from jax import lax
from jax.experimental import pallas as pl
from jax.experimental.pallas import tpu as pltpu

F32 = jnp.float32
BF16 = jnp.bfloat16

D_MODEL = 1024
HEAD_DIM = 64
GRID_W = 64
ROPE_THETA = 10000.0
RMS_EPS = 1e-6
GN_EPS = 64e-5
A_Q_HEADS = 8
A_KV_HEADS = 2
B_Q_HEADS = 8
B_KV_HEADS = 2
GROUP = 4
WINDOW = 128
N_EXPERTS = 16
N_GROUPS = 4
EXPERTS_PER_GROUP = 4
TOP_K = 2
LANES = 128
TM = 256
MOE_ROWS = 512
SCAN_CHUNK = 64
VMEM_LIMIT = 56 * 1024 * 1024
NEG = -1e30
LOG2E = 1.4426950408889634


def _cparams(sem):
    return pltpu.CompilerParams(dimension_semantics=sem, vmem_limit_bytes=VMEM_LIMIT)


def _dot(a, b):
    return jnp.dot(a, b, preferred_element_type=F32)


def _dot_nt(a, b):
    return lax.dot_general(a, b, (((1,), (1,)), ((), ())), preferred_element_type=F32)


def _split(x):
    hi = x.astype(BF16)
    lo = (x - hi.astype(F32)).astype(BF16)
    return hi, lo


def _dot3(x, w):
    xh, xl = _split(x)
    wh, wl = _split(w)
    return _dot(xh, wh) + _dot(xh, wl) + _dot(xl, wh)


def _segsum(v, ones):
    hi, lo = _split(v)
    return _dot(hi, ones) + _dot(lo, ones)


def _segsum_wide(v, ones):
    n = v.shape[1] // LANES
    return jnp.concatenate([_segsum(v[:, j * LANES:(j + 1) * LANES], ones) for j in range(n)], axis=1)


def _norm_mod(x, g, shift, scale):
    ms = jnp.mean(x * x, axis=-1, keepdims=True)
    return (x * lax.rsqrt(ms + RMS_EPS) * g) * (1.0 + scale) + shift


def _sigmoid(x):
    return 1.0 / (1.0 + jnp.exp(-x))


def _seg_ones():
    i = np.arange(LANES)
    return jnp.asarray((i[:, None] // HEAD_DIM) == (i[None, :] // HEAD_DIM), dtype=BF16)


def _ada_kernel(c_ref, w_ref, b_ref, o_ref):
    c = c_ref[...]
    s = c * _sigmoid(c)
    o_ref[0] = _dot3(s, w_ref[0]) + b_ref[0]


def _ada(cs, ada_w, ada_b):
    depth, d, n = ada_w.shape
    tn = 1536
    return pl.pallas_call(
        _ada_kernel,
        out_shape=jax.ShapeDtypeStruct((depth, 8, n), F32),
        grid=(depth, n // tn),
        in_specs=[
            pl.BlockSpec((8, d), lambda l, j: (0, 0)),
            pl.BlockSpec((1, d, tn), lambda l, j: (l, 0, j)),
            pl.BlockSpec((1, 1, tn), lambda l, j: (l, 0, j)),
        ],
        out_specs=pl.BlockSpec((1, 8, tn), lambda l, j: (l, 0, j)),
        compiler_params=_cparams(("arbitrary", "arbitrary")),
        name="ada",
    )(cs, ada_w, ada_b.reshape(depth, 1, n))


def _inproj_kernel(x_ref, mod_ref, g_ref, w_ref, wrot_ref, cos_ref, sin_ref, gq_ref, gk_ref, ones_ref,
                   qa_ref, ka_ref, va_ref, qb_ref, kb_ref, vb_ref):
    mod = mod_ref[0, 0]
    h = _norm_mod(x_ref[0], g_ref[...], mod[0:1], mod[1:2]).astype(BF16)
    y = _dot(h, w_ref[...])
    yr = _dot(h, wrot_ref[...])
    cos = cos_ref[...]
    sin = sin_ref[...]
    ones = ones_ref[...]
    qscale = HEAD_DIM ** -0.5 * LOG2E

    def put(ref, tile, val):
        ref[0, 2 * tile] = val[:, :HEAD_DIM].astype(ref.dtype)
        ref[0, 2 * tile + 1] = val[:, HEAD_DIM:].astype(ref.dtype)

    def chunk(a, c):
        return a[:, c * LANES:(c + 1) * LANES]

    for c in range(4):
        put(qa_ref, c, (chunk(y, c) * cos + chunk(yr, c) * sin) * qscale)
    put(ka_ref, 0, chunk(y, 4) * cos + chunk(yr, 4) * sin)
    put(va_ref, 0, chunk(y, 5))

    def normed(c, cr, gain_ref):
        v = chunk(y, c)
        rs = lax.rsqrt(_segsum(v * v, ones) * (1.0 / HEAD_DIM) + RMS_EPS)
        return (v * rs * gain_ref[0:1]) * cos + (chunk(yr, cr) * rs * gain_ref[1:2]) * sin

    for c in range(4):
        put(qb_ref, c, normed(6 + c, 5 + c, gq_ref) * qscale)
    put(kb_ref, 0, normed(10, 9, gk_ref))
    vt = chunk(y, 11).T
    tail = jnp.where(lax.broadcasted_iota(jnp.int32, (VT_ROWS - HEAD_DIM, vt.shape[1]), 0) == 0, 1.0, 0.0)
    for hh in range(B_KV_HEADS):
        tile = jnp.concatenate([vt[hh * HEAD_DIM:(hh + 1) * HEAD_DIM], tail], axis=0)
        vb_ref[0, hh] = tile.astype(vb_ref.dtype)


def _inproj(x, mods, g, w_in, w_rot, cos, sin, gq2, gk2, ones):
    B, T, D = x.shape
    nt = T // TM
    heads = lambda n: jax.ShapeDtypeStruct((B, n, T, HEAD_DIM), BF16)
    hspec = lambda n: pl.BlockSpec((1, n, TM, HEAD_DIM), lambda b, i: (b, 0, i, 0))
    full = lambda a: pl.BlockSpec(a.shape, lambda b, i: (0,) * a.ndim)
    vt_shape = jax.ShapeDtypeStruct((B, B_KV_HEADS, VT_ROWS, T), BF16)
    vt_spec = pl.BlockSpec((1, B_KV_HEADS, VT_ROWS, TM), lambda b, i: (b, 0, 0, i))
    return pl.pallas_call(
        _inproj_kernel,
        out_shape=(heads(8), heads(2), heads(2), heads(8), heads(2), vt_shape),
        grid=(B, nt),
        in_specs=[
            pl.BlockSpec((1, TM, D), lambda b, i: (b, i, 0)),
            pl.BlockSpec((1, 1, 6, D), lambda b, i: (b, jnp.minimum(i, 1), 0, 0)),
            full(g), full(w_in), full(w_rot),
            pl.BlockSpec((TM, LANES), lambda b, i: (i, 0)),
            pl.BlockSpec((TM, LANES), lambda b, i: (i, 0)),
            full(gq2), full(gk2), full(ones),
        ],
        out_specs=(hspec(8), hspec(2), hspec(2), hspec(8), hspec(2), vt_spec),
        compiler_params=_cparams(("parallel", "parallel")),
        name="attn_inproj",
    )(x, mods, g, w_in, w_rot, cos, sin, gq2, gk2, ones)


LOOKAHEAD = 3
VT_ROWS = 80


def _flash_kernel(sink_ref, q_ref, k_ref, v_ref, o_ref, m_scr, acc_scr, s_scr, *, tk, nk):
    h = pl.program_id(1)
    m_scr[...] = jnp.full(m_scr.shape, NEG, F32)
    acc_scr[...] = jnp.zeros(acc_scr.shape, F32)

    def scores(j, g):
        return _dot_nt(k_ref[0, 0, pl.ds(pl.multiple_of(j * tk, tk), tk), :], q_ref[0, 0, g])

    for g in range(LOOKAHEAD):
        s_scr[g] = scores(0, g)

    def body(j, carry):
        vt = v_ref[0, 0, :, pl.ds(pl.multiple_of(j * tk, tk), tk)]
        jn = jnp.minimum(j + 1, nk - 1)
        ahead = {}
        for g in range(GROUP):
            st = s_scr[g] if g < LOOKAHEAD else ahead.pop(g)
            if g + LOOKAHEAD < GROUP:
                ahead[g + LOOKAHEAD] = scores(j, g + LOOKAHEAD)
            m_prev = m_scr[g]
            m_new = jnp.maximum(m_prev, jnp.max(st, axis=0, keepdims=True))
            p = jnp.exp2(st - m_new).astype(BF16)
            if g + LOOKAHEAD >= GROUP:
                s_scr[g + LOOKAHEAD - GROUP] = scores(jn, g + LOOKAHEAD - GROUP)
            acc_scr[g] = jnp.exp2(m_prev - m_new) * acc_scr[g] + _dot(vt, p)
            m_scr[g] = m_new
        return carry

    lax.fori_loop(0, nk, body, 0)
    outs = []
    for g in range(GROUP):
        acc = acc_scr[g]
        l = acc[HEAD_DIM:HEAD_DIM + 1] + jnp.exp2(sink_ref[h * GROUP + g] - m_scr[g])
        outs.append(acc[:HEAD_DIM] / l)
    o_ref[0] = jnp.concatenate(outs, axis=0).T.astype(o_ref.dtype)


def _flash(sink, q, k, v, *, q_rows, q_off, k_rows, tq, tk):
    B, Hkv = k.shape[:2]
    nq, nk = q_rows // tq, k_rows // tk
    qo = q_off // tq
    return pl.pallas_call(
        functools.partial(_flash_kernel, tk=tk, nk=nk),
        out_shape=jax.ShapeDtypeStruct((B, q_rows, Hkv * GROUP * HEAD_DIM), BF16),
        grid=(B, Hkv, nq),
        in_specs=[
            pl.BlockSpec(memory_space=pltpu.SMEM),
            pl.BlockSpec((1, 1, GROUP, tq, HEAD_DIM), lambda b, h, i: (b, h, 0, i + qo, 0)),
            pl.BlockSpec((1, 1, k_rows, HEAD_DIM), lambda b, h, i: (b, h, 0, 0)),
            pl.BlockSpec((1, 1, VT_ROWS, k_rows), lambda b, h, i: (b, h, 0, 0)),
        ],
        out_specs=pl.BlockSpec((1, tq, GROUP * HEAD_DIM), lambda b, h, i: (b, i, h)),
        scratch_shapes=[
            pltpu.VMEM((GROUP, 1, tq), F32),
            pltpu.VMEM((GROUP, VT_ROWS, tq), F32),
            pltpu.VMEM((LOOKAHEAD, tk, tq), F32),
        ],
        compiler_params=_cparams(("parallel", "parallel", "arbitrary")),
        name="flash_attn",
    )(sink, q, k, v)


def _window_kernel(sink_ref, q_ref, kc_ref, vc_ref, k0_ref, k1_ref, k2_ref, v0_ref, v1_ref, v2_ref, o_ref, *, nb):
    h = pl.program_id(1)
    i = pl.program_id(2)
    rows = GROUP * WINDOW
    q = q_ref[0, 0].reshape(rows, HEAD_DIM)
    r = lax.broadcasted_iota(jnp.int32, (rows, WINDOW), 0) & (WINDOW - 1)
    c = lax.broadcasted_iota(jnp.int32, (rows, WINDOW), 1)
    sc = _dot_nt(q, kc_ref[0, 0])
    s0 = jnp.where((c >= r) & (i > 0), _dot_nt(q, k0_ref[0, 0]), NEG)
    s1 = _dot_nt(q, k1_ref[0, 0])
    s2 = jnp.where((c <= r) & (i < nb - 1), _dot_nt(q, k2_ref[0, 0]), NEG)
    sink = jnp.concatenate(
        [jnp.full((WINDOW, 1), sink_ref[h * GROUP + g], F32) for g in range(GROUP)], axis=0)
    rowmax = lambda s: jnp.max(s, axis=-1, keepdims=True)
    m = jnp.maximum(jnp.maximum(rowmax(sc), rowmax(s0)), jnp.maximum(rowmax(s1), rowmax(s2)))
    m = jnp.maximum(m, sink)
    pc, p0, p1, p2 = (jnp.exp2(s - m) for s in (sc, s0, s1, s2))
    rowsum = lambda p: jnp.sum(p, axis=-1, keepdims=True)
    l = rowsum(pc) + rowsum(p0) + rowsum(p1) + rowsum(p2) + jnp.exp2(sink - m)
    acc = (_dot(pc.astype(BF16), vc_ref[0, 0]) + _dot(p0.astype(BF16), v0_ref[0, 0])
           + _dot(p1.astype(BF16), v1_ref[0, 0]) + _dot(p2.astype(BF16), v2_ref[0, 0]))
    out = acc / l
    for g in range(GROUP):
        o_ref[0, :, g * HEAD_DIM:(g + 1) * HEAD_DIM] = out[g * WINDOW:(g + 1) * WINDOW].astype(o_ref.dtype)


def _window_attn(sink, q, k, v, L, S):
    B, Hkv = k.shape[:2]
    nb = S // WINDOW
    pad = ((0, 0), (0, 0), (WINDOW, WINDOW), (0, 0))
    kp = jnp.pad(k[:, :, L:], pad)
    vp = jnp.pad(v[:, :, L:], pad)
    qo = L // WINDOW
    band = lambda j: pl.BlockSpec((1, 1, WINDOW, HEAD_DIM), lambda b, h, i: (b, h, i + j, 0))
    ctx = pl.BlockSpec((1, 1, L, HEAD_DIM), lambda b, h, i: (b, h, 0, 0))
    return pl.pallas_call(
        functools.partial(_window_kernel, nb=nb),
        out_shape=jax.ShapeDtypeStruct((B, S, Hkv * GROUP * HEAD_DIM), BF16),
        grid=(B, Hkv, nb),
        in_specs=[
            pl.BlockSpec(memory_space=pltpu.SMEM),
            pl.BlockSpec((1, 1, GROUP, WINDOW, HEAD_DIM), lambda b, h, i: (b, h, 0, i + qo, 0)),
            ctx, ctx, band(0), band(1), band(2), band(0), band(1), band(2),
        ],
        out_specs=pl.BlockSpec((1, WINDOW, GROUP * HEAD_DIM), lambda b, h, i: (b, i, h)),
        compiler_params=_cparams(("parallel", "parallel", "parallel")),
        name="window_attn",
    )(sink, q, k, v, kp, kp, kp, vp, vp, vp)


def _ffn_prep(x, y, mod, gffn, rwh_ref, rwl_ref, xo_ref, h_ref, lg_ref):
    xn = x + mod[2:3] * y
    h = _norm_mod(xn, gffn, mod[3:4], mod[4:5])
    xo_ref[0] = xn
    hh, hl = _split(h)
    h_ref[0] = hh
    rwh = rwh_ref[...]
    lg_ref[0] = _dot(hh, rwh) + _dot(hl, rwh) + _dot(hh, rwl_ref[...])


def _attn_out_kernel(oa_ref, ob_ref, wa_ref, wb_ref, x_ref, mod_ref, g_ref, rwh_ref, rwl_ref,
                     xo_ref, h_ref, lg_ref):
    y = _dot(oa_ref[0], wa_ref[...]) + _dot(ob_ref[0], wb_ref[...])
    _ffn_prep(x_ref[0], y, mod_ref[0, 0], g_ref[...], rwh_ref, rwl_ref, xo_ref, h_ref, lg_ref)


def _row_specs(D):
    row = lambda w: pl.BlockSpec((1, TM, w), lambda b, i: (b, i, 0))
    mod = pl.BlockSpec((1, 1, 6, D), lambda b, i: (b, jnp.minimum(i, 1), 0, 0))
    full = lambda a: pl.BlockSpec(a.shape, lambda b, i: (0,) * a.ndim)
    return row, mod, full


def _ffn_prep_outs(B, T, D):
    row, _, _ = _row_specs(D)
    shapes = (jax.ShapeDtypeStruct((B, T, D), F32), jax.ShapeDtypeStruct((B, T, D), BF16),
              jax.ShapeDtypeStruct((B, T, LANES), F32))
    return shapes, (row(D), row(D), row(LANES))


def _attn_out(oa, ob, wa, wb, x, mods, g, rwh, rwl):
    B, T, D = x.shape
    row, mod, full = _row_specs(D)
    shapes, specs = _ffn_prep_outs(B, T, D)
    return pl.pallas_call(
        _attn_out_kernel,
        out_shape=shapes,
        grid=(B, T // TM),
        in_specs=[row(oa.shape[-1]), row(ob.shape[-1]), full(wa), full(wb), row(D), mod, full(g),
                  full(rwh), full(rwl)],
        out_specs=specs,
        compiler_params=_cparams(("parallel", "parallel")),
        name="attn_out",
    )(oa, ob, wa, wb, x, mods, g, rwh, rwl)


def _gmm_kernel(be_ref, nu_ref, x_ref, w1_ref, w3_ref, w2_ref, o_ref):
    i = pl.program_id(0)

    @pl.when(i < nu_ref[0])
    def _():
        x = x_ref[...]
        a = _dot(x, w1_ref[0, 0].astype(BF16))
        b = _dot(x, w3_ref[0, 0].astype(BF16))
        mid = (a * _sigmoid(a)) * b
        o_ref[...] = _dot(mid.astype(BF16), w2_ref[0, 0].astype(BF16)).astype(o_ref.dtype)

    @pl.when(i >= nu_ref[0])
    def _():
        o_ref[...] = jnp.zeros(o_ref.shape, o_ref.dtype)


def _gmm(block_expert, n_used, xs, w1, w3, w2, layer):
    n_slots, D = xs.shape
    F = w1.shape[-1]
    nblk = n_slots // MOE_ROWS
    return pl.pallas_call(
        _gmm_kernel,
        out_shape=jax.ShapeDtypeStruct((n_slots, D), BF16),
        grid_spec=pltpu.PrefetchScalarGridSpec(
            num_scalar_prefetch=2,
            grid=(nblk,),
            in_specs=[
                pl.BlockSpec((MOE_ROWS, D), lambda i, be, nu: (i, 0)),
                pl.BlockSpec((1, 1, D, F), lambda i, be, nu: (layer, be[i], 0, 0)),
                pl.BlockSpec((1, 1, D, F), lambda i, be, nu: (layer, be[i], 0, 0)),
                pl.BlockSpec((1, 1, F, D), lambda i, be, nu: (layer, be[i], 0, 0)),
            ],
            out_specs=pl.BlockSpec((MOE_ROWS, D), lambda i, be, nu: (i, 0)),
        ),
        compiler_params=_cparams(("arbitrary",)),
        name="moe_gmm",
    )(block_expert, n_used, xs, w1, w3, w2)


ROUTE_ROWS = 512


def _route_kernel(lg_ref, bias_ref, idx_ref, w_ref):
    x = lg_ref[...].T[:N_EXPERTS]
    m = jnp.max(x, axis=0, keepdims=True)
    e = jnp.exp(x - m)
    probs = e / jnp.sum(e, axis=0, keepdims=True)
    sel = probs + bias_ref[...][:, 0:1]
    row = lambda a, i: a[i:i + 1, :]
    G = EXPERTS_PER_GROUP
    scores = []
    for g in range(N_GROUPS):
        s = [row(sel, g * G + i) for i in range(G)]
        best = None
        for i in range(G):
            for j in range(i + 1, G):
                best = s[i] + s[j] if best is None else jnp.maximum(best, s[i] + s[j])
        scores.append(best)
    top = functools.reduce(jnp.maximum, scores)
    gi = jnp.full(top.shape, N_GROUPS - 1, jnp.int32)
    for g in range(N_GROUPS - 2, -1, -1):
        gi = jnp.where(scores[g] == top, g, gi)

    def pick(a, i):
        out = row(a, (N_GROUPS - 1) * G + i)
        for g in range(N_GROUPS - 2, -1, -1):
            out = jnp.where(gi == g, row(a, g * G + i), out)
        return out

    c = [pick(sel, i) for i in range(G)]
    pc = [pick(probs, i) for i in range(G)]

    def first_argmax(vals):
        mx = functools.reduce(jnp.maximum, vals)
        idx = jnp.full(mx.shape, G - 1, jnp.int32)
        for i in range(G - 2, -1, -1):
            idx = jnp.where(vals[i] == mx, i, idx)
        return idx

    i1 = first_argmax(c)
    i2 = first_argmax([jnp.where(i1 == i, -jnp.inf, c[i]) for i in range(G)])
    take = lambda vals, idx: functools.reduce(
        lambda acc, i: jnp.where(idx == i, vals[i], acc), range(G - 2, -1, -1), vals[G - 1])
    w1, w2 = take(pc, i1), take(pc, i2)
    tot = w1 + w2
    zi = jnp.zeros((6,) + top.shape[1:], jnp.int32)
    idx_ref[...] = jnp.concatenate([gi * G + i1, gi * G + i2, zi], axis=0)
    w_ref[...] = jnp.concatenate([w1 / tot, w2 / tot, zi.astype(F32)], axis=0)


def _route(logits, router_bias):
    N = logits.shape[0]
    bias = jnp.broadcast_to(router_bias.astype(F32)[:, None], (N_EXPERTS, LANES))
    idx, w = pl.pallas_call(
        _route_kernel,
        out_shape=(jax.ShapeDtypeStruct((8, N), jnp.int32), jax.ShapeDtypeStruct((8, N), F32)),
        grid=(N // ROUTE_ROWS,),
        in_specs=[pl.BlockSpec((ROUTE_ROWS, LANES), lambda i: (i, 0)),
                  pl.BlockSpec((N_EXPERTS, LANES), lambda i: (0, 0))],
        out_specs=(pl.BlockSpec((8, ROUTE_ROWS), lambda i: (0, i)),
                   pl.BlockSpec((8, ROUTE_ROWS), lambda i: (0, i))),
        compiler_params=_cparams(("parallel",)),
        name="route",
    )(logits, bias)
    return idx[:TOP_K], w[:TOP_K]


def _moe(h, logits, router_bias, w1, w3, w2, layer):
    N, D = h.shape
    expert_idx, gate_w = _route(logits, router_bias)
    NK = N * TOP_K
    flat_e = expert_idx.reshape(NK)
    onehot = (flat_e[None, :] == jnp.arange(N_EXPERTS, dtype=jnp.int32)[:, None]).astype(jnp.int32)
    csum = jnp.cumsum(onehot, axis=1)
    counts = csum[:, -1]
    padded = (counts + MOE_ROWS - 1) // MOE_ROWS * MOE_ROWS
    pad_end = jnp.cumsum(padded)
    pad_start = pad_end - padded
    dest = jnp.sum(onehot * (csum - 1 + pad_start[:, None]), axis=0)
    nblk = -(-NK // MOE_ROWS) + N_EXPERTS
    n_slots = nblk * MOE_ROWS
    n_used = (pad_end[-1] // MOE_ROWS).astype(jnp.int32)
    blk = jnp.arange(nblk, dtype=jnp.int32)
    be = jnp.sum((pad_end[None, :] <= (blk * MOE_ROWS)[:, None]).astype(jnp.int32), axis=1)
    be = jnp.minimum(be, N_EXPERTS - 1)
    be = jnp.where(blk < n_used, be, be[jnp.maximum(n_used - 1, 0)])
    flat_tok = jnp.arange(NK, dtype=jnp.int32) % N
    slot_tok = jnp.zeros((n_slots,), jnp.int32).at[dest].set(flat_tok, unique_indices=True)
    xs = h[slot_tok]
    ys = _gmm(be, n_used.reshape(1), xs, w1, w3, w2, layer)
    return ys[dest[:N]], ys[dest[N:]], gate_w.T


def _combine(x_ref, y0_ref, y1_ref, gw_ref, mod_ref):
    gw = gw_ref[0]
    f = y0_ref[0].astype(F32) * gw[:, 0:1] + y1_ref[0].astype(F32) * gw[:, 1:2]
    return x_ref[0] + mod_ref[0, 0][5:6] * f


def _residual_kernel(x_ref, y0_ref, y1_ref, gw_ref, mod_ref, o_ref):
    o_ref[0] = _combine(x_ref, y0_ref, y1_ref, gw_ref, mod_ref)


def _final_kernel(x_ref, y0_ref, y1_ref, gw_ref, mod_ref, g_ref, o_ref):
    x = _combine(x_ref, y0_ref, y1_ref, gw_ref, mod_ref)
    ms = jnp.mean(x * x, axis=-1, keepdims=True)
    o_ref[0] = x * lax.rsqrt(ms + RMS_EPS) * g_ref[...]


def _residual(x, y0, y1, gw, mods):
    B, T, D = x.shape
    row, mod, _ = _row_specs(D)
    return pl.pallas_call(
        _residual_kernel,
        out_shape=jax.ShapeDtypeStruct((B, T, D), F32),
        grid=(B, T // TM),
        in_specs=[row(D), row(D), row(D), row(TOP_K), mod],
        out_specs=row(D),
        compiler_params=_cparams(("parallel", "parallel")),
        name="residual",
    )(x, y0, y1, gw, mods)


def _final(x, y0, y1, gw, mods, g, L):
    B, S, D = y0.shape
    off = L // TM
    mod = pl.BlockSpec((1, 1, 6, D), lambda b, i: (b, 1, 0, 0))
    lat = lambda w: pl.BlockSpec((1, TM, w), lambda b, i: (b, i, 0))
    return pl.pallas_call(
        _final_kernel,
        out_shape=jax.ShapeDtypeStruct((B, S, D), F32),
        grid=(B, S // TM),
        in_specs=[pl.BlockSpec((1, TM, D), lambda b, i: (b, i + off, 0)), lat(D), lat(D), lat(TOP_K), mod,
                  pl.BlockSpec(g.shape, lambda b, i: (0, 0))],
        out_specs=lat(D),
        compiler_params=_cparams(("parallel", "parallel")),
        name="final_norm",
    )(x, y0, y1, gw, mods, g)


def _rwkv_proj_kernel(x_ref, xp_ref, xn_ref, mod_ref, g_ref, xmix_ref, wr_ref, wk_ref, wv_ref,
                      dw1_ref, dw2_ref, da1_ref, da2_ref, g1_ref, g2_ref, vec_ref, ones_ref,
                      r_ref, v_ref, kk_ref, bv_ref, gate_ref, w0_ref, w1_ref, kd0_ref, kd1_ref, bd0_ref, bd1_ref,
                      *, nt):
    i = pl.program_id(1)
    mod = mod_ref[0, 0]
    g = g_ref[...]
    nm = lambda x: _norm_mod(x, g, mod[0:1], mod[1:2])
    h = nm(x_ref[0])
    hp = nm(xp_ref[0])[7:8] * jnp.where(i >= 2, 1.0, 0.0)
    hn = nm(xn_ref[0])[0:1] * jnp.where((i >= 1) & (i < nt - 1), 1.0, 0.0)
    ridx = lax.broadcasted_iota(jnp.int32, h.shape, 0)
    h_dn = jnp.where(ridx == 0, hp, pltpu.roll(h, 1, axis=0))
    h_up = jnp.where(ridx == TM - 1, hn, pltpu.roll(h, TM - 1, axis=0))
    xx = 0.5 * (h_dn + h_up) - h
    xmix = xmix_ref[...]
    mix = lambda j: (h + xx * xmix[j:j + 1]).astype(BF16)
    vec = vec_ref[...]
    ones = ones_ref[...]

    r = _dot(mix(0), wr_ref[...])
    k = _dot(mix(2), wk_ref[...])
    v = _dot(mix(3), wv_ref[...])
    gate_ref[0] = _dot(_sigmoid(_dot(mix(5), g1_ref[...])).astype(BF16), g2_ref[...])
    kk = k * vec[0:1]
    kk = kk * lax.rsqrt(jnp.maximum(_segsum_wide(kk * kk, ones), 1e-24))
    lw = jnp.tanh(_dot(mix(1), dw1_ref[...])).astype(BF16)
    la = _dot(mix(4), da1_ref[...]).astype(BF16)
    r_ref[0] = r
    v_ref[0] = v
    kk_ref[0] = kk
    bonus = jnp.zeros_like(r)
    lora = DECAY_LORA
    for d, (w_ref, kd_ref, bd_ref) in enumerate(((w0_ref, kd0_ref, bd0_ref), (w1_ref, kd1_ref, bd1_ref))):
        z = -(vec[3 + d:4 + d] + _dot(lw[:, d * lora:(d + 1) * lora], dw2_ref[d]))
        softplus = jnp.maximum(z, 0.0) + jnp.log(1.0 + jnp.exp(-jnp.abs(z)))
        w_ref[0] = jnp.exp(-jnp.exp(-softplus - 0.5))
        iclr = _sigmoid(vec[5 + d:6 + d] + _dot(la[:, d * lora:(d + 1) * lora], da2_ref[d]))
        kd = k * (1.0 + (iclr - 1.0) * vec[1:2])
        kd_ref[0] = kd
        bd_ref[0] = kk * iclr
        bonus = bonus + _segsum_wide(r * kd * vec[2:3], ones)
    bv_ref[0] = bonus * v


DECAY_LORA = 64


def _rwkv_proj(x, mods, g, xmix, wr, wk, wv, dw1, dw2, da1, da2, g1, g2, vec, ones):
    B, T, D = x.shape
    nt = T // TM
    row, mod, full = _row_specs(D)
    r8 = TM // 8
    prev = pl.BlockSpec((1, 8, D), lambda b, i: (b, jnp.maximum(i * r8 - 1, 0), 0))
    nxt = pl.BlockSpec((1, 8, D), lambda b, i: (b, jnp.minimum((i + 1) * r8, T // 8 - 1), 0))
    out = jax.ShapeDtypeStruct((B, T, D), F32)
    return pl.pallas_call(
        functools.partial(_rwkv_proj_kernel, nt=nt),
        out_shape=(out,) * 11,
        grid=(B, nt),
        in_specs=[row(D), prev, nxt, mod, full(g), full(xmix), full(wr), full(wk), full(wv),
                  full(dw1), full(dw2), full(da1), full(da2), full(g1), full(g2), full(vec), full(ones)],
        out_specs=(row(D),) * 11,
        compiler_params=_cparams(("parallel", "parallel")),
        name="rwkv_proj",
    )(x, x, x, mods, g, xmix, wr, wk, wv, dw1, dw2, da1, da2, g1, g2, vec, ones)


CHUNK = 4
N_HEADS = D_MODEL // HEAD_DIM
MAP_LANES = 3 * CHUNK * N_HEADS


def _coef_kernel(r_ref, kk_ref, w_ref, kd_ref, bd_ref, sel_ref, at_ref, rt_ref, bh_ref, kh_ref, gc_ref,
                 cu_ref, cy_ref, *, reverse):
    r, a, w, kd, bd = r_ref[0], -kk_ref[0], w_ref[0], kd_ref[0], bd_ref[0]
    rows = r.shape[0]
    p = lax.broadcasted_iota(jnp.int32, r.shape, 0) & (CHUNK - 1)
    s = (CHUNK - 1 - p) if reverse else p
    back = lambda x, k: pltpu.roll(x, (rows - k) if reverse else k, axis=0)
    ahead = lambda x, k: pltpu.roll(x, k if reverse else (rows - k), axis=0)
    wb = [None] + [back(w, k) for k in range(1, CHUNK)]
    excl = jnp.ones_like(w)
    rest = jnp.ones_like(w)
    for k in range(1, CHUNK):
        excl = excl * jnp.where(s >= k, wb[k], 1.0)
        rest = rest * jnp.where(s + k <= CHUNK - 1, ahead(w, k), 1.0)
    at_ref[0] = a * excl
    rt_ref[0] = r * (excl * w)
    bh_ref[0] = bd * rest
    kh_ref[0] = kd * rest
    gc_ref[0] = excl * w * rest
    between = [None, None, wb[1], wb[1] * wb[2]]
    rw = r * w
    s1 = s[:, :LANES]
    seg = lambda x: _dot(x.astype(BF16), sel_ref[...])

    def by_dist(lead, y, first):
        out = [None] * CHUNK
        for dist in range(first, CHUNK):
            if dist == 0:
                out[0] = seg(r * y)
                continue
            e = back(y, dist) if between[dist] is None else between[dist] * back(y, dist)
            out[dist] = jnp.where(s1 >= dist, seg(lead * e), 0.0)
        return out

    lab, lak = by_dist(a, bd, 1), by_dist(a, kd, 1)
    rb, rk = by_dist(rw, bd, 0), by_dist(rw, kd, 0)
    one = jnp.ones_like(rb[0])
    bk = lambda x, k: x if k == 0 else pltpu.roll(x, (rows - k) if reverse else k, axis=0)
    md = [one]
    for dist in range(1, CHUNK):
        md.append(sum(lab[e] * (bk(md[dist - e], e) if dist - e else 1.0) for e in range(1, dist + 1)))
    gd = [None] + [sum((md[e] if e else 1.0) * bk(lak[dist - e], e) for e in range(dist)) for dist in range(1, CHUNK)]
    yzd = [sum(rb[e] * (bk(md[dist - e], e) if dist - e else 1.0) for e in range(dist + 1)) for dist in range(CHUNK)]
    yvd = [rk[dist] + sum(rb[e] * bk(gd[dist - e], e) for e in range(dist)) for dist in range(CHUNK)]

    def at_pos(table, j, first):
        out = jnp.zeros_like(one)
        for dist in range(first, CHUNK - j):
            out = jnp.where(s1 == j + dist, table[dist], out)
        return out

    zero = jnp.zeros_like(one)
    u_blocks = ([at_pos(md, j, 0) for j in range(CHUNK)] + [at_pos(gd, j, 1) for j in range(CHUNK)]
                + [zero] * CHUNK)
    y_blocks = ([at_pos(yzd, j, 0) for j in range(CHUNK)] + [at_pos(yvd, j, 0) for j in range(CHUNK)]
                + [jnp.where(s1 == j, 1.0, 0.0) for j in range(CHUNK)])
    lane_blk = lax.broadcasted_iota(jnp.int32, one.shape, 1) // N_HEADS

    def place(blocks):
        per_tile = LANES // N_HEADS
        tiles = []
        for t in range(2):
            acc = zero
            for i in range(t * per_tile, min((t + 1) * per_tile, len(blocks))):
                acc = jnp.where(lane_blk == i - t * per_tile, blocks[i], acc)
            tiles.append(acc)
        return jnp.concatenate(tiles, axis=1)

    cu_ref[0] = place(u_blocks)
    cy_ref[0] = place(y_blocks)


def _rwkv_coef(r, kk, w, kd, bd, sel, reverse):
    B, T, D = r.shape
    row, _, full = _row_specs(D)
    out = jax.ShapeDtypeStruct((B, T, D), F32)
    maps = jax.ShapeDtypeStruct((B, T, 2 * LANES), F32)
    return pl.pallas_call(
        functools.partial(_coef_kernel, reverse=reverse),
        out_shape=(out,) * 5 + (maps, maps),
        grid=(B, T // TM),
        in_specs=[row(D)] * 5 + [full(sel)],
        out_specs=(row(D),) * 5 + (row(2 * LANES), row(2 * LANES)),
        compiler_params=_cparams(("parallel", "parallel")),
        name="rwkv_coef",
    )(r, kk, w, kd, bd, sel)


def _scan_kernel(atf, atb, rtf, rtb, vf, vb, bhf, bhb, khf, khb, gcf, gcb, cuf, cub, cyf, cyb,
                 mask_ref, eye_ref, e16_ref, yf, yb, st, *, tc, nb):
    n = pl.program_id(0)

    @pl.when(n == 0)
    def _():
        st[...] = jnp.zeros(st.shape, F32)

    N = HEAD_DIM
    W = CHUNK * N_HEADS
    nch = tc // CHUNK
    dirs = ((atf, rtf, vf, bhf, khf, gcf, cuf, cyf, yf), (atb, rtb, vb, bhb, khb, gcb, cub, cyb, yb))
    lane1 = lax.broadcasted_iota(jnp.int32, (N, 3 * W), 1)
    lane2 = lax.broadcasted_iota(jnp.int32, (N, 2 * W), 1)
    spread = lambda x: (mask_ref[...] * x).astype(BF16)

    def chunk(ci, carry):
        work = []
        for d, refs in enumerate(dirs):
            cc = ci if d == 0 else nch - 1 - ci
            rows = [pl.ds(cc * CHUNK + (s if d == 0 else CHUNK - 1 - s), 1) for s in range(CHUNK)]
            for b in range(nb):
                work.append((d * nb + b, b, cc, rows, refs))
        firsts = []
        for gi, b, cc, rows, (AT, RT, V, BH, KH, GC, CU, CY, Y) in work:
            lhs = jnp.concatenate([st[gi].astype(BF16), eye_ref[...]], axis=0)
            w1 = jnp.concatenate([spread(X[b, rw, :]) for X in (AT, V, RT) for rw in rows], axis=0)
            firsts.append(_dot_nt(lhs, w1))
        mids = []
        for (gi, b, cc, rows, (AT, RT, V, BH, KH, GC, CU, CY, Y)), out in zip(work, firsts):
            zvq = jnp.where((lane1 >= W) & (lane1 < 2 * W), out[N:], out[:N])
            zvq16 = zvq.astype(BF16)
            maprows = lambda M: [e16_ref[...] * M[b, rw, :MAP_LANES] for rw in rows]
            wu = jnp.concatenate(maprows(CU) + [jnp.zeros((W, MAP_LANES), F32)], axis=0)
            u = _dot_nt(zvq16, wu.astype(BF16))
            wy = jnp.concatenate(maprows(CY), axis=0)
            yt = _dot_nt(wy.astype(BF16), zvq16)
            for s, rw in enumerate(rows):
                Y[b, rw] = yt[s * N_HEADS:(s + 1) * N_HEADS][None]
            mids.append((zvq, u))
        for (gi, b, cc, rows, (AT, RT, V, BH, KH, GC, CU, CY, Y)), (zvq, u) in zip(work, mids):
            uv = jnp.where(lane2 < W, u, zvq[:, :2 * W]).astype(BF16)
            w2 = jnp.concatenate([spread(X[b, rw, :]) for X in (BH, KH) for rw in rows], axis=0)
            st[gi] = st[gi] * GC[b, rows[0], :] + _dot(uv, w2)
        return carry

    lax.fori_loop(0, nch, chunk, 0)


def _rwkv_scan(ins_f, ins_b, v, mask, eye, e16, L):
    B, T, D = v.shape
    tc = SCAN_CHUNK
    nch = tc // CHUNK
    nc, nchunks = L // tc, T // tc
    fwd_idx = lambda n: n
    rev_idx = lambda n: jnp.where(n < nc, nc - 1 - n, nchunks - 1 - (n - nc))
    tok = lambda idx, w: pl.BlockSpec((B, tc, w), lambda n: (0, idx(n), 0))
    ys = lambda idx: pl.BlockSpec((B, tc, N_HEADS, HEAD_DIM), lambda n: (0, idx(n), 0, 0))
    full = lambda a: pl.BlockSpec(a.shape, lambda n: (0,) * a.ndim)
    out = jax.ShapeDtypeStruct((B, T, N_HEADS, HEAD_DIM), F32)
    atf, rtf, bhf, khf, gcf, cuf, cyf = ins_f
    atb, rtb, bhb, khb, gcb, cub, cyb = ins_b
    f, r_ = tok(fwd_idx, D), tok(rev_idx, D)
    wy = cyf.shape[-1]
    return pl.pallas_call(
        functools.partial(_scan_kernel, tc=tc, nb=B),
        out_shape=(out, out),
        grid=(nchunks,),
        in_specs=[f, r_, f, r_, f, r_, f, r_, f, r_, f, r_, tok(fwd_idx, wy), tok(rev_idx, wy),
                  tok(fwd_idx, wy), tok(rev_idx, wy), full(mask), full(eye), full(e16)],
        out_specs=(ys(fwd_idx), ys(rev_idx)),
        scratch_shapes=[pltpu.VMEM((2 * B, HEAD_DIM, D), F32)],
        compiler_params=_cparams(("arbitrary",)),
        name="rwkv_scan",
    )(atf, atb, rtf, rtb, v, v, bhf, bhb, khf, khb, gcf, gcb, cuf, cub, cyf, cyb, mask, eye, e16)


def _rwkv_out_kernel(yf_ref, yb_ref, bv_ref, gate_ref, ln_ref, wo_ref, ones_ref, x_ref, mod_ref, g_ref,
                     rwh_ref, rwl_ref, xo_ref, h_ref, lg_ref):
    ones = ones_ref[...]
    y = yf_ref[0] + yb_ref[0]
    inv = 1.0 / HEAD_DIM
    dlt = y - _segsum_wide(y, ones) * inv
    yn = dlt * lax.rsqrt(_segsum_wide(dlt * dlt, ones) * inv + GN_EPS)
    ln = ln_ref[...]
    o = (yn * ln[0:1] + ln[1:2] + bv_ref[0]) * gate_ref[0]
    yl = _dot(o.astype(BF16), wo_ref[...])
    _ffn_prep(x_ref[0], yl, mod_ref[0, 0], g_ref[...], rwh_ref, rwl_ref, xo_ref, h_ref, lg_ref)


def _rwkv_out(yf, yb, bv, gate, ln, wo, ones, x, mods, g, rwh, rwl):
    B, T, D = x.shape
    row, mod, full = _row_specs(D)
    shapes, specs = _ffn_prep_outs(B, T, D)
    return pl.pallas_call(
        _rwkv_out_kernel,
        out_shape=shapes,
        grid=(B, T // TM),
        in_specs=[row(D), row(D), row(D), row(D), full(ln), full(wo), full(ones), row(D), mod, full(g),
                  full(rwh), full(rwl)],
        out_specs=specs,
        compiler_params=_cparams(("parallel", "parallel")),
        name="rwkv_out",
    )(yf, yb, bv, gate, ln, wo, ones, x, mods, g, rwh, rwl)


def _rope_tables(S, L):
    rows = S // GRID_W
    row = jnp.repeat(jnp.arange(rows, dtype=F32), GRID_W)
    col = (jnp.arange(rows * GRID_W) % GRID_W).astype(F32)
    n_freq = HEAD_DIM // 4
    inv = ROPE_THETA ** (-jnp.arange(n_freq, dtype=F32) / n_freq)
    lane = np.arange(LANES) % HEAD_DIM
    axis, half, freq = lane // 32, (lane % 32) // 16, lane % 16
    pos = jnp.where(jnp.asarray(axis == 0)[None, :], row[:, None], col[:, None])
    ang = pos * inv[freq][None, :]
    sgn = jnp.asarray(np.where(half == 0, -1.0, 1.0), dtype=F32)
    cos = jnp.concatenate([jnp.ones((L, LANES), F32), jnp.cos(ang)], axis=0)
    sin = jnp.concatenate([jnp.zeros((L, LANES), F32), jnp.sin(ang) * sgn[None, :]], axis=0)
    return cos, sin


def kernel(x, c, ctx, c_ctx, ada_w, ada_b, norm_mix_g, norm_ffn_g, attn_w_in, attn_w_out, attn_sink,
           attn_q_norm_g, attn_k_norm_g, rwkv_x_mix, rwkv_w_r, rwkv_w_k, rwkv_w_v, rwkv_w_o,
           rwkv_decay_w0, rwkv_decay_w1, rwkv_decay_w2, rwkv_iclr_a0, rwkv_iclr_a1, rwkv_iclr_a2,
           rwkv_gate_g1, rwkv_gate_g2, rwkv_k_k, rwkv_k_a, rwkv_r_k, rwkv_ln_g, rwkv_ln_b,
           router_w, router_bias, moe_w1, moe_w3, moe_w2, final_norm_g):
    B, S, D = x.shape
    L = ctx.shape[1]
    T = L + S
    depth = ada_w.shape[0]
    assert D == D_MODEL and L == TM and S % TM == 0 and B == 2 and depth == 2
    ones = _seg_ones()
    bf = lambda a: a.astype(BF16)

    cs = jnp.zeros((8, D), F32).at[:B].set(c).at[B].set(c_ctx)
    ada = _ada(cs, ada_w, ada_b).reshape(depth, 8, 6, D)
    mods = [jnp.stack([jnp.broadcast_to(ada[i, B], (B, 6, D)), ada[i, :B]], axis=1) for i in range(depth)]

    xa = jnp.concatenate([ctx, x], axis=1)
    rw = jnp.zeros((D, LANES), F32).at[:, :N_EXPERTS].set(router_w)
    rwh, rwl = _split(rw)

    w_in = attn_w_in[0]
    roped = np.concatenate([np.arange(0, 640), np.arange(768, 1408)])
    w_rot = w_in[:, roped ^ 16]
    cos, sin = _rope_tables(S, L)
    lane = np.arange(LANES) % HEAD_DIM
    gains = lambda g: jnp.stack([g[lane], g[lane ^ 16]], axis=0)
    qa, ka, va, qb, kb, vb_x = _inproj(xa, mods[0], norm_mix_g[0].reshape(1, D), bf(w_in), bf(w_rot), cos, sin,
                                       gains(attn_q_norm_g[0]), gains(attn_k_norm_g[0]), ones)
    grouped = lambda q: q.reshape(B, A_KV_HEADS, GROUP, T, HEAD_DIM)
    qa, qb = grouped(qa), grouped(qb)
    sink = attn_sink[0].astype(F32) * LOG2E
    vc = va[:, :, :L]
    va_x = jnp.swapaxes(jnp.concatenate(
        [vc, jnp.ones_like(vc[..., :1]), jnp.zeros_like(vc[..., :VT_ROWS - HEAD_DIM - 1])], axis=-1), 2, 3)
    nosink = jnp.full((B_Q_HEADS,), NEG, F32)
    oa_l = _window_attn(sink, qa, ka, va, L, S)
    oa_c = _flash(sink, qa, ka, va_x, q_rows=L, q_off=0, k_rows=L, tq=L, tk=L)
    ob_l = _flash(nosink, qb, kb, vb_x, q_rows=S, q_off=L, k_rows=T, tq=256, tk=_key_tile(T))
    ob_c = _flash(nosink, qb, kb, vb_x, q_rows=L, q_off=0, k_rows=L, tq=L, tk=L)
    oa = jnp.concatenate([oa_c, oa_l], axis=1)
    ob = jnp.concatenate([ob_c, ob_l], axis=1)
    w_out = bf(attn_w_out[0])
    na = A_Q_HEADS * HEAD_DIM
    xa, h, lg = _attn_out(oa, ob, w_out[:na], w_out[na:], xa, mods[0], norm_ffn_g[0].reshape(1, D), rwh, rwl)
    y0, y1, gw = _moe(h.reshape(B * T, D), lg.reshape(B * T, LANES), router_bias,
                      moe_w1, moe_w3, moe_w2, 0)
    xa = _residual(xa, y0.reshape(B, T, D), y1.reshape(B, T, D), gw.reshape(B, T, TOP_K), mods[0])

    cat2 = lambda a: jnp.concatenate([a[0], a[1]], axis=1)
    vec = jnp.stack([rwkv_k_k[0], rwkv_k_a[0], rwkv_r_k[0].reshape(D), rwkv_decay_w0[0, 0], rwkv_decay_w0[0, 1],
                     rwkv_iclr_a0[0, 0], rwkv_iclr_a0[0, 1], jnp.zeros((D,), F32)], axis=0)
    outs = _rwkv_proj(xa, mods[1], norm_mix_g[1].reshape(1, D), jnp.pad(rwkv_x_mix[0], ((0, 2), (0, 0))),
                      bf(rwkv_w_r[0]), bf(rwkv_w_k[0]), bf(rwkv_w_v[0]),
                      bf(cat2(rwkv_decay_w1[0])), bf(rwkv_decay_w2[0]),
                      bf(cat2(rwkv_iclr_a1[0])), bf(rwkv_iclr_a2[0]),
                      bf(rwkv_gate_g1[0]), bf(rwkv_gate_g2[0]), vec, ones)
    r, v, kk, bv, gate, w0, w1, kd0, kd1, bd0, bd1 = outs
    lane_id = np.arange(D)
    eye = jnp.asarray(np.arange(HEAD_DIM)[:, None] == (lane_id % HEAD_DIM)[None, :], dtype=BF16)
    n_heads = D // HEAD_DIM
    head_mask = jnp.asarray(np.arange(n_heads)[:, None] == (lane_id // HEAD_DIM)[None, :], dtype=F32)
    sel = jnp.asarray((lane_id // HEAD_DIM)[:, None] == (np.arange(LANES) % n_heads)[None, :], dtype=BF16)
    e16 = jnp.asarray(np.arange(n_heads)[:, None] == (np.arange(MAP_LANES) % n_heads)[None, :], dtype=F32)
    scan_ins = [_rwkv_coef(r, kk, w_d, kd_d, bd_d, sel, reverse=d == 1)
                for d, (w_d, kd_d, bd_d) in enumerate(((w0, kd0, bd0), (w1, kd1, bd1)))]
    yf, yb = _rwkv_scan(scan_ins[0], scan_ins[1], v, head_mask, eye, e16, L)
    ln = jnp.stack([rwkv_ln_g[0], rwkv_ln_b[0]] + [jnp.zeros((D,), F32)] * 6, axis=0)
    xa, h, lg = _rwkv_out(yf.reshape(B, T, D), yb.reshape(B, T, D), bv, gate, ln, bf(rwkv_w_o[0]), ones, xa, mods[1],
                          norm_ffn_g[1].reshape(1, D), rwh, rwl)
    y0, y1, gw = _moe(h[:, L:].reshape(B * S, D), lg[:, L:].reshape(B * S, LANES), router_bias,
                      moe_w1, moe_w3, moe_w2, 1)
    return _final(xa, y0.reshape(B, S, D), y1.reshape(B, S, D), gw.reshape(B, S, TOP_K), mods[1],
                  final_norm_g.reshape(1, D), L)


def _key_tile(T):
    for tk in (1280, 1024, 768, 512, 256):
        if T % tk == 0:
            return tk
    raise ValueError(T)
```

```python
import functools

import numpy as np
import jax
import jax.numpy as jnp
from jax import lax
from jax.experimental import pallas as pl
from jax.experimental.pallas import tpu as pltpu

F32 = jnp.float32
BF16 = jnp.bfloat16

D_MODEL = 1024
HEAD_DIM = 64
GRID_W = 64
ROPE_THETA = 10000.0
RMS_EPS = 1e-6
GN_EPS = 64e-5
A_Q_HEADS = 8
A_KV_HEADS = 2
B_Q_HEADS = 8
B_KV_HEADS = 2
GROUP = 4
WINDOW = 128
N_EXPERTS = 16
N_GROUPS = 4
EXPERTS_PER_GROUP = 4
TOP_K = 2
LANES = 128
TM = 256
MOE_ROWS = 512
SCAN_CHUNK = 64
VMEM_LIMIT = 56 * 1024 * 1024
NEG = -1e30
LOG2E = 1.4426950408889634


def _cparams(sem):
    return pltpu.CompilerParams(dimension_semantics=sem, vmem_limit_bytes=VMEM_LIMIT)


def _dot(a, b):
    return jnp.dot(a, b, preferred_element_type=F32)


def _dot_nt(a, b):
    return lax.dot_general(a, b, (((1,), (1,)), ((), ())), preferred_element_type=F32)


def _split(x):
    hi = x.astype(BF16)
    lo = (x - hi.astype(F32)).astype(BF16)
    return hi, lo


def _dot3(x, w):
    xh, xl = _split(x)
    wh, wl = _split(w)
    return _dot(xh, wh) + _dot(xh, wl) + _dot(xl, wh)


def _segsum(v, ones):
    hi, lo = _split(v)
    return _dot(hi, ones) + _dot(lo, ones)


def _segsum_wide(v, ones):
    n = v.shape[1] // LANES
    return jnp.concatenate([_segsum(v[:, j * LANES:(j + 1) * LANES], ones) for j in range(n)], axis=1)


def _norm_mod(x, g, shift, scale):
    ms = jnp.mean(x * x, axis=-1, keepdims=True)
    return (x * lax.rsqrt(ms + RMS_EPS) * g) * (1.0 + scale) + shift


def _sigmoid(x):
    return 1.0 / (1.0 + jnp.exp(-x))


def _seg_ones():
    i = np.arange(LANES)
    return jnp.asarray((i[:, None] // HEAD_DIM) == (i[None, :] // HEAD_DIM), dtype=BF16)


def _ada_kernel(c_ref, w_ref, b_ref, o_ref):
    c = c_ref[...]
    s = c * _sigmoid(c)
    o_ref[0] = _dot3(s, w_ref[0]) + b_ref[0]


def _ada(cs, ada_w, ada_b):
    depth, d, n = ada_w.shape
    tn = 1536
    return pl.pallas_call(
        _ada_kernel,
        out_shape=jax.ShapeDtypeStruct((depth, 8, n), F32),
        grid=(depth, n // tn),
        in_specs=[
            pl.BlockSpec((8, d), lambda l, j: (0, 0)),
            pl.BlockSpec((1, d, tn), lambda l, j: (l, 0, j)),
            pl.BlockSpec((1, 1, tn), lambda l, j: (l, 0, j)),
        ],
        out_specs=pl.BlockSpec((1, 8, tn), lambda l, j: (l, 0, j)),
        compiler_params=_cparams(("arbitrary", "arbitrary")),
        name="ada",
    )(cs, ada_w, ada_b.reshape(depth, 1, n))


def _inproj_kernel(x_ref, mod_ref, g_ref, w_ref, wrot_ref, cos_ref, sin_ref, gq_ref, gk_ref, ones_ref,
                   qa_ref, ka_ref, va_ref, qb_ref, kb_ref, vb_ref):
    mod = mod_ref[0, 0]
    h = _norm_mod(x_ref[0], g_ref[...], mod[0:1], mod[1:2]).astype(BF16)
    y = _dot(h, w_ref[...])
    yr = _dot(h, wrot_ref[...])
    cos = cos_ref[...]
    sin = sin_ref[...]
    ones = ones_ref[...]
    qscale = HEAD_DIM ** -0.5 * LOG2E

    def put(ref, tile, val):
        ref[0, 2 * tile] = val[:, :HEAD_DIM].astype(ref.dtype)
        ref[0, 2 * tile + 1] = val[:, HEAD_DIM:].astype(ref.dtype)

    def chunk(a, c):
        return a[:, c * LANES:(c + 1) * LANES]

    for c in range(4):
        put(qa_ref, c, (chunk(y, c) * cos + chunk(yr, c) * sin) * qscale)
    put(ka_ref, 0, chunk(y, 4) * cos + chunk(yr, 4) * sin)
    put(va_ref, 0, chunk(y, 5))

    def normed(c, cr, gain_ref):
        v = chunk(y, c)
        rs = lax.rsqrt(_segsum(v * v, ones) * (1.0 / HEAD_DIM) + RMS_EPS)
        return (v * rs * gain_ref[0:1]) * cos + (chunk(yr, cr) * rs * gain_ref[1:2]) * sin

    for c in range(4):
        put(qb_ref, c, normed(6 + c, 5 + c, gq_ref) * qscale)
    put(kb_ref, 0, normed(10, 9, gk_ref))
    vt = chunk(y, 11).T
    tail = jnp.where(lax.broadcasted_iota(jnp.int32, (VT_ROWS - HEAD_DIM, vt.shape[1]), 0) == 0, 1.0, 0.0)
    for hh in range(B_KV_HEADS):
        tile = jnp.concatenate([vt[hh * HEAD_DIM:(hh + 1) * HEAD_DIM], tail], axis=0)
        vb_ref[0, hh] = tile.astype(vb_ref.dtype)


def _inproj(x, mods, g, w_in, w_rot, cos, sin, gq2, gk2, ones):
    B, T, D = x.shape
    nt = T // TM
    heads = lambda n: jax.ShapeDtypeStruct((B, n, T, HEAD_DIM), BF16)
    hspec = lambda n: pl.BlockSpec((1, n, TM, HEAD_DIM), lambda b, i: (b, 0, i, 0))
    full = lambda a: pl.BlockSpec(a.shape, lambda b, i: (0,) * a.ndim)
    vt_shape = jax.ShapeDtypeStruct((B, B_KV_HEADS, VT_ROWS, T), BF16)
    vt_spec = pl.BlockSpec((1, B_KV_HEADS, VT_ROWS, TM), lambda b, i: (b, 0, 0, i))
    return pl.pallas_call(
        _inproj_kernel,
        out_shape=(heads(8), heads(2), heads(2), heads(8), heads(2), vt_shape),
        grid=(B, nt),
        in_specs=[
            pl.BlockSpec((1, TM, D), lambda b, i: (b, i, 0)),
            pl.BlockSpec((1, 1, 6, D), lambda b, i: (b, jnp.minimum(i, 1), 0, 0)),
            full(g), full(w_in), full(w_rot),
            pl.BlockSpec((TM, LANES), lambda b, i: (i, 0)),
            pl.BlockSpec((TM, LANES), lambda b, i: (i, 0)),
            full(gq2), full(gk2), full(ones),
        ],
        out_specs=(hspec(8), hspec(2), hspec(2), hspec(8), hspec(2), vt_spec),
        compiler_params=_cparams(("parallel", "parallel")),
        name="attn_inproj",
    )(x, mods, g, w_in, w_rot, cos, sin, gq2, gk2, ones)


LOOKAHEAD = 3
VT_ROWS = 80


def _flash_kernel(sink_ref, q_ref, k_ref, v_ref, o_ref, m_scr, acc_scr, s_scr, *, tk, nk):
    h = pl.program_id(1)
    m_scr[...] = jnp.full(m_scr.shape, NEG, F32)
    acc_scr[...] = jnp.zeros(acc_scr.shape, F32)

    def scores(j, g):
        return _dot_nt(k_ref[0, 0, pl.ds(pl.multiple_of(j * tk, tk), tk), :], q_ref[0, 0, g])

    for g in range(LOOKAHEAD):
        s_scr[g] = scores(0, g)

    def body(j, carry):
        vt = v_ref[0, 0, :, pl.ds(pl.multiple_of(j * tk, tk), tk)]
        jn = jnp.minimum(j + 1, nk - 1)
        ahead = {}
        for g in range(GROUP):
            st = s_scr[g] if g < LOOKAHEAD else ahead.pop(g)
            if g + LOOKAHEAD < GROUP:
                ahead[g + LOOKAHEAD] = scores(j, g + LOOKAHEAD)
            m_prev = m_scr[g]
            m_new = jnp.maximum(m_prev, jnp.max(st, axis=0, keepdims=True))
            p = jnp.exp2((st - m_new).astype(BF16))
            if g + LOOKAHEAD >= GROUP:
                s_scr[g + LOOKAHEAD - GROUP] = scores(jn, g + LOOKAHEAD - GROUP)
            acc_scr[g] = jnp.exp2(m_prev - m_new) * acc_scr[g] + _dot(vt, p)
            m_scr[g] = m_new
        return carry

    lax.fori_loop(0, nk, body, 0)
    outs = []
    for g in range(GROUP):
        acc = acc_scr[g]
        l = acc[HEAD_DIM:HEAD_DIM + 1] + jnp.exp2(sink_ref[h * GROUP + g] - m_scr[g])
        outs.append(acc[:HEAD_DIM] / l)
    o_ref[0] = jnp.concatenate(outs, axis=0).T.astype(o_ref.dtype)


def _flash(sink, q, k, v, *, q_rows, q_off, k_rows, tq, tk):
    B, Hkv = k.shape[:2]
    nq, nk = q_rows // tq, k_rows // tk
    qo = q_off // tq
    return pl.pallas_call(
        functools.partial(_flash_kernel, tk=tk, nk=nk),
        out_shape=jax.ShapeDtypeStruct((B, q_rows, Hkv * GROUP * HEAD_DIM), BF16),
        grid=(B, Hkv, nq),
        in_specs=[
            pl.BlockSpec(memory_space=pltpu.SMEM),
            pl.BlockSpec((1, 1, GROUP, tq, HEAD_DIM), lambda b, h, i: (b, h, 0, i + qo, 0)),
            pl.BlockSpec((1, 1, k_rows, HEAD_DIM), lambda b, h, i: (b, h, 0, 0)),
            pl.BlockSpec((1, 1, VT_ROWS, k_rows), lambda b, h, i: (b, h, 0, 0)),
        ],
        out_specs=pl.BlockSpec((1, tq, GROUP * HEAD_DIM), lambda b, h, i: (b, i, h)),
        scratch_shapes=[
            pltpu.VMEM((GROUP, 1, tq), F32),
            pltpu.VMEM((GROUP, VT_ROWS, tq), F32),
            pltpu.VMEM((LOOKAHEAD, tk, tq), F32),
        ],
        compiler_params=_cparams(("parallel", "parallel", "arbitrary")),
        name="flash_attn",
    )(sink, q, k, v)


def _window_kernel(sink_ref, q_ref, kc_ref, vc_ref, k0_ref, k1_ref, k2_ref, v0_ref, v1_ref, v2_ref, o_ref, *, nb):
    h = pl.program_id(1)
    i = pl.program_id(2)
    rows = GROUP * WINDOW
    q = q_ref[0, 0].reshape(rows, HEAD_DIM)
    r = lax.broadcasted_iota(jnp.int32, (rows, WINDOW), 0) & (WINDOW - 1)
    c = lax.broadcasted_iota(jnp.int32, (rows, WINDOW), 1)
    sc = _dot_nt(q, kc_ref[0, 0])
    s0 = jnp.where((c >= r) & (i > 0), _dot_nt(q, k0_ref[0, 0]), NEG)
    s1 = _dot_nt(q, k1_ref[0, 0])
    s2 = jnp.where((c <= r) & (i < nb - 1), _dot_nt(q, k2_ref[0, 0]), NEG)
    sink = jnp.concatenate(
        [jnp.full((WINDOW, 1), sink_ref[h * GROUP + g], F32) for g in range(GROUP)], axis=0)
    rowmax = lambda s: jnp.max(s, axis=-1, keepdims=True)
    m = jnp.maximum(jnp.maximum(rowmax(sc), rowmax(s0)), jnp.maximum(rowmax(s1), rowmax(s2)))
    m = jnp.maximum(m, sink)
    pc, p0, p1, p2 = (jnp.exp2(s - m) for s in (sc, s0, s1, s2))
    rowsum = lambda p: jnp.sum(p, axis=-1, keepdims=True)
    l = rowsum(pc) + rowsum(p0) + rowsum(p1) + rowsum(p2) + jnp.exp2(sink - m)
    acc = (_dot(pc.astype(BF16), vc_ref[0, 0]) + _dot(p0.astype(BF16), v0_ref[0, 0])
           + _dot(p1.astype(BF16), v1_ref[0, 0]) + _dot(p2.astype(BF16), v2_ref[0, 0]))
    out = acc / l
    for g in range(GROUP):
        o_ref[0, :, g * HEAD_DIM:(g + 1) * HEAD_DIM] = out[g * WINDOW:(g + 1) * WINDOW].astype(o_ref.dtype)


def _window_attn(sink, q, k, v, L, S):
    B, Hkv = k.shape[:2]
    nb = S // WINDOW
    pad = ((0, 0), (0, 0), (WINDOW, WINDOW), (0, 0))
    kp = jnp.pad(k[:, :, L:], pad)
    vp = jnp.pad(v[:, :, L:], pad)
    qo = L // WINDOW
    band = lambda j: pl.BlockSpec((1, 1, WINDOW, HEAD_DIM), lambda b, h, i: (b, h, i + j, 0))
    ctx = pl.BlockSpec((1, 1, L, HEAD_DIM), lambda b, h, i: (b, h, 0, 0))
    return pl.pallas_call(
        functools.partial(_window_kernel, nb=nb),
        out_shape=jax.ShapeDtypeStruct((B, S, Hkv * GROUP * HEAD_DIM), BF16),
        grid=(B, Hkv, nb),
        in_specs=[
            pl.BlockSpec(memory_space=pltpu.SMEM),
            pl.BlockSpec((1, 1, GROUP, WINDOW, HEAD_DIM), lambda b, h, i: (b, h, 0, i + qo, 0)),
            ctx, ctx, band(0), band(1), band(2), band(0), band(1), band(2),
        ],
        out_specs=pl.BlockSpec((1, WINDOW, GROUP * HEAD_DIM), lambda b, h, i: (b, i, h)),
        compiler_params=_cparams(("parallel", "parallel", "parallel")),
        name="window_attn",
    )(sink, q, k, v, kp, kp, kp, vp, vp, vp)


def _ffn_prep(x, y, mod, gffn, rwh_ref, rwl_ref, xo_ref, h_ref, lg_ref):
    xn = x + mod[2:3] * y
    h = _norm_mod(xn, gffn, mod[3:4], mod[4:5])
    xo_ref[0] = xn
    hh, hl = _split(h)
    h_ref[0] = hh
    rwh = rwh_ref[...]
    lg_ref[0] = _dot(hh, rwh) + _dot(hl, rwh) + _dot(hh, rwl_ref[...])


def _attn_out_kernel(oa_ref, ob_ref, wa_ref, wb_ref, x_ref, mod_ref, g_ref, rwh_ref, rwl_ref,
                     xo_ref, h_ref, lg_ref):
    y = _dot(oa_ref[0], wa_ref[...]) + _dot(ob_ref[0], wb_ref[...])
    _ffn_prep(x_ref[0], y, mod_ref[0, 0], g_ref[...], rwh_ref, rwl_ref, xo_ref, h_ref, lg_ref)


def _row_specs(D):
    row = lambda w: pl.BlockSpec((1, TM, w), lambda b, i: (b, i, 0))
    mod = pl.BlockSpec((1, 1, 6, D), lambda b, i: (b, jnp.minimum(i, 1), 0, 0))
    full = lambda a: pl.BlockSpec(a.shape, lambda b, i: (0,) * a.ndim)
    return row, mod, full


def _ffn_prep_outs(B, T, D):
    row, _, _ = _row_specs(D)
    shapes = (jax.ShapeDtypeStruct((B, T, D), F32), jax.ShapeDtypeStruct((B, T, D), BF16),
              jax.ShapeDtypeStruct((B, T, LANES), F32))
    return shapes, (row(D), row(D), row(LANES))


def _attn_out(oa, ob, wa, wb, x, mods, g, rwh, rwl):
    B, T, D = x.shape
    row, mod, full = _row_specs(D)
    shapes, specs = _ffn_prep_outs(B, T, D)
    return pl.pallas_call(
        _attn_out_kernel,
        out_shape=shapes,
        grid=(B, T // TM),
        in_specs=[row(oa.shape[-1]), row(ob.shape[-1]), full(wa), full(wb), row(D), mod, full(g),
                  full(rwh), full(rwl)],
        out_specs=specs,
        compiler_params=_cparams(("parallel", "parallel")),
        name="attn_out",
    )(oa, ob, wa, wb, x, mods, g, rwh, rwl)


def _gmm_kernel(be_ref, nu_ref, x_ref, w1_ref, w3_ref, w2_ref, o_ref):
    i = pl.program_id(0)

    @pl.when(i < nu_ref[0])
    def _():
        x = x_ref[...]
        a = _dot(x, w1_ref[0, 0].astype(BF16))
        b = _dot(x, w3_ref[0, 0].astype(BF16))
        mid = (a * _sigmoid(a)) * b
        o_ref[...] = _dot(mid.astype(BF16), w2_ref[0, 0].astype(BF16)).astype(o_ref.dtype)

    @pl.when(i >= nu_ref[0])
    def _():
        o_ref[...] = jnp.zeros(o_ref.shape, o_ref.dtype)


def _gmm(block_expert, n_used, xs, w1, w3, w2, layer):
    n_slots, D = xs.shape
    F = w1.shape[-1]
    nblk = n_slots // MOE_ROWS
    return pl.pallas_call(
        _gmm_kernel,
        out_shape=jax.ShapeDtypeStruct((n_slots, D), BF16),
        grid_spec=pltpu.PrefetchScalarGridSpec(
            num_scalar_prefetch=2,
            grid=(nblk,),
            in_specs=[
                pl.BlockSpec((MOE_ROWS, D), lambda i, be, nu: (i, 0)),
                pl.BlockSpec((1, 1, D, F), lambda i, be, nu: (layer, be[i], 0, 0)),
                pl.BlockSpec((1, 1, D, F), lambda i, be, nu: (layer, be[i], 0, 0)),
                pl.BlockSpec((1, 1, F, D), lambda i, be, nu: (layer, be[i], 0, 0)),
            ],
            out_specs=pl.BlockSpec((MOE_ROWS, D), lambda i, be, nu: (i, 0)),
        ),
        compiler_params=_cparams(("arbitrary",)),
        name="moe_gmm",
    )(block_expert, n_used, xs, w1, w3, w2)


ROUTE_ROWS = 512


def _route_kernel(lg_ref, bias_ref, idx_ref, w_ref):
    x = lg_ref[...].T[:N_EXPERTS]
    m = jnp.max(x, axis=0, keepdims=True)
    e = jnp.exp(x - m)
    probs = e / jnp.sum(e, axis=0, keepdims=True)
    sel = probs + bias_ref[...][:, 0:1]
    row = lambda a, i: a[i:i + 1, :]
    G = EXPERTS_PER_GROUP
    scores = []
    for g in range(N_GROUPS):
        s = [row(sel, g * G + i) for i in range(G)]
        best = None
        for i in range(G):
            for j in range(i + 1, G):
                best = s[i] + s[j] if best is None else jnp.maximum(best, s[i] + s[j])
        scores.append(best)
    top = functools.reduce(jnp.maximum, scores)
    gi = jnp.full(top.shape, N_GROUPS - 1, jnp.int32)
    for g in range(N_GROUPS - 2, -1, -1):
        gi = jnp.where(scores[g] == top, g, gi)

    def pick(a, i):
        out = row(a, (N_GROUPS - 1) * G + i)
        for g in range(N_GROUPS - 2, -1, -1):
            out = jnp.where(gi == g, row(a, g * G + i), out)
        return out

    c = [pick(sel, i) for i in range(G)]
    pc = [pick(probs, i) for i in range(G)]

    def first_argmax(vals):
        mx = functools.reduce(jnp.maximum, vals)
        idx = jnp.full(mx.shape, G - 1, jnp.int32)
        for i in range(G - 2, -1, -1):
            idx = jnp.where(vals[i] == mx, i, idx)
        return idx

    i1 = first_argmax(c)
    i2 = first_argmax([jnp.where(i1 == i, -jnp.inf, c[i]) for i in range(G)])
    take = lambda vals, idx: functools.reduce(
        lambda acc, i: jnp.where(idx == i, vals[i], acc), range(G - 2, -1, -1), vals[G - 1])
    w1, w2 = take(pc, i1), take(pc, i2)
    tot = w1 + w2
    zi = jnp.zeros((6,) + top.shape[1:], jnp.int32)
    idx_ref[...] = jnp.concatenate([gi * G + i1, gi * G + i2, zi], axis=0)
    w_ref[...] = jnp.concatenate([w1 / tot, w2 / tot, zi.astype(F32)], axis=0)


def _route(logits, router_bias):
    N = logits.shape[0]
    bias = jnp.broadcast_to(router_bias.astype(F32)[:, None], (N_EXPERTS, LANES))
    idx, w = pl.pallas_call(
        _route_kernel,
        out_shape=(jax.ShapeDtypeStruct((8, N), jnp.int32), jax.ShapeDtypeStruct((8, N), F32)),
        grid=(N // ROUTE_ROWS,),
        in_specs=[pl.BlockSpec((ROUTE_ROWS, LANES), lambda i: (i, 0)),
                  pl.BlockSpec((N_EXPERTS, LANES), lambda i: (0, 0))],
        out_specs=(pl.BlockSpec((8, ROUTE_ROWS), lambda i: (0, i)),
                   pl.BlockSpec((8, ROUTE_ROWS), lambda i: (0, i))),
        compiler_params=_cparams(("parallel",)),
        name="route",
    )(logits, bias)
    return idx[:TOP_K], w[:TOP_K]


def _moe(h, logits, router_bias, w1, w3, w2, layer):
    N, D = h.shape
    expert_idx, gate_w = _route(logits, router_bias)
    NK = N * TOP_K
    flat_e = expert_idx.reshape(NK)
    onehot = (flat_e[None, :] == jnp.arange(N_EXPERTS, dtype=jnp.int32)[:, None]).astype(jnp.int32)
    csum = jnp.cumsum(onehot, axis=1)
    counts = csum[:, -1]
    padded = (counts + MOE_ROWS - 1) // MOE_ROWS * MOE_ROWS
    pad_end = jnp.cumsum(padded)
    pad_start = pad_end - padded
    dest = jnp.sum(onehot * (csum - 1 + pad_start[:, None]), axis=0)
    nblk = -(-NK // MOE_ROWS) + N_EXPERTS
    n_slots = nblk * MOE_ROWS
    n_used = (pad_end[-1] // MOE_ROWS).astype(jnp.int32)
    blk = jnp.arange(nblk, dtype=jnp.int32)
    be = jnp.sum((pad_end[None, :] <= (blk * MOE_ROWS)[:, None]).astype(jnp.int32), axis=1)
    be = jnp.minimum(be, N_EXPERTS - 1)
    be = jnp.where(blk < n_used, be, be[jnp.maximum(n_used - 1, 0)])
    flat_tok = jnp.arange(NK, dtype=jnp.int32) % N
    slot_tok = jnp.zeros((n_slots,), jnp.int32).at[dest].set(flat_tok, unique_indices=True)
    xs = h[slot_tok]
    ys = _gmm(be, n_used.reshape(1), xs, w1, w3, w2, layer)
    return ys[dest[:N]], ys[dest[N:]], gate_w.T


def _combine(x_ref, y0_ref, y1_ref, gw_ref, mod_ref):
    gw = gw_ref[0]
    f = y0_ref[0].astype(F32) * gw[:, 0:1] + y1_ref[0].astype(F32) * gw[:, 1:2]
    return x_ref[0] + mod_ref[0, 0][5:6] * f


def _residual_kernel(x_ref, y0_ref, y1_ref, gw_ref, mod_ref, o_ref):
    o_ref[0] = _combine(x_ref, y0_ref, y1_ref, gw_ref, mod_ref)


def _final_kernel(x_ref, y0_ref, y1_ref, gw_ref, mod_ref, g_ref, o_ref):
    x = _combine(x_ref, y0_ref, y1_ref, gw_ref, mod_ref)
    ms = jnp.mean(x * x, axis=-1, keepdims=True)
    o_ref[0] = x * lax.rsqrt(ms + RMS_EPS) * g_ref[...]


def _residual(x, y0, y1, gw, mods):
    B, T, D = x.shape
    row, mod, _ = _row_specs(D)
    return pl.pallas_call(
        _residual_kernel,
        out_shape=jax.ShapeDtypeStruct((B, T, D), F32),
        grid=(B, T // TM),
        in_specs=[row(D), row(D), row(D), row(TOP_K), mod],
        out_specs=row(D),
        compiler_params=_cparams(("parallel", "parallel")),
        name="residual",
    )(x, y0, y1, gw, mods)


def _final(x, y0, y1, gw, mods, g, L):
    B, S, D = y0.shape
    off = L // TM
    mod = pl.BlockSpec((1, 1, 6, D), lambda b, i: (b, 1, 0, 0))
    lat = lambda w: pl.BlockSpec((1, TM, w), lambda b, i: (b, i, 0))
    return pl.pallas_call(
        _final_kernel,
        out_shape=jax.ShapeDtypeStruct((B, S, D), F32),
        grid=(B, S // TM),
        in_specs=[pl.BlockSpec((1, TM, D), lambda b, i: (b, i + off, 0)), lat(D), lat(D), lat(TOP_K), mod,
                  pl.BlockSpec(g.shape, lambda b, i: (0, 0))],
        out_specs=lat(D),
        compiler_params=_cparams(("parallel", "parallel")),
        name="final_norm",
    )(x, y0, y1, gw, mods, g)


def _rwkv_proj_kernel(x_ref, xp_ref, xn_ref, mod_ref, g_ref, xmix_ref, wr_ref, wk_ref, wv_ref,
                      dw1_ref, dw2_ref, da1_ref, da2_ref, g1_ref, g2_ref, vec_ref, ones_ref,
                      r_ref, v_ref, kk_ref, bv_ref, gate_ref, w0_ref, w1_ref, kd0_ref, kd1_ref, bd0_ref, bd1_ref,
                      *, nt):
    i = pl.program_id(1)
    mod = mod_ref[0, 0]
    g = g_ref[...]
    nm = lambda x: _norm_mod(x, g, mod[0:1], mod[1:2])
    h = nm(x_ref[0])
    hp = nm(xp_ref[0])[7:8] * jnp.where(i >= 2, 1.0, 0.0)
    hn = nm(xn_ref[0])[0:1] * jnp.where((i >= 1) & (i < nt - 1), 1.0, 0.0)
    ridx = lax.broadcasted_iota(jnp.int32, h.shape, 0)
    h_dn = jnp.where(ridx == 0, hp, pltpu.roll(h, 1, axis=0))
    h_up = jnp.where(ridx == TM - 1, hn, pltpu.roll(h, TM - 1, axis=0))
    xx = 0.5 * (h_dn + h_up) - h
    xmix = xmix_ref[...]
    mix = lambda j: (h + xx * xmix[j:j + 1]).astype(BF16)
    vec = vec_ref[...]
    ones = ones_ref[...]

    r = _dot(mix(0), wr_ref[...])
    k = _dot(mix(2), wk_ref[...])
    v = _dot(mix(3), wv_ref[...])
    gate_ref[0] = _dot(_sigmoid(_dot(mix(5), g1_ref[...])).astype(BF16), g2_ref[...])
    kk = k * vec[0:1]
    kk = kk * lax.rsqrt(jnp.maximum(_segsum_wide(kk * kk, ones), 1e-24))
    lw = jnp.tanh(_dot(mix(1), dw1_ref[...])).astype(BF16)
    la = _dot(mix(4), da1_ref[...]).astype(BF16)
    r_ref[0] = r
    v_ref[0] = v
    kk_ref[0] = kk
    bonus = jnp.zeros_like(r)
    lora = DECAY_LORA
    for d, (w_ref, kd_ref, bd_ref) in enumerate(((w0_ref, kd0_ref, bd0_ref), (w1_ref, kd1_ref, bd1_ref))):
        z = -(vec[3 + d:4 + d] + _dot(lw[:, d * lora:(d + 1) * lora], dw2_ref[d]))
        softplus = jnp.maximum(z, 0.0) + jnp.log(1.0 + jnp.exp(-jnp.abs(z)))
        w_ref[0] = jnp.exp(-jnp.exp(-softplus - 0.5))
        iclr = _sigmoid(vec[5 + d:6 + d] + _dot(la[:, d * lora:(d + 1) * lora], da2_ref[d]))
        kd = k * (1.0 + (iclr - 1.0) * vec[1:2])
        kd_ref[0] = kd
        bd_ref[0] = kk * iclr
        bonus = bonus + _segsum_wide(r * kd * vec[2:3], ones)
    bv_ref[0] = bonus * v


DECAY_LORA = 64


def _rwkv_proj(x, mods, g, xmix, wr, wk, wv, dw1, dw2, da1, da2, g1, g2, vec, ones):
    B, T, D = x.shape
    nt = T // TM
    row, mod, full = _row_specs(D)
    r8 = TM // 8
    prev = pl.BlockSpec((1, 8, D), lambda b, i: (b, jnp.maximum(i * r8 - 1, 0), 0))
    nxt = pl.BlockSpec((1, 8, D), lambda b, i: (b, jnp.minimum((i + 1) * r8, T // 8 - 1), 0))
    out = jax.ShapeDtypeStruct((B, T, D), F32)
    return pl.pallas_call(
        functools.partial(_rwkv_proj_kernel, nt=nt),
        out_shape=(out,) * 11,
        grid=(B, nt),
        in_specs=[row(D), prev, nxt, mod, full(g), full(xmix), full(wr), full(wk), full(wv),
                  full(dw1), full(dw2), full(da1), full(da2), full(g1), full(g2), full(vec), full(ones)],
        out_specs=(row(D),) * 11,
        compiler_params=_cparams(("parallel", "parallel")),
        name="rwkv_proj",
    )(x, x, x, mods, g, xmix, wr, wk, wv, dw1, dw2, da1, da2, g1, g2, vec, ones)


CHUNK = 4
N_HEADS = D_MODEL // HEAD_DIM
MAP_LANES = 3 * CHUNK * N_HEADS


def _coef_kernel(r_ref, kk_ref, w_ref, kd_ref, bd_ref, sel_ref, at_ref, rt_ref, bh_ref, kh_ref, gc_ref,
                 cu_ref, cy_ref, *, reverse):
    r, a, w, kd, bd = r_ref[0], -kk_ref[0], w_ref[0], kd_ref[0], bd_ref[0]
    rows = r.shape[0]
    p = lax.broadcasted_iota(jnp.int32, r.shape, 0) & (CHUNK - 1)
    s = (CHUNK - 1 - p) if reverse else p
    back = lambda x, k: pltpu.roll(x, (rows - k) if reverse else k, axis=0)
    ahead = lambda x, k: pltpu.roll(x, k if reverse else (rows - k), axis=0)
    wb = [None] + [back(w, k) for k in range(1, CHUNK)]
    excl = jnp.ones_like(w)
    rest = jnp.ones_like(w)
    for k in range(1, CHUNK):
        excl = excl * jnp.where(s >= k, wb[k], 1.0)
        rest = rest * jnp.where(s + k <= CHUNK - 1, ahead(w, k), 1.0)
    at_ref[0] = a * excl
    rt_ref[0] = r * (excl * w)
    bh_ref[0] = bd * rest
    kh_ref[0] = kd * rest
    gc_ref[0] = excl * w * rest
    between = [None, None, wb[1], wb[1] * wb[2]]
    rw = r * w
    s1 = s[:, :LANES]
    seg = lambda x: _dot(x.astype(BF16), sel_ref[...])

    def by_dist(lead, y, first):
        out = [None] * CHUNK
        for dist in range(first, CHUNK):
            if dist == 0:
                out[0] = seg(r * y)
                continue
            e = back(y, dist) if between[dist] is None else between[dist] * back(y, dist)
            out[dist] = jnp.where(s1 >= dist, seg(lead * e), 0.0)
        return out

    lab, lak = by_dist(a, bd, 1), by_dist(a, kd, 1)
    rb, rk = by_dist(rw, bd, 0), by_dist(rw, kd, 0)
    one = jnp.ones_like(rb[0])
    bk = lambda x, k: x if k == 0 else pltpu.roll(x, (rows - k) if reverse else k, axis=0)
    md = [one]
    for dist in range(1, CHUNK):
        md.append(sum(lab[e] * (bk(md[dist - e], e) if dist - e else 1.0) for e in range(1, dist + 1)))
    gd = [None] + [sum((md[e] if e else 1.0) * bk(lak[dist - e], e) for e in range(dist)) for dist in range(1, CHUNK)]
    yzd = [sum(rb[e] * (bk(md[dist - e], e) if dist - e else 1.0) for e in range(dist + 1)) for dist in range(CHUNK)]
    yvd = [rk[dist] + sum(rb[e] * bk(gd[dist - e], e) for e in range(dist)) for dist in range(CHUNK)]

    def at_pos(table, j, first):
        out = jnp.zeros_like(one)
        for dist in range(first, CHUNK - j):
            out = jnp.where(s1 == j + dist, table[dist], out)
        return out

    zero = jnp.zeros_like(one)
    u_blocks = ([at_pos(md, j, 0) for j in range(CHUNK)] + [at_pos(gd, j, 1) for j in range(CHUNK)]
                + [zero] * CHUNK)
    y_blocks = ([at_pos(yzd, j, 0) for j in range(CHUNK)] + [at_pos(yvd, j, 0) for j in range(CHUNK)]
                + [jnp.where(s1 == j, 1.0, 0.0) for j in range(CHUNK)])
    lane_blk = lax.broadcasted_iota(jnp.int32, one.shape, 1) // N_HEADS

    def place(blocks):
        per_tile = LANES // N_HEADS
        tiles = []
        for t in range(2):
            acc = zero
            for i in range(t * per_tile, min((t + 1) * per_tile, len(blocks))):
                acc = jnp.where(lane_blk == i - t * per_tile, blocks[i], acc)
            tiles.append(acc)
        return jnp.concatenate(tiles, axis=1)

    cu_ref[0] = place(u_blocks)
    cy_ref[0] = place(y_blocks)


def _rwkv_coef(r, kk, w, kd, bd, sel, reverse):
    B, T, D = r.shape
    row, _, full = _row_specs(D)
    out = jax.ShapeDtypeStruct((B, T, D), F32)
    maps = jax.ShapeDtypeStruct((B, T, 2 * LANES), F32)
    return pl.pallas_call(
        functools.partial(_coef_kernel, reverse=reverse),
        out_shape=(out,) * 5 + (maps, maps),
        grid=(B, T // TM),
        in_specs=[row(D)] * 5 + [full(sel)],
        out_specs=(row(D),) * 5 + (row(2 * LANES), row(2 * LANES)),
        compiler_params=_cparams(("parallel", "parallel")),
        name="rwkv_coef",
    )(r, kk, w, kd, bd, sel)


def _scan_kernel(atf, atb, rtf, rtb, vf, vb, bhf, bhb, khf, khb, gcf, gcb, cuf, cub, cyf, cyb,
                 mask_ref, eye_ref, e16_ref, yf, yb, st, *, tc, nb):
    n = pl.program_id(0)

    @pl.when(n == 0)
    def _():
        st[...] = jnp.zeros(st.shape, F32)

    N = HEAD_DIM
    W = CHUNK * N_HEADS
    nch = tc // CHUNK
    dirs = ((atf, rtf, vf, bhf, khf, gcf, cuf, cyf, yf), (atb, rtb, vb, bhb, khb, gcb, cub, cyb, yb))
    lane1 = lax.broadcasted_iota(jnp.int32, (N, 3 * W), 1)
    lane2 = lax.broadcasted_iota(jnp.int32, (N, 2 * W), 1)
    spread = lambda x: (mask_ref[...] * x).astype(BF16)

    def chunk(ci, carry):
        work = []
        for d, refs in enumerate(dirs):
            cc = ci if d == 0 else nch - 1 - ci
            rows = [pl.ds(cc * CHUNK + (s if d == 0 else CHUNK - 1 - s), 1) for s in range(CHUNK)]
            for b in range(nb):
                work.append((d * nb + b, b, cc, rows, refs))
        firsts = []
        for gi, b, cc, rows, (AT, RT, V, BH, KH, GC, CU, CY, Y) in work:
            lhs = jnp.concatenate([st[gi].astype(BF16), eye_ref[...]], axis=0)
            w1 = jnp.concatenate([spread(X[b, rw, :]) for X in (AT, V, RT) for rw in rows], axis=0)
            firsts.append(_dot_nt(lhs, w1))
        mids = []
        for (gi, b, cc, rows, (AT, RT, V, BH, KH, GC, CU, CY, Y)), out in zip(work, firsts):
            zvq = jnp.where((lane1 >= W) & (lane1 < 2 * W), out[N:], out[:N])
            zvq16 = zvq.astype(BF16)
            maprows = lambda M: [e16_ref[...] * M[b, rw, :MAP_LANES] for rw in rows]
            wu = jnp.concatenate(maprows(CU) + [jnp.zeros((W, MAP_LANES), F32)], axis=0)
            u = _dot_nt(zvq16, wu.astype(BF16))
            wy = jnp.concatenate(maprows(CY), axis=0)
            yt = _dot_nt(wy.astype(BF16), zvq16)
            for s, rw in enumerate(rows):
                Y[b, rw] = yt[s * N_HEADS:(s + 1) * N_HEADS][None]
            mids.append((zvq, u))
        for (gi, b, cc, rows, (AT, RT, V, BH, KH, GC, CU, CY, Y)), (zvq, u) in zip(work, mids):
            uv = jnp.where(lane2 < W, u, zvq[:, :2 * W]).astype(BF16)
            w2 = jnp.concatenate([spread(X[b, rw, :]) for X in (BH, KH) for rw in rows], axis=0)
            st[gi] = st[gi] * GC[b, rows[0], :] + _dot(uv, w2)
        return carry

    lax.fori_loop(0, nch, chunk, 0)


def _rwkv_scan(ins_f, ins_b, v, mask, eye, e16, L):
    B, T, D = v.shape
    tc = SCAN_CHUNK
    nch = tc // CHUNK
    nc, nchunks = L // tc, T // tc
    fwd_idx = lambda n: n
    rev_idx = lambda n: jnp.where(n < nc, nc - 1 - n, nchunks - 1 - (n - nc))
    tok = lambda idx, w: pl.BlockSpec((B, tc, w), lambda n: (0, idx(n), 0))
    ys = lambda idx: pl.BlockSpec((B, tc, N_HEADS, HEAD_DIM), lambda n: (0, idx(n), 0, 0))
    full = lambda a: pl.BlockSpec(a.shape, lambda n: (0,) * a.ndim)
    out = jax.ShapeDtypeStruct((B, T, N_HEADS, HEAD_DIM), F32)
    atf, rtf, bhf, khf, gcf, cuf, cyf = ins_f
    atb, rtb, bhb, khb, gcb, cub, cyb = ins_b
    f, r_ = tok(fwd_idx, D), tok(rev_idx, D)
    wy = cyf.shape[-1]
    return pl.pallas_call(
        functools.partial(_scan_kernel, tc=tc, nb=B),
        out_shape=(out, out),
        grid=(nchunks,),
        in_specs=[f, r_, f, r_, f, r_, f, r_, f, r_, f, r_, tok(fwd_idx, wy), tok(rev_idx, wy),
                  tok(fwd_idx, wy), tok(rev_idx, wy), full(mask), full(eye), full(e16)],
        out_specs=(ys(fwd_idx), ys(rev_idx)),
        scratch_shapes=[pltpu.VMEM((2 * B, HEAD_DIM, D), F32)],
        compiler_params=_cparams(("arbitrary",)),
        name="rwkv_scan",
    )(atf, atb, rtf, rtb, v, v, bhf, bhb, khf, khb, gcf, gcb, cuf, cub, cyf, cyb, mask, eye, e16)


def _rwkv_out_kernel(yf_ref, yb_ref, bv_ref, gate_ref, ln_ref, wo_ref, ones_ref, x_ref, mod_ref, g_ref,
                     rwh_ref, rwl_ref, xo_ref, h_ref, lg_ref):
    ones = ones_ref[...]
    y = yf_ref[0] + yb_ref[0]
    inv = 1.0 / HEAD_DIM
    dlt = y - _segsum_wide(y, ones) * inv
    yn = dlt * lax.rsqrt(_segsum_wide(dlt * dlt, ones) * inv + GN_EPS)
    ln = ln_ref[...]
    o = (yn * ln[0:1] + ln[1:2] + bv_ref[0]) * gate_ref[0]
    yl = _dot(o.astype(BF16), wo_ref[...])
    _ffn_prep(x_ref[0], yl, mod_ref[0, 0], g_ref[...], rwh_ref, rwl_ref, xo_ref, h_ref, lg_ref)


def _rwkv_out(yf, yb, bv, gate, ln, wo, ones, x, mods, g, rwh, rwl):
    B, T, D = x.shape
    row, mod, full = _row_specs(D)
    shapes, specs = _ffn_prep_outs(B, T, D)
    return pl.pallas_call(
        _rwkv_out_kernel,
        out_shape=shapes,
        grid=(B, T // TM),
        in_specs=[row(D), row(D), row(D), row(D), full(ln), full(wo), full(ones), row(D), mod, full(g),
                  full(rwh), full(rwl)],
        out_specs=specs,
        compiler_params=_cparams(("parallel", "parallel")),
        name="rwkv_out",
    )(yf, yb, bv, gate, ln, wo, ones, x, mods, g, rwh, rwl)


def _rope_tables(S, L):
    rows = S // GRID_W
    row = jnp.repeat(jnp.arange(rows, dtype=F32), GRID_W)
    col = (jnp.arange(rows * GRID_W) % GRID_W).astype(F32)
    n_freq = HEAD_DIM // 4
    inv = ROPE_THETA ** (-jnp.arange(n_freq, dtype=F32) / n_freq)
    lane = np.arange(LANES) % HEAD_DIM
    axis, half, freq = lane // 32, (lane % 32) // 16, lane % 16
    pos = jnp.where(jnp.asarray(axis == 0)[None, :], row[:, None], col[:, None])
    ang = pos * inv[freq][None, :]
    sgn = jnp.asarray(np.where(half == 0, -1.0, 1.0), dtype=F32)
    cos = jnp.concatenate([jnp.ones((L, LANES), F32), jnp.cos(ang)], axis=0)
    sin = jnp.concatenate([jnp.zeros((L, LANES), F32), jnp.sin(ang) * sgn[None, :]], axis=0)
    return cos, sin


def kernel(x, c, ctx, c_ctx, ada_w, ada_b, norm_mix_g, norm_ffn_g, attn_w_in, attn_w_out, attn_sink,
           attn_q_norm_g, attn_k_norm_g, rwkv_x_mix, rwkv_w_r, rwkv_w_k, rwkv_w_v, rwkv_w_o,
           rwkv_decay_w0, rwkv_decay_w1, rwkv_decay_w2, rwkv_iclr_a0, rwkv_iclr_a1, rwkv_iclr_a2,
           rwkv_gate_g1, rwkv_gate_g2, rwkv_k_k, rwkv_k_a, rwkv_r_k, rwkv_ln_g, rwkv_ln_b,
           router_w, router_bias, moe_w1, moe_w3, moe_w2, final_norm_g):
    B, S, D = x.shape
    L = ctx.shape[1]
    T = L + S
    depth = ada_w.shape[0]
    assert D == D_MODEL and L == TM and S % TM == 0 and B == 2 and depth == 2
    ones = _seg_ones()
    bf = lambda a: a.astype(BF16)

    cs = jnp.zeros((8, D), F32).at[:B].set(c).at[B].set(c_ctx)
    ada = _ada(cs, ada_w, ada_b).reshape(depth, 8, 6, D)
    mods = [jnp.stack([jnp.broadcast_to(ada[i, B], (B, 6, D)), ada[i, :B]], axis=1) for i in range(depth)]

    xa = jnp.concatenate([ctx, x], axis=1)
    rw = jnp.zeros((D, LANES), F32).at[:, :N_EXPERTS].set(router_w)
    rwh, rwl = _split(rw)

    w_in = attn_w_in[0]
    roped = np.concatenate([np.arange(0, 640), np.arange(768, 1408)])
    w_rot = w_in[:, roped ^ 16]
    cos, sin = _rope_tables(S, L)
    lane = np.arange(LANES) % HEAD_DIM
    gains = lambda g: jnp.stack([g[lane], g[lane ^ 16]], axis=0)
    qa, ka, va, qb, kb, vb_x = _inproj(xa, mods[0], norm_mix_g[0].reshape(1, D), bf(w_in), bf(w_rot), cos, sin,
                                       gains(attn_q_norm_g[0]), gains(attn_k_norm_g[0]), ones)
    grouped = lambda q: q.reshape(B, A_KV_HEADS, GROUP, T, HEAD_DIM)
    qa, qb = grouped(qa), grouped(qb)
    sink = attn_sink[0].astype(F32) * LOG2E
    vc = va[:, :, :L]
    va_x = jnp.swapaxes(jnp.concatenate(
        [vc, jnp.ones_like(vc[..., :1]), jnp.zeros_like(vc[..., :VT_ROWS - HEAD_DIM - 1])], axis=-1), 2, 3)
    nosink = jnp.full((B_Q_HEADS,), NEG, F32)
    oa_l = _window_attn(sink, qa, ka, va, L, S)
    oa_c = _flash(sink, qa, ka, va_x, q_rows=L, q_off=0, k_rows=L, tq=L, tk=L)
    ob_l = _flash(nosink, qb, kb, vb_x, q_rows=S, q_off=L, k_rows=T, tq=256, tk=_key_tile(T))
    ob_c = _flash(nosink, qb, kb, vb_x, q_rows=L, q_off=0, k_rows=L, tq=L, tk=L)
    oa = jnp.concatenate([oa_c, oa_l], axis=1)
    ob = jnp.concatenate([ob_c, ob_l], axis=1)
    w_out = bf(attn_w_out[0])
    na = A_Q_HEADS * HEAD_DIM
    xa, h, lg = _attn_out(oa, ob, w_out[:na], w_out[na:], xa, mods[0], norm_ffn_g[0].reshape(1, D), rwh, rwl)
    y0, y1, gw = _moe(h.reshape(B * T, D), lg.reshape(B * T, LANES), router_bias,
                      moe_w1, moe_w3, moe_w2, 0)
    xa = _residual(xa, y0.reshape(B, T, D), y1.reshape(B, T, D), gw.reshape(B, T, TOP_K), mods[0])

    cat2 = lambda a: jnp.concatenate([a[0], a[1]], axis=1)
    vec = jnp.stack([rwkv_k_k[0], rwkv_k_a[0], rwkv_r_k[0].reshape(D), rwkv_decay_w0[0, 0], rwkv_decay_w0[0, 1],
                     rwkv_iclr_a0[0, 0], rwkv_iclr_a0[0, 1], jnp.zeros((D,), F32)], axis=0)
    outs = _rwkv_proj(xa, mods[1], norm_mix_g[1].reshape(1, D), jnp.pad(rwkv_x_mix[0], ((0, 2), (0, 0))),
                      bf(rwkv_w_r[0]), bf(rwkv_w_k[0]), bf(rwkv_w_v[0]),
                      bf(cat2(rwkv_decay_w1[0])), bf(rwkv_decay_w2[0]),
                      bf(cat2(rwkv_iclr_a1[0])), bf(rwkv_iclr_a2[0]),
                      bf(rwkv_gate_g1[0]), bf(rwkv_gate_g2[0]), vec, ones)
    r, v, kk, bv, gate, w0, w1, kd0, kd1, bd0, bd1 = outs
    lane_id = np.arange(D)
    eye = jnp.asarray(np.arange(HEAD_DIM)[:, None] == (lane_id % HEAD_DIM)[None, :], dtype=BF16)
    n_heads = D // HEAD_DIM
    head_mask = jnp.asarray(np.arange(n_heads)[:, None] == (lane_id // HEAD_DIM)[None, :], dtype=F32)
    sel = jnp.asarray((lane_id // HEAD_DIM)[:, None] == (np.arange(LANES) % n_heads)[None, :], dtype=BF16)
    e16 = jnp.asarray(np.arange(n_heads)[:, None] == (np.arange(MAP_LANES) % n_heads)[None, :], dtype=F32)
    scan_ins = [_rwkv_coef(r, kk, w_d, kd_d, bd_d, sel, reverse=d == 1)
                for d, (w_d, kd_d, bd_d) in enumerate(((w0, kd0, bd0), (w1, kd1, bd1)))]
    yf, yb = _rwkv_scan(scan_ins[0], scan_ins[1], v, head_mask, eye, e16, L)
    ln = jnp.stack([rwkv_ln_g[0], rwkv_ln_b[0]] + [jnp.zeros((D,), F32)] * 6, axis=0)
    xa, h, lg = _rwkv_out(yf.reshape(B, T, D), yb.reshape(B, T, D), bv, gate, ln, bf(rwkv_w_o[0]), ones, xa, mods[1],
                          norm_ffn_g[1].reshape(1, D), rwh, rwl)
    y0, y1, gw = _moe(h[:, L:].reshape(B * S, D), lg[:, L:].reshape(B * S, LANES), router_bias,
                      moe_w1, moe_w3, moe_w2, 1)
    return _final(xa, y0.reshape(B, S, D), y1.reshape(B, S, D), gw.reshape(B, S, TOP_K), mods[1],
                  final_norm_g.reshape(1, D), L)


def _key_tile(T):
    for tk in (1280, 1024, 768, 512, 256):
        if T % tk == 0:
            return tk
    raise ValueError(T)
```

```python
import functools

import numpy as np
import jax
import jax.numpy as jnp
from jax import lax
from jax.experimental import pallas as pl
from jax.experimental.pallas import tpu as pltpu

F32 = jnp.float32
BF16 = jnp.bfloat16

D_MODEL = 1024
HEAD_DIM = 64
GRID_W = 64
ROPE_THETA = 10000.0
RMS_EPS = 1e-6
GN_EPS = 64e-5
A_Q_HEADS = 8
A_KV_HEADS = 2
B_Q_HEADS = 8
B_KV_HEADS = 2
GROUP = 4
WINDOW = 128
N_EXPERTS = 16
N_GROUPS = 4
EXPERTS_PER_GROUP = 4
TOP_K = 2
LANES = 128
TM = 256
MOE_ROWS = 512
SCAN_CHUNK = 64
VMEM_LIMIT = 56 * 1024 * 1024
NEG = -1e30
LOG2E = 1.4426950408889634


def _cparams(sem):
    return pltpu.CompilerParams(dimension_semantics=sem, vmem_limit_bytes=VMEM_LIMIT)


def _dot(a, b):
    return jnp.dot(a, b, preferred_element_type=F32)


def _dot_nt(a, b):
    return lax.dot_general(a, b, (((1,), (1,)), ((), ())), preferred_element_type=F32)


def _split(x):
    hi = x.astype(BF16)
    lo = (x - hi.astype(F32)).astype(BF16)
    return hi, lo


def _dot3(x, w):
    xh, xl = _split(x)
    wh, wl = _split(w)
    return _dot(xh, wh) + _dot(xh, wl) + _dot(xl, wh)


def _segsum(v, ones):
    hi, lo = _split(v)
    return _dot(hi, ones) + _dot(lo, ones)


def _segsum_wide(v, ones):
    n = v.shape[1] // LANES
    return jnp.concatenate([_segsum(v[:, j * LANES:(j + 1) * LANES], ones) for j in range(n)], axis=1)


def _norm_mod(x, g, shift, scale):
    ms = jnp.mean(x * x, axis=-1, keepdims=True)
    return (x * lax.rsqrt(ms + RMS_EPS) * g) * (1.0 + scale) + shift


def _sigmoid(x):
    return 1.0 / (1.0 + jnp.exp(-x))


def _seg_ones():
    i = np.arange(LANES)
    return jnp.asarray((i[:, None] // HEAD_DIM) == (i[None, :] // HEAD_DIM), dtype=BF16)


def _ada_kernel(c_ref, w_ref, b_ref, o_ref):
    c = c_ref[...]
    s = c * _sigmoid(c)
    o_ref[0] = _dot3(s, w_ref[0]) + b_ref[0]


def _ada(cs, ada_w, ada_b):
    depth, d, n = ada_w.shape
    tn = 1536
    return pl.pallas_call(
        _ada_kernel,
        out_shape=jax.ShapeDtypeStruct((depth, 8, n), F32),
        grid=(depth, n // tn),
        in_specs=[
            pl.BlockSpec((8, d), lambda l, j: (0, 0)),
            pl.BlockSpec((1, d, tn), lambda l, j: (l, 0, j)),
            pl.BlockSpec((1, 1, tn), lambda l, j: (l, 0, j)),
        ],
        out_specs=pl.BlockSpec((1, 8, tn), lambda l, j: (l, 0, j)),
        compiler_params=_cparams(("arbitrary", "arbitrary")),
        name="ada",
    )(cs, ada_w, ada_b.reshape(depth, 1, n))


def _seg_specs(w):
    ctx = pl.BlockSpec((1, TM, w), lambda b, i: (b, 0, 0))
    lat = pl.BlockSpec((1, TM, w), lambda b, i: (b, jnp.maximum(i - 1, 0), 0))
    return ctx, lat


def _seg_tile(c_ref, l_ref):
    return jnp.where(pl.program_id(1) == 0, c_ref[0], l_ref[0])


def _inproj_kernel(xc_ref, xl_ref, mod_ref, g_ref, w_ref, wrot_ref, cos_ref, sin_ref, gq_ref, gk_ref, ones_ref,
                   qa_ref, ka_ref, va_ref, qb_ref, kb_ref, vb_ref):
    mod = mod_ref[0, 0]
    h = _norm_mod(_seg_tile(xc_ref, xl_ref), g_ref[...], mod[0:1], mod[1:2]).astype(BF16)
    y = _dot(h, w_ref[...])
    yr = _dot(h, wrot_ref[...])
    cos = cos_ref[...]
    sin = sin_ref[...]
    ones = ones_ref[...]
    qscale = HEAD_DIM ** -0.5 * LOG2E

    def put(ref, tile, val):
        ref[0, 2 * tile] = val[:, :HEAD_DIM].astype(ref.dtype)
        ref[0, 2 * tile + 1] = val[:, HEAD_DIM:].astype(ref.dtype)

    def chunk(a, c):
        return a[:, c * LANES:(c + 1) * LANES]

    for c in range(4):
        put(qa_ref, c, (chunk(y, c) * cos + chunk(yr, c) * sin) * qscale)
    put(ka_ref, 0, chunk(y, 4) * cos + chunk(yr, 4) * sin)
    put(va_ref, 0, chunk(y, 5))

    def normed(c, cr, gain_ref):
        v = chunk(y, c)
        rs = lax.rsqrt(_segsum(v * v, ones) * (1.0 / HEAD_DIM) + RMS_EPS)
        return (v * rs * gain_ref[0:1]) * cos + (chunk(yr, cr) * rs * gain_ref[1:2]) * sin

    for c in range(4):
        put(qb_ref, c, normed(6 + c, 5 + c, gq_ref) * qscale)
    put(kb_ref, 0, normed(10, 9, gk_ref))
    vt = chunk(y, 11).T
    tail = jnp.where(lax.broadcasted_iota(jnp.int32, (VT_ROWS - HEAD_DIM, vt.shape[1]), 0) == 0, 1.0, 0.0)
    for hh in range(B_KV_HEADS):
        tile = jnp.concatenate([vt[hh * HEAD_DIM:(hh + 1) * HEAD_DIM], tail], axis=0)
        vb_ref[0, hh] = tile.astype(vb_ref.dtype)


def _inproj(xc, xl, mods, g, w_in, w_rot, cos, sin, gq2, gk2, ones):
    B, S, D = xl.shape
    T = xc.shape[1] + S
    nt = T // TM
    heads = lambda n: jax.ShapeDtypeStruct((B, n, T, HEAD_DIM), BF16)
    hspec = lambda n: pl.BlockSpec((1, n, TM, HEAD_DIM), lambda b, i: (b, 0, i, 0))
    full = lambda a: pl.BlockSpec(a.shape, lambda b, i: (0,) * a.ndim)
    vt_shape = jax.ShapeDtypeStruct((B, B_KV_HEADS, VT_ROWS, T), BF16)
    vt_spec = pl.BlockSpec((1, B_KV_HEADS, VT_ROWS, TM), lambda b, i: (b, 0, 0, i))
    return pl.pallas_call(
        _inproj_kernel,
        out_shape=(heads(8), heads(2), heads(2), heads(8), heads(2), vt_shape),
        grid=(B, nt),
        in_specs=[
            *_seg_specs(D),
            pl.BlockSpec((1, 1, 6, D), lambda b, i: (b, jnp.minimum(i, 1), 0, 0)),
            full(g), full(w_in), full(w_rot),
            pl.BlockSpec((TM, LANES), lambda b, i: (i, 0)),
            pl.BlockSpec((TM, LANES), lambda b, i: (i, 0)),
            full(gq2), full(gk2), full(ones),
        ],
        out_specs=(hspec(8), hspec(2), hspec(2), hspec(8), hspec(2), vt_spec),
        compiler_params=_cparams(("parallel", "parallel")),
        name="attn_inproj",
    )(xc, xl, mods, g, w_in, w_rot, cos, sin, gq2, gk2, ones)


LOOKAHEAD = 3
VT_ROWS = 80


def _flash_kernel(sink_ref, q_ref, k_ref, v_ref, o_ref, m_scr, acc_scr, s_scr, *, tk, nk):
    h = pl.program_id(1)
    m_scr[...] = jnp.full(m_scr.shape, NEG, F32)
    acc_scr[...] = jnp.zeros(acc_scr.shape, F32)

    def scores(j, g):
        return _dot_nt(k_ref[0, 0, pl.ds(pl.multiple_of(j * tk, tk), tk), :], q_ref[0, 0, g])

    for g in range(LOOKAHEAD):
        s_scr[g] = scores(0, g)

    def body(j, carry):
        vt = v_ref[0, 0, :, pl.ds(pl.multiple_of(j * tk, tk), tk)]
        jn = jnp.minimum(j + 1, nk - 1)
        ahead = {}
        for g in range(GROUP):
            st = s_scr[g] if g < LOOKAHEAD else ahead.pop(g)
            if g + LOOKAHEAD < GROUP:
                ahead[g + LOOKAHEAD] = scores(j, g + LOOKAHEAD)
            m_prev = m_scr[g]
            m_new = jnp.maximum(m_prev, jnp.max(st, axis=0, keepdims=True))
            p = jnp.exp2((st - m_new).astype(BF16))
            if g + LOOKAHEAD >= GROUP:
                s_scr[g + LOOKAHEAD - GROUP] = scores(jn, g + LOOKAHEAD - GROUP)
            acc_scr[g] = jnp.exp2(m_prev - m_new) * acc_scr[g] + _dot(vt, p)
            m_scr[g] = m_new
        return carry

    lax.fori_loop(0, nk, body, 0)
    outs = []
    for g in range(GROUP):
        acc = acc_scr[g]
        l = acc[HEAD_DIM:HEAD_DIM + 1] + jnp.exp2(sink_ref[h * GROUP + g] - m_scr[g])
        outs.append(acc[:HEAD_DIM] / l)
    o_ref[0] = jnp.concatenate(outs, axis=0).T.astype(o_ref.dtype)


def _flash(sink, q, k, v, *, q_rows, q_off, k_rows, tq, tk):
    B, Hkv = k.shape[:2]
    nq, nk = q_rows // tq, k_rows // tk
    qo = q_off // tq
    return pl.pallas_call(
        functools.partial(_flash_kernel, tk=tk, nk=nk),
        out_shape=jax.ShapeDtypeStruct((B, q_rows, Hkv * GROUP * HEAD_DIM), BF16),
        grid=(B, Hkv, nq),
        in_specs=[
            pl.BlockSpec(memory_space=pltpu.SMEM),
            pl.BlockSpec((1, 1, GROUP, tq, HEAD_DIM), lambda b, h, i: (b, h, 0, i + qo, 0)),
            pl.BlockSpec((1, 1, k_rows, HEAD_DIM), lambda b, h, i: (b, h, 0, 0)),
            pl.BlockSpec((1, 1, VT_ROWS, k_rows), lambda b, h, i: (b, h, 0, 0)),
        ],
        out_specs=pl.BlockSpec((1, tq, GROUP * HEAD_DIM), lambda b, h, i: (b, i, h)),
        scratch_shapes=[
            pltpu.VMEM((GROUP, 1, tq), F32),
            pltpu.VMEM((GROUP, VT_ROWS, tq), F32),
            pltpu.VMEM((LOOKAHEAD, tk, tq), F32),
        ],
        compiler_params=_cparams(("parallel", "parallel", "arbitrary")),
        name="flash_attn",
    )(sink, q, k, v)


def _window_kernel(sink_ref, q_ref, kc_ref, vc_ref, k0_ref, k1_ref, k2_ref, v0_ref, v1_ref, v2_ref, o_ref, *, nb):
    h = pl.program_id(1)
    i = pl.program_id(2)
    rows = GROUP * WINDOW
    q = q_ref[0, 0].reshape(rows, HEAD_DIM)
    r = lax.broadcasted_iota(jnp.int32, (rows, WINDOW), 0) & (WINDOW - 1)
    c = lax.broadcasted_iota(jnp.int32, (rows, WINDOW), 1)
    sc = _dot_nt(q, kc_ref[0, 0])
    s0 = jnp.where((c >= r) & (i > 0), _dot_nt(q, k0_ref[0, 0]), NEG)
    s1 = _dot_nt(q, k1_ref[0, 0])
    s2 = jnp.where((c <= r) & (i < nb - 1), _dot_nt(q, k2_ref[0, 0]), NEG)
    sink = jnp.concatenate(
        [jnp.full((WINDOW, 1), sink_ref[h * GROUP + g], F32) for g in range(GROUP)], axis=0)
    rowmax = lambda s: jnp.max(s, axis=-1, keepdims=True)
    m = jnp.maximum(jnp.maximum(rowmax(sc), rowmax(s0)), jnp.maximum(rowmax(s1), rowmax(s2)))
    m = jnp.maximum(m, sink)
    pc, p0, p1, p2 = (jnp.exp2(s - m) for s in (sc, s0, s1, s2))
    rowsum = lambda p: jnp.sum(p, axis=-1, keepdims=True)
    l = rowsum(pc) + rowsum(p0) + rowsum(p1) + rowsum(p2) + jnp.exp2(sink - m)
    acc = (_dot(pc.astype(BF16), vc_ref[0, 0]) + _dot(p0.astype(BF16), v0_ref[0, 0])
           + _dot(p1.astype(BF16), v1_ref[0, 0]) + _dot(p2.astype(BF16), v2_ref[0, 0]))
    out = acc / l
    for g in range(GROUP):
        o_ref[0, :, g * HEAD_DIM:(g + 1) * HEAD_DIM] = out[g * WINDOW:(g + 1) * WINDOW].astype(o_ref.dtype)


def _window_attn(sink, q, k, v, L, S):
    B, Hkv = k.shape[:2]
    nb = S // WINDOW
    pad = ((0, 0), (0, 0), (WINDOW, WINDOW), (0, 0))
    kp = jnp.pad(k[:, :, L:], pad)
    vp = jnp.pad(v[:, :, L:], pad)
    qo = L // WINDOW
    band = lambda j: pl.BlockSpec((1, 1, WINDOW, HEAD_DIM), lambda b, h, i: (b, h, i + j, 0))
    ctx = pl.BlockSpec((1, 1, L, HEAD_DIM), lambda b, h, i: (b, h, 0, 0))
    return pl.pallas_call(
        functools.partial(_window_kernel, nb=nb),
        out_shape=jax.ShapeDtypeStruct((B, S, Hkv * GROUP * HEAD_DIM), BF16),
        grid=(B, Hkv, nb),
        in_specs=[
            pl.BlockSpec(memory_space=pltpu.SMEM),
            pl.BlockSpec((1, 1, GROUP, WINDOW, HEAD_DIM), lambda b, h, i: (b, h, 0, i + qo, 0)),
            ctx, ctx, band(0), band(1), band(2), band(0), band(1), band(2),
        ],
        out_specs=pl.BlockSpec((1, WINDOW, GROUP * HEAD_DIM), lambda b, h, i: (b, i, h)),
        compiler_params=_cparams(("parallel", "parallel", "parallel")),
        name="window_attn",
    )(sink, q, k, v, kp, kp, kp, vp, vp, vp)


def _ffn_prep(x, y, mod, gffn, rwh_ref, rwl_ref, xo_ref, h_ref, lg_ref):
    xn = x + mod[2:3] * y
    h = _norm_mod(xn, gffn, mod[3:4], mod[4:5])
    xo_ref[0] = xn
    hh, hl = _split(h)
    h_ref[0] = hh
    rwh = rwh_ref[...]
    lg_ref[0] = _dot(hh, rwh) + _dot(hl, rwh) + _dot(hh, rwl_ref[...])


def _attn_out_kernel(oac_ref, oal_ref, obc_ref, obl_ref, wa_ref, wb_ref, xc_ref, xl_ref, mod_ref, g_ref,
                     rwh_ref, rwl_ref, xo_ref, h_ref, lg_ref):
    y = _dot(_seg_tile(oac_ref, oal_ref), wa_ref[...]) + _dot(_seg_tile(obc_ref, obl_ref), wb_ref[...])
    _ffn_prep(_seg_tile(xc_ref, xl_ref), y, mod_ref[0, 0], g_ref[...], rwh_ref, rwl_ref, xo_ref, h_ref, lg_ref)


def _row_specs(D):
    row = lambda w: pl.BlockSpec((1, TM, w), lambda b, i: (b, i, 0))
    mod = pl.BlockSpec((1, 1, 6, D), lambda b, i: (b, jnp.minimum(i, 1), 0, 0))
    full = lambda a: pl.BlockSpec(a.shape, lambda b, i: (0,) * a.ndim)
    return row, mod, full


def _ffn_prep_outs(B, T, D):
    row, _, _ = _row_specs(D)
    shapes = (jax.ShapeDtypeStruct((B, T, D), F32), jax.ShapeDtypeStruct((B, T, D), BF16),
              jax.ShapeDtypeStruct((B, T, LANES), F32))
    return shapes, (row(D), row(D), row(LANES))


def _attn_out(oac, oal, obc, obl, wa, wb, xc, xl, mods, g, rwh, rwl):
    B, S, D = xl.shape
    T = xc.shape[1] + S
    _, mod, full = _row_specs(D)
    shapes, specs = _ffn_prep_outs(B, T, D)
    return pl.pallas_call(
        _attn_out_kernel,
        out_shape=shapes,
        grid=(B, T // TM),
        in_specs=[*_seg_specs(oac.shape[-1]), *_seg_specs(obc.shape[-1]), full(wa), full(wb), *_seg_specs(D),
                  mod, full(g), full(rwh), full(rwl)],
        out_specs=specs,
        compiler_params=_cparams(("parallel", "parallel")),
        name="attn_out",
    )(oac, oal, obc, obl, wa, wb, xc, xl, mods, g, rwh, rwl)


def _gmm_kernel(be_ref, nu_ref, x_ref, w1_ref, w3_ref, w2_ref, o_ref):
    i = pl.program_id(0)

    @pl.when(i < nu_ref[0])
    def _():
        x = x_ref[...]
        a = _dot(x, w1_ref[0, 0].astype(BF16))
        b = _dot(x, w3_ref[0, 0].astype(BF16))
        mid = (a * _sigmoid(a)) * b
        o_ref[...] = _dot(mid.astype(BF16), w2_ref[0, 0].astype(BF16)).astype(o_ref.dtype)

    @pl.when(i >= nu_ref[0])
    def _():
        o_ref[...] = jnp.zeros(o_ref.shape, o_ref.dtype)


def _gmm(block_expert, n_used, xs, w1, w3, w2, layer):
    n_slots, D = xs.shape
    F = w1.shape[-1]
    nblk = n_slots // MOE_ROWS
    return pl.pallas_call(
        _gmm_kernel,
        out_shape=jax.ShapeDtypeStruct((n_slots, D), BF16),
        grid_spec=pltpu.PrefetchScalarGridSpec(
            num_scalar_prefetch=2,
            grid=(nblk,),
            in_specs=[
                pl.BlockSpec((MOE_ROWS, D), lambda i, be, nu: (i, 0)),
                pl.BlockSpec((1, 1, D, F), lambda i, be, nu: (layer, be[i], 0, 0)),
                pl.BlockSpec((1, 1, D, F), lambda i, be, nu: (layer, be[i], 0, 0)),
                pl.BlockSpec((1, 1, F, D), lambda i, be, nu: (layer, be[i], 0, 0)),
            ],
            out_specs=pl.BlockSpec((MOE_ROWS, D), lambda i, be, nu: (i, 0)),
        ),
        compiler_params=_cparams(("arbitrary",)),
        name="moe_gmm",
    )(block_expert, n_used, xs, w1, w3, w2)


ROUTE_ROWS = 512


def _route_kernel(lg_ref, bias_ref, idx_ref, w_ref):
    x = lg_ref[...].T[:N_EXPERTS]
    m = jnp.max(x, axis=0, keepdims=True)
    e = jnp.exp(x - m)
    probs = e / jnp.sum(e, axis=0, keepdims=True)
    sel = probs + bias_ref[...][:, 0:1]
    row = lambda a, i: a[i:i + 1, :]
    G = EXPERTS_PER_GROUP
    scores = []
    for g in range(N_GROUPS):
        s = [row(sel, g * G + i) for i in range(G)]
        best = None
        for i in range(G):
            for j in range(i + 1, G):
                best = s[i] + s[j] if best is None else jnp.maximum(best, s[i] + s[j])
        scores.append(best)
    top = functools.reduce(jnp.maximum, scores)
    gi = jnp.full(top.shape, N_GROUPS - 1, jnp.int32)
    for g in range(N_GROUPS - 2, -1, -1):
        gi = jnp.where(scores[g] == top, g, gi)

    def pick(a, i):
        out = row(a, (N_GROUPS - 1) * G + i)
        for g in range(N_GROUPS - 2, -1, -1):
            out = jnp.where(gi == g, row(a, g * G + i), out)
        return out

    c = [pick(sel, i) for i in range(G)]
    pc = [pick(probs, i) for i in range(G)]

    def first_argmax(vals):
        mx = functools.reduce(jnp.maximum, vals)
        idx = jnp.full(mx.shape, G - 1, jnp.int32)
        for i in range(G - 2, -1, -1):
            idx = jnp.where(vals[i] == mx, i, idx)
        return idx

    i1 = first_argmax(c)
    i2 = first_argmax([jnp.where(i1 == i, -jnp.inf, c[i]) for i in range(G)])
    take = lambda vals, idx: functools.reduce(
        lambda acc, i: jnp.where(idx == i, vals[i], acc), range(G - 2, -1, -1), vals[G - 1])
    w1, w2 = take(pc, i1), take(pc, i2)
    tot = w1 + w2
    zi = jnp.zeros((6,) + top.shape[1:], jnp.int32)
    idx_ref[...] = jnp.concatenate([gi * G + i1, gi * G + i2, zi], axis=0)
    w_ref[...] = jnp.concatenate([w1 / tot, w2 / tot, zi.astype(F32)], axis=0)


def _route(logits, router_bias):
    N = logits.shape[0]
    bias = jnp.broadcast_to(router_bias.astype(F32)[:, None], (N_EXPERTS, LANES))
    idx, w = pl.pallas_call(
        _route_kernel,
        out_shape=(jax.ShapeDtypeStruct((8, N), jnp.int32), jax.ShapeDtypeStruct((8, N), F32)),
        grid=(N // ROUTE_ROWS,),
        in_specs=[pl.BlockSpec((ROUTE_ROWS, LANES), lambda i: (i, 0)),
                  pl.BlockSpec((N_EXPERTS, LANES), lambda i: (0, 0))],
        out_specs=(pl.BlockSpec((8, ROUTE_ROWS), lambda i: (0, i)),
                   pl.BlockSpec((8, ROUTE_ROWS), lambda i: (0, i))),
        compiler_params=_cparams(("parallel",)),
        name="route",
    )(logits, bias)
    return idx[:TOP_K], w[:TOP_K]


def _moe(h, logits, router_bias, w1, w3, w2, layer):
    N, D = h.shape
    expert_idx, gate_w = _route(logits, router_bias)
    NK = N * TOP_K
    flat_e = expert_idx.reshape(NK)
    onehot = (flat_e[None, :] == jnp.arange(N_EXPERTS, dtype=jnp.int32)[:, None]).astype(jnp.int32)
    csum = jnp.cumsum(onehot, axis=1)
    counts = csum[:, -1]
    padded = (counts + MOE_ROWS - 1) // MOE_ROWS * MOE_ROWS
    pad_end = jnp.cumsum(padded)
    pad_start = pad_end - padded
    dest = jnp.sum(onehot * (csum - 1 + pad_start[:, None]), axis=0)
    nblk = -(-NK // MOE_ROWS) + N_EXPERTS
    n_slots = nblk * MOE_ROWS
    n_used = (pad_end[-1] // MOE_ROWS).astype(jnp.int32)
    blk = jnp.arange(nblk, dtype=jnp.int32)
    be = jnp.sum((pad_end[None, :] <= (blk * MOE_ROWS)[:, None]).astype(jnp.int32), axis=1)
    be = jnp.minimum(be, N_EXPERTS - 1)
    be = jnp.where(blk < n_used, be, be[jnp.maximum(n_used - 1, 0)])
    flat_tok = jnp.arange(NK, dtype=jnp.int32) % N
    slot_tok = jnp.zeros((n_slots,), jnp.int32).at[dest].set(flat_tok, unique_indices=True)
    xs = h[slot_tok]
    ys = _gmm(be, n_used.reshape(1), xs, w1, w3, w2, layer)
    return ys[dest[:N]], ys[dest[N:]], gate_w.T


def _combine(x_ref, y0_ref, y1_ref, gw_ref, mod_ref):
    gw = gw_ref[0]
    f = y0_ref[0].astype(F32) * gw[:, 0:1] + y1_ref[0].astype(F32) * gw[:, 1:2]
    return x_ref[0] + mod_ref[0, 0][5:6] * f


def _residual_kernel(x_ref, y0_ref, y1_ref, gw_ref, mod_ref, o_ref):
    o_ref[0] = _combine(x_ref, y0_ref, y1_ref, gw_ref, mod_ref)


def _final_kernel(x_ref, y0_ref, y1_ref, gw_ref, mod_ref, g_ref, o_ref):
    x = _combine(x_ref, y0_ref, y1_ref, gw_ref, mod_ref)
    ms = jnp.mean(x * x, axis=-1, keepdims=True)
    o_ref[0] = x * lax.rsqrt(ms + RMS_EPS) * g_ref[...]


def _residual(x, y0, y1, gw, mods):
    B, T, D = x.shape
    row, mod, _ = _row_specs(D)
    return pl.pallas_call(
        _residual_kernel,
        out_shape=jax.ShapeDtypeStruct((B, T, D), F32),
        grid=(B, T // TM),
        in_specs=[row(D), row(D), row(D), row(TOP_K), mod],
        out_specs=row(D),
        compiler_params=_cparams(("parallel", "parallel")),
        name="residual",
    )(x, y0, y1, gw, mods)


def _final(x, y0, y1, gw, mods, g, L):
    B, S, D = y0.shape
    off = L // TM
    mod = pl.BlockSpec((1, 1, 6, D), lambda b, i: (b, 1, 0, 0))
    lat = lambda w: pl.BlockSpec((1, TM, w), lambda b, i: (b, i, 0))
    return pl.pallas_call(
        _final_kernel,
        out_shape=jax.ShapeDtypeStruct((B, S, D), F32),
        grid=(B, S // TM),
        in_specs=[pl.BlockSpec((1, TM, D), lambda b, i: (b, i + off, 0)), lat(D), lat(D), lat(TOP_K), mod,
                  pl.BlockSpec(g.shape, lambda b, i: (0, 0))],
        out_specs=lat(D),
        compiler_params=_cparams(("parallel", "parallel")),
        name="final_norm",
    )(x, y0, y1, gw, mods, g)


def _rwkv_proj_kernel(x_ref, xp_ref, xn_ref, mod_ref, g_ref, xmix_ref, wr_ref, wk_ref, wv_ref,
                      dw1_ref, dw2_ref, da1_ref, da2_ref, g1_ref, g2_ref, vec_ref, ones_ref,
                      r_ref, v_ref, kk_ref, bv_ref, gate_ref, w0_ref, w1_ref, kd0_ref, kd1_ref, bd0_ref, bd1_ref,
                      *, nt):
    i = pl.program_id(1)
    mod = mod_ref[0, 0]
    g = g_ref[...]
    nm = lambda x: _norm_mod(x, g, mod[0:1], mod[1:2])
    h = nm(x_ref[0])
    hp = nm(xp_ref[0])[7:8] * jnp.where(i >= 2, 1.0, 0.0)
    hn = nm(xn_ref[0])[0:1] * jnp.where((i >= 1) & (i < nt - 1), 1.0, 0.0)
    ridx = lax.broadcasted_iota(jnp.int32, h.shape, 0)
    h_dn = jnp.where(ridx == 0, hp, pltpu.roll(h, 1, axis=0))
    h_up = jnp.where(ridx == TM - 1, hn, pltpu.roll(h, TM - 1, axis=0))
    xx = 0.5 * (h_dn + h_up) - h
    xmix = xmix_ref[...]
    mix = lambda j: (h + xx * xmix[j:j + 1]).astype(BF16)
    vec = vec_ref[...]
    ones = ones_ref[...]

    r = _dot(mix(0), wr_ref[...])
    k = _dot(mix(2), wk_ref[...])
    v = _dot(mix(3), wv_ref[...])
    gate_ref[0] = _dot(_sigmoid(_dot(mix(5), g1_ref[...])).astype(BF16), g2_ref[...])
    kk = k * vec[0:1]
    kk = kk * lax.rsqrt(jnp.maximum(_segsum_wide(kk * kk, ones), 1e-24))
    lw = jnp.tanh(_dot(mix(1), dw1_ref[...])).astype(BF16)
    la = _dot(mix(4), da1_ref[...]).astype(BF16)
    r_ref[0] = r
    v_ref[0] = v
    kk_ref[0] = kk
    bonus = jnp.zeros_like(r)
    lora = DECAY_LORA
    for d, (w_ref, kd_ref, bd_ref) in enumerate(((w0_ref, kd0_ref, bd0_ref), (w1_ref, kd1_ref, bd1_ref))):
        z = -(vec[3 + d:4 + d] + _dot(lw[:, d * lora:(d + 1) * lora], dw2_ref[d]))
        softplus = jnp.maximum(z, 0.0) + jnp.log(1.0 + jnp.exp(-jnp.abs(z)))
        w_ref[0] = jnp.exp(-jnp.exp(-softplus - 0.5))
        iclr = _sigmoid(vec[5 + d:6 + d] + _dot(la[:, d * lora:(d + 1) * lora], da2_ref[d]))
        kd = k * (1.0 + (iclr - 1.0) * vec[1:2])
        kd_ref[0] = kd
        bd_ref[0] = kk * iclr
        bonus = bonus + _segsum_wide(r * kd * vec[2:3], ones)
    bv_ref[0] = bonus * v


DECAY_LORA = 64


def _rwkv_proj(x, mods, g, xmix, wr, wk, wv, dw1, dw2, da1, da2, g1, g2, vec, ones):
    B, T, D = x.shape
    nt = T // TM
    row, mod, full = _row_specs(D)
    r8 = TM // 8
    prev = pl.BlockSpec((1, 8, D), lambda b, i: (b, jnp.maximum(i * r8 - 1, 0), 0))
    nxt = pl.BlockSpec((1, 8, D), lambda b, i: (b, jnp.minimum((i + 1) * r8, T // 8 - 1), 0))
    out = jax.ShapeDtypeStruct((B, T, D), F32)
    return pl.pallas_call(
        functools.partial(_rwkv_proj_kernel, nt=nt),
        out_shape=(out,) * 11,
        grid=(B, nt),
        in_specs=[row(D), prev, nxt, mod, full(g), full(xmix), full(wr), full(wk), full(wv),
                  full(dw1), full(dw2), full(da1), full(da2), full(g1), full(g2), full(vec), full(ones)],
        out_specs=(row(D),) * 11,
        compiler_params=_cparams(("parallel", "parallel")),
        name="rwkv_proj",
    )(x, x, x, mods, g, xmix, wr, wk, wv, dw1, dw2, da1, da2, g1, g2, vec, ones)


CHUNK = 4
N_HEADS = D_MODEL // HEAD_DIM
MAP_LANES = 3 * CHUNK * N_HEADS


def _coef_kernel(r_ref, kk_ref, w_ref, kd_ref, bd_ref, sel_ref, at_ref, rt_ref, bh_ref, kh_ref, gc_ref,
                 cu_ref, cy_ref, *, reverse):
    r, a, w, kd, bd = r_ref[0], -kk_ref[0], w_ref[0], kd_ref[0], bd_ref[0]
    rows = r.shape[0]
    p = lax.broadcasted_iota(jnp.int32, r.shape, 0) & (CHUNK - 1)
    s = (CHUNK - 1 - p) if reverse else p
    back = lambda x, k: pltpu.roll(x, (rows - k) if reverse else k, axis=0)
    ahead = lambda x, k: pltpu.roll(x, k if reverse else (rows - k), axis=0)
    wb = [None] + [back(w, k) for k in range(1, CHUNK)]
    excl = jnp.ones_like(w)
    rest = jnp.ones_like(w)
    for k in range(1, CHUNK):
        excl = excl * jnp.where(s >= k, wb[k], 1.0)
        rest = rest * jnp.where(s + k <= CHUNK - 1, ahead(w, k), 1.0)
    at_ref[0] = a * excl
    rt_ref[0] = r * (excl * w)
    bh_ref[0] = bd * rest
    kh_ref[0] = kd * rest
    gc_ref[0] = excl * w * rest
    between = [None, None, wb[1], wb[1] * wb[2]]
    rw = r * w
    s1 = s[:, :LANES]
    seg = lambda x: _dot(x.astype(BF16), sel_ref[...])

    def by_dist(lead, y, first):
        out = [None] * CHUNK
        for dist in range(first, CHUNK):
            if dist == 0:
                out[0] = seg(r * y)
                continue
            e = back(y, dist) if between[dist] is None else between[dist] * back(y, dist)
            out[dist] = jnp.where(s1 >= dist, seg(lead * e), 0.0)
        return out

    lab, lak = by_dist(a, bd, 1), by_dist(a, kd, 1)
    rb, rk = by_dist(rw, bd, 0), by_dist(rw, kd, 0)
    one = jnp.ones_like(rb[0])
    bk = lambda x, k: x if k == 0 else pltpu.roll(x, (rows - k) if reverse else k, axis=0)
    md = [one]
    for dist in range(1, CHUNK):
        md.append(sum(lab[e] * (bk(md[dist - e], e) if dist - e else 1.0) for e in range(1, dist + 1)))
    gd = [None] + [sum((md[e] if e else 1.0) * bk(lak[dist - e], e) for e in range(dist)) for dist in range(1, CHUNK)]
    yzd = [sum(rb[e] * (bk(md[dist - e], e) if dist - e else 1.0) for e in range(dist + 1)) for dist in range(CHUNK)]
    yvd = [rk[dist] + sum(rb[e] * bk(gd[dist - e], e) for e in range(dist)) for dist in range(CHUNK)]

    def at_pos(table, j, first):
        out = jnp.zeros_like(one)
        for dist in range(first, CHUNK - j):
            out = jnp.where(s1 == j + dist, table[dist], out)
        return out

    zero = jnp.zeros_like(one)
    u_blocks = ([at_pos(md, j, 0) for j in range(CHUNK)] + [at_pos(gd, j, 1) for j in range(CHUNK)]
                + [zero] * CHUNK)
    y_blocks = ([at_pos(yzd, j, 0) for j in range(CHUNK)] + [at_pos(yvd, j, 0) for j in range(CHUNK)]
                + [jnp.where(s1 == j, 1.0, 0.0) for j in range(CHUNK)])
    lane_blk = lax.broadcasted_iota(jnp.int32, one.shape, 1) // N_HEADS

    def place(blocks):
        per_tile = LANES // N_HEADS
        tiles = []
        for t in range(2):
            acc = zero
            for i in range(t * per_tile, min((t + 1) * per_tile, len(blocks))):
                acc = jnp.where(lane_blk == i - t * per_tile, blocks[i], acc)
            tiles.append(acc)
        return jnp.concatenate(tiles, axis=1)

    cu_ref[0] = place(u_blocks)
    cy_ref[0] = place(y_blocks)


def _rwkv_coef(r, kk, w, kd, bd, sel, reverse):
    B, T, D = r.shape
    row, _, full = _row_specs(D)
    out = jax.ShapeDtypeStruct((B, T, D), F32)
    maps = jax.ShapeDtypeStruct((B, T, 2 * LANES), F32)
    return pl.pallas_call(
        functools.partial(_coef_kernel, reverse=reverse),
        out_shape=(out,) * 5 + (maps, maps),
        grid=(B, T // TM),
        in_specs=[row(D)] * 5 + [full(sel)],
        out_specs=(row(D),) * 5 + (row(2 * LANES), row(2 * LANES)),
        compiler_params=_cparams(("parallel", "parallel")),
        name="rwkv_coef",
    )(r, kk, w, kd, bd, sel)


def _scan_kernel(atf, atb, rtf, rtb, vf, vb, bhf, bhb, khf, khb, gcf, gcb, cuf, cub, cyf, cyb,
                 mask_ref, eye_ref, e16_ref, yf, yb, st, *, tc, nb):
    n = pl.program_id(0)

    @pl.when(n == 0)
    def _():
        st[...] = jnp.zeros(st.shape, F32)

    N = HEAD_DIM
    W = CHUNK * N_HEADS
    nch = tc // CHUNK
    dirs = ((atf, rtf, vf, bhf, khf, gcf, cuf, cyf, yf), (atb, rtb, vb, bhb, khb, gcb, cub, cyb, yb))
    lane1 = lax.broadcasted_iota(jnp.int32, (N, 3 * W), 1)
    lane2 = lax.broadcasted_iota(jnp.int32, (N, 2 * W), 1)
    spread = lambda x: (mask_ref[...] * x).astype(BF16)

    def chunk(ci, carry):
        work = []
        for d, refs in enumerate(dirs):
            cc = ci if d == 0 else nch - 1 - ci
            rows = [pl.ds(cc * CHUNK + (s if d == 0 else CHUNK - 1 - s), 1) for s in range(CHUNK)]
            for b in range(nb):
                work.append((d * nb + b, b, cc, rows, refs))
        firsts = []
        for gi, b, cc, rows, (AT, RT, V, BH, KH, GC, CU, CY, Y) in work:
            lhs = jnp.concatenate([st[gi].astype(BF16), eye_ref[...]], axis=0)
            w1 = jnp.concatenate([spread(X[b, rw, :]) for X in (AT, V, RT) for rw in rows], axis=0)
            firsts.append(_dot_nt(lhs, w1))
        mids = []
        for (gi, b, cc, rows, (AT, RT, V, BH, KH, GC, CU, CY, Y)), out in zip(work, firsts):
            zvq = jnp.where((lane1 >= W) & (lane1 < 2 * W), out[N:], out[:N])
            zvq16 = zvq.astype(BF16)
            maprows = lambda M: [e16_ref[...] * M[b, rw, :MAP_LANES] for rw in rows]
            wu = jnp.concatenate(maprows(CU) + [jnp.zeros((W, MAP_LANES), F32)], axis=0)
            u = _dot_nt(zvq16, wu.astype(BF16))
            wy = jnp.concatenate(maprows(CY), axis=0)
            yt = _dot_nt(wy.astype(BF16), zvq16)
            for s, rw in enumerate(rows):
                Y[b, rw] = yt[s * N_HEADS:(s + 1) * N_HEADS][None]
            mids.append((zvq, u))
        for (gi, b, cc, rows, (AT, RT, V, BH, KH, GC, CU, CY, Y)), (zvq, u) in zip(work, mids):
            uv = jnp.where(lane2 < W, u, zvq[:, :2 * W]).astype(BF16)
            w2 = jnp.concatenate([spread(X[b, rw, :]) for X in (BH, KH) for rw in rows], axis=0)
            st[gi] = st[gi] * GC[b, rows[0], :] + _dot(uv, w2)
        return carry

    lax.fori_loop(0, nch, chunk, 0)


def _rwkv_scan(ins_f, ins_b, v, mask, eye, e16, L):
    B, T, D = v.shape
    tc = SCAN_CHUNK
    nch = tc // CHUNK
    nc, nchunks = L // tc, T // tc
    fwd_idx = lambda n: n
    rev_idx = lambda n: jnp.where(n < nc, nc - 1 - n, nchunks - 1 - (n - nc))
    tok = lambda idx, w: pl.BlockSpec((B, tc, w), lambda n: (0, idx(n), 0))
    ys = lambda idx: pl.BlockSpec((B, tc, N_HEADS, HEAD_DIM), lambda n: (0, idx(n), 0, 0))
    full = lambda a: pl.BlockSpec(a.shape, lambda n: (0,) * a.ndim)
    out = jax.ShapeDtypeStruct((B, T, N_HEADS, HEAD_DIM), F32)
    atf, rtf, bhf, khf, gcf, cuf, cyf = ins_f
    atb, rtb, bhb, khb, gcb, cub, cyb = ins_b
    f, r_ = tok(fwd_idx, D), tok(rev_idx, D)
    wy = cyf.shape[-1]
    return pl.pallas_call(
        functools.partial(_scan_kernel, tc=tc, nb=B),
        out_shape=(out, out),
        grid=(nchunks,),
        in_specs=[f, r_, f, r_, f, r_, f, r_, f, r_, f, r_, tok(fwd_idx, wy), tok(rev_idx, wy),
                  tok(fwd_idx, wy), tok(rev_idx, wy), full(mask), full(eye), full(e16)],
        out_specs=(ys(fwd_idx), ys(rev_idx)),
        scratch_shapes=[pltpu.VMEM((2 * B, HEAD_DIM, D), F32)],
        compiler_params=_cparams(("arbitrary",)),
        name="rwkv_scan",
    )(atf, atb, rtf, rtb, v, v, bhf, bhb, khf, khb, gcf, gcb, cuf, cub, cyf, cyb, mask, eye, e16)


def _rwkv_out_kernel(yf_ref, yb_ref, bv_ref, gate_ref, ln_ref, wo_ref, ones_ref, x_ref, mod_ref, g_ref,
                     rwh_ref, rwl_ref, xo_ref, h_ref, lg_ref):
    ones = ones_ref[...]
    y = yf_ref[0] + yb_ref[0]
    inv = 1.0 / HEAD_DIM
    dlt = y - _segsum_wide(y, ones) * inv
    yn = dlt * lax.rsqrt(_segsum_wide(dlt * dlt, ones) * inv + GN_EPS)
    ln = ln_ref[...]
    o = (yn * ln[0:1] + ln[1:2] + bv_ref[0]) * gate_ref[0]
    yl = _dot(o.astype(BF16), wo_ref[...])
    _ffn_prep(x_ref[0], yl, mod_ref[0, 0], g_ref[...], rwh_ref, rwl_ref, xo_ref, h_ref, lg_ref)


def _rwkv_out(yf, yb, bv, gate, ln, wo, ones, x, mods, g, rwh, rwl):
    B, T, D = x.shape
    row, mod, full = _row_specs(D)
    shapes, specs = _ffn_prep_outs(B, T, D)
    return pl.pallas_call(
        _rwkv_out_kernel,
        out_shape=shapes,
        grid=(B, T // TM),
        in_specs=[row(D), row(D), row(D), row(D), full(ln), full(wo), full(ones), row(D), mod, full(g),
                  full(rwh), full(rwl)],
        out_specs=specs,
        compiler_params=_cparams(("parallel", "parallel")),
        name="rwkv_out",
    )(yf, yb, bv, gate, ln, wo, ones, x, mods, g, rwh, rwl)


def _rope_tables(S, L):
    rows = S // GRID_W
    row = jnp.repeat(jnp.arange(rows, dtype=F32), GRID_W)
    col = (jnp.arange(rows * GRID_W) % GRID_W).astype(F32)
    n_freq = HEAD_DIM // 4
    inv = ROPE_THETA ** (-jnp.arange(n_freq, dtype=F32) / n_freq)
    lane = np.arange(LANES) % HEAD_DIM
    axis, half, freq = lane // 32, (lane % 32) // 16, lane % 16
    pos = jnp.where(jnp.asarray(axis == 0)[None, :], row[:, None], col[:, None])
    ang = pos * inv[freq][None, :]
    sgn = jnp.asarray(np.where(half == 0, -1.0, 1.0), dtype=F32)
    cos = jnp.concatenate([jnp.ones((L, LANES), F32), jnp.cos(ang)], axis=0)
    sin = jnp.concatenate([jnp.zeros((L, LANES), F32), jnp.sin(ang) * sgn[None, :]], axis=0)
    return cos, sin


def kernel(x, c, ctx, c_ctx, ada_w, ada_b, norm_mix_g, norm_ffn_g, attn_w_in, attn_w_out, attn_sink,
           attn_q_norm_g, attn_k_norm_g, rwkv_x_mix, rwkv_w_r, rwkv_w_k, rwkv_w_v, rwkv_w_o,
           rwkv_decay_w0, rwkv_decay_w1, rwkv_decay_w2, rwkv_iclr_a0, rwkv_iclr_a1, rwkv_iclr_a2,
           rwkv_gate_g1, rwkv_gate_g2, rwkv_k_k, rwkv_k_a, rwkv_r_k, rwkv_ln_g, rwkv_ln_b,
           router_w, router_bias, moe_w1, moe_w3, moe_w2, final_norm_g):
    B, S, D = x.shape
    L = ctx.shape[1]
    T = L + S
    depth = ada_w.shape[0]
    assert D == D_MODEL and L == TM and S % TM == 0 and B == 2 and depth == 2
    ones = _seg_ones()
    bf = lambda a: a.astype(BF16)

    cs = jnp.zeros((8, D), F32).at[:B].set(c).at[B].set(c_ctx)
    ada = _ada(cs, ada_w, ada_b).reshape(depth, 8, 6, D)
    mods = [jnp.stack([jnp.broadcast_to(ada[i, B], (B, 6, D)), ada[i, :B]], axis=1) for i in range(depth)]

    rw = jnp.zeros((D, LANES), F32).at[:, :N_EXPERTS].set(router_w)
    rwh, rwl = _split(rw)

    w_in = attn_w_in[0]
    roped = np.concatenate([np.arange(0, 640), np.arange(768, 1408)])
    w_rot = w_in[:, roped ^ 16]
    cos, sin = _rope_tables(S, L)
    lane = np.arange(LANES) % HEAD_DIM
    gains = lambda g: jnp.stack([g[lane], g[lane ^ 16]], axis=0)
    qa, ka, va, qb, kb, vb_x = _inproj(ctx, x, mods[0], norm_mix_g[0].reshape(1, D), bf(w_in), bf(w_rot), cos, sin,
                                       gains(attn_q_norm_g[0]), gains(attn_k_norm_g[0]), ones)
    grouped = lambda q: q.reshape(B, A_KV_HEADS, GROUP, T, HEAD_DIM)
    qa, qb = grouped(qa), grouped(qb)
    sink = attn_sink[0].astype(F32) * LOG2E
    vc = va[:, :, :L]
    va_x = jnp.swapaxes(jnp.concatenate(
        [vc, jnp.ones_like(vc[..., :1]), jnp.zeros_like(vc[..., :VT_ROWS - HEAD_DIM - 1])], axis=-1), 2, 3)
    nosink = jnp.full((B_Q_HEADS,), NEG, F32)
    oa_l = _window_attn(sink, qa, ka, va, L, S)
    oa_c = _flash(sink, qa, ka, va_x, q_rows=L, q_off=0, k_rows=L, tq=L, tk=L)
    ob_l = _flash(nosink, qb, kb, vb_x, q_rows=S, q_off=L, k_rows=T, tq=256, tk=_key_tile(T))
    ob_c = _flash(nosink, qb, kb, vb_x, q_rows=L, q_off=0, k_rows=L, tq=L, tk=L)
    w_out = bf(attn_w_out[0])
    na = A_Q_HEADS * HEAD_DIM
    xa, h, lg = _attn_out(oa_c, oa_l, ob_c, ob_l, w_out[:na], w_out[na:], ctx, x, mods[0],
                          norm_ffn_g[0].reshape(1, D), rwh, rwl)
    y0, y1, gw = _moe(h.reshape(B * T, D), lg.reshape(B * T, LANES), router_bias,
                      moe_w1, moe_w3, moe_w2, 0)
    xa = _residual(xa, y0.reshape(B, T, D), y1.reshape(B, T, D), gw.reshape(B, T, TOP_K), mods[0])

    cat2 = lambda a: jnp.concatenate([a[0], a[1]], axis=1)
    vec = jnp.stack([rwkv_k_k[0], rwkv_k_a[0], rwkv_r_k[0].reshape(D), rwkv_decay_w0[0, 0], rwkv_decay_w0[0, 1],
                     rwkv_iclr_a0[0, 0], rwkv_iclr_a0[0, 1], jnp.zeros((D,), F32)], axis=0)
    outs = _rwkv_proj(xa, mods[1], norm_mix_g[1].reshape(1, D), jnp.pad(rwkv_x_mix[0], ((0, 2), (0, 0))),
                      bf(rwkv_w_r[0]), bf(rwkv_w_k[0]), bf(rwkv_w_v[0]),
                      bf(cat2(rwkv_decay_w1[0])), bf(rwkv_decay_w2[0]),
                      bf(cat2(rwkv_iclr_a1[0])), bf(rwkv_iclr_a2[0]),
                      bf(rwkv_gate_g1[0]), bf(rwkv_gate_g2[0]), vec, ones)
    r, v, kk, bv, gate, w0, w1, kd0, kd1, bd0, bd1 = outs
    lane_id = np.arange(D)
    eye = jnp.asarray(np.arange(HEAD_DIM)[:, None] == (lane_id % HEAD_DIM)[None, :], dtype=BF16)
    n_heads = D // HEAD_DIM
    head_mask = jnp.asarray(np.arange(n_heads)[:, None] == (lane_id // HEAD_DIM)[None, :], dtype=F32)
    sel = jnp.asarray((lane_id // HEAD_DIM)[:, None] == (np.arange(LANES) % n_heads)[None, :], dtype=BF16)
    e16 = jnp.asarray(np.arange(n_heads)[:, None] == (np.arange(MAP_LANES) % n_heads)[None, :], dtype=F32)
    scan_ins = [_rwkv_coef(r, kk, w_d, kd_d, bd_d, sel, reverse=d == 1)
                for d, (w_d, kd_d, bd_d) in enumerate(((w0, kd0, bd0), (w1, kd1, bd1)))]
    yf, yb = _rwkv_scan(scan_ins[0], scan_ins[1], v, head_mask, eye, e16, L)
    ln = jnp.stack([rwkv_ln_g[0], rwkv_ln_b[0]] + [jnp.zeros((D,), F32)] * 6, axis=0)
    xa, h, lg = _rwkv_out(yf.reshape(B, T, D), yb.reshape(B, T, D), bv, gate, ln, bf(rwkv_w_o[0]), ones, xa, mods[1],
                          norm_ffn_g[1].reshape(1, D), rwh, rwl)
    y0, y1, gw = _moe(h[:, L:].reshape(B * S, D), lg[:, L:].reshape(B * S, LANES), router_bias,
                      moe_w1, moe_w3, moe_w2, 1)
    return _final(xa, y0.reshape(B, S, D), y1.reshape(B, S, D), gw.reshape(B, S, TOP_K), mods[1],
                  final_norm_g.reshape(1, D), L)


def _key_tile(T):
    for tk in (1280, 1024, 768, 512, 256):
        if T % tk == 0:
            return tk
    raise ValueError(T)
```

```python
import functools

import numpy as np
import jax
import jax.numpy as jnp
from jax import lax
from jax.experimental import pallas as pl
from jax.experimental.pallas import tpu as pltpu

F32 = jnp.float32
BF16 = jnp.bfloat16

D_MODEL = 1024
HEAD_DIM = 64
GRID_W = 64
ROPE_THETA = 10000.0
RMS_EPS = 1e-6
GN_EPS = 64e-5
A_Q_HEADS = 8
A_KV_HEADS = 2
B_Q_HEADS = 8
B_KV_HEADS = 2
GROUP = 4
WINDOW = 128
N_EXPERTS = 16
N_GROUPS = 4
EXPERTS_PER_GROUP = 4
TOP_K = 2
LANES = 128
TM = 256
MOE_ROWS = 512
SCAN_CHUNK = 64
VMEM_LIMIT = 56 * 1024 * 1024
NEG = -1e30
LOG2E = 1.4426950408889634


def _cparams(sem):
    return pltpu.CompilerParams(dimension_semantics=sem, vmem_limit_bytes=VMEM_LIMIT)


def _dot(a, b):
    return jnp.dot(a, b, preferred_element_type=F32)


def _dot_nt(a, b):
    return lax.dot_general(a, b, (((1,), (1,)), ((), ())), preferred_element_type=F32)


def _split(x):
    hi = x.astype(BF16)
    lo = (x - hi.astype(F32)).astype(BF16)
    return hi, lo


def _dot3(x, w):
    xh, xl = _split(x)
    wh, wl = _split(w)
    return _dot(xh, wh) + _dot(xh, wl) + _dot(xl, wh)


def _segsum(v, ones):
    hi, lo = _split(v)
    return _dot(hi, ones) + _dot(lo, ones)


def _segsum_wide(v, ones):
    n = v.shape[1] // LANES
    return jnp.concatenate([_segsum(v[:, j * LANES:(j + 1) * LANES], ones) for j in range(n)], axis=1)


def _norm_mod(x, g, shift, scale):
    ms = jnp.mean(x * x, axis=-1, keepdims=True)
    return (x * lax.rsqrt(ms + RMS_EPS) * g) * (1.0 + scale) + shift


def _sigmoid(x):
    return 1.0 / (1.0 + jnp.exp(-x))


def _seg_ones():
    i = np.arange(LANES)
    return jnp.asarray((i[:, None] // HEAD_DIM) == (i[None, :] // HEAD_DIM), dtype=BF16)


def _ada_kernel(c_ref, w_ref, b_ref, o_ref):
    c = c_ref[...]
    s = c * _sigmoid(c)
    o_ref[0] = _dot3(s, w_ref[0]) + b_ref[0]


def _ada(cs, ada_w, ada_b):
    depth, d, n = ada_w.shape
    tn = 1536
    return pl.pallas_call(
        _ada_kernel,
        out_shape=jax.ShapeDtypeStruct((depth, 8, n), F32),
        grid=(depth, n // tn),
        in_specs=[
            pl.BlockSpec((8, d), lambda l, j: (0, 0)),
            pl.BlockSpec((1, d, tn), lambda l, j: (l, 0, j)),
            pl.BlockSpec((1, 1, tn), lambda l, j: (l, 0, j)),
        ],
        out_specs=pl.BlockSpec((1, 8, tn), lambda l, j: (l, 0, j)),
        compiler_params=_cparams(("arbitrary", "arbitrary")),
        name="ada",
    )(cs, ada_w, ada_b.reshape(depth, 1, n))


def _seg_specs(w):
    ctx = pl.BlockSpec((1, TM, w), lambda b, i: (b, 0, 0))
    lat = pl.BlockSpec((1, TM, w), lambda b, i: (b, jnp.maximum(i - 1, 0), 0))
    return ctx, lat


def _seg_tile(c_ref, l_ref):
    return jnp.where(pl.program_id(1) == 0, c_ref[0], l_ref[0])


def _inproj_kernel(xc_ref, xl_ref, mod_ref, g_ref, w_ref, wrot_ref, cos_ref, sin_ref, gq_ref, gk_ref, ones_ref,
                   qa_ref, ka_ref, va_ref, qb_ref, kb_ref, vb_ref):
    mod = mod_ref[0, 0]
    h = _norm_mod(_seg_tile(xc_ref, xl_ref), g_ref[...], mod[0:1], mod[1:2]).astype(BF16)
    y = _dot(h, w_ref[...])
    yr = _dot(h, wrot_ref[...])
    cos = cos_ref[...]
    sin = sin_ref[...]
    ones = ones_ref[...]
    qscale = HEAD_DIM ** -0.5 * LOG2E

    def put(ref, tile, val):
        ref[0, 2 * tile] = val[:, :HEAD_DIM].astype(ref.dtype)
        ref[0, 2 * tile + 1] = val[:, HEAD_DIM:].astype(ref.dtype)

    def chunk(a, c):
        return a[:, c * LANES:(c + 1) * LANES]

    for c in range(4):
        put(qa_ref, c, (chunk(y, c) * cos + chunk(yr, c) * sin) * qscale)
    put(ka_ref, 0, chunk(y, 4) * cos + chunk(yr, 4) * sin)
    put(va_ref, 0, chunk(y, 5))

    def normed(c, cr, gain_ref):
        v = chunk(y, c)
        rs = lax.rsqrt(_segsum(v * v, ones) * (1.0 / HEAD_DIM) + RMS_EPS)
        return (v * rs * gain_ref[0:1]) * cos + (chunk(yr, cr) * rs * gain_ref[1:2]) * sin

    for c in range(4):
        put(qb_ref, c, normed(6 + c, 5 + c, gq_ref) * qscale)
    put(kb_ref, 0, normed(10, 9, gk_ref))
    vt = chunk(y, 11).T
    tail = jnp.where(lax.broadcasted_iota(jnp.int32, (VT_ROWS - HEAD_DIM, vt.shape[1]), 0) == 0, 1.0, 0.0)
    for hh in range(B_KV_HEADS):
        tile = jnp.concatenate([vt[hh * HEAD_DIM:(hh + 1) * HEAD_DIM], tail], axis=0)
        vb_ref[0, hh] = tile.astype(vb_ref.dtype)


def _inproj(xc, xl, mods, g, w_in, w_rot, cos, sin, gq2, gk2, ones):
    B, S, D = xl.shape
    T = xc.shape[1] + S
    nt = T // TM
    heads = lambda n: jax.ShapeDtypeStruct((B, n, T, HEAD_DIM), BF16)
    hspec = lambda n: pl.BlockSpec((1, n, TM, HEAD_DIM), lambda b, i: (b, 0, i, 0))
    full = lambda a: pl.BlockSpec(a.shape, lambda b, i: (0,) * a.ndim)
    vt_shape = jax.ShapeDtypeStruct((B, B_KV_HEADS, VT_ROWS, T), BF16)
    vt_spec = pl.BlockSpec((1, B_KV_HEADS, VT_ROWS, TM), lambda b, i: (b, 0, 0, i))
    return pl.pallas_call(
        _inproj_kernel,
        out_shape=(heads(8), heads(2), heads(2), heads(8), heads(2), vt_shape),
        grid=(B, nt),
        in_specs=[
            *_seg_specs(D),
            pl.BlockSpec((1, 1, 6, D), lambda b, i: (b, jnp.minimum(i, 1), 0, 0)),
            full(g), full(w_in), full(w_rot),
            pl.BlockSpec((TM, LANES), lambda b, i: (i, 0)),
            pl.BlockSpec((TM, LANES), lambda b, i: (i, 0)),
            full(gq2), full(gk2), full(ones),
        ],
        out_specs=(hspec(8), hspec(2), hspec(2), hspec(8), hspec(2), vt_spec),
        compiler_params=_cparams(("parallel", "parallel")),
        name="attn_inproj",
    )(xc, xl, mods, g, w_in, w_rot, cos, sin, gq2, gk2, ones)


LOOKAHEAD = 3
VT_ROWS = 80


def _flash_kernel(sink_ref, q_ref, k_ref, v_ref, o_ref, m_scr, acc_scr, s_scr, *, tk, nk):
    h = pl.program_id(1)
    m_scr[...] = jnp.full(m_scr.shape, NEG, F32)
    acc_scr[...] = jnp.zeros(acc_scr.shape, F32)

    def scores(j, g):
        return _dot_nt(k_ref[0, 0, pl.ds(pl.multiple_of(j * tk, tk), tk), :], q_ref[0, 0, g])

    for g in range(LOOKAHEAD):
        s_scr[g] = scores(0, g)

    def body(j, carry):
        vt = v_ref[0, 0, :, pl.ds(pl.multiple_of(j * tk, tk), tk)]
        jn = jnp.minimum(j + 1, nk - 1)
        ahead = {}
        for g in range(GROUP):
            st = s_scr[g] if g < LOOKAHEAD else ahead.pop(g)
            if g + LOOKAHEAD < GROUP:
                ahead[g + LOOKAHEAD] = scores(j, g + LOOKAHEAD)
            m_prev = m_scr[g]
            m_new = jnp.maximum(m_prev, jnp.max(st, axis=0, keepdims=True))
            p = jnp.exp2((st - m_new).astype(BF16))
            if g + LOOKAHEAD >= GROUP:
                s_scr[g + LOOKAHEAD - GROUP] = scores(jn, g + LOOKAHEAD - GROUP)
            acc_scr[g] = jnp.exp2(m_prev - m_new) * acc_scr[g] + _dot(vt, p)
            m_scr[g] = m_new
        return carry

    lax.fori_loop(0, nk, body, 0)
    outs = []
    for g in range(GROUP):
        acc = acc_scr[g]
        l = acc[HEAD_DIM:HEAD_DIM + 1] + jnp.exp2(sink_ref[h * GROUP + g] - m_scr[g])
        outs.append(acc[:HEAD_DIM] / l)
    o_ref[0] = jnp.concatenate(outs, axis=0).T.astype(o_ref.dtype)


def _flash(sink, q, k, v, *, q_rows, q_off, k_rows, tq, tk):
    B, Hkv = k.shape[:2]
    nq, nk = q_rows // tq, k_rows // tk
    qo = q_off // tq
    return pl.pallas_call(
        functools.partial(_flash_kernel, tk=tk, nk=nk),
        out_shape=jax.ShapeDtypeStruct((B, q_rows, Hkv * GROUP * HEAD_DIM), BF16),
        grid=(B, Hkv, nq),
        in_specs=[
            pl.BlockSpec(memory_space=pltpu.SMEM),
            pl.BlockSpec((1, 1, GROUP, tq, HEAD_DIM), lambda b, h, i: (b, h, 0, i + qo, 0)),
            pl.BlockSpec((1, 1, k_rows, HEAD_DIM), lambda b, h, i: (b, h, 0, 0)),
            pl.BlockSpec((1, 1, VT_ROWS, k_rows), lambda b, h, i: (b, h, 0, 0)),
        ],
        out_specs=pl.BlockSpec((1, tq, GROUP * HEAD_DIM), lambda b, h, i: (b, i, h)),
        scratch_shapes=[
            pltpu.VMEM((GROUP, 1, tq), F32),
            pltpu.VMEM((GROUP, VT_ROWS, tq), F32),
            pltpu.VMEM((LOOKAHEAD, tk, tq), F32),
        ],
        compiler_params=_cparams(("parallel", "parallel", "arbitrary")),
        name="flash_attn",
    )(sink, q, k, v)


def _window_kernel(sink_ref, q_ref, kc_ref, vc_ref, k0_ref, k1_ref, k2_ref, v0_ref, v1_ref, v2_ref, o_ref, *, nb):
    h = pl.program_id(1)
    i = pl.program_id(2)
    rows = GROUP * WINDOW
    q = q_ref[0, 0].reshape(rows, HEAD_DIM)
    r = lax.broadcasted_iota(jnp.int32, (rows, WINDOW), 0) & (WINDOW - 1)
    c = lax.broadcasted_iota(jnp.int32, (rows, WINDOW), 1)
    sc = _dot_nt(q, kc_ref[0, 0])
    s0 = jnp.where((c >= r) & (i > 0), _dot_nt(q, k0_ref[0, 0]), NEG)
    s1 = _dot_nt(q, k1_ref[0, 0])
    s2 = jnp.where((c <= r) & (i < nb - 1), _dot_nt(q, k2_ref[0, 0]), NEG)
    sink = jnp.concatenate(
        [jnp.full((WINDOW, 1), sink_ref[h * GROUP + g], F32) for g in range(GROUP)], axis=0)
    rowmax = lambda s: jnp.max(s, axis=-1, keepdims=True)
    m = jnp.maximum(jnp.maximum(rowmax(sc), rowmax(s0)), jnp.maximum(rowmax(s1), rowmax(s2)))
    m = jnp.maximum(m, sink)
    pc, p0, p1, p2 = (jnp.exp2(s - m) for s in (sc, s0, s1, s2))
    rowsum = lambda p: jnp.sum(p, axis=-1, keepdims=True)
    l = rowsum(pc) + rowsum(p0) + rowsum(p1) + rowsum(p2) + jnp.exp2(sink - m)
    acc = (_dot(pc.astype(BF16), vc_ref[0, 0]) + _dot(p0.astype(BF16), v0_ref[0, 0])
           + _dot(p1.astype(BF16), v1_ref[0, 0]) + _dot(p2.astype(BF16), v2_ref[0, 0]))
    out = acc / l
    for g in range(GROUP):
        o_ref[0, :, g * HEAD_DIM:(g + 1) * HEAD_DIM] = out[g * WINDOW:(g + 1) * WINDOW].astype(o_ref.dtype)


def _window_attn(sink, q, k, v, L, S):
    B, Hkv = k.shape[:2]
    nb = S // WINDOW
    pad = ((0, 0), (0, 0), (WINDOW, WINDOW), (0, 0))
    kp = jnp.pad(k[:, :, L:], pad)
    vp = jnp.pad(v[:, :, L:], pad)
    qo = L // WINDOW
    band = lambda j: pl.BlockSpec((1, 1, WINDOW, HEAD_DIM), lambda b, h, i: (b, h, i + j, 0))
    ctx = pl.BlockSpec((1, 1, L, HEAD_DIM), lambda b, h, i: (b, h, 0, 0))
    return pl.pallas_call(
        functools.partial(_window_kernel, nb=nb),
        out_shape=jax.ShapeDtypeStruct((B, S, Hkv * GROUP * HEAD_DIM), BF16),
        grid=(B, Hkv, nb),
        in_specs=[
            pl.BlockSpec(memory_space=pltpu.SMEM),
            pl.BlockSpec((1, 1, GROUP, WINDOW, HEAD_DIM), lambda b, h, i: (b, h, 0, i + qo, 0)),
            ctx, ctx, band(0), band(1), band(2), band(0), band(1), band(2),
        ],
        out_specs=pl.BlockSpec((1, WINDOW, GROUP * HEAD_DIM), lambda b, h, i: (b, i, h)),
        compiler_params=_cparams(("parallel", "parallel", "parallel")),
        name="window_attn",
    )(sink, q, k, v, kp, kp, kp, vp, vp, vp)


def _ffn_prep(x, y, mod, gffn, rwh_ref, rwl_ref, xo_ref, h_ref, lg_ref):
    xn = x + mod[2:3] * y
    h = _norm_mod(xn, gffn, mod[3:4], mod[4:5])
    xo_ref[0] = xn
    hh, hl = _split(h)
    h_ref[0] = hh
    rwh = rwh_ref[...]
    lg_ref[0] = _dot(hh, rwh) + _dot(hl, rwh) + _dot(hh, rwl_ref[...])


def _attn_out_kernel(oac_ref, oal_ref, obc_ref, obl_ref, wa_ref, wb_ref, xc_ref, xl_ref, mod_ref, g_ref,
                     rwh_ref, rwl_ref, xo_ref, h_ref, lg_ref):
    y = _dot(_seg_tile(oac_ref, oal_ref), wa_ref[...]) + _dot(_seg_tile(obc_ref, obl_ref), wb_ref[...])
    _ffn_prep(_seg_tile(xc_ref, xl_ref), y, mod_ref[0, 0], g_ref[...], rwh_ref, rwl_ref, xo_ref, h_ref, lg_ref)


def _row_specs(D):
    row = lambda w: pl.BlockSpec((1, TM, w), lambda b, i: (b, i, 0))
    mod = pl.BlockSpec((1, 1, 6, D), lambda b, i: (b, jnp.minimum(i, 1), 0, 0))
    full = lambda a: pl.BlockSpec(a.shape, lambda b, i: (0,) * a.ndim)
    return row, mod, full


def _ffn_prep_outs(B, T, D):
    row, _, _ = _row_specs(D)
    shapes = (jax.ShapeDtypeStruct((B, T, D), F32), jax.ShapeDtypeStruct((B, T, D), BF16),
              jax.ShapeDtypeStruct((B, T, LANES), F32))
    return shapes, (row(D), row(D), row(LANES))


def _attn_out(oac, oal, obc, obl, wa, wb, xc, xl, mods, g, rwh, rwl):
    B, S, D = xl.shape
    T = xc.shape[1] + S
    _, mod, full = _row_specs(D)
    shapes, specs = _ffn_prep_outs(B, T, D)
    return pl.pallas_call(
        _attn_out_kernel,
        out_shape=shapes,
        grid=(B, T // TM),
        in_specs=[*_seg_specs(oac.shape[-1]), *_seg_specs(obc.shape[-1]), full(wa), full(wb), *_seg_specs(D),
                  mod, full(g), full(rwh), full(rwl)],
        out_specs=specs,
        compiler_params=_cparams(("parallel", "parallel")),
        name="attn_out",
    )(oac, oal, obc, obl, wa, wb, xc, xl, mods, g, rwh, rwl)


def _gmm_kernel(be_ref, nu_ref, x_ref, w1_ref, w3_ref, w2_ref, o_ref):
    i = pl.program_id(0)

    @pl.when(i < nu_ref[0])
    def _():
        x = x_ref[...]
        a = _dot(x, w1_ref[0, 0].astype(BF16))
        b = _dot(x, w3_ref[0, 0].astype(BF16))
        mid = (a * _sigmoid(a)) * b
        o_ref[...] = _dot(mid.astype(BF16), w2_ref[0, 0].astype(BF16)).astype(o_ref.dtype)

    @pl.when(i >= nu_ref[0])
    def _():
        o_ref[...] = jnp.zeros(o_ref.shape, o_ref.dtype)


def _gmm(block_expert, n_used, xs, w1, w3, w2, layer):
    n_slots, D = xs.shape
    F = w1.shape[-1]
    nblk = n_slots // MOE_ROWS
    return pl.pallas_call(
        _gmm_kernel,
        out_shape=jax.ShapeDtypeStruct((n_slots, D), BF16),
        grid_spec=pltpu.PrefetchScalarGridSpec(
            num_scalar_prefetch=2,
            grid=(nblk,),
            in_specs=[
                pl.BlockSpec((MOE_ROWS, D), lambda i, be, nu: (i, 0)),
                pl.BlockSpec((1, 1, D, F), lambda i, be, nu: (layer, be[i], 0, 0)),
                pl.BlockSpec((1, 1, D, F), lambda i, be, nu: (layer, be[i], 0, 0)),
                pl.BlockSpec((1, 1, F, D), lambda i, be, nu: (layer, be[i], 0, 0)),
            ],
            out_specs=pl.BlockSpec((MOE_ROWS, D), lambda i, be, nu: (i, 0)),
        ),
        compiler_params=_cparams(("arbitrary",)),
        name="moe_gmm",
    )(block_expert, n_used, xs, w1, w3, w2)


ROUTE_ROWS = 512


def _route_kernel(lg_ref, bias_ref, idx_ref, w_ref):
    x = lg_ref[...].T[:N_EXPERTS]
    m = jnp.max(x, axis=0, keepdims=True)
    e = jnp.exp(x - m)
    probs = e / jnp.sum(e, axis=0, keepdims=True)
    sel = probs + bias_ref[...][:, 0:1]
    row = lambda a, i: a[i:i + 1, :]
    G = EXPERTS_PER_GROUP
    scores = []
    for g in range(N_GROUPS):
        s = [row(sel, g * G + i) for i in range(G)]
        best = None
        for i in range(G):
            for j in range(i + 1, G):
                best = s[i] + s[j] if best is None else jnp.maximum(best, s[i] + s[j])
        scores.append(best)
    top = functools.reduce(jnp.maximum, scores)
    gi = jnp.full(top.shape, N_GROUPS - 1, jnp.int32)
    for g in range(N_GROUPS - 2, -1, -1):
        gi = jnp.where(scores[g] == top, g, gi)

    def pick(a, i):
        out = row(a, (N_GROUPS - 1) * G + i)
        for g in range(N_GROUPS - 2, -1, -1):
            out = jnp.where(gi == g, row(a, g * G + i), out)
        return out

    c = [pick(sel, i) for i in range(G)]
    pc = [pick(probs, i) for i in range(G)]

    def first_argmax(vals):
        mx = functools.reduce(jnp.maximum, vals)
        idx = jnp.full(mx.shape, G - 1, jnp.int32)
        for i in range(G - 2, -1, -1):
            idx = jnp.where(vals[i] == mx, i, idx)
        return idx

    i1 = first_argmax(c)
    i2 = first_argmax([jnp.where(i1 == i, -jnp.inf, c[i]) for i in range(G)])
    take = lambda vals, idx: functools.reduce(
        lambda acc, i: jnp.where(idx == i, vals[i], acc), range(G - 2, -1, -1), vals[G - 1])
    w1, w2 = take(pc, i1), take(pc, i2)
    tot = w1 + w2
    zi = jnp.zeros((6,) + top.shape[1:], jnp.int32)
    idx_ref[...] = jnp.concatenate([gi * G + i1, gi * G + i2, zi], axis=0)
    pad = jnp.zeros((LANES - TOP_K,) + top.shape[1:], F32)
    w_ref[...] = jnp.concatenate([w1 / tot, w2 / tot, pad], axis=0).T


def _route(logits, router_bias):
    N = logits.shape[0]
    bias = jnp.broadcast_to(router_bias.astype(F32)[:, None], (N_EXPERTS, LANES))
    idx, w = pl.pallas_call(
        _route_kernel,
        out_shape=(jax.ShapeDtypeStruct((8, N), jnp.int32), jax.ShapeDtypeStruct((N, LANES), F32)),
        grid=(N // ROUTE_ROWS,),
        in_specs=[pl.BlockSpec((ROUTE_ROWS, LANES), lambda i: (i, 0)),
                  pl.BlockSpec((N_EXPERTS, LANES), lambda i: (0, 0))],
        out_specs=(pl.BlockSpec((8, ROUTE_ROWS), lambda i: (0, i)),
                   pl.BlockSpec((ROUTE_ROWS, LANES), lambda i: (i, 0))),
        compiler_params=_cparams(("parallel",)),
        name="route",
    )(logits, bias)
    return idx[:TOP_K], w


def _moe(h, logits, router_bias, w1, w3, w2, layer):
    N, D = h.shape
    expert_idx, gate_w = _route(logits, router_bias)
    NK = N * TOP_K
    flat_e = expert_idx.reshape(NK)
    onehot = (flat_e[None, :] == jnp.arange(N_EXPERTS, dtype=jnp.int32)[:, None]).astype(jnp.int32)
    csum = jnp.cumsum(onehot, axis=1)
    counts = csum[:, -1]
    padded = (counts + MOE_ROWS - 1) // MOE_ROWS * MOE_ROWS
    pad_end = jnp.cumsum(padded)
    pad_start = pad_end - padded
    dest = jnp.sum(onehot * (csum - 1 + pad_start[:, None]), axis=0)
    nblk = -(-NK // MOE_ROWS) + N_EXPERTS
    n_slots = nblk * MOE_ROWS
    n_used = (pad_end[-1] // MOE_ROWS).astype(jnp.int32)
    blk = jnp.arange(nblk, dtype=jnp.int32)
    be = jnp.sum((pad_end[None, :] <= (blk * MOE_ROWS)[:, None]).astype(jnp.int32), axis=1)
    be = jnp.minimum(be, N_EXPERTS - 1)
    be = jnp.where(blk < n_used, be, be[jnp.maximum(n_used - 1, 0)])
    flat_tok = jnp.arange(NK, dtype=jnp.int32) % N
    slot_tok = jnp.zeros((n_slots,), jnp.int32).at[dest].set(flat_tok, unique_indices=True)
    xs = h[slot_tok]
    ys = _gmm(be, n_used.reshape(1), xs, w1, w3, w2, layer)
    return ys[dest[:N]], ys[dest[N:]], gate_w


def _combine(x_ref, y0_ref, y1_ref, gw_ref, mod_ref):
    gw = gw_ref[0]
    f = y0_ref[0].astype(F32) * gw[:, 0:1] + y1_ref[0].astype(F32) * gw[:, 1:2]
    return x_ref[0] + mod_ref[0, 0][5:6] * f


def _residual_kernel(x_ref, y0_ref, y1_ref, gw_ref, mod_ref, o_ref):
    o_ref[0] = _combine(x_ref, y0_ref, y1_ref, gw_ref, mod_ref)


def _final_kernel(x_ref, y0_ref, y1_ref, gw_ref, mod_ref, g_ref, o_ref):
    x = _combine(x_ref, y0_ref, y1_ref, gw_ref, mod_ref)
    ms = jnp.mean(x * x, axis=-1, keepdims=True)
    o_ref[0] = x * lax.rsqrt(ms + RMS_EPS) * g_ref[...]


def _residual(x, y0, y1, gw, mods):
    B, T, D = x.shape
    row, mod, _ = _row_specs(D)
    return pl.pallas_call(
        _residual_kernel,
        out_shape=jax.ShapeDtypeStruct((B, T, D), F32),
        grid=(B, T // TM),
        in_specs=[row(D), row(D), row(D), row(LANES), mod],
        out_specs=row(D),
        compiler_params=_cparams(("parallel", "parallel")),
        name="residual",
    )(x, y0, y1, gw, mods)


def _final(x, y0, y1, gw, mods, g, L):
    B, S, D = y0.shape
    off = L // TM
    mod = pl.BlockSpec((1, 1, 6, D), lambda b, i: (b, 1, 0, 0))
    lat = lambda w: pl.BlockSpec((1, TM, w), lambda b, i: (b, i, 0))
    return pl.pallas_call(
        _final_kernel,
        out_shape=jax.ShapeDtypeStruct((B, S, D), F32),
        grid=(B, S // TM),
        in_specs=[pl.BlockSpec((1, TM, D), lambda b, i: (b, i + off, 0)), lat(D), lat(D), lat(LANES), mod,
                  pl.BlockSpec(g.shape, lambda b, i: (0, 0))],
        out_specs=lat(D),
        compiler_params=_cparams(("parallel", "parallel")),
        name="final_norm",
    )(x, y0, y1, gw, mods, g)


def _rwkv_proj_kernel(x_ref, xp_ref, xn_ref, mod_ref, g_ref, xmix_ref, wr_ref, wk_ref, wv_ref,
                      dw1_ref, dw2_ref, da1_ref, da2_ref, g1_ref, g2_ref, vec_ref, ones_ref,
                      r_ref, v_ref, kk_ref, bv_ref, gate_ref, w0_ref, w1_ref, kd0_ref, kd1_ref, bd0_ref, bd1_ref,
                      *, nt):
    i = pl.program_id(1)
    mod = mod_ref[0, 0]
    g = g_ref[...]
    nm = lambda x: _norm_mod(x, g, mod[0:1], mod[1:2])
    h = nm(x_ref[0])
    hp = nm(xp_ref[0])[7:8] * jnp.where(i >= 2, 1.0, 0.0)
    hn = nm(xn_ref[0])[0:1] * jnp.where((i >= 1) & (i < nt - 1), 1.0, 0.0)
    ridx = lax.broadcasted_iota(jnp.int32, h.shape, 0)
    h_dn = jnp.where(ridx == 0, hp, pltpu.roll(h, 1, axis=0))
    h_up = jnp.where(ridx == TM - 1, hn, pltpu.roll(h, TM - 1, axis=0))
    xx = 0.5 * (h_dn + h_up) - h
    xmix = xmix_ref[...]
    mix = lambda j: (h + xx * xmix[j:j + 1]).astype(BF16)
    vec = vec_ref[...]
    ones = ones_ref[...]

    r = _dot(mix(0), wr_ref[...])
    k = _dot(mix(2), wk_ref[...])
    v = _dot(mix(3), wv_ref[...])
    gate_ref[0] = _dot(_sigmoid(_dot(mix(5), g1_ref[...])).astype(BF16), g2_ref[...])
    kk = k * vec[0:1]
    kk = kk * lax.rsqrt(jnp.maximum(_segsum_wide(kk * kk, ones), 1e-24))
    lw = jnp.tanh(_dot(mix(1), dw1_ref[...])).astype(BF16)
    la = _dot(mix(4), da1_ref[...]).astype(BF16)
    r_ref[0] = r
    v_ref[0] = v
    kk_ref[0] = kk
    bonus = jnp.zeros_like(r)
    lora = DECAY_LORA
    for d, (w_ref, kd_ref, bd_ref) in enumerate(((w0_ref, kd0_ref, bd0_ref), (w1_ref, kd1_ref, bd1_ref))):
        z = -(vec[3 + d:4 + d] + _dot(lw[:, d * lora:(d + 1) * lora], dw2_ref[d]))
        softplus = jnp.maximum(z, 0.0) + jnp.log(1.0 + jnp.exp(-jnp.abs(z)))
        w_ref[0] = jnp.exp(-jnp.exp(-softplus - 0.5))
        iclr = _sigmoid(vec[5 + d:6 + d] + _dot(la[:, d * lora:(d + 1) * lora], da2_ref[d]))
        kd = k * (1.0 + (iclr - 1.0) * vec[1:2])
        kd_ref[0] = kd
        bd_ref[0] = kk * iclr
        bonus = bonus + _segsum_wide(r * kd * vec[2:3], ones)
    bv_ref[0] = bonus * v


DECAY_LORA = 64


def _rwkv_proj(x, mods, g, xmix, wr, wk, wv, dw1, dw2, da1, da2, g1, g2, vec, ones):
    B, T, D = x.shape
    nt = T // TM
    row, mod, full = _row_specs(D)
    r8 = TM // 8
    prev = pl.BlockSpec((1, 8, D), lambda b, i: (b, jnp.maximum(i * r8 - 1, 0), 0))
    nxt = pl.BlockSpec((1, 8, D), lambda b, i: (b, jnp.minimum((i + 1) * r8, T // 8 - 1), 0))
    out = jax.ShapeDtypeStruct((B, T, D), F32)
    return pl.pallas_call(
        functools.partial(_rwkv_proj_kernel, nt=nt),
        out_shape=(out,) * 11,
        grid=(B, nt),
        in_specs=[row(D), prev, nxt, mod, full(g), full(xmix), full(wr), full(wk), full(wv),
                  full(dw1), full(dw2), full(da1), full(da2), full(g1), full(g2), full(vec), full(ones)],
        out_specs=(row(D),) * 11,
        compiler_params=_cparams(("parallel", "parallel")),
        name="rwkv_proj",
    )(x, x, x, mods, g, xmix, wr, wk, wv, dw1, dw2, da1, da2, g1, g2, vec, ones)


CHUNK = 4
N_HEADS = D_MODEL // HEAD_DIM
MAP_LANES = 3 * CHUNK * N_HEADS


def _coef_kernel(r_ref, kk_ref, w_ref, kd_ref, bd_ref, sel_ref, at_ref, rt_ref, bh_ref, kh_ref, gc_ref,
                 cu_ref, cy_ref, *, reverse):
    r, a, w, kd, bd = r_ref[0], -kk_ref[0], w_ref[0], kd_ref[0], bd_ref[0]
    rows = r.shape[0]
    p = lax.broadcasted_iota(jnp.int32, r.shape, 0) & (CHUNK - 1)
    s = (CHUNK - 1 - p) if reverse else p
    back = lambda x, k: pltpu.roll(x, (rows - k) if reverse else k, axis=0)
    ahead = lambda x, k: pltpu.roll(x, k if reverse else (rows - k), axis=0)
    wb = [None] + [back(w, k) for k in range(1, CHUNK)]
    excl = jnp.ones_like(w)
    rest = jnp.ones_like(w)
    for k in range(1, CHUNK):
        excl = excl * jnp.where(s >= k, wb[k], 1.0)
        rest = rest * jnp.where(s + k <= CHUNK - 1, ahead(w, k), 1.0)
    at_ref[0] = a * excl
    rt_ref[0] = r * (excl * w)
    bh_ref[0] = bd * rest
    kh_ref[0] = kd * rest
    gc_ref[0] = excl * w * rest
    between = [None, None, wb[1], wb[1] * wb[2]]
    rw = r * w
    s1 = s[:, :LANES]
    seg = lambda x: _dot(x.astype(BF16), sel_ref[...])

    def by_dist(lead, y, first):
        out = [None] * CHUNK
        for dist in range(first, CHUNK):
            if dist == 0:
                out[0] = seg(r * y)
                continue
            e = back(y, dist) if between[dist] is None else between[dist] * back(y, dist)
            out[dist] = jnp.where(s1 >= dist, seg(lead * e), 0.0)
        return out

    lab, lak = by_dist(a, bd, 1), by_dist(a, kd, 1)
    rb, rk = by_dist(rw, bd, 0), by_dist(rw, kd, 0)
    one = jnp.ones_like(rb[0])
    bk = lambda x, k: x if k == 0 else pltpu.roll(x, (rows - k) if reverse else k, axis=0)
    md = [one]
    for dist in range(1, CHUNK):
        md.append(sum(lab[e] * (bk(md[dist - e], e) if dist - e else 1.0) for e in range(1, dist + 1)))
    gd = [None] + [sum((md[e] if e else 1.0) * bk(lak[dist - e], e) for e in range(dist)) for dist in range(1, CHUNK)]
    yzd = [sum(rb[e] * (bk(md[dist - e], e) if dist - e else 1.0) for e in range(dist + 1)) for dist in range(CHUNK)]
    yvd = [rk[dist] + sum(rb[e] * bk(gd[dist - e], e) for e in range(dist)) for dist in range(CHUNK)]

    def at_pos(table, j, first):
        out = jnp.zeros_like(one)
        for dist in range(first, CHUNK - j):
            out = jnp.where(s1 == j + dist, table[dist], out)
        return out

    zero = jnp.zeros_like(one)
    u_blocks = ([at_pos(md, j, 0) for j in range(CHUNK)] + [at_pos(gd, j, 1) for j in range(CHUNK)]
                + [zero] * CHUNK)
    y_blocks = ([at_pos(yzd, j, 0) for j in range(CHUNK)] + [at_pos(yvd, j, 0) for j in range(CHUNK)]
                + [jnp.where(s1 == j, 1.0, 0.0) for j in range(CHUNK)])
    lane_blk = lax.broadcasted_iota(jnp.int32, one.shape, 1) // N_HEADS

    def place(blocks):
        per_tile = LANES // N_HEADS
        tiles = []
        for t in range(2):
            acc = zero
            for i in range(t * per_tile, min((t + 1) * per_tile, len(blocks))):
                acc = jnp.where(lane_blk == i - t * per_tile, blocks[i], acc)
            tiles.append(acc)
        return jnp.concatenate(tiles, axis=1)

    cu_ref[0] = place(u_blocks)
    cy_ref[0] = place(y_blocks)


def _rwkv_coef(r, kk, w, kd, bd, sel, reverse):
    B, T, D = r.shape
    row, _, full = _row_specs(D)
    out = jax.ShapeDtypeStruct((B, T, D), F32)
    maps = jax.ShapeDtypeStruct((B, T, 2 * LANES), F32)
    return pl.pallas_call(
        functools.partial(_coef_kernel, reverse=reverse),
        out_shape=(out,) * 5 + (maps, maps),
        grid=(B, T // TM),
        in_specs=[row(D)] * 5 + [full(sel)],
        out_specs=(row(D),) * 5 + (row(2 * LANES), row(2 * LANES)),
        compiler_params=_cparams(("parallel", "parallel")),
        name="rwkv_coef",
    )(r, kk, w, kd, bd, sel)


def _scan_kernel(atf, atb, rtf, rtb, vf, vb, bhf, bhb, khf, khb, gcf, gcb, cuf, cub, cyf, cyb,
                 mask_ref, eye_ref, e16_ref, yf, yb, st, *, tc, nb):
    n = pl.program_id(0)

    @pl.when(n == 0)
    def _():
        st[...] = jnp.zeros(st.shape, F32)

    N = HEAD_DIM
    W = CHUNK * N_HEADS
    nch = tc // CHUNK
    dirs = ((atf, rtf, vf, bhf, khf, gcf, cuf, cyf, yf), (atb, rtb, vb, bhb, khb, gcb, cub, cyb, yb))
    lane1 = lax.broadcasted_iota(jnp.int32, (N, 3 * W), 1)
    lane2 = lax.broadcasted_iota(jnp.int32, (N, 2 * W), 1)
    spread = lambda x: (mask_ref[...] * x).astype(BF16)

    def chunk(ci, carry):
        work = []
        for d, refs in enumerate(dirs):
            cc = ci if d == 0 else nch - 1 - ci
            rows = [pl.ds(cc * CHUNK + (s if d == 0 else CHUNK - 1 - s), 1) for s in range(CHUNK)]
            for b in range(nb):
                work.append((d * nb + b, b, cc, rows, refs))
        firsts = []
        for gi, b, cc, rows, (AT, RT, V, BH, KH, GC, CU, CY, Y) in work:
            lhs = jnp.concatenate([st[gi].astype(BF16), eye_ref[...]], axis=0)
            w1 = jnp.concatenate([spread(X[b, rw, :]) for X in (AT, V, RT) for rw in rows], axis=0)
            firsts.append(_dot_nt(lhs, w1))
        mids = []
        for (gi, b, cc, rows, (AT, RT, V, BH, KH, GC, CU, CY, Y)), out in zip(work, firsts):
            zvq = jnp.where((lane1 >= W) & (lane1 < 2 * W), out[N:], out[:N])
            zvq16 = zvq.astype(BF16)
            maprows = lambda M: [e16_ref[...] * M[b, rw, :MAP_LANES] for rw in rows]
            wu = jnp.concatenate(maprows(CU) + [jnp.zeros((W, MAP_LANES), F32)], axis=0)
            u = _dot_nt(zvq16, wu.astype(BF16))
            wy = jnp.concatenate(maprows(CY), axis=0)
            yt = _dot_nt(wy.astype(BF16), zvq16)
            for s, rw in enumerate(rows):
                Y[b, rw] = yt[s * N_HEADS:(s + 1) * N_HEADS][None]
            mids.append((zvq, u))
        for (gi, b, cc, rows, (AT, RT, V, BH, KH, GC, CU, CY, Y)), (zvq, u) in zip(work, mids):
            uv = jnp.where(lane2 < W, u, zvq[:, :2 * W]).astype(BF16)
            w2 = jnp.concatenate([spread(X[b, rw, :]) for X in (BH, KH) for rw in rows], axis=0)
            st[gi] = st[gi] * GC[b, rows[0], :] + _dot(uv, w2)
        return carry

    lax.fori_loop(0, nch, chunk, 0)


def _rwkv_scan(ins_f, ins_b, v, mask, eye, e16, L):
    B, T, D = v.shape
    tc = SCAN_CHUNK
    nch = tc // CHUNK
    nc, nchunks = L // tc, T // tc
    fwd_idx = lambda n: n
    rev_idx = lambda n: jnp.where(n < nc, nc - 1 - n, nchunks - 1 - (n - nc))
    tok = lambda idx, w: pl.BlockSpec((B, tc, w), lambda n: (0, idx(n), 0))
    ys = lambda idx: pl.BlockSpec((B, tc, N_HEADS, HEAD_DIM), lambda n: (0, idx(n), 0, 0))
    full = lambda a: pl.BlockSpec(a.shape, lambda n: (0,) * a.ndim)
    out = jax.ShapeDtypeStruct((B, T, N_HEADS, HEAD_DIM), F32)
    atf, rtf, bhf, khf, gcf, cuf, cyf = ins_f
    atb, rtb, bhb, khb, gcb, cub, cyb = ins_b
    f, r_ = tok(fwd_idx, D), tok(rev_idx, D)
    wy = cyf.shape[-1]
    return pl.pallas_call(
        functools.partial(_scan_kernel, tc=tc, nb=B),
        out_shape=(out, out),
        grid=(nchunks,),
        in_specs=[f, r_, f, r_, f, r_, f, r_, f, r_, f, r_, tok(fwd_idx, wy), tok(rev_idx, wy),
                  tok(fwd_idx, wy), tok(rev_idx, wy), full(mask), full(eye), full(e16)],
        out_specs=(ys(fwd_idx), ys(rev_idx)),
        scratch_shapes=[pltpu.VMEM((2 * B, HEAD_DIM, D), F32)],
        compiler_params=_cparams(("arbitrary",)),
        name="rwkv_scan",
    )(atf, atb, rtf, rtb, v, v, bhf, bhb, khf, khb, gcf, gcb, cuf, cub, cyf, cyb, mask, eye, e16)


def _rwkv_out_kernel(yf_ref, yb_ref, bv_ref, gate_ref, ln_ref, wo_ref, ones_ref, x_ref, mod_ref, g_ref,
                     rwh_ref, rwl_ref, xo_ref, h_ref, lg_ref):
    ones = ones_ref[...]
    y = yf_ref[0] + yb_ref[0]
    inv = 1.0 / HEAD_DIM
    dlt = y - _segsum_wide(y, ones) * inv
    yn = dlt * lax.rsqrt(_segsum_wide(dlt * dlt, ones) * inv + GN_EPS)
    ln = ln_ref[...]
    o = (yn * ln[0:1] + ln[1:2] + bv_ref[0]) * gate_ref[0]
    yl = _dot(o.astype(BF16), wo_ref[...])
    _ffn_prep(x_ref[0], yl, mod_ref[0, 0], g_ref[...], rwh_ref, rwl_ref, xo_ref, h_ref, lg_ref)


def _rwkv_out(yf, yb, bv, gate, ln, wo, ones, x, mods, g, rwh, rwl):
    B, T, D = x.shape
    row, mod, full = _row_specs(D)
    shapes, specs = _ffn_prep_outs(B, T, D)
    return pl.pallas_call(
        _rwkv_out_kernel,
        out_shape=shapes,
        grid=(B, T // TM),
        in_specs=[row(D), row(D), row(D), row(D), full(ln), full(wo), full(ones), row(D), mod, full(g),
                  full(rwh), full(rwl)],
        out_specs=specs,
        compiler_params=_cparams(("parallel", "parallel")),
        name="rwkv_out",
    )(yf, yb, bv, gate, ln, wo, ones, x, mods, g, rwh, rwl)


def _rope_tables(S, L):
    rows = S // GRID_W
    row = jnp.repeat(jnp.arange(rows, dtype=F32), GRID_W)
    col = (jnp.arange(rows * GRID_W) % GRID_W).astype(F32)
    n_freq = HEAD_DIM // 4
    inv = ROPE_THETA ** (-jnp.arange(n_freq, dtype=F32) / n_freq)
    lane = np.arange(LANES) % HEAD_DIM
    axis, half, freq = lane // 32, (lane % 32) // 16, lane % 16
    pos = jnp.where(jnp.asarray(axis == 0)[None, :], row[:, None], col[:, None])
    ang = pos * inv[freq][None, :]
    sgn = jnp.asarray(np.where(half == 0, -1.0, 1.0), dtype=F32)
    cos = jnp.concatenate([jnp.ones((L, LANES), F32), jnp.cos(ang)], axis=0)
    sin = jnp.concatenate([jnp.zeros((L, LANES), F32), jnp.sin(ang) * sgn[None, :]], axis=0)
    return cos, sin


def kernel(x, c, ctx, c_ctx, ada_w, ada_b, norm_mix_g, norm_ffn_g, attn_w_in, attn_w_out, attn_sink,
           attn_q_norm_g, attn_k_norm_g, rwkv_x_mix, rwkv_w_r, rwkv_w_k, rwkv_w_v, rwkv_w_o,
           rwkv_decay_w0, rwkv_decay_w1, rwkv_decay_w2, rwkv_iclr_a0, rwkv_iclr_a1, rwkv_iclr_a2,
           rwkv_gate_g1, rwkv_gate_g2, rwkv_k_k, rwkv_k_a, rwkv_r_k, rwkv_ln_g, rwkv_ln_b,
           router_w, router_bias, moe_w1, moe_w3, moe_w2, final_norm_g):
    B, S, D = x.shape
    L = ctx.shape[1]
    T = L + S
    depth = ada_w.shape[0]
    assert D == D_MODEL and L == TM and S % TM == 0 and B == 2 and depth == 2
    ones = _seg_ones()
    bf = lambda a: a.astype(BF16)

    cs = jnp.zeros((8, D), F32).at[:B].set(c).at[B].set(c_ctx)
    ada = _ada(cs, ada_w, ada_b).reshape(depth, 8, 6, D)
    mods = [jnp.stack([jnp.broadcast_to(ada[i, B], (B, 6, D)), ada[i, :B]], axis=1) for i in range(depth)]

    rw = jnp.zeros((D, LANES), F32).at[:, :N_EXPERTS].set(router_w)
    rwh, rwl = _split(rw)

    w_in = attn_w_in[0]
    roped = np.concatenate([np.arange(0, 640), np.arange(768, 1408)])
    w_rot = w_in[:, roped ^ 16]
    cos, sin = _rope_tables(S, L)
    lane = np.arange(LANES) % HEAD_DIM
    gains = lambda g: jnp.stack([g[lane], g[lane ^ 16]], axis=0)
    qa, ka, va, qb, kb, vb_x = _inproj(ctx, x, mods[0], norm_mix_g[0].reshape(1, D), bf(w_in), bf(w_rot), cos, sin,
                                       gains(attn_q_norm_g[0]), gains(attn_k_norm_g[0]), ones)
    grouped = lambda q: q.reshape(B, A_KV_HEADS, GROUP, T, HEAD_DIM)
    qa, qb = grouped(qa), grouped(qb)
    sink = attn_sink[0].astype(F32) * LOG2E
    vc = va[:, :, :L]
    va_x = jnp.swapaxes(jnp.concatenate(
        [vc, jnp.ones_like(vc[..., :1]), jnp.zeros_like(vc[..., :VT_ROWS - HEAD_DIM - 1])], axis=-1), 2, 3)
    nosink = jnp.full((B_Q_HEADS,), NEG, F32)
    oa_l = _window_attn(sink, qa, ka, va, L, S)
    oa_c = _flash(sink, qa, ka, va_x, q_rows=L, q_off=0, k_rows=L, tq=L, tk=L)
    ob_l = _flash(nosink, qb, kb, vb_x, q_rows=S, q_off=L, k_rows=T, tq=256, tk=_key_tile(T))
    ob_c = _flash(nosink, qb, kb, vb_x, q_rows=L, q_off=0, k_rows=L, tq=L, tk=L)
    w_out = bf(attn_w_out[0])
    na = A_Q_HEADS * HEAD_DIM
    xa, h, lg = _attn_out(oa_c, oa_l, ob_c, ob_l, w_out[:na], w_out[na:], ctx, x, mods[0],
                          norm_ffn_g[0].reshape(1, D), rwh, rwl)
    y0, y1, gw = _moe(h.reshape(B * T, D), lg.reshape(B * T, LANES), router_bias,
                      moe_w1, moe_w3, moe_w2, 0)
    xa = _residual(xa, y0.reshape(B, T, D), y1.reshape(B, T, D), gw.reshape(B, T, LANES), mods[0])

    cat2 = lambda a: jnp.concatenate([a[0], a[1]], axis=1)
    vec = jnp.stack([rwkv_k_k[0], rwkv_k_a[0], rwkv_r_k[0].reshape(D), rwkv_decay_w0[0, 0], rwkv_decay_w0[0, 1],
                     rwkv_iclr_a0[0, 0], rwkv_iclr_a0[0, 1], jnp.zeros((D,), F32)], axis=0)
    outs = _rwkv_proj(xa, mods[1], norm_mix_g[1].reshape(1, D), jnp.pad(rwkv_x_mix[0], ((0, 2), (0, 0))),
                      bf(rwkv_w_r[0]), bf(rwkv_w_k[0]), bf(rwkv_w_v[0]),
                      bf(cat2(rwkv_decay_w1[0])), bf(rwkv_decay_w2[0]),
                      bf(cat2(rwkv_iclr_a1[0])), bf(rwkv_iclr_a2[0]),
                      bf(rwkv_gate_g1[0]), bf(rwkv_gate_g2[0]), vec, ones)
    r, v, kk, bv, gate, w0, w1, kd0, kd1, bd0, bd1 = outs
    lane_id = np.arange(D)
    eye = jnp.asarray(np.arange(HEAD_DIM)[:, None] == (lane_id % HEAD_DIM)[None, :], dtype=BF16)
    n_heads = D // HEAD_DIM
    head_mask = jnp.asarray(np.arange(n_heads)[:, None] == (lane_id // HEAD_DIM)[None, :], dtype=F32)
    sel = jnp.asarray((lane_id // HEAD_DIM)[:, None] == (np.arange(LANES) % n_heads)[None, :], dtype=BF16)
    e16 = jnp.asarray(np.arange(n_heads)[:, None] == (np.arange(MAP_LANES) % n_heads)[None, :], dtype=F32)
    scan_ins = [_rwkv_coef(r, kk, w_d, kd_d, bd_d, sel, reverse=d == 1)
                for d, (w_d, kd_d, bd_d) in enumerate(((w0, kd0, bd0), (w1, kd1, bd1)))]
    yf, yb = _rwkv_scan(scan_ins[0], scan_ins[1], v, head_mask, eye, e16, L)
    ln = jnp.stack([rwkv_ln_g[0], rwkv_ln_b[0]] + [jnp.zeros((D,), F32)] * 6, axis=0)
    xa, h, lg = _rwkv_out(yf.reshape(B, T, D), yb.reshape(B, T, D), bv, gate, ln, bf(rwkv_w_o[0]), ones, xa, mods[1],
                          norm_ffn_g[1].reshape(1, D), rwh, rwl)
    y0, y1, gw = _moe(h[:, L:].reshape(B * S, D), lg[:, L:].reshape(B * S, LANES), router_bias,
                      moe_w1, moe_w3, moe_w2, 1)
    return _final(xa, y0.reshape(B, S, D), y1.reshape(B, S, D), gw.reshape(B, S, LANES), mods[1],
                  final_norm_g.reshape(1, D), L)


def _key_tile(T):
    for tk in (1280, 1024, 768, 512, 256):
        if T % tk == 0:
            return tk
    raise ValueError(T)
```

```python
import functools

import numpy as np
import jax
import jax.numpy as jnp
from jax import lax
from jax.experimental import pallas as pl
from jax.experimental.pallas import tpu as pltpu

F32 = jnp.float32
BF16 = jnp.bfloat16

D_MODEL = 1024
HEAD_DIM = 64
GRID_W = 64
ROPE_THETA = 10000.0
RMS_EPS = 1e-6
GN_EPS = 64e-5
A_Q_HEADS = 8
A_KV_HEADS = 2
B_Q_HEADS = 8
B_KV_HEADS = 2
GROUP = 4
WINDOW = 128
N_EXPERTS = 16
N_GROUPS = 4
EXPERTS_PER_GROUP = 4
TOP_K = 2
LANES = 128
TM = 256
MOE_ROWS = 512
SCAN_CHUNK = 128
VMEM_LIMIT = 56 * 1024 * 1024
NEG = -1e30
LOG2E = 1.4426950408889634


def _cparams(sem):
    return pltpu.CompilerParams(dimension_semantics=sem, vmem_limit_bytes=VMEM_LIMIT)


def _dot(a, b):
    return jnp.dot(a, b, preferred_element_type=F32)


def _dot_nt(a, b):
    return lax.dot_general(a, b, (((1,), (1,)), ((), ())), preferred_element_type=F32)


def _split(x):
    hi = x.astype(BF16)
    lo = (x - hi.astype(F32)).astype(BF16)
    return hi, lo


def _dot3(x, w):
    xh, xl = _split(x)
    wh, wl = _split(w)
    return _dot(xh, wh) + _dot(xh, wl) + _dot(xl, wh)


def _segsum(v, ones):
    hi, lo = _split(v)
    return _dot(hi, ones) + _dot(lo, ones)


def _segsum_wide(v, ones):
    n = v.shape[1] // LANES
    return jnp.concatenate([_segsum(v[:, j * LANES:(j + 1) * LANES], ones) for j in range(n)], axis=1)


def _norm_mod(x, g, shift, scale):
    ms = jnp.mean(x * x, axis=-1, keepdims=True)
    return (x * lax.rsqrt(ms + RMS_EPS) * g) * (1.0 + scale) + shift


def _sigmoid(x):
    return 1.0 / (1.0 + jnp.exp(-x))


def _seg_ones():
    i = np.arange(LANES)
    return jnp.asarray((i[:, None] // HEAD_DIM) == (i[None, :] // HEAD_DIM), dtype=BF16)


def _ada_kernel(c_ref, w_ref, b_ref, o_ref):
    c = c_ref[...]
    s = c * _sigmoid(c)
    o_ref[0] = _dot3(s, w_ref[0]) + b_ref[0]


def _ada(cs, ada_w, ada_b):
    depth, d, n = ada_w.shape
    tn = 1536
    return pl.pallas_call(
        _ada_kernel,
        out_shape=jax.ShapeDtypeStruct((depth, 8, n), F32),
        grid=(depth, n // tn),
        in_specs=[
            pl.BlockSpec((8, d), lambda l, j: (0, 0)),
            pl.BlockSpec((1, d, tn), lambda l, j: (l, 0, j)),
            pl.BlockSpec((1, 1, tn), lambda l, j: (l, 0, j)),
        ],
        out_specs=pl.BlockSpec((1, 8, tn), lambda l, j: (l, 0, j)),
        compiler_params=_cparams(("arbitrary", "arbitrary")),
        name="ada",
    )(cs, ada_w, ada_b.reshape(depth, 1, n))


def _seg_specs(w):
    ctx = pl.BlockSpec((1, TM, w), lambda b, i: (b, 0, 0))
    lat = pl.BlockSpec((1, TM, w), lambda b, i: (b, jnp.maximum(i - 1, 0), 0))
    return ctx, lat


def _seg_tile(c_ref, l_ref):
    return jnp.where(pl.program_id(1) == 0, c_ref[0], l_ref[0])


def _inproj_kernel(xc_ref, xl_ref, mod_ref, g_ref, w_ref, wrot_ref, cos_ref, sin_ref, gq_ref, gk_ref, ones_ref,
                   qa_ref, ka_ref, va_ref, qb_ref, kb_ref, vb_ref):
    mod = mod_ref[0, 0]
    h = _norm_mod(_seg_tile(xc_ref, xl_ref), g_ref[...], mod[0:1], mod[1:2]).astype(BF16)
    y = _dot(h, w_ref[...])
    yr = _dot(h, wrot_ref[...])
    cos = cos_ref[...]
    sin = sin_ref[...]
    ones = ones_ref[...]
    qscale = HEAD_DIM ** -0.5 * LOG2E

    def put(ref, tile, val):
        ref[0, 2 * tile] = val[:, :HEAD_DIM].astype(ref.dtype)
        ref[0, 2 * tile + 1] = val[:, HEAD_DIM:].astype(ref.dtype)

    def chunk(a, c):
        return a[:, c * LANES:(c + 1) * LANES]

    for c in range(4):
        put(qa_ref, c, (chunk(y, c) * cos + chunk(yr, c) * sin) * qscale)
    put(ka_ref, 0, chunk(y, 4) * cos + chunk(yr, 4) * sin)
    put(va_ref, 0, chunk(y, 5))

    def normed(c, cr, gain_ref):
        v = chunk(y, c)
        rs = lax.rsqrt(_segsum(v * v, ones) * (1.0 / HEAD_DIM) + RMS_EPS)
        return (v * rs * gain_ref[0:1]) * cos + (chunk(yr, cr) * rs * gain_ref[1:2]) * sin

    for c in range(4):
        put(qb_ref, c, normed(6 + c, 5 + c, gq_ref) * qscale)
    put(kb_ref, 0, normed(10, 9, gk_ref))
    vt = chunk(y, 11).T
    tail = jnp.where(lax.broadcasted_iota(jnp.int32, (VT_ROWS - HEAD_DIM, vt.shape[1]), 0) == 0, 1.0, 0.0)
    for hh in range(B_KV_HEADS):
        tile = jnp.concatenate([vt[hh * HEAD_DIM:(hh + 1) * HEAD_DIM], tail], axis=0)
        vb_ref[0, hh] = tile.astype(vb_ref.dtype)


def _inproj(xc, xl, mods, g, w_in, w_rot, cos, sin, gq2, gk2, ones):
    B, S, D = xl.shape
    T = xc.shape[1] + S
    nt = T // TM
    heads = lambda n: jax.ShapeDtypeStruct((B, n, T, HEAD_DIM), BF16)
    hspec = lambda n: pl.BlockSpec((1, n, TM, HEAD_DIM), lambda b, i: (b, 0, i, 0))
    full = lambda a: pl.BlockSpec(a.shape, lambda b, i: (0,) * a.ndim)
    vt_shape = jax.ShapeDtypeStruct((B, B_KV_HEADS, VT_ROWS, T), BF16)
    vt_spec = pl.BlockSpec((1, B_KV_HEADS, VT_ROWS, TM), lambda b, i: (b, 0, 0, i))
    return pl.pallas_call(
        _inproj_kernel,
        out_shape=(heads(8), heads(2), heads(2), heads(8), heads(2), vt_shape),
        grid=(B, nt),
        in_specs=[
            *_seg_specs(D),
            pl.BlockSpec((1, 1, 6, D), lambda b, i: (b, jnp.minimum(i, 1), 0, 0)),
            full(g), full(w_in), full(w_rot),
            pl.BlockSpec((TM, LANES), lambda b, i: (i, 0)),
            pl.BlockSpec((TM, LANES), lambda b, i: (i, 0)),
            full(gq2), full(gk2), full(ones),
        ],
        out_specs=(hspec(8), hspec(2), hspec(2), hspec(8), hspec(2), vt_spec),
        compiler_params=_cparams(("parallel", "parallel")),
        name="attn_inproj",
    )(xc, xl, mods, g, w_in, w_rot, cos, sin, gq2, gk2, ones)


LOOKAHEAD = 3
VT_ROWS = 80


def _flash_kernel(sink_ref, q_ref, k_ref, v_ref, o_ref, m_scr, acc_scr, s_scr, *, tk, nk):
    h = pl.program_id(1)
    m_scr[...] = jnp.full(m_scr.shape, NEG, F32)
    acc_scr[...] = jnp.zeros(acc_scr.shape, F32)

    def scores(j, g):
        return _dot_nt(k_ref[0, 0, pl.ds(pl.multiple_of(j * tk, tk), tk), :], q_ref[0, 0, g])

    for g in range(LOOKAHEAD):
        s_scr[g] = scores(0, g)

    def body(j, carry):
        vt = v_ref[0, 0, :, pl.ds(pl.multiple_of(j * tk, tk), tk)]
        jn = jnp.minimum(j + 1, nk - 1)
        ahead = {}
        for g in range(GROUP):
            st = s_scr[g] if g < LOOKAHEAD else ahead.pop(g)
            if g + LOOKAHEAD < GROUP:
                ahead[g + LOOKAHEAD] = scores(j, g + LOOKAHEAD)
            m_prev = m_scr[g]
            m_new = jnp.maximum(m_prev, jnp.max(st, axis=0, keepdims=True))
            p = jnp.exp2((st - m_new).astype(BF16))
            if g + LOOKAHEAD >= GROUP:
                s_scr[g + LOOKAHEAD - GROUP] = scores(jn, g + LOOKAHEAD - GROUP)
            acc_scr[g] = jnp.exp2(m_prev - m_new) * acc_scr[g] + _dot(vt, p)
            m_scr[g] = m_new
        return carry

    lax.fori_loop(0, nk, body, 0)
    outs = []
    for g in range(GROUP):
        acc = acc_scr[g]
        l = acc[HEAD_DIM:HEAD_DIM + 1] + jnp.exp2(sink_ref[h * GROUP + g] - m_scr[g])
        outs.append(acc[:HEAD_DIM] / l)
    o_ref[0] = jnp.concatenate(outs, axis=0).T.astype(o_ref.dtype)


def _flash(sink, q, k, v, *, q_rows, q_off, k_rows, tq, tk):
    B, Hkv = k.shape[:2]
    nq, nk = q_rows // tq, k_rows // tk
    qo = q_off // tq
    return pl.pallas_call(
        functools.partial(_flash_kernel, tk=tk, nk=nk),
        out_shape=jax.ShapeDtypeStruct((B, q_rows, Hkv * GROUP * HEAD_DIM), BF16),
        grid=(B, Hkv, nq),
        in_specs=[
            pl.BlockSpec(memory_space=pltpu.SMEM),
            pl.BlockSpec((1, 1, GROUP, tq, HEAD_DIM), lambda b, h, i: (b, h, 0, i + qo, 0)),
            pl.BlockSpec((1, 1, k_rows, HEAD_DIM), lambda b, h, i: (b, h, 0, 0)),
            pl.BlockSpec((1, 1, VT_ROWS, k_rows), lambda b, h, i: (b, h, 0, 0)),
        ],
        out_specs=pl.BlockSpec((1, tq, GROUP * HEAD_DIM), lambda b, h, i: (b, i, h)),
        scratch_shapes=[
            pltpu.VMEM((GROUP, 1, tq), F32),
            pltpu.VMEM((GROUP, VT_ROWS, tq), F32),
            pltpu.VMEM((LOOKAHEAD, tk, tq), F32),
        ],
        compiler_params=_cparams(("parallel", "parallel", "arbitrary")),
        name="flash_attn",
    )(sink, q, k, v)


def _window_kernel(sink_ref, q_ref, kc_ref, vc_ref, k0_ref, k1_ref, k2_ref, v0_ref, v1_ref, v2_ref, o_ref, *, nb):
    h = pl.program_id(1)
    i = pl.program_id(2)
    rows = GROUP * WINDOW
    q = q_ref[0, 0].reshape(rows, HEAD_DIM)
    r = lax.broadcasted_iota(jnp.int32, (rows, WINDOW), 0) & (WINDOW - 1)
    c = lax.broadcasted_iota(jnp.int32, (rows, WINDOW), 1)
    sc = _dot_nt(q, kc_ref[0, 0])
    s0 = jnp.where((c >= r) & (i > 0), _dot_nt(q, k0_ref[0, 0]), NEG)
    s1 = _dot_nt(q, k1_ref[0, 0])
    s2 = jnp.where((c <= r) & (i < nb - 1), _dot_nt(q, k2_ref[0, 0]), NEG)
    sink = jnp.concatenate(
        [jnp.full((WINDOW, 1), sink_ref[h * GROUP + g], F32) for g in range(GROUP)], axis=0)
    rowmax = lambda s: jnp.max(s, axis=-1, keepdims=True)
    m = jnp.maximum(jnp.maximum(rowmax(sc), rowmax(s0)), jnp.maximum(rowmax(s1), rowmax(s2)))
    m = jnp.maximum(m, sink)
    pc, p0, p1, p2 = (jnp.exp2(s - m) for s in (sc, s0, s1, s2))
    rowsum = lambda p: jnp.sum(p, axis=-1, keepdims=True)
    l = rowsum(pc) + rowsum(p0) + rowsum(p1) + rowsum(p2) + jnp.exp2(sink - m)
    acc = (_dot(pc.astype(BF16), vc_ref[0, 0]) + _dot(p0.astype(BF16), v0_ref[0, 0])
           + _dot(p1.astype(BF16), v1_ref[0, 0]) + _dot(p2.astype(BF16), v2_ref[0, 0]))
    out = acc / l
    for g in range(GROUP):
        o_ref[0, :, g * HEAD_DIM:(g + 1) * HEAD_DIM] = out[g * WINDOW:(g + 1) * WINDOW].astype(o_ref.dtype)


def _window_attn(sink, q, k, v, L, S):
    B, Hkv = k.shape[:2]
    nb = S // WINDOW
    pad = ((0, 0), (0, 0), (WINDOW, WINDOW), (0, 0))
    kp = jnp.pad(k[:, :, L:], pad)
    vp = jnp.pad(v[:, :, L:], pad)
    qo = L // WINDOW
    band = lambda j: pl.BlockSpec((1, 1, WINDOW, HEAD_DIM), lambda b, h, i: (b, h, i + j, 0))
    ctx = pl.BlockSpec((1, 1, L, HEAD_DIM), lambda b, h, i: (b, h, 0, 0))
    return pl.pallas_call(
        functools.partial(_window_kernel, nb=nb),
        out_shape=jax.ShapeDtypeStruct((B, S, Hkv * GROUP * HEAD_DIM), BF16),
        grid=(B, Hkv, nb),
        in_specs=[
            pl.BlockSpec(memory_space=pltpu.SMEM),
            pl.BlockSpec((1, 1, GROUP, WINDOW, HEAD_DIM), lambda b, h, i: (b, h, 0, i + qo, 0)),
            ctx, ctx, band(0), band(1), band(2), band(0), band(1), band(2),
        ],
        out_specs=pl.BlockSpec((1, WINDOW, GROUP * HEAD_DIM), lambda b, h, i: (b, i, h)),
        compiler_params=_cparams(("parallel", "parallel", "parallel")),
        name="window_attn",
    )(sink, q, k, v, kp, kp, kp, vp, vp, vp)


def _ffn_prep(x, y, mod, gffn, rwh_ref, rwl_ref, xo_ref, h_ref, lg_ref):
    xn = x + mod[2:3] * y
    h = _norm_mod(xn, gffn, mod[3:4], mod[4:5])
    xo_ref[0] = xn
    hh, hl = _split(h)
    h_ref[0] = hh
    rwh = rwh_ref[...]
    lg_ref[0] = _dot(hh, rwh) + _dot(hl, rwh) + _dot(hh, rwl_ref[...])


def _attn_out_kernel(oac_ref, oal_ref, obc_ref, obl_ref, wa_ref, wb_ref, xc_ref, xl_ref, mod_ref, g_ref,
                     rwh_ref, rwl_ref, xo_ref, h_ref, lg_ref):
    y = _dot(_seg_tile(oac_ref, oal_ref), wa_ref[...]) + _dot(_seg_tile(obc_ref, obl_ref), wb_ref[...])
    _ffn_prep(_seg_tile(xc_ref, xl_ref), y, mod_ref[0, 0], g_ref[...], rwh_ref, rwl_ref, xo_ref, h_ref, lg_ref)


def _row_specs(D):
    row = lambda w: pl.BlockSpec((1, TM, w), lambda b, i: (b, i, 0))
    mod = pl.BlockSpec((1, 1, 6, D), lambda b, i: (b, jnp.minimum(i, 1), 0, 0))
    full = lambda a: pl.BlockSpec(a.shape, lambda b, i: (0,) * a.ndim)
    return row, mod, full


def _ffn_prep_outs(B, T, D):
    row, _, _ = _row_specs(D)
    shapes = (jax.ShapeDtypeStruct((B, T, D), F32), jax.ShapeDtypeStruct((B, T, D), BF16),
              jax.ShapeDtypeStruct((B, T, LANES), F32))
    return shapes, (row(D), row(D), row(LANES))


def _attn_out(oac, oal, obc, obl, wa, wb, xc, xl, mods, g, rwh, rwl):
    B, S, D = xl.shape
    T = xc.shape[1] + S
    _, mod, full = _row_specs(D)
    shapes, specs = _ffn_prep_outs(B, T, D)
    return pl.pallas_call(
        _attn_out_kernel,
        out_shape=shapes,
        grid=(B, T // TM),
        in_specs=[*_seg_specs(oac.shape[-1]), *_seg_specs(obc.shape[-1]), full(wa), full(wb), *_seg_specs(D),
                  mod, full(g), full(rwh), full(rwl)],
        out_specs=specs,
        compiler_params=_cparams(("parallel", "parallel")),
        name="attn_out",
    )(oac, oal, obc, obl, wa, wb, xc, xl, mods, g, rwh, rwl)


def _gmm_kernel(be_ref, nu_ref, x_ref, w1_ref, w3_ref, w2_ref, o_ref):
    i = pl.program_id(0)

    @pl.when(i < nu_ref[0])
    def _():
        x = x_ref[...]
        a = _dot(x, w1_ref[0, 0].astype(BF16))
        b = _dot(x, w3_ref[0, 0].astype(BF16))
        mid = (a * _sigmoid(a)) * b
        o_ref[...] = _dot(mid.astype(BF16), w2_ref[0, 0].astype(BF16)).astype(o_ref.dtype)

    @pl.when(i >= nu_ref[0])
    def _():
        o_ref[...] = jnp.zeros(o_ref.shape, o_ref.dtype)


def _gmm(block_expert, n_used, xs, w1, w3, w2, layer):
    n_slots, D = xs.shape
    F = w1.shape[-1]
    nblk = n_slots // MOE_ROWS
    return pl.pallas_call(
        _gmm_kernel,
        out_shape=jax.ShapeDtypeStruct((n_slots, D), BF16),
        grid_spec=pltpu.PrefetchScalarGridSpec(
            num_scalar_prefetch=2,
            grid=(nblk,),
            in_specs=[
                pl.BlockSpec((MOE_ROWS, D), lambda i, be, nu: (i, 0)),
                pl.BlockSpec((1, 1, D, F), lambda i, be, nu: (layer, be[i], 0, 0)),
                pl.BlockSpec((1, 1, D, F), lambda i, be, nu: (layer, be[i], 0, 0)),
                pl.BlockSpec((1, 1, F, D), lambda i, be, nu: (layer, be[i], 0, 0)),
            ],
            out_specs=pl.BlockSpec((MOE_ROWS, D), lambda i, be, nu: (i, 0)),
        ),
        compiler_params=_cparams(("arbitrary",)),
        name="moe_gmm",
    )(block_expert, n_used, xs, w1, w3, w2)


ROUTE_ROWS = 512


def _route_kernel(lg_ref, bias_ref, idx_ref, w_ref):
    x = lg_ref[...].T[:N_EXPERTS]
    m = jnp.max(x, axis=0, keepdims=True)
    e = jnp.exp(x - m)
    probs = e / jnp.sum(e, axis=0, keepdims=True)
    sel = probs + bias_ref[...][:, 0:1]
    row = lambda a, i: a[i:i + 1, :]
    G = EXPERTS_PER_GROUP
    scores = []
    for g in range(N_GROUPS):
        s = [row(sel, g * G + i) for i in range(G)]
        best = None
        for i in range(G):
            for j in range(i + 1, G):
                best = s[i] + s[j] if best is None else jnp.maximum(best, s[i] + s[j])
        scores.append(best)
    top = functools.reduce(jnp.maximum, scores)
    gi = jnp.full(top.shape, N_GROUPS - 1, jnp.int32)
    for g in range(N_GROUPS - 2, -1, -1):
        gi = jnp.where(scores[g] == top, g, gi)

    def pick(a, i):
        out = row(a, (N_GROUPS - 1) * G + i)
        for g in range(N_GROUPS - 2, -1, -1):
            out = jnp.where(gi == g, row(a, g * G + i), out)
        return out

    c = [pick(sel, i) for i in range(G)]
    pc = [pick(probs, i) for i in range(G)]

    def first_argmax(vals):
        mx = functools.reduce(jnp.maximum, vals)
        idx = jnp.full(mx.shape, G - 1, jnp.int32)
        for i in range(G - 2, -1, -1):
            idx = jnp.where(vals[i] == mx, i, idx)
        return idx

    i1 = first_argmax(c)
    i2 = first_argmax([jnp.where(i1 == i, -jnp.inf, c[i]) for i in range(G)])
    take = lambda vals, idx: functools.reduce(
        lambda acc, i: jnp.where(idx == i, vals[i], acc), range(G - 2, -1, -1), vals[G - 1])
    w1, w2 = take(pc, i1), take(pc, i2)
    tot = w1 + w2
    zi = jnp.zeros((6,) + top.shape[1:], jnp.int32)
    idx_ref[...] = jnp.concatenate([gi * G + i1, gi * G + i2, zi], axis=0)
    w_ref[...] = jnp.concatenate([w1 / tot, w2 / tot, zi.astype(F32)], axis=0)


def _route(logits, router_bias):
    N = logits.shape[0]
    bias = jnp.broadcast_to(router_bias.astype(F32)[:, None], (N_EXPERTS, LANES))
    idx, w = pl.pallas_call(
        _route_kernel,
        out_shape=(jax.ShapeDtypeStruct((8, N), jnp.int32), jax.ShapeDtypeStruct((8, N), F32)),
        grid=(N // ROUTE_ROWS,),
        in_specs=[pl.BlockSpec((ROUTE_ROWS, LANES), lambda i: (i, 0)),
                  pl.BlockSpec((N_EXPERTS, LANES), lambda i: (0, 0))],
        out_specs=(pl.BlockSpec((8, ROUTE_ROWS), lambda i: (0, i)),
                   pl.BlockSpec((8, ROUTE_ROWS), lambda i: (0, i))),
        compiler_params=_cparams(("parallel",)),
        name="route",
    )(logits, bias)
    return idx[:TOP_K], w[:TOP_K]


def _moe(h, logits, router_bias, w1, w3, w2, layer):
    N, D = h.shape
    expert_idx, gate_w = _route(logits, router_bias)
    NK = N * TOP_K
    flat_e = expert_idx.reshape(NK)
    onehot = (flat_e[None, :] == jnp.arange(N_EXPERTS, dtype=jnp.int32)[:, None]).astype(jnp.int32)
    csum = jnp.cumsum(onehot, axis=1)
    counts = csum[:, -1]
    padded = (counts + MOE_ROWS - 1) // MOE_ROWS * MOE_ROWS
    pad_end = jnp.cumsum(padded)
    pad_start = pad_end - padded
    dest = jnp.sum(onehot * (csum - 1 + pad_start[:, None]), axis=0)
    nblk = -(-NK // MOE_ROWS) + N_EXPERTS
    n_slots = nblk * MOE_ROWS
    n_used = (pad_end[-1] // MOE_ROWS).astype(jnp.int32)
    blk = jnp.arange(nblk, dtype=jnp.int32)
    be = jnp.sum((pad_end[None, :] <= (blk * MOE_ROWS)[:, None]).astype(jnp.int32), axis=1)
    be = jnp.minimum(be, N_EXPERTS - 1)
    be = jnp.where(blk < n_used, be, be[jnp.maximum(n_used - 1, 0)])
    flat_tok = jnp.arange(NK, dtype=jnp.int32) % N
    slot_tok = jnp.zeros((n_slots,), jnp.int32).at[dest].set(flat_tok, unique_indices=True)
    xs = h[slot_tok]
    ys = _gmm(be, n_used.reshape(1), xs, w1, w3, w2, layer)
    return ys[dest[:N]], ys[dest[N:]], gate_w.T


def _combine(x_ref, y0_ref, y1_ref, gw_ref, mod_ref):
    gw = gw_ref[0]
    f = y0_ref[0].astype(F32) * gw[:, 0:1] + y1_ref[0].astype(F32) * gw[:, 1:2]
    return x_ref[0] + mod_ref[0, 0][5:6] * f


def _residual_kernel(x_ref, y0_ref, y1_ref, gw_ref, mod_ref, o_ref):
    o_ref[0] = _combine(x_ref, y0_ref, y1_ref, gw_ref, mod_ref)


def _final_kernel(x_ref, y0_ref, y1_ref, gw_ref, mod_ref, g_ref, o_ref):
    x = _combine(x_ref, y0_ref, y1_ref, gw_ref, mod_ref)
    ms = jnp.mean(x * x, axis=-1, keepdims=True)
    o_ref[0] = x * lax.rsqrt(ms + RMS_EPS) * g_ref[...]


def _residual(x, y0, y1, gw, mods):
    B, T, D = x.shape
    row, mod, _ = _row_specs(D)
    return pl.pallas_call(
        _residual_kernel,
        out_shape=jax.ShapeDtypeStruct((B, T, D), F32),
        grid=(B, T // TM),
        in_specs=[row(D), row(D), row(D), row(TOP_K), mod],
        out_specs=row(D),
        compiler_params=_cparams(("parallel", "parallel")),
        name="residual",
    )(x, y0, y1, gw, mods)


def _final(x, y0, y1, gw, mods, g, L):
    B, S, D = y0.shape
    off = L // TM
    mod = pl.BlockSpec((1, 1, 6, D), lambda b, i: (b, 1, 0, 0))
    lat = lambda w: pl.BlockSpec((1, TM, w), lambda b, i: (b, i, 0))
    return pl.pallas_call(
        _final_kernel,
        out_shape=jax.ShapeDtypeStruct((B, S, D), F32),
        grid=(B, S // TM),
        in_specs=[pl.BlockSpec((1, TM, D), lambda b, i: (b, i + off, 0)), lat(D), lat(D), lat(TOP_K), mod,
                  pl.BlockSpec(g.shape, lambda b, i: (0, 0))],
        out_specs=lat(D),
        compiler_params=_cparams(("parallel", "parallel")),
        name="final_norm",
    )(x, y0, y1, gw, mods, g)


def _rwkv_proj_kernel(x_ref, xp_ref, xn_ref, mod_ref, g_ref, xmix_ref, wr_ref, wk_ref, wv_ref,
                      dw1_ref, dw2_ref, da1_ref, da2_ref, g1_ref, g2_ref, vec_ref, ones_ref,
                      r_ref, v_ref, kk_ref, bv_ref, gate_ref, w0_ref, w1_ref, kd0_ref, kd1_ref, bd0_ref, bd1_ref,
                      *, nt):
    i = pl.program_id(1)
    mod = mod_ref[0, 0]
    g = g_ref[...]
    nm = lambda x: _norm_mod(x, g, mod[0:1], mod[1:2])
    h = nm(x_ref[0])
    hp = nm(xp_ref[0])[7:8] * jnp.where(i >= 2, 1.0, 0.0)
    hn = nm(xn_ref[0])[0:1] * jnp.where((i >= 1) & (i < nt - 1), 1.0, 0.0)
    ridx = lax.broadcasted_iota(jnp.int32, h.shape, 0)
    h_dn = jnp.where(ridx == 0, hp, pltpu.roll(h, 1, axis=0))
    h_up = jnp.where(ridx == TM - 1, hn, pltpu.roll(h, TM - 1, axis=0))
    xx = 0.5 * (h_dn + h_up) - h
    xmix = xmix_ref[...]
    mix = lambda j: (h + xx * xmix[j:j + 1]).astype(BF16)
    vec = vec_ref[...]
    ones = ones_ref[...]

    r = _dot(mix(0), wr_ref[...])
    k = _dot(mix(2), wk_ref[...])
    v = _dot(mix(3), wv_ref[...])
    gate_ref[0] = _dot(_sigmoid(_dot(mix(5), g1_ref[...])).astype(BF16), g2_ref[...])
    kk = k * vec[0:1]
    kk = kk * lax.rsqrt(jnp.maximum(_segsum_wide(kk * kk, ones), 1e-24))
    lw = jnp.tanh(_dot(mix(1), dw1_ref[...])).astype(BF16)
    la = _dot(mix(4), da1_ref[...]).astype(BF16)
    r_ref[0] = r
    v_ref[0] = v
    kk_ref[0] = kk
    bonus = jnp.zeros_like(r)
    lora = DECAY_LORA
    for d, (w_ref, kd_ref, bd_ref) in enumerate(((w0_ref, kd0_ref, bd0_ref), (w1_ref, kd1_ref, bd1_ref))):
        z = -(vec[3 + d:4 + d] + _dot(lw[:, d * lora:(d + 1) * lora], dw2_ref[d]))
        softplus = jnp.maximum(z, 0.0) + jnp.log(1.0 + jnp.exp(-jnp.abs(z)))
        w_ref[0] = jnp.exp(-jnp.exp(-softplus - 0.5))
        iclr = _sigmoid(vec[5 + d:6 + d] + _dot(la[:, d * lora:(d + 1) * lora], da2_ref[d]))
        kd = k * (1.0 + (iclr - 1.0) * vec[1:2])
        kd_ref[0] = kd
        bd_ref[0] = kk * iclr
        bonus = bonus + _segsum_wide(r * kd * vec[2:3], ones)
    bv_ref[0] = bonus * v


DECAY_LORA = 64


def _rwkv_proj(x, mods, g, xmix, wr, wk, wv, dw1, dw2, da1, da2, g1, g2, vec, ones):
    B, T, D = x.shape
    nt = T // TM
    row, mod, full = _row_specs(D)
    r8 = TM // 8
    prev = pl.BlockSpec((1, 8, D), lambda b, i: (b, jnp.maximum(i * r8 - 1, 0), 0))
    nxt = pl.BlockSpec((1, 8, D), lambda b, i: (b, jnp.minimum((i + 1) * r8, T // 8 - 1), 0))
    out = jax.ShapeDtypeStruct((B, T, D), F32)
    return pl.pallas_call(
        functools.partial(_rwkv_proj_kernel, nt=nt),
        out_shape=(out,) * 11,
        grid=(B, nt),
        in_specs=[row(D), prev, nxt, mod, full(g), full(xmix), full(wr), full(wk), full(wv),
                  full(dw1), full(dw2), full(da1), full(da2), full(g1), full(g2), full(vec), full(ones)],
        out_specs=(row(D),) * 11,
        compiler_params=_cparams(("parallel", "parallel")),
        name="rwkv_proj",
    )(x, x, x, mods, g, xmix, wr, wk, wv, dw1, dw2, da1, da2, g1, g2, vec, ones)


CHUNK = 4
N_HEADS = D_MODEL // HEAD_DIM
MAP_LANES = 3 * CHUNK * N_HEADS


def _coef_kernel(r_ref, kk_ref, w_ref, kd_ref, bd_ref, sel_ref, at_ref, rt_ref, bh_ref, kh_ref, gc_ref,
                 cu_ref, cy_ref, *, reverse):
    r, a, w, kd, bd = r_ref[0], -kk_ref[0], w_ref[0], kd_ref[0], bd_ref[0]
    rows = r.shape[0]
    p = lax.broadcasted_iota(jnp.int32, r.shape, 0) & (CHUNK - 1)
    s = (CHUNK - 1 - p) if reverse else p
    back = lambda x, k: pltpu.roll(x, (rows - k) if reverse else k, axis=0)
    ahead = lambda x, k: pltpu.roll(x, k if reverse else (rows - k), axis=0)
    wb = [None] + [back(w, k) for k in range(1, CHUNK)]
    excl = jnp.ones_like(w)
    rest = jnp.ones_like(w)
    for k in range(1, CHUNK):
        excl = excl * jnp.where(s >= k, wb[k], 1.0)
        rest = rest * jnp.where(s + k <= CHUNK - 1, ahead(w, k), 1.0)
    at_ref[0] = a * excl
    rt_ref[0] = r * (excl * w)
    bh_ref[0] = bd * rest
    kh_ref[0] = kd * rest
    gc_ref[0] = excl * w * rest
    between = [None, None, wb[1], wb[1] * wb[2]]
    rw = r * w
    s1 = s[:, :LANES]
    seg = lambda x: _dot(x.astype(BF16), sel_ref[...])

    def by_dist(lead, y, first):
        out = [None] * CHUNK
        for dist in range(first, CHUNK):
            if dist == 0:
                out[0] = seg(r * y)
                continue
            e = back(y, dist) if between[dist] is None else between[dist] * back(y, dist)
            out[dist] = jnp.where(s1 >= dist, seg(lead * e), 0.0)
        return out

    lab, lak = by_dist(a, bd, 1), by_dist(a, kd, 1)
    rb, rk = by_dist(rw, bd, 0), by_dist(rw, kd, 0)
    one = jnp.ones_like(rb[0])
    bk = lambda x, k: x if k == 0 else pltpu.roll(x, (rows - k) if reverse else k, axis=0)
    md = [one]
    for dist in range(1, CHUNK):
        md.append(sum(lab[e] * (bk(md[dist - e], e) if dist - e else 1.0) for e in range(1, dist + 1)))
    gd = [None] + [sum((md[e] if e else 1.0) * bk(lak[dist - e], e) for e in range(dist)) for dist in range(1, CHUNK)]
    yzd = [sum(rb[e] * (bk(md[dist - e], e) if dist - e else 1.0) for e in range(dist + 1)) for dist in range(CHUNK)]
    yvd = [rk[dist] + sum(rb[e] * bk(gd[dist - e], e) for e in range(dist)) for dist in range(CHUNK)]

    def at_pos(table, j, first):
        out = jnp.zeros_like(one)
        for dist in range(first, CHUNK - j):
            out = jnp.where(s1 == j + dist, table[dist], out)
        return out

    zero = jnp.zeros_like(one)
    u_blocks = ([at_pos(md, j, 0) for j in range(CHUNK)] + [at_pos(gd, j, 1) for j in range(CHUNK)]
                + [zero] * CHUNK)
    y_blocks = ([at_pos(yzd, j, 0) for j in range(CHUNK)] + [at_pos(yvd, j, 0) for j in range(CHUNK)]
                + [jnp.where(s1 == j, 1.0, 0.0) for j in range(CHUNK)])
    lane_blk = lax.broadcasted_iota(jnp.int32, one.shape, 1) // N_HEADS

    def place(blocks):
        per_tile = LANES // N_HEADS
        tiles = []
        for t in range(2):
            acc = zero
            for i in range(t * per_tile, min((t + 1) * per_tile, len(blocks))):
                acc = jnp.where(lane_blk == i - t * per_tile, blocks[i], acc)
            tiles.append(acc)
        return jnp.concatenate(tiles, axis=1)

    cu_ref[0] = place(u_blocks)
    cy_ref[0] = place(y_blocks)


def _rwkv_coef(r, kk, w, kd, bd, sel, reverse):
    B, T, D = r.shape
    row, _, full = _row_specs(D)
    out = jax.ShapeDtypeStruct((B, T, D), F32)
    maps = jax.ShapeDtypeStruct((B, T, 2 * LANES), F32)
    return pl.pallas_call(
        functools.partial(_coef_kernel, reverse=reverse),
        out_shape=(out,) * 5 + (maps, maps),
        grid=(B, T // TM),
        in_specs=[row(D)] * 5 + [full(sel)],
        out_specs=(row(D),) * 5 + (row(2 * LANES), row(2 * LANES)),
        compiler_params=_cparams(("parallel", "parallel")),
        name="rwkv_coef",
    )(r, kk, w, kd, bd, sel)


def _scan_kernel(atf, atb, rtf, rtb, vf, vb, bhf, bhb, khf, khb, gcf, gcb, cuf, cub, cyf, cyb,
                 mask_ref, eye_ref, e16_ref, yf, yb, st, *, tc, nb):
    n = pl.program_id(0)

    @pl.when(n == 0)
    def _():
        st[...] = jnp.zeros(st.shape, F32)

    N = HEAD_DIM
    W = CHUNK * N_HEADS
    nch = tc // CHUNK
    dirs = ((atf, rtf, vf, bhf, khf, gcf, cuf, cyf, yf), (atb, rtb, vb, bhb, khb, gcb, cub, cyb, yb))
    lane1 = lax.broadcasted_iota(jnp.int32, (N, 3 * W), 1)
    lane2 = lax.broadcasted_iota(jnp.int32, (N, 2 * W), 1)
    spread = lambda x: (mask_ref[...] * x).astype(BF16)

    def chunk(ci, carry):
        work = []
        for d, refs in enumerate(dirs):
            cc = ci if d == 0 else nch - 1 - ci
            rows = [pl.ds(cc * CHUNK + (s if d == 0 else CHUNK - 1 - s), 1) for s in range(CHUNK)]
            for b in range(nb):
                work.append((d * nb + b, b, cc, rows, refs))
        firsts = []
        for gi, b, cc, rows, (AT, RT, V, BH, KH, GC, CU, CY, Y) in work:
            lhs = jnp.concatenate([st[gi].astype(BF16), eye_ref[...]], axis=0)
            w1 = jnp.concatenate([spread(X[b, rw, :]) for X in (AT, V, RT) for rw in rows], axis=0)
            firsts.append(_dot_nt(lhs, w1))
        mids = []
        for (gi, b, cc, rows, (AT, RT, V, BH, KH, GC, CU, CY, Y)), out in zip(work, firsts):
            zvq = jnp.where((lane1 >= W) & (lane1 < 2 * W), out[N:], out[:N])
            zvq16 = zvq.astype(BF16)
            maprows = lambda M: [e16_ref[...] * M[b, rw, :MAP_LANES] for rw in rows]
            wu = jnp.concatenate(maprows(CU) + [jnp.zeros((W, MAP_LANES), F32)], axis=0)
            u = _dot_nt(zvq16, wu.astype(BF16))
            wy = jnp.concatenate(maprows(CY), axis=0)
            yt = _dot_nt(wy.astype(BF16), zvq16)
            for s, rw in enumerate(rows):
                Y[b, rw] = yt[s * N_HEADS:(s + 1) * N_HEADS][None]
            mids.append((zvq, u))
        for (gi, b, cc, rows, (AT, RT, V, BH, KH, GC, CU, CY, Y)), (zvq, u) in zip(work, mids):
            uv = jnp.where(lane2 < W, u, zvq[:, :2 * W]).astype(BF16)
            w2 = jnp.concatenate([spread(X[b, rw, :]) for X in (BH, KH) for rw in rows], axis=0)
            st[gi] = st[gi] * GC[b, rows[0], :] + _dot(uv, w2)
        return carry

    lax.fori_loop(0, nch, chunk, 0)


def _rwkv_scan(ins_f, ins_b, v, mask, eye, e16, L):
    B, T, D = v.shape
    tc = SCAN_CHUNK
    nch = tc // CHUNK
    nc, nchunks = L // tc, T // tc
    fwd_idx = lambda n: n
    rev_idx = lambda n: jnp.where(n < nc, nc - 1 - n, nchunks - 1 - (n - nc))
    tok = lambda idx, w: pl.BlockSpec((B, tc, w), lambda n: (0, idx(n), 0))
    ys = lambda idx: pl.BlockSpec((B, tc, N_HEADS, HEAD_DIM), lambda n: (0, idx(n), 0, 0))
    full = lambda a: pl.BlockSpec(a.shape, lambda n: (0,) * a.ndim)
    out = jax.ShapeDtypeStruct((B, T, N_HEADS, HEAD_DIM), F32)
    atf, rtf, bhf, khf, gcf, cuf, cyf = ins_f
    atb, rtb, bhb, khb, gcb, cub, cyb = ins_b
    f, r_ = tok(fwd_idx, D), tok(rev_idx, D)
    wy = cyf.shape[-1]
    return pl.pallas_call(
        functools.partial(_scan_kernel, tc=tc, nb=B),
        out_shape=(out, out),
        grid=(nchunks,),
        in_specs=[f, r_, f, r_, f, r_, f, r_, f, r_, f, r_, tok(fwd_idx, wy), tok(rev_idx, wy),
                  tok(fwd_idx, wy), tok(rev_idx, wy), full(mask), full(eye), full(e16)],
        out_specs=(ys(fwd_idx), ys(rev_idx)),
        scratch_shapes=[pltpu.VMEM((2 * B, HEAD_DIM, D), F32)],
        compiler_params=_cparams(("arbitrary",)),
        name="rwkv_scan",
    )(atf, atb, rtf, rtb, v, v, bhf, bhb, khf, khb, gcf, gcb, cuf, cub, cyf, cyb, mask, eye, e16)


def _rwkv_out_kernel(yf_ref, yb_ref, bv_ref, gate_ref, ln_ref, wo_ref, ones_ref, x_ref, mod_ref, g_ref,
                     rwh_ref, rwl_ref, xo_ref, h_ref, lg_ref):
    ones = ones_ref[...]
    y = yf_ref[0] + yb_ref[0]
    inv = 1.0 / HEAD_DIM
    dlt = y - _segsum_wide(y, ones) * inv
    yn = dlt * lax.rsqrt(_segsum_wide(dlt * dlt, ones) * inv + GN_EPS)
    ln = ln_ref[...]
    o = (yn * ln[0:1] + ln[1:2] + bv_ref[0]) * gate_ref[0]
    yl = _dot(o.astype(BF16), wo_ref[...])
    _ffn_prep(x_ref[0], yl, mod_ref[0, 0], g_ref[...], rwh_ref, rwl_ref, xo_ref, h_ref, lg_ref)


def _rwkv_out(yf, yb, bv, gate, ln, wo, ones, x, mods, g, rwh, rwl):
    B, T, D = x.shape
    row, mod, full = _row_specs(D)
    shapes, specs = _ffn_prep_outs(B, T, D)
    return pl.pallas_call(
        _rwkv_out_kernel,
        out_shape=shapes,
        grid=(B, T // TM),
        in_specs=[row(D), row(D), row(D), row(D), full(ln), full(wo), full(ones), row(D), mod, full(g),
                  full(rwh), full(rwl)],
        out_specs=specs,
        compiler_params=_cparams(("parallel", "parallel")),
        name="rwkv_out",
    )(yf, yb, bv, gate, ln, wo, ones, x, mods, g, rwh, rwl)


def _rope_tables(S, L):
    rows = S // GRID_W
    row = jnp.repeat(jnp.arange(rows, dtype=F32), GRID_W)
    col = (jnp.arange(rows * GRID_W) % GRID_W).astype(F32)
    n_freq = HEAD_DIM // 4
    inv = ROPE_THETA ** (-jnp.arange(n_freq, dtype=F32) / n_freq)
    lane = np.arange(LANES) % HEAD_DIM
    axis, half, freq = lane // 32, (lane % 32) // 16, lane % 16
    pos = jnp.where(jnp.asarray(axis == 0)[None, :], row[:, None], col[:, None])
    ang = pos * inv[freq][None, :]
    sgn = jnp.asarray(np.where(half == 0, -1.0, 1.0), dtype=F32)
    cos = jnp.concatenate([jnp.ones((L, LANES), F32), jnp.cos(ang)], axis=0)
    sin = jnp.concatenate([jnp.zeros((L, LANES), F32), jnp.sin(ang) * sgn[None, :]], axis=0)
    return cos, sin


def kernel(x, c, ctx, c_ctx, ada_w, ada_b, norm_mix_g, norm_ffn_g, attn_w_in, attn_w_out, attn_sink,
           attn_q_norm_g, attn_k_norm_g, rwkv_x_mix, rwkv_w_r, rwkv_w_k, rwkv_w_v, rwkv_w_o,
           rwkv_decay_w0, rwkv_decay_w1, rwkv_decay_w2, rwkv_iclr_a0, rwkv_iclr_a1, rwkv_iclr_a2,
           rwkv_gate_g1, rwkv_gate_g2, rwkv_k_k, rwkv_k_a, rwkv_r_k, rwkv_ln_g, rwkv_ln_b,
           router_w, router_bias, moe_w1, moe_w3, moe_w2, final_norm_g):
    B, S, D = x.shape
    L = ctx.shape[1]
    T = L + S
    depth = ada_w.shape[0]
    assert D == D_MODEL and L == TM and S % TM == 0 and B == 2 and depth == 2
    ones = _seg_ones()
    bf = lambda a: a.astype(BF16)

    cs = jnp.zeros((8, D), F32).at[:B].set(c).at[B].set(c_ctx)
    ada = _ada(cs, ada_w, ada_b).reshape(depth, 8, 6, D)
    mods = [jnp.stack([jnp.broadcast_to(ada[i, B], (B, 6, D)), ada[i, :B]], axis=1) for i in range(depth)]

    rw = jnp.zeros((D, LANES), F32).at[:, :N_EXPERTS].set(router_w)
    rwh, rwl = _split(rw)

    w_in = attn_w_in[0]
    roped = np.concatenate([np.arange(0, 640), np.arange(768, 1408)])
    w_rot = w_in[:, roped ^ 16]
    cos, sin = _rope_tables(S, L)
    lane = np.arange(LANES) % HEAD_DIM
    gains = lambda g: jnp.stack([g[lane], g[lane ^ 16]], axis=0)
    qa, ka, va, qb, kb, vb_x = _inproj(ctx, x, mods[0], norm_mix_g[0].reshape(1, D), bf(w_in), bf(w_rot), cos, sin,
                                       gains(attn_q_norm_g[0]), gains(attn_k_norm_g[0]), ones)
    grouped = lambda q: q.reshape(B, A_KV_HEADS, GROUP, T, HEAD_DIM)
    qa, qb = grouped(qa), grouped(qb)
    sink = attn_sink[0].astype(F32) * LOG2E
    vc = va[:, :, :L]
    va_x = jnp.swapaxes(jnp.concatenate(
        [vc, jnp.ones_like(vc[..., :1]), jnp.zeros_like(vc[..., :VT_ROWS - HEAD_DIM - 1])], axis=-1), 2, 3)
    nosink = jnp.full((B_Q_HEADS,), NEG, F32)
    oa_l = _window_attn(sink, qa, ka, va, L, S)
    oa_c = _flash(sink, qa, ka, va_x, q_rows=L, q_off=0, k_rows=L, tq=L, tk=L)
    ob_l = _flash(nosink, qb, kb, vb_x, q_rows=S, q_off=L, k_rows=T, tq=256, tk=_key_tile(T))
    ob_c = _flash(nosink, qb, kb, vb_x, q_rows=L, q_off=0, k_rows=L, tq=L, tk=L)
    w_out = bf(attn_w_out[0])
    na = A_Q_HEADS * HEAD_DIM
    xa, h, lg = _attn_out(oa_c, oa_l, ob_c, ob_l, w_out[:na], w_out[na:], ctx, x, mods[0],
                          norm_ffn_g[0].reshape(1, D), rwh, rwl)
    y0, y1, gw = _moe(h.reshape(B * T, D), lg.reshape(B * T, LANES), router_bias,
                      moe_w1, moe_w3, moe_w2, 0)
    xa = _residual(xa, y0.reshape(B, T, D), y1.reshape(B, T, D), gw.reshape(B, T, TOP_K), mods[0])

    cat2 = lambda a: jnp.concatenate([a[0], a[1]], axis=1)
    vec = jnp.stack([rwkv_k_k[0], rwkv_k_a[0], rwkv_r_k[0].reshape(D), rwkv_decay_w0[0, 0], rwkv_decay_w0[0, 1],
                     rwkv_iclr_a0[0, 0], rwkv_iclr_a0[0, 1], jnp.zeros((D,), F32)], axis=0)
    outs = _rwkv_proj(xa, mods[1], norm_mix_g[1].reshape(1, D), jnp.pad(rwkv_x_mix[0], ((0, 2), (0, 0))),
                      bf(rwkv_w_r[0]), bf(rwkv_w_k[0]), bf(rwkv_w_v[0]),
                      bf(cat2(rwkv_decay_w1[0])), bf(rwkv_decay_w2[0]),
                      bf(cat2(rwkv_iclr_a1[0])), bf(rwkv_iclr_a2[0]),
                      bf(rwkv_gate_g1[0]), bf(rwkv_gate_g2[0]), vec, ones)
    r, v, kk, bv, gate, w0, w1, kd0, kd1, bd0, bd1 = outs
    lane_id = np.arange(D)
    eye = jnp.asarray(np.arange(HEAD_DIM)[:, None] == (lane_id % HEAD_DIM)[None, :], dtype=BF16)
    n_heads = D // HEAD_DIM
    head_mask = jnp.asarray(np.arange(n_heads)[:, None] == (lane_id // HEAD_DIM)[None, :], dtype=F32)
    sel = jnp.asarray((lane_id // HEAD_DIM)[:, None] == (np.arange(LANES) % n_heads)[None, :], dtype=BF16)
    e16 = jnp.asarray(np.arange(n_heads)[:, None] == (np.arange(MAP_LANES) % n_heads)[None, :], dtype=F32)
    scan_ins = [_rwkv_coef(r, kk, w_d, kd_d, bd_d, sel, reverse=d == 1)
                for d, (w_d, kd_d, bd_d) in enumerate(((w0, kd0, bd0), (w1, kd1, bd1)))]
    yf, yb = _rwkv_scan(scan_ins[0], scan_ins[1], v, head_mask, eye, e16, L)
    ln = jnp.stack([rwkv_ln_g[0], rwkv_ln_b[0]] + [jnp.zeros((D,), F32)] * 6, axis=0)
    xa, h, lg = _rwkv_out(yf.reshape(B, T, D), yb.reshape(B, T, D), bv, gate, ln, bf(rwkv_w_o[0]), ones, xa, mods[1],
                          norm_ffn_g[1].reshape(1, D), rwh, rwl)
    y0, y1, gw = _moe(h[:, L:].reshape(B * S, D), lg[:, L:].reshape(B * S, LANES), router_bias,
                      moe_w1, moe_w3, moe_w2, 1)
    return _final(xa, y0.reshape(B, S, D), y1.reshape(B, S, D), gw.reshape(B, S, TOP_K), mods[1],
                  final_norm_g.reshape(1, D), L)


def _key_tile(T):
    for tk in (1280, 1024, 768, 512, 256):
        if T % tk == 0:
            return tk
    raise ValueError(T)
```
